```python
import jax, jax.numpy as jnp
from jax import lax
import numpy as np

D_MODEL = 1024
BATCH = 32
SEQ = 256
DEPTH = 4
DEC_BATCH = 2
DEC_SEQ = 2048
PAST_LEN = 512

GRID_W = 64
N_MIX = (DEPTH + 1) // 2
N_CONV = DEPTH // 2

MLA_HEADS = 8
MLA_NOPE = 64
MLA_ROPE = 32
MLA_V = 64
Q_LORA = 384
KV_LORA = 256
ROPE_BASE = 10000.0
Q_BLOCK = 128
RET_HEADS = 8
RET_DK = 64
RET_DV = 64
RET_CHUNK = 128
EVEN_SPLITS = (Q_LORA, KV_LORA, MLA_ROPE, RET_HEADS * RET_DK, RET_HEADS * RET_DK,
               RET_HEADS * RET_DV, RET_HEADS * RET_DV)
IN_EVEN = sum(EVEN_SPLITS)
MIX_WIDTH = MLA_HEADS * MLA_V + RET_HEADS * RET_DV
CONV_W = 3
D_FF = 2816
N_EXPERTS = 8
TOP_K = 2
D_FF_EXPERT = 2816
ALPHA = (2.0 * DEPTH) ** 0.25
BETA = (8.0 * DEPTH) ** -0.25

kernel_name = 'hybrid_mla_retention_shortconv_diffusion_step'


def _layer_norm(x, g, b, eps=1e-5):
    xf = x.astype(jnp.float32)
    mu = jnp.mean(xf, -1, keepdims=True)
    var = jnp.mean(jnp.square(xf - mu), -1, keepdims=True)
    y = (xf - mu) * lax.rsqrt(var + eps)
    return (y * g.astype(jnp.float32) + b.astype(jnp.float32)).astype(x.dtype)


def _rms_norm(x, g, eps=1e-6):
    xf = x.astype(jnp.float32)
    y = xf * lax.rsqrt(jnp.mean(jnp.square(xf), -1, keepdims=True) + eps)
    return (y * g.astype(jnp.float32)).astype(x.dtype)


def _head_norm(x, eps=1e-5):
    xf = x.astype(jnp.float32)
    mu = jnp.mean(xf, -1, keepdims=True)
    var = jnp.mean(jnp.square(xf - mu), -1, keepdims=True)
    return (xf - mu) * lax.rsqrt(var + eps)


def _rope(x, cos, sin):
    xf = x.astype(jnp.float32)
    x1, x2 = xf[..., 0::2], xf[..., 1::2]
    out = jnp.stack([x1 * cos - x2 * sin, x1 * sin + x2 * cos], axis=-1)
    return out.reshape(x.shape).astype(x.dtype)


def _axial_angles(n_tok):
    rows = n_tok // GRID_W
    r, col = jnp.meshgrid(jnp.arange(rows, dtype=jnp.float32), jnp.arange(GRID_W, dtype=jnp.float32),
                          indexing='ij')
    r = r.reshape(-1)
    col = col.reshape(-1)
    n_freq = MLA_ROPE // 4
    freqs = 1.0 / (ROPE_BASE ** (jnp.arange(n_freq, dtype=jnp.float32) / n_freq))
    ang = jnp.concatenate([r[:, None] * freqs, col[:, None] * freqs], axis=-1)
    return jnp.cos(ang), jnp.sin(ang)


def _ret_angles(n_tok):
    theta = 1.0 / (10000.0 ** jnp.linspace(0.0, 1.0, RET_DK // 2, dtype=jnp.float32))
    ang = jnp.arange(n_tok, dtype=jnp.float32)[:, None] * theta
    return jnp.cos(ang), jnp.sin(ang)


def _split(x, sizes):
    idx = np.cumsum(np.array(sizes))[:-1].tolist()
    return jnp.split(x, idx, axis=-1)


def _modulation(cond, w_mod, b_mod):
    m = jax.nn.silu(cond) @ w_mod + b_mod
    return jnp.split(m[:, None, :], 6, axis=-1)


def _modulate(x, shift, scale):
    return x * (1.0 + scale) + shift


def _attention(q, k, v):
    B, Tq, H, dq = q.shape
    dv = v.shape[-1]
    nb = Tq // Q_BLOCK
    scale = dq ** -0.5
    qb = q.reshape(B, nb, Q_BLOCK, H, dq).transpose(1, 0, 2, 3, 4)

    def one_block(qblk):
        s = jnp.einsum('bqhd,bkhd->bhqk', qblk, k).astype(jnp.float32) * scale
        p = jax.nn.softmax(s, axis=-1).astype(v.dtype)
        return jnp.einsum('bhqk,bkhd->bqhd', p, v)

    out = lax.map(one_block, qb)
    return out.transpose(1, 0, 2, 3, 4).reshape(B, Tq, H, dv)


def _retention_scan(q, k, v, log_gamma, s0):
    B, T, H, _ = q.shape
    dv = v.shape[-1]
    C = RET_CHUNK
    n = T // C
    idx = jnp.arange(C, dtype=jnp.float32)
    diff = idx[:, None] - idx[None, :]
    decay_mat = jnp.where(diff >= 0, jnp.exp(jnp.maximum(diff, 0.0) * log_gamma[:, None, None]), 0.0)
    q_decay = jnp.exp((idx + 1.0) * log_gamma[:, None])
    k_decay = jnp.exp((C - 1.0 - idx) * log_gamma[:, None])
    c_decay = jnp.exp(C * log_gamma)

    def to_chunks(a):
        return a.astype(jnp.float32).reshape(B, n, C, H, a.shape[-1]).transpose(1, 0, 3, 2, 4)

    def step(S, qkv):
        qc, kc, vc = qkv
        scores = jnp.einsum('bhid,bhjd->bhij', qc, kc) * decay_mat
        inner = jnp.einsum('bhij,bhjv->bhiv', scores, vc)
        cross = jnp.einsum('bhid,bhdv->bhiv', qc, S) * q_decay[None, :, :, None]
        S = S * c_decay[None, :, None, None] + jnp.einsum('bhjd,bhjv->bhdv', kc * k_decay[None, :, :, None], vc)
        return S, inner + cross

    S_fin, out = lax.scan(step, s0.astype(jnp.float32), (to_chunks(q), to_chunks(k), to_chunks(v)))
    out = out.transpose(1, 0, 3, 2, 4).reshape(B, T, H, dv)
    return out, S_fin


def _retention(rq, rk, rv, rg, lg_f, lg_b, s0_f, s0_b):
    B, T = rq.shape[:2]
    yf, sf = _retention_scan(rq, rk, rv, lg_f, s0_f)
    yb, sb = _retention_scan(jnp.flip(rq, 1), jnp.flip(rk, 1), jnp.flip(rv, 1), lg_b, s0_b)
    y = _head_norm(yf + jnp.flip(yb, 1)).reshape(B, T, RET_HEADS * RET_DV).astype(rv.dtype)
    return y * jax.nn.silu(rg), sf.astype(rv.dtype), sb.astype(rv.dtype)


def _even_project(h, w_in, q_a_gain, kv_a_gain, w_q_b):
    B, T, _ = h.shape
    q_a, kv_a, k_pe, rq, rk, rv, rg = _split(h @ w_in, EVEN_SPLITS)
    q = (_rms_norm(q_a, q_a_gain) @ w_q_b).reshape(B, T, MLA_HEADS, MLA_NOPE + MLA_ROPE)
    ckv = _rms_norm(kv_a, kv_a_gain)
    rq = rq.reshape(B, T, RET_HEADS, RET_DK)
    rk = rk.reshape(B, T, RET_HEADS, RET_DK) * (RET_DK ** -0.5)
    rv = rv.reshape(B, T, RET_HEADS, RET_DV)
    return q, ckv, k_pe, rq, rk, rv, rg


def _mla_kv(ckv, k_pe, w_kv_b):
    B, T, _ = ckv.shape
    kv = (ckv @ w_kv_b).reshape(B, T, MLA_HEADS, MLA_NOPE + MLA_V)
    k_nope, v = kv[..., :MLA_NOPE], kv[..., MLA_NOPE:]
    k = jnp.concatenate([k_nope, jnp.broadcast_to(k_pe[:, :, None, :], (B, T, MLA_HEADS, MLA_ROPE))], axis=-1)
    return k, v


def _mix_context(h, w_in, q_a_gain, kv_a_gain, w_q_b, w_kv_b, lg_f, lg_b, w_out):
    B, L, _ = h.shape
    q, ckv, k_pe, rq, rk, rv, rg = _even_project(h, w_in, q_a_gain, kv_a_gain, w_q_b)
    k, v = _mla_kv(ckv, k_pe, w_kv_b)
    attn = _attention(q, k, v)
    zeros = jnp.zeros((B, RET_HEADS, RET_DK, RET_DV), jnp.float32)
    ret, sf, sb = _retention(rq, rk, rv, rg, lg_f, lg_b, zeros, zeros)
    y = jnp.concatenate([attn.reshape(B, L, MLA_HEADS * MLA_V), ret], axis=-1) @ w_out
    return y, ckv, k_pe, sf, sb


def _mix_latent(h, ckv_c, kpe_c, sf0, sb0, rope_cs, ret_cs, w_in, q_a_gain, kv_a_gain, w_q_b, w_kv_b,
                lg_f, lg_b, w_out):
    B, T, _ = h.shape
    q, ckv, k_pe, rq, rk, rv, rg = _even_project(h, w_in, q_a_gain, kv_a_gain, w_q_b)
    cos, sin = rope_cs
    q = jnp.concatenate([q[..., :MLA_NOPE], _rope(q[..., MLA_NOPE:], cos[:, None, :], sin[:, None, :])], axis=-1)
    k_pe = _rope(k_pe, cos, sin)
    k_l, v_l = _mla_kv(ckv, k_pe, w_kv_b)
    k_c, v_c = _mla_kv(ckv_c, kpe_c, w_kv_b)
    attn = _attention(q, jnp.concatenate([k_c, k_l], axis=1), jnp.concatenate([v_c, v_l], axis=1))
    rcos, rsin = ret_cs
    rq = _rope(rq, rcos[:, None, :], rsin[:, None, :])
    rk = _rope(rk, rcos[:, None, :], rsin[:, None, :])
    ret, _, _ = _retention(rq, rk, rv, rg, lg_f, lg_b, sf0, sb0)
    return jnp.concatenate([attn.reshape(B, T, MLA_HEADS * MLA_V), ret], axis=-1) @ w_out


def _short_conv(h, w_in, conv_w, w_out):
    b_gate, c_gate, u = jnp.split(h @ w_in, 3, axis=-1)
    z = c_gate * u
    zp = jnp.pad(z, ((0, 0), (1, 1), (0, 0)))
    y = zp[:, :-2] * conv_w[0] + zp[:, 1:-1] * conv_w[1] + zp[:, 2:] * conv_w[2]
    return (b_gate * y) @ w_out


def _swiglu(h, w_gate, w_up, w_down):
    return (jax.nn.silu(h @ w_gate) * (h @ w_up)) @ w_down


def _moe(h, router_w, router_b, exp_gate, exp_up, exp_down):
    logits = (h @ router_w).astype(jnp.float32) + router_b.astype(jnp.float32)
    top_val, top_idx = lax.top_k(logits, TOP_K)
    wts = jax.nn.softmax(top_val, axis=-1)
    combine = jnp.sum(jax.nn.one_hot(top_idx, N_EXPERTS, dtype=jnp.float32) * wts[..., None], axis=-2)
    combine = combine.astype(h.dtype)
    y = jnp.zeros_like(h)
    for e in range(N_EXPERTS):
        y = y + combine[..., e:e + 1] * _swiglu(h, exp_gate[e], exp_up[e], exp_down[e])
    return y


def setup_inputs(seed: int = 0) -> dict:
    key = jax.random.key(seed)
    ks = jax.random.split(key, 32)
    f32 = jnp.float32
    D = D_MODEL

    def nrm(i, shape, scale):
        return jax.random.normal(ks[i], shape, f32) * scale

    heads = np.arange(RET_HEADS, dtype=np.float32)
    decay_logit = jnp.asarray(np.log(2.0 ** (5.0 + heads) - 1.0), dtype=f32)
    return {
        'x_prompt': nrm(0, (BATCH, SEQ, D), 1.0),
        'x_sample': nrm(1, (DEC_BATCH, DEC_SEQ, D), 1.0),
        'c': nrm(2, (DEC_BATCH, D), 1.0),
        'cache_ckv': nrm(3, (DEC_BATCH, N_MIX, PAST_LEN, KV_LORA), 1.0),
        'cache_kpe': nrm(4, (DEC_BATCH, N_MIX, PAST_LEN, MLA_ROPE), 1.0),
        'state_ret_fwd': nrm(5, (DEC_BATCH, N_MIX, RET_HEADS, RET_DK, RET_DV), 0.5),
        'state_ret_bwd': nrm(6, (DEC_BATCH, N_MIX, RET_HEADS, RET_DK, RET_DV), 0.5),
        'c_ctx': nrm(7, (D,), 1.0),
        'w_mod': nrm(8, (DEPTH, D, 6 * D), D ** -0.5),
        'b_mod': nrm(9, (DEPTH, 6 * D), 0.02),
        'ln_g': 1.0 + nrm(10, (DEPTH, 2, D), 0.02),
        'ln_b': nrm(11, (DEPTH, 2, D), 0.02),
        'w_in_mix': nrm(12, (N_MIX, D, IN_EVEN), D ** -0.5),
        'q_a_gain': 1.0 + nrm(13, (N_MIX, Q_LORA), 0.02),
        'kv_a_gain': 1.0 + nrm(14, (N_MIX, KV_LORA), 0.02),
        'w_q_b': nrm(15, (N_MIX, Q_LORA, MLA_HEADS * (MLA_NOPE + MLA_ROPE)), Q_LORA ** -0.5),
        'w_kv_b': nrm(16, (N_MIX, KV_LORA, MLA_HEADS * (MLA_NOPE + MLA_V)), KV_LORA ** -0.5),
        'ret_decay_fwd': decay_logit[None, :] + nrm(17, (N_MIX, RET_HEADS), 0.1),
        'ret_decay_bwd': decay_logit[None, :] + nrm(18, (N_MIX, RET_HEADS), 0.1),
        'w_out_mix': nrm(19, (N_MIX, MIX_WIDTH, D), BETA * MIX_WIDTH ** -0.5),
        'w_in_conv': nrm(20, (N_CONV, D, 3 * D), D ** -0.5),
        'conv_w': nrm(21, (N_CONV, CONV_W, D), CONV_W ** -0.5),
        'w_out_conv': nrm(22, (N_CONV, D, D), BETA * D ** -0.5),
        'ffn_gate': nrm(23, (N_MIX, D, D_FF), D ** -0.5),
        'ffn_up': nrm(24, (N_MIX, D, D_FF), D ** -0.5),
        'ffn_down': nrm(25, (N_MIX, D_FF, D), BETA * D_FF ** -0.5),
        'router_w': nrm(26, (N_CONV, D, N_EXPERTS), D ** -0.5),
        'router_b': nrm(27, (N_CONV, N_EXPERTS), 0.01),
        'exp_gate': nrm(28, (N_CONV, N_EXPERTS, D, D_FF_EXPERT), D ** -0.5),
        'exp_up': nrm(29, (N_CONV, N_EXPERTS, D, D_FF_EXPERT), D ** -0.5),
        'exp_down': nrm(30, (N_CONV, N_EXPERTS, D_FF_EXPERT, D), BETA * D_FF_EXPERT ** -0.5),
    }


def reference(x_prompt, x_sample, c, cache_ckv, cache_kpe, state_ret_fwd, state_ret_bwd,
              c_ctx, w_mod, b_mod, ln_g, ln_b,
              w_in_mix, q_a_gain, kv_a_gain, w_q_b, w_kv_b, ret_decay_fwd, ret_decay_bwd, w_out_mix,
              w_in_conv, conv_w, w_out_conv,
              ffn_gate, ffn_up, ffn_down,
              router_w, router_b, exp_gate, exp_up, exp_down):
    n_lat = x_sample.shape[1]
    rope_cs = _axial_angles(n_lat)
    ret_cs = _ret_angles(n_lat)
    xp, xs = x_prompt, x_sample
    ckv_new, kpe_new, sf_new, sb_new = [], [], [], []
    for layer in range(DEPTH):
        j = layer // 2
        mp = _modulation(c_ctx[None], w_mod[layer], b_mod[layer])
        ms = _modulation(c, w_mod[layer], b_mod[layer])
        hp = _modulate(xp, mp[0], mp[1])
        hs = _modulate(xs, ms[0], ms[1])
        if layer % 2 == 0:
            lg_f = jax.nn.log_sigmoid(ret_decay_fwd[j].astype(jnp.float32))
            lg_b = jax.nn.log_sigmoid(ret_decay_bwd[j].astype(jnp.float32))
            yp, ckv, kpe, sf, sb = _mix_context(hp, w_in_mix[j], q_a_gain[j], kv_a_gain[j], w_q_b[j],
                                                w_kv_b[j], lg_f, lg_b, w_out_mix[j])
            ys = _mix_latent(hs, cache_ckv[:, j], cache_kpe[:, j], state_ret_fwd[:, j], state_ret_bwd[:, j],
                             rope_cs, ret_cs, w_in_mix[j], q_a_gain[j], kv_a_gain[j], w_q_b[j], w_kv_b[j],
                             lg_f, lg_b, w_out_mix[j])
            ckv_new.append(ckv)
            kpe_new.append(kpe)
            sf_new.append(sf)
            sb_new.append(sb)
        else:
            yp = _short_conv(hp, w_in_conv[j], conv_w[j], w_out_conv[j])
            ys = _short_conv(hs, w_in_conv[j], conv_w[j], w_out_conv[j])
        xp = _layer_norm(ALPHA * xp + mp[2] * yp, ln_g[layer, 0], ln_b[layer, 0])
        xs = _layer_norm(ALPHA * xs + ms[2] * ys, ln_g[layer, 0], ln_b[layer, 0])
        hp = _modulate(xp, mp[3], mp[4])
        hs = _modulate(xs, ms[3], ms[4])
        if layer % 2 == 0:
            fp = _swiglu(hp, ffn_gate[j], ffn_up[j], ffn_down[j])
            fs = _swiglu(hs, ffn_gate[j], ffn_up[j], ffn_down[j])
        else:
            fp = _moe(hp, router_w[j], router_b[j], exp_gate[j], exp_up[j], exp_down[j])
            fs = _moe(hs, router_w[j], router_b[j], exp_gate[j], exp_up[j], exp_down[j])
        xp = _layer_norm(ALPHA * xp + mp[5] * fp, ln_g[layer, 1], ln_b[layer, 1])
        xs = _layer_norm(ALPHA * xs + ms[5] * fs, ln_g[layer, 1], ln_b[layer, 1])
    new_ckv = jnp.stack(ckv_new, axis=1)
    new_kpe = jnp.stack(kpe_new, axis=1)
    new_ret_fwd = jnp.stack(sf_new, axis=1)
    new_ret_bwd = jnp.stack(sb_new, axis=1)
    return (xp, xs, new_ckv, new_kpe, new_ret_fwd, new_ret_bwd)
```

```python
import functools

import numpy as np
import jax
import jax.numpy as jnp
from jax import lax
from jax.experimental import pallas as pl
from jax.experimental.pallas import tpu as pltpu

F32 = jnp.float32
BF16 = jnp.bfloat16

D = 1024
DEPTH = 4
N_PROMPT_SEQ, PROMPT_LEN = 32, 256
N_LAT_SEQ, LAT_LEN = 2, 2048
PAST = 512
GRID_W = 64
TP = N_PROMPT_SEQ * PROMPT_LEN
TS = N_LAT_SEQ * LAT_LEN
T = TP + TS
HEADS = 8
NOPE, ROPE, VDIM = 64, 32, 64
Q_LORA, KV_LORA = 384, 256
RDK = 64
D_FF = 2816
N_EXP = 8
ALPHA = (2.0 * DEPTH) ** 0.25
Q_SCALE = float((NOPE + ROPE) ** -0.5)
LANES = 128
N_GROUPS = 8

TM = 256
TM_FF = 512
TF = D_FF // 2
TQ = 256
NP_ROWS = 2 * T + N_EXP * TM_FF
VMEM_LIMIT = 56 * 1024 * 1024

IN_COLS = 3072


def _cparams(sem):
    return pltpu.CompilerParams(dimension_semantics=sem, vmem_limit_bytes=VMEM_LIMIT)


def _group_of_tile(i, tm):
    per_seq = LAT_LEN // tm
    return jnp.maximum(i - TP // tm + per_seq, 0) // per_seq


def _bdot(a, b):
    return jnp.dot(a, b, preferred_element_type=F32)


def _sigmoid(v):
    return 1.0 / (1.0 + jnp.exp(-v))


def _layer_norm(v, g, b):
    mu = jnp.mean(v, axis=-1, keepdims=True)
    d = v - mu
    var = jnp.mean(d * d, axis=-1, keepdims=True)
    return d * lax.rsqrt(var + 1e-5) * g + b


def _rms(v, g):
    return v * lax.rsqrt(jnp.mean(v * v, axis=-1, keepdims=True) + 1e-6) * g


def _mod_kernel(c_ref, w_ref, b_ref, o_ref):
    c = c_ref[...]
    s = (c * _sigmoid(c)).astype(BF16)
    o_ref[...] = _bdot(s, w_ref[...].astype(BF16)) + b_ref[...]


def _modulation(cond8, w_mod, b_mod):
    tn = 1536
    return pl.pallas_call(
        _mod_kernel,
        grid=(DEPTH, 6 * D // tn),
        in_specs=[pl.BlockSpec((8, D), lambda l, n: (0, 0)),
                  pl.BlockSpec((None, D, tn), lambda l, n: (l, 0, n)),
                  pl.BlockSpec((None, 1, tn), lambda l, n: (l, 0, n))],
        out_specs=pl.BlockSpec((None, 8, tn), lambda l, n: (l, 0, n)),
        out_shape=jax.ShapeDtypeStruct((DEPTH, 8, 6 * D), F32),
        compiler_params=_cparams(("arbitrary", "arbitrary")),
        name="modulation",
    )(cond8, w_mod, b_mod.reshape(DEPTH, 1, 6 * D))


def _swap_halves(a, half):
    n = a.shape[-1]
    lane = lax.broadcasted_iota(jnp.int32, a.shape, 1)
    first = (lane & (2 * half - 1)) < half
    return jnp.where(first, -pltpu.roll(a, n - half, axis=1), pltpu.roll(a, half, axis=1))


def _even_in_kernel(x_ref, mod_ref, w_in_ref, qg_ref, kvg_ref, wq_ref, wkv_ref, ek_ref,
                    cq_ref, sq_ref, ck_ref, sk_ref, cr_ref, sr_ref,
                    q_ref, k_ref, v_ref, ckv_ref, kpe_ref, rq_ref, rk_ref, rv_ref, rg_ref):
    x = x_ref[...]
    h = (x * (1.0 + mod_ref[1:2, :]) + mod_ref[0:1, :]).astype(BF16)
    p = _bdot(h, w_in_ref[...])
    qn = _rms(p[:, 0:Q_LORA], qg_ref[...]).astype(BF16)
    qa = _bdot(qn, wq_ref[...])
    lane = lax.broadcasted_iota(jnp.int32, qa.shape, 1) & (LANES - 1)
    qb = jnp.where(lane < NOPE + ROPE // 2,
                   -pltpu.roll(qa, qa.shape[1] - ROPE // 2, axis=1),
                   pltpu.roll(qa, ROPE // 2, axis=1))
    cq = jnp.concatenate([cq_ref[...]] * HEADS, axis=1)
    sq = jnp.concatenate([sq_ref[...]] * HEADS, axis=1)
    q_ref[...] = ((qa * cq + qb * sq) * Q_SCALE).astype(BF16)
    ckv = _rms(p[:, Q_LORA:Q_LORA + KV_LORA], kvg_ref[...])
    ckv_ref[...] = ckv
    kv = _bdot(ckv.astype(BF16), wkv_ref[...])
    base = 2688
    kpe_ref[...] = p[:, base:base + LANES]
    kpe_rot = p[:, base + LANES:base + 2 * LANES] * ck_ref[...] + p[:, base + 2 * LANES:base + 3 * LANES] * sk_ref[...]
    k_ref[...] = (kv[:, :HEADS * LANES] + _bdot(kpe_rot.astype(BF16), ek_ref[...])).astype(BF16)
    v_ref[...] = kv[:, HEADS * LANES:].astype(BF16)
    cr = jnp.concatenate([cr_ref[...]] * 4, axis=1)
    sr = jnp.concatenate([sr_ref[...]] * 4, axis=1)
    rq = p[:, 640:1152]
    rk = p[:, 1152:1664]
    rq_ref[...] = (rq * cr + _swap_halves(rq, RDK // 2) * sr).astype(BF16)
    rk_ref[...] = (rk * cr + _swap_halves(rk, RDK // 2) * sr).astype(BF16)
    rv_ref[...] = p[:, 1664:2176].astype(BF16)
    rg_ref[...] = p[:, 2176:2688]


def _even_in(x, mod_l, w_in, qg, kvg, wq, wkv, ek, tabs):
    tok = lambda w: pl.BlockSpec((TM, w), lambda i: (i, 0))
    full = lambda a: pl.BlockSpec(a.shape, lambda i: (0,) * a.ndim)
    outs = [(HEADS * LANES, BF16), (HEADS * LANES, BF16), (HEADS * LANES, BF16), (KV_LORA, F32), (LANES, F32),
            (512, BF16), (512, BF16), (512, BF16), (512, F32)]
    return pl.pallas_call(
        _even_in_kernel,
        grid=(T // TM,),
        in_specs=[tok(D), pl.BlockSpec((None, 8, D), lambda i: (_group_of_tile(i, TM), 0, 0)),
                  full(w_in), full(qg), full(kvg), full(wq), full(wkv), full(ek)] + [tok(LANES)] * 6,
        out_specs=[tok(w) for w, _ in outs],
        out_shape=[jax.ShapeDtypeStruct((T, w), dt) for w, dt in outs],
        compiler_params=_cparams(("parallel",)),
        name="even_in",
    )(x, mod_l, w_in, qg, kvg, wq, wkv, ek, *tabs)


def _ctx_kv_kernel(ckv_ref, kpe_ref, wkv_ref, ek_ref, k_ref, v_ref):
    kv = _bdot(ckv_ref[...].astype(BF16), wkv_ref[...])
    k_ref[...] = (kv[:, :HEADS * LANES] + _bdot(kpe_ref[...].astype(BF16), ek_ref[...])).astype(BF16)
    v_ref[...] = kv[:, HEADS * LANES:].astype(BF16)


def _ctx_kv(ckv_c, kpe_c, wkv, ek):
    n = ckv_c.shape[0]
    full = lambda a: pl.BlockSpec(a.shape, lambda i: (0,) * a.ndim)
    return pl.pallas_call(
        _ctx_kv_kernel,
        grid=(n // PAST,),
        in_specs=[pl.BlockSpec((PAST, KV_LORA), lambda i: (i, 0)), pl.BlockSpec((PAST, LANES), lambda i: (i, 0)),
                  full(wkv), full(ek)],
        out_specs=[pl.BlockSpec((PAST, HEADS * LANES), lambda i: (i, 0))] * 2,
        out_shape=[jax.ShapeDtypeStruct((n, HEADS * LANES), BF16)] * 2,
        compiler_params=_cparams(("parallel",)),
        name="ctx_kv",
    )(ckv_c, kpe_c, wkv, ek)


def _attn_kernel(n_kv, q_ref, *refs):
    k_refs = refs[0:2 * n_kv:2]
    v_refs = refs[1:2 * n_kv:2]
    o_ref = refs[-1]
    nt = (((1,), (1,)), ((), ()))
    for pair in range(HEADS // 2):
        acc = None
        for sub in range(2):
            sl = slice((2 * pair + sub) * LANES, (2 * pair + sub + 1) * LANES)
            qh = q_ref[:, sl]
            s = [lax.dot_general(qh, k[:, sl], nt, preferred_element_type=F32) for k in k_refs]
            m = functools.reduce(jnp.maximum, [jnp.max(a, axis=-1, keepdims=True) for a in s])
            e = [jnp.exp(a - m) for a in s]
            den = functools.reduce(jnp.add, [jnp.sum(a, axis=-1, keepdims=True) for a in e])
            o = functools.reduce(jnp.add, [_bdot(a.astype(BF16), v[:, sl]) for a, v in zip(e, v_refs)])
            o = o / den
            acc = o if acc is None else acc + o
        o_ref[:, pair * LANES:(pair + 1) * LANES] = acc.astype(BF16)


def _attention(q, k, v, kc, vc):
    w = HEADS * LANES
    out_shape = jax.ShapeDtypeStruct((T, HEADS * VDIM), BF16)
    blk = lambda i: (i, 0)
    attn = pl.pallas_call(
        functools.partial(_attn_kernel, 1),
        grid=(N_PROMPT_SEQ,),
        in_specs=[pl.BlockSpec((PROMPT_LEN, w), blk)] * 3,
        out_specs=pl.BlockSpec((PROMPT_LEN, HEADS * VDIM), blk),
        out_shape=out_shape,
        compiler_params=_cparams(("parallel",)),
        name="attn_context",
    )(q, k, v)
    qpb = LAT_LEN // TQ
    qmap = lambda b, i: (TP // TQ + b * qpb + i, 0)
    own = lambda b, i: (TP // LAT_LEN + b, 0)
    ctx = lambda b, i: (b, 0)
    return pl.pallas_call(
        functools.partial(_attn_kernel, 2),
        grid=(N_LAT_SEQ, qpb),
        in_specs=[pl.BlockSpec((TQ, w), qmap),
                  pl.BlockSpec((PAST, w), ctx), pl.BlockSpec((PAST, w), ctx),
                  pl.BlockSpec((LAT_LEN, w), own), pl.BlockSpec((LAT_LEN, w), own),
                  pl.BlockSpec(memory_space=pl.ANY)],
        out_specs=pl.BlockSpec((TQ, HEADS * VDIM), qmap),
        out_shape=out_shape,
        input_output_aliases={5: 0},
        compiler_params=_cparams(("parallel", "arbitrary")),
        name="attn_latent",
    )(q, kc, vc, k, v, attn)


def _retention_kernel(latent, lg_ref, rq_ref, rk_ref, rv_ref, rg_ref, *refs):
    if latent:
        s0f_ref, s0b_ref, _, o_ref = refs
        q0 = pl.program_id(1) * TQ
        seq_len = LAT_LEN
    else:
        o_ref, sf_ref, sb_ref = refs
        q0 = 0
        seq_len = PROMPT_LEN
    tq, tk = rq_ref.shape[0], rk_ref.shape[0]
    nt = (((1,), (1,)), ((), ()))
    n_idx = (q0 + lax.broadcasted_iota(jnp.int32, (tq, tk), 0)).astype(F32)
    m_idx = lax.broadcasted_iota(jnp.int32, (tq, tk), 1).astype(F32)
    dist = n_idx - m_idx
    adist = jnp.abs(dist)
    fwd = dist > 0.0
    diag = jnp.where(dist == 0.0, 1.0, 0.0)
    lane = lax.broadcasted_iota(jnp.int32, (1, LANES), 1)
    lo = lane < RDK
    n_col = (q0 + lax.broadcasted_iota(jnp.int32, (tq, 1), 0)).astype(F32)
    m_col = lax.broadcasted_iota(jnp.int32, (tk, 1), 0).astype(F32)
    for pair in range(HEADS // 2):
        sl = slice(pair * LANES, (pair + 1) * LANES)
        qb, kb, vb = rq_ref[:, sl], rk_ref[:, sl], rv_ref[:, sl]
        acc = jnp.zeros((tq, LANES), F32)
        for sub in range(2):
            h = 2 * pair + sub
            lgf, lgb = lg_ref[h], lg_ref[HEADS + h]
            half = lo if sub == 0 else jnp.logical_not(lo)
            qm = jnp.where(half, qb, jnp.zeros_like(qb))
            vm = jnp.where(half, vb, jnp.zeros_like(vb))
            s = lax.dot_general(qm, kb, nt, preferred_element_type=F32)
            w = jnp.exp(adist * jnp.where(fwd, lgf, lgb)) + diag
            acc = acc + _bdot((s * w).astype(BF16), vm)
        lgf_l = jnp.where(lo, lg_ref[2 * pair], lg_ref[2 * pair + 1])
        lgb_l = jnp.where(lo, lg_ref[HEADS + 2 * pair], lg_ref[HEADS + 2 * pair + 1])
        if latent:
            acc = acc + _bdot(qb, s0f_ref[pair]) * jnp.exp((n_col + 1.0) * lgf_l)
            acc = acc + _bdot(qb, s0b_ref[pair]) * jnp.exp((seq_len - n_col) * lgb_l)
        else:
            kf = (kb.astype(F32) * jnp.exp((seq_len - 1.0 - m_col) * lgf_l)).T.astype(BF16)
            kr = (kb.astype(F32) * jnp.exp(m_col * lgb_l)).T.astype(BF16)
            sf_ref[pair] = _bdot(kf, vb)
            sb_ref[pair] = _bdot(kr, vb)
        inv = 1.0 / RDK
        mu = jnp.where(lo, jnp.sum(jnp.where(lo, acc, 0.0), axis=-1, keepdims=True),
                       jnp.sum(jnp.where(lo, 0.0, acc), axis=-1, keepdims=True)) * inv
        dlt = acc - mu
        d2 = dlt * dlt
        var = jnp.where(lo, jnp.sum(jnp.where(lo, d2, 0.0), axis=-1, keepdims=True),
                        jnp.sum(jnp.where(lo, 0.0, d2), axis=-1, keepdims=True)) * inv
        g = rg_ref[:, sl]
        o_ref[:, sl] = (dlt * lax.rsqrt(var + 1e-5) * (g * _sigmoid(g))).astype(BF16)


def _retention(lg, rq, rk, rv, rg, s0f, s0b):
    w = HEADS * RDK
    out_shape = jax.ShapeDtypeStruct((T, w), BF16)
    st_shape = jax.ShapeDtypeStruct((N_PROMPT_SEQ, HEADS // 2, LANES, LANES), F32)
    blk = lambda i, lg: (i, 0)
    st_blk = pl.BlockSpec((None, HEADS // 2, LANES, LANES), lambda i, lg: (i, 0, 0, 0))
    ret, sf, sb = pl.pallas_call(
        functools.partial(_retention_kernel, False),
        grid_spec=pltpu.PrefetchScalarGridSpec(
            num_scalar_prefetch=1, grid=(N_PROMPT_SEQ,),
            in_specs=[pl.BlockSpec((PROMPT_LEN, w), blk)] * 4,
            out_specs=[pl.BlockSpec((PROMPT_LEN, w), blk), st_blk, st_blk]),
        out_shape=[out_shape, st_shape, st_shape],
        compiler_params=_cparams(("parallel",)),
        name="retention_context",
    )(lg, rq, rk, rv, rg)
    qpb = LAT_LEN // TQ
    qmap = lambda b, i, lg: (TP // TQ + b * qpb + i, 0)
    own = lambda b, i, lg: (TP // LAT_LEN + b, 0)
    s0_blk = pl.BlockSpec((None, HEADS // 2, LANES, LANES), lambda b, i, lg: (b, 0, 0, 0))
    ret = pl.pallas_call(
        functools.partial(_retention_kernel, True),
        grid_spec=pltpu.PrefetchScalarGridSpec(
            num_scalar_prefetch=1, grid=(N_LAT_SEQ, qpb),
            in_specs=[pl.BlockSpec((TQ, w), qmap), pl.BlockSpec((LAT_LEN, w), own), pl.BlockSpec((LAT_LEN, w), own),
                      pl.BlockSpec((TQ, w), qmap), s0_blk, s0_blk, pl.BlockSpec(memory_space=pl.ANY)],
            out_specs=pl.BlockSpec((TQ, w), qmap)),
        out_shape=out_shape,
        input_output_aliases={7: 0},
        compiler_params=_cparams(("parallel", "arbitrary")),
        name="retention_latent",
    )(lg, rq, rk, rv, rg, s0f, s0b, ret)
    return ret, sf, sb


def _mix_out_kernel(x_ref, a_ref, r_ref, mod_ref, w_ref, ln_ref, o_ref):
    half = HEADS * VDIM
    y = _bdot(a_ref[...], w_ref[0:half, :]) + _bdot(r_ref[...], w_ref[half:, :])
    o_ref[...] = _layer_norm(ALPHA * x_ref[...] + mod_ref[2:3, :] * y, ln_ref[0:1, :], ln_ref[1:2, :])


def _mix_out(x, attn, ret, mod_l, w_out, ln):
    tok = lambda w: pl.BlockSpec((TM, w), lambda i: (i, 0))
    return pl.pallas_call(
        _mix_out_kernel,
        grid=(T // TM,),
        in_specs=[tok(D), tok(512), tok(512), pl.BlockSpec((None, 8, D), lambda i: (_group_of_tile(i, TM), 0, 0)),
                  pl.BlockSpec(w_out.shape, lambda i: (0, 0)), pl.BlockSpec((8, D), lambda i: (0, 0))],
        out_specs=tok(D),
        out_shape=jax.ShapeDtypeStruct((T, D), F32),
        compiler_params=_cparams(("parallel",)),
        name="mix_out",
    )(x, attn, ret, mod_l, w_out, ln)


def _ffn_kernel(x_ref, mod_ref, wg_ref, wu_ref, wd_ref, ln_ref, o_ref, h_scr, acc_scr):
    f = pl.program_id(1)

    @pl.when(f == 0)
    def _():
        h_scr[...] = (x_ref[...] * (1.0 + mod_ref[4:5, :]) + mod_ref[3:4, :]).astype(BF16)
        acc_scr[...] = jnp.zeros_like(acc_scr)

    h = h_scr[...]
    g = _bdot(h, wg_ref[...])
    u = _bdot(h, wu_ref[...])
    acc_scr[...] += _bdot((g * _sigmoid(g) * u).astype(BF16), wd_ref[...])

    @pl.when(f == pl.num_programs(1) - 1)
    def _():
        o_ref[...] = _layer_norm(ALPHA * x_ref[...] + mod_ref[5:6, :] * acc_scr[...], ln_ref[2:3, :], ln_ref[3:4, :])


def _ffn(x, mod_l, wg, wu, wd, ln):
    tok = pl.BlockSpec((TM_FF, D), lambda i, f: (i, 0))
    return pl.pallas_call(
        _ffn_kernel,
        grid=(T // TM_FF, D_FF // TF),
        in_specs=[tok, pl.BlockSpec((None, 8, D), lambda i, f: (_group_of_tile(i, TM_FF), 0, 0)),
                  pl.BlockSpec((D, TF), lambda i, f: (0, f)), pl.BlockSpec((D, TF), lambda i, f: (0, f)),
                  pl.BlockSpec((TF, D), lambda i, f: (f, 0)), pl.BlockSpec((8, D), lambda i, f: (0, 0))],
        out_specs=tok,
        out_shape=jax.ShapeDtypeStruct((T, D), F32),
        scratch_shapes=[pltpu.VMEM((TM_FF, D), BF16), pltpu.VMEM((TM_FF, D), F32)],
        compiler_params=_cparams(("parallel", "arbitrary")),
        name="ffn",
    )(x, mod_l, wg, wu, wd, ln)


def _conv_in_kernel(x_ref, mod_ref, w_ref, b_ref, z_ref):
    h = (x_ref[...] * (1.0 + mod_ref[1:2, :]) + mod_ref[0:1, :]).astype(BF16)
    p = _bdot(h, w_ref[...])
    b_ref[...] = p[:, 0:D]
    z_ref[...] = p[:, D:2 * D] * p[:, 2 * D:3 * D]


def _conv_in(x, mod_l, w_in):
    tok = pl.BlockSpec((TM, D), lambda i: (i, 0))
    return pl.pallas_call(
        _conv_in_kernel,
        grid=(T // TM,),
        in_specs=[tok, pl.BlockSpec((None, 8, D), lambda i: (_group_of_tile(i, TM), 0, 0)),
                  pl.BlockSpec(w_in.shape, lambda i: (0, 0))],
        out_specs=[tok, tok],
        out_shape=[jax.ShapeDtypeStruct((T, D), F32)] * 2,
        compiler_params=_cparams(("parallel",)),
        name="conv_in",
    )(x, mod_l, w_in)


def _conv_out_kernel(x_ref, b_ref, z_ref, zp_ref, zn_ref, mod_ref, cw_ref, w_ref, ln_ref, rw_ref, rb_ref,
                     o_ref, h_ref, route_ref):
    i = pl.program_id(0)
    z = z_ref[...]
    row = lax.broadcasted_iota(jnp.int32, (TM, 1), 0)
    seq_len = jnp.where(i < TP // TM, PROMPT_LEN, LAT_LEN)
    pos = (i * TM + row) & (seq_len - 1)
    prev = jnp.where(row == 0, zp_ref[7:8, :], pltpu.roll(z, 1, axis=0))
    prev = jnp.where(pos == 0, 0.0, prev)
    nxt = jnp.where(row == TM - 1, zn_ref[0:1, :], pltpu.roll(z, TM - 1, axis=0))
    nxt = jnp.where(pos == seq_len - 1, 0.0, nxt)
    y = prev * cw_ref[0:1, :] + z * cw_ref[1:2, :] + nxt * cw_ref[2:3, :]
    t = _bdot((b_ref[...] * y).astype(BF16), w_ref[...])
    x1 = _layer_norm(ALPHA * x_ref[...] + mod_ref[2:3, :] * t, ln_ref[0:1, :], ln_ref[1:2, :])
    o_ref[...] = x1
    h = x1 * (1.0 + mod_ref[4:5, :]) + mod_ref[3:4, :]
    h_ref[...] = h.astype(BF16)
    logits = jnp.dot(h, rw_ref[...], precision=lax.Precision.HIGHEST, preferred_element_type=F32) + rb_ref[...]
    lane = lax.broadcasted_iota(jnp.int32, logits.shape, 1).astype(F32)
    t1 = jnp.max(logits, axis=-1, keepdims=True)
    i1 = jnp.min(jnp.where(logits == t1, lane, float(LANES)), axis=-1, keepdims=True)
    rest = jnp.where(lane == i1, -jnp.inf, logits)
    t2 = jnp.max(rest, axis=-1, keepdims=True)
    i2 = jnp.min(jnp.where(rest == t2, lane, float(LANES)), axis=-1, keepdims=True)
    e = jnp.exp(t2 - t1)
    den = 1.0 + e
    route_ref[...] = jnp.where(lane == 0.0, i1, jnp.where(lane == 1.0, i2,
                               jnp.where(lane == 2.0, 1.0 / den, jnp.where(lane == 3.0, e / den, 0.0))))


def _conv_out(x, b, z, mod_l, cw, w_out, ln, rw, rb):
    tok = pl.BlockSpec((TM, D), lambda i: (i, 0))
    sub = TM // 8
    return pl.pallas_call(
        _conv_out_kernel,
        grid=(T // TM,),
        in_specs=[tok, tok, tok,
                  pl.BlockSpec((8, D), lambda i: (jnp.maximum(i * sub - 1, 0), 0)),
                  pl.BlockSpec((8, D), lambda i: (jnp.minimum((i + 1) * sub, T // 8 - 1), 0)),
                  pl.BlockSpec((None, 8, D), lambda i: (_group_of_tile(i, TM), 0, 0)),
                  pl.BlockSpec((8, D), lambda i: (0, 0)), pl.BlockSpec((D, D), lambda i: (0, 0)),
                  pl.BlockSpec((8, D), lambda i: (0, 0)),
                  pl.BlockSpec((D, LANES), lambda i: (0, 0)), pl.BlockSpec((1, LANES), lambda i: (0, 0))],
        out_specs=[tok, tok, pl.BlockSpec((TM, LANES), lambda i: (i, 0))],
        out_shape=[jax.ShapeDtypeStruct((T, D), F32), jax.ShapeDtypeStruct((T, D), BF16),
                   jax.ShapeDtypeStruct((T, LANES), F32)],
        compiler_params=_cparams(("parallel",)),
        name="conv_out",
    )(x, b, z, z, z, mod_l, cw, w_out, ln, rw, rb)


def _moe_kernel(te_ref, nv_ref, x_ref, wg_ref, wu_ref, wd_ref, o_ref, acc_scr):
    i, f = pl.program_id(0), pl.program_id(1)
    valid = i < nv_ref[0]

    @pl.when(f == 0)
    def _():
        acc_scr[...] = jnp.zeros_like(acc_scr)

    @pl.when(valid)
    def _():
        h = x_ref[...]
        g = _bdot(h, wg_ref[...])
        u = _bdot(h, wu_ref[...])
        acc_scr[...] += _bdot((g * _sigmoid(g) * u).astype(BF16), wd_ref[...])

    @pl.when(f == pl.num_programs(1) - 1)
    def _():
        o_ref[...] = acc_scr[...]


def _moe(tile_expert, n_valid, xs, wg, wu, wd):
    def wcol(i, f, te, nv):
        return (te[i], 0, jnp.where(i < nv[0], f, D_FF // TF - 1))

    def wrow(i, f, te, nv):
        return (te[i], jnp.where(i < nv[0], f, D_FF // TF - 1), 0)

    tok = pl.BlockSpec((TM_FF, D), lambda i, f, te, nv: (i, 0))
    return pl.pallas_call(
        _moe_kernel,
        grid_spec=pltpu.PrefetchScalarGridSpec(
            num_scalar_prefetch=2, grid=(NP_ROWS // TM_FF, D_FF // TF),
            in_specs=[tok, pl.BlockSpec((None, D, TF), wcol), pl.BlockSpec((None, D, TF), wcol),
                      pl.BlockSpec((None, TF, D), wrow)],
            out_specs=tok,
            scratch_shapes=[pltpu.VMEM((TM_FF, D), F32)]),
        out_shape=jax.ShapeDtypeStruct((NP_ROWS, D), F32),
        compiler_params=_cparams(("arbitrary", "arbitrary")),
        name="moe",
    )(tile_expert, n_valid, xs, wg, wu, wd)


def _combine_kernel(x_ref, o0_ref, o1_ref, route_ref, mod_ref, ln_ref, o_ref):
    y = route_ref[:, 2:3] * o0_ref[...] + route_ref[:, 3:4] * o1_ref[...]
    o_ref[...] = _layer_norm(ALPHA * x_ref[...] + mod_ref[5:6, :] * y, ln_ref[2:3, :], ln_ref[3:4, :])


def _combine(x, o0, o1, route, mod_l, ln):
    tok = pl.BlockSpec((TM, D), lambda i: (i, 0))
    return pl.pallas_call(
        _combine_kernel,
        grid=(T // TM,),
        in_specs=[tok, tok, tok, pl.BlockSpec((TM, LANES), lambda i: (i, 0)),
                  pl.BlockSpec((None, 8, D), lambda i: (_group_of_tile(i, TM), 0, 0)),
                  pl.BlockSpec((8, D), lambda i: (0, 0))],
        out_specs=tok,
        out_shape=jax.ShapeDtypeStruct((T, D), F32),
        compiler_params=_cparams(("parallel",)),
        name="moe_combine",
    )(x, o0, o1, route, mod_l, ln)


def _routing_plan(route):
    e = route[:, 0:2].astype(jnp.int32).reshape(-1)
    onehot = (e[:, None] == jnp.arange(N_EXP, dtype=jnp.int32)[None, :]).astype(jnp.int32)
    csum = jnp.cumsum(onehot, axis=0)
    counts = csum[-1]
    rank = jnp.sum((csum - onehot) * onehot, axis=1)
    padded = (counts + TM_FF - 1) // TM_FF * TM_FF
    pend = jnp.cumsum(padded)
    dest = (pend - padded)[e] + rank
    row_token = jnp.zeros((NP_ROWS,), jnp.int32).at[dest].set(jnp.arange(2 * T, dtype=jnp.int32) // 2)
    n_valid = (pend[-1] // TM_FF).astype(jnp.int32)
    tile_start = jnp.minimum(jnp.arange(NP_ROWS // TM_FF, dtype=jnp.int32), n_valid - 1) * TM_FF
    tile_expert = jnp.minimum(jnp.sum((tile_start[:, None] >= pend[None, :]).astype(jnp.int32), axis=1), N_EXP - 1)
    return dest.reshape(T, 2), row_token, tile_expert.astype(jnp.int32), n_valid.reshape(1)


_PERM_ROPE = np.concatenate([np.arange(0, ROPE, 2), np.arange(1, ROPE, 2)])
_PERM_RDK = np.concatenate([np.arange(0, RDK, 2), np.arange(1, RDK, 2)])
_INV_PERM_RDK = np.argsort(_PERM_RDK)


def _pad_cols(a, width):
    return jnp.pad(a, ((0, 0), (0, width - a.shape[1])))


def _prep_even_weights(w_in, w_q_b, w_kv_b):
    o_kpe, o_rq, o_rk, o_rv = 640, 672, 1184, 1696
    head_perm = (np.arange(HEADS)[:, None] * RDK + _PERM_RDK[None, :]).reshape(-1)
    kpe = w_in[:, o_kpe:o_kpe + ROPE]
    w_in_p = jnp.concatenate([
        w_in[:, :640], w_in[:, o_rq + head_perm], w_in[:, o_rk + head_perm] * (RDK ** -0.5), w_in[:, o_rv:],
        _pad_cols(kpe, LANES), _pad_cols(kpe[:, _PERM_ROPE], LANES),
        _pad_cols(jnp.concatenate([-kpe[:, 1::2], kpe[:, 0::2]], axis=1), LANES)], axis=1).astype(BF16)
    wq = w_q_b.reshape(Q_LORA, HEADS, NOPE + ROPE)
    wq = jnp.concatenate([wq[:, :, :NOPE], wq[:, :, NOPE + _PERM_ROPE],
                          jnp.zeros((Q_LORA, HEADS, LANES - NOPE - ROPE), F32)], axis=2)
    wq = wq.reshape(Q_LORA, HEADS * LANES).astype(BF16)
    wkv = w_kv_b.reshape(KV_LORA, HEADS, NOPE + VDIM)
    zero = jnp.zeros((KV_LORA, HEADS, LANES - NOPE), F32)
    wk = jnp.concatenate([wkv[:, :, :NOPE], zero], axis=2).reshape(KV_LORA, HEADS * LANES)
    wv = wkv[:, :, NOPE:].reshape(KV_LORA, HEADS // 2, 2, VDIM)
    zv = jnp.zeros((KV_LORA, HEADS // 2, VDIM), F32)
    wv = jnp.stack([jnp.concatenate([wv[:, :, 0], zv], axis=2), jnp.concatenate([zv, wv[:, :, 1]], axis=2)], axis=2)
    wkv_p = jnp.concatenate([wk, wv.reshape(KV_LORA, HEADS * LANES)], axis=1).astype(BF16)
    return w_in_p, wq, wkv_p


def _placement():
    ek = np.zeros((LANES, HEADS * LANES), np.float32)
    for h in range(HEADS):
        ek[np.arange(ROPE), h * LANES + NOPE + np.arange(ROPE)] = 1.0
    return jnp.asarray(ek, BF16)


def _rotary_tables():
    rows = LAT_LEN // GRID_W
    r, col = jnp.meshgrid(jnp.arange(rows, dtype=F32), jnp.arange(GRID_W, dtype=F32), indexing='ij')
    n_freq = ROPE // 4
    freqs = 1.0 / (10000.0 ** (jnp.arange(n_freq, dtype=F32) / n_freq))
    ang = jnp.concatenate([r.reshape(-1)[:, None] * freqs, col.reshape(-1)[:, None] * freqs], axis=-1)
    cos, sin = jnp.cos(ang), jnp.sin(ang)
    theta = 1.0 / (10000.0 ** jnp.linspace(0.0, 1.0, RDK // 2, dtype=F32))
    rang = jnp.arange(LAT_LEN, dtype=F32)[:, None] * theta
    rcos, rsin = jnp.cos(rang), jnp.sin(rang)
    one = lambda w: jnp.ones((LAT_LEN, w), F32)
    zero = lambda w: jnp.zeros((LAT_LEN, w), F32)
    lat = [jnp.concatenate([one(NOPE), cos, cos, one(LANES - NOPE - ROPE)], axis=1),
           jnp.concatenate([zero(NOPE), sin, sin, zero(LANES - NOPE - ROPE)], axis=1),
           jnp.concatenate([cos, cos, zero(LANES - ROPE)], axis=1),
           jnp.concatenate([sin, sin, zero(LANES - ROPE)], axis=1),
           jnp.concatenate([rcos] * 4, axis=1), jnp.concatenate([rsin] * 4, axis=1)]
    ctx_ck = jnp.concatenate([jnp.ones((TP, ROPE), F32), jnp.zeros((TP, LANES - ROPE), F32)], axis=1)
    ctx = [jnp.ones((TP, LANES), F32), jnp.zeros((TP, LANES), F32), ctx_ck, jnp.zeros((TP, LANES), F32),
           jnp.ones((TP, LANES), F32), jnp.zeros((TP, LANES), F32)]
    return [jnp.concatenate([c] + [l] * N_LAT_SEQ, axis=0) for c, l in zip(ctx, lat)]


def _block_diag_states(s0):
    s = s0[:, :, _PERM_RDK, :].reshape(s0.shape[0], HEADS // 2, 2, RDK, RDK)
    z = jnp.zeros_like(s[:, :, 0])
    top = jnp.concatenate([s[:, :, 0], z], axis=-1)
    bot = jnp.concatenate([z, s[:, :, 1]], axis=-1)
    return jnp.concatenate([top, bot], axis=-2).astype(BF16)


def _diag_states(g):
    g = g.reshape(g.shape[0], HEADS // 2, 2, RDK, 2, RDK)
    s = jnp.stack([g[:, :, 0, :, 0, :], g[:, :, 1, :, 1, :]], axis=2).reshape(g.shape[0], HEADS, RDK, RDK)
    return s[:, :, _INV_PERM_RDK, :]


def kernel(x_prompt, x_sample, c, cache_ckv, cache_kpe, state_ret_fwd, state_ret_bwd, c_ctx, w_mod, b_mod, ln_g, ln_b, w_in_mix, q_a_gain, kv_a_gain, w_q_b, w_kv_b, ret_decay_fwd, ret_decay_bwd, w_out_mix, w_in_conv, conv_w, w_out_conv, ffn_gate, ffn_up, ffn_down, router_w, router_b, exp_gate, exp_up, exp_down):
    x = jnp.concatenate([x_prompt.reshape(TP, D), x_sample.reshape(TS, D)], axis=0)
    cond8 = jnp.concatenate([c_ctx[None], c, jnp.zeros((8 - 1 - N_LAT_SEQ, D), F32)], axis=0)
    mods = _modulation(cond8, w_mod, b_mod)
    mods = jnp.pad(mods.reshape(DEPTH, N_GROUPS, 6, D), ((0, 0), (0, 0), (0, 2), (0, 0)))
    ln = jnp.pad(jnp.concatenate([ln_g, ln_b], axis=1)[:, jnp.array([0, 2, 1, 3])], ((0, 0), (0, 4), (0, 0)))
    tabs = _rotary_tables()
    ek = _placement()
    new_ckv, new_kpe, new_sf, new_sb = [], [], [], []
    for layer in range(DEPTH):
        j = layer // 2
        mod_l, ln_l = mods[layer], ln[layer]
        if layer % 2 == 0:
            w_in_p, wq, wkv = _prep_even_weights(w_in_mix[j], w_q_b[j], w_kv_b[j])
            q, k, v, ckv, kpe, rq, rk, rv, rg = _even_in(
                x, mod_l, w_in_p, q_a_gain[j][None], kv_a_gain[j][None], wq, wkv, ek, tabs)
            kpe_c = _pad_cols(cache_kpe[:, j][..., _PERM_ROPE].reshape(N_LAT_SEQ * PAST, ROPE), LANES)
            kc, vc = _ctx_kv(cache_ckv[:, j].reshape(N_LAT_SEQ * PAST, KV_LORA), kpe_c, wkv, ek)
            attn = _attention(q, k, v, kc, vc)
            lg = jnp.concatenate([jax.nn.log_sigmoid(ret_decay_fwd[j].astype(F32)),
                                  jax.nn.log_sigmoid(ret_decay_bwd[j].astype(F32))])
            ret, sf, sb = _retention(lg, rq, rk, rv, rg, _block_diag_states(state_ret_fwd[:, j]),
                                     _block_diag_states(state_ret_bwd[:, j]))
            x = _mix_out(x, attn, ret, mod_l, w_out_mix[j].astype(BF16), ln_l)
            x = _ffn(x, mod_l, ffn_gate[j].astype(BF16), ffn_up[j].astype(BF16), ffn_down[j].astype(BF16), ln_l)
            new_ckv.append(ckv[:TP].reshape(N_PROMPT_SEQ, PROMPT_LEN, KV_LORA))
            new_kpe.append(kpe[:TP, :ROPE].reshape(N_PROMPT_SEQ, PROMPT_LEN, ROPE))
            new_sf.append(_diag_states(sf))
            new_sb.append(_diag_states(sb))
        else:
            b, z = _conv_in(x, mod_l, w_in_conv[j].astype(BF16))
            cw = jnp.pad(conv_w[j], ((0, 5), (0, 0)))
            rw = _pad_cols(router_w[j], LANES)
            rb = jnp.concatenate([router_b[j].astype(F32), jnp.full((LANES - N_EXP,), -1e30, F32)])[None]
            x, h, route = _conv_out(x, b, z, mod_l, cw, w_out_conv[j].astype(BF16), ln_l, rw, rb)
            dest, row_token, tile_expert, n_valid = _routing_plan(route)
            out_sorted = _moe(tile_expert, n_valid, h[row_token], exp_gate[j].astype(BF16),
                              exp_up[j].astype(BF16), exp_down[j].astype(BF16))
            x = _combine(x, out_sorted[dest[:, 0]], out_sorted[dest[:, 1]], route, mod_l, ln_l)
    y_prompt = x[:TP].reshape(N_PROMPT_SEQ, PROMPT_LEN, D)
    y_sample = x[TP:].reshape(N_LAT_SEQ, LAT_LEN, D)
    return (y_prompt, y_sample, jnp.stack(new_ckv, axis=1), jnp.stack(new_kpe, axis=1),
            jnp.stack(new_sf, axis=1), jnp.stack(new_sb, axis=1))
```

```python
import functools

import numpy as np
import jax
import jax.numpy as jnp
from jax import lax
from jax.experimental import pallas as pl
from jax.experimental.pallas import tpu as pltpu

F32 = jnp.float32
BF16 = jnp.bfloat16

D = 1024
DEPTH = 4
N_PROMPT_SEQ, PROMPT_LEN = 32, 256
N_LAT_SEQ, LAT_LEN = 2, 2048
PAST = 512
GRID_W = 64
TP = N_PROMPT_SEQ * PROMPT_LEN
TS = N_LAT_SEQ * LAT_LEN
T = TP + TS
HEADS = 8
NOPE, ROPE, VDIM = 64, 32, 64
Q_LORA, KV_LORA = 384, 256
RDK = 64
D_FF = 2816
N_EXP = 8
ALPHA = (2.0 * DEPTH) ** 0.25
Q_SCALE = float((NOPE + ROPE) ** -0.5)
LANES = 128
N_GROUPS = 8

TM = 256
TM_FF = 512
TF = D_FF // 2
TQ = 256
NP_ROWS = 2 * T + N_EXP * TM_FF
VMEM_LIMIT = 56 * 1024 * 1024

IN_COLS = 3072


def _cparams(sem):
    return pltpu.CompilerParams(dimension_semantics=sem, vmem_limit_bytes=VMEM_LIMIT)


def _group_of_tile(i, tm):
    per_seq = LAT_LEN // tm
    return jnp.maximum(i - TP // tm + per_seq, 0) // per_seq


def _bdot(a, b):
    return jnp.dot(a, b, preferred_element_type=F32)


def _sigmoid(v):
    return 1.0 / (1.0 + jnp.exp(-v))


def _layer_norm(v, g, b):
    mu = jnp.mean(v, axis=-1, keepdims=True)
    d = v - mu
    var = jnp.mean(d * d, axis=-1, keepdims=True)
    return d * lax.rsqrt(var + 1e-5) * g + b


def _rms(v, g):
    return v * lax.rsqrt(jnp.mean(v * v, axis=-1, keepdims=True) + 1e-6) * g


def _mod_kernel(c_ref, w_ref, b_ref, o_ref):
    c = c_ref[...]
    s = (c * _sigmoid(c)).astype(BF16)
    o_ref[...] = _bdot(s, w_ref[...].astype(BF16)) + b_ref[...]


def _modulation(cond8, w_mod, b_mod):
    tn = 1536
    return pl.pallas_call(
        _mod_kernel,
        grid=(DEPTH, 6 * D // tn),
        in_specs=[pl.BlockSpec((8, D), lambda l, n: (0, 0)),
                  pl.BlockSpec((None, D, tn), lambda l, n: (l, 0, n)),
                  pl.BlockSpec((None, 1, tn), lambda l, n: (l, 0, n))],
        out_specs=pl.BlockSpec((None, 8, tn), lambda l, n: (l, 0, n)),
        out_shape=jax.ShapeDtypeStruct((DEPTH, 8, 6 * D), F32),
        compiler_params=_cparams(("arbitrary", "arbitrary")),
        name="modulation",
    )(cond8, w_mod, b_mod.reshape(DEPTH, 1, 6 * D))


def _swap_halves(a, half):
    n = a.shape[-1]
    lane = lax.broadcasted_iota(jnp.int32, a.shape, 1)
    first = (lane & (2 * half - 1)) < half
    return jnp.where(first, -pltpu.roll(a, n - half, axis=1), pltpu.roll(a, half, axis=1))


def _even_in_kernel(x_ref, mod_ref, w_in_ref, qg_ref, kvg_ref, wq_ref, wkv_ref, ek_ref,
                    cq_ref, sq_ref, ck_ref, sk_ref, cr_ref, sr_ref,
                    q_ref, k_ref, v_ref, ckv_ref, kpe_ref, rq_ref, rk_ref, rv_ref, rg_ref):
    x = x_ref[...]
    h = (x * (1.0 + mod_ref[1:2, :]) + mod_ref[0:1, :]).astype(BF16)
    p = _bdot(h, w_in_ref[...])
    qn = _rms(p[:, 0:Q_LORA], qg_ref[...]).astype(BF16)
    qa = _bdot(qn, wq_ref[...])
    ckv = _rms(p[:, Q_LORA:Q_LORA + KV_LORA], kvg_ref[...])
    ckv_ref[...] = ckv
    kv = _bdot(ckv.astype(BF16), wkv_ref[...])
    v_ref[...] = kv[:, HEADS * LANES:].astype(BF16)
    base = 2688
    kpe_ref[...] = p[:, base:base + LANES]
    ka = p[:, base + LANES:base + 2 * LANES]
    kb = p[:, base + 2 * LANES:base + 3 * LANES]
    rq = p[:, 640:1152]
    rk = p[:, 1152:1664]
    rv_ref[...] = p[:, 1664:2176].astype(BF16)
    rg_ref[...] = p[:, 2176:2688]
    is_latent = pl.program_id(0) >= TP // TM

    @pl.when(is_latent)
    def _():
        lane = lax.broadcasted_iota(jnp.int32, qa.shape, 1) & (LANES - 1)
        qb = jnp.where(lane < NOPE + ROPE // 2,
                       -pltpu.roll(qa, qa.shape[1] - ROPE // 2, axis=1),
                       pltpu.roll(qa, ROPE // 2, axis=1))
        cq = jnp.concatenate([cq_ref[...]] * HEADS, axis=1)
        sq = jnp.concatenate([sq_ref[...]] * HEADS, axis=1)
        q_ref[...] = ((qa * cq + qb * sq) * Q_SCALE).astype(BF16)
        kpe_rot = ka * ck_ref[...] + kb * sk_ref[...]
        k_ref[...] = (kv[:, :HEADS * LANES] + _bdot(kpe_rot.astype(BF16), ek_ref[...])).astype(BF16)
        cr = jnp.concatenate([cr_ref[...]] * 4, axis=1)
        sr = jnp.concatenate([sr_ref[...]] * 4, axis=1)
        rq_ref[...] = (rq * cr + _swap_halves(rq, RDK // 2) * sr).astype(BF16)
        rk_ref[...] = (rk * cr + _swap_halves(rk, RDK // 2) * sr).astype(BF16)

    @pl.when(jnp.logical_not(is_latent))
    def _():
        q_ref[...] = (qa * Q_SCALE).astype(BF16)
        k_ref[...] = (kv[:, :HEADS * LANES] + _bdot(ka.astype(BF16), ek_ref[...])).astype(BF16)
        rq_ref[...] = rq.astype(BF16)
        rk_ref[...] = rk.astype(BF16)


def _even_in(x, mod_l, w_in, qg, kvg, wq, wkv, ek, tabs):
    tok = lambda w: pl.BlockSpec((TM, w), lambda i: (i, 0))
    full = lambda a: pl.BlockSpec(a.shape, lambda i: (0,) * a.ndim)
    tab = pl.BlockSpec((TM, LANES), lambda i: (jnp.maximum(i - TP // TM, 0) % (LAT_LEN // TM), 0))
    outs = [(HEADS * LANES, BF16), (HEADS * LANES, BF16), (HEADS * LANES, BF16), (KV_LORA, F32), (LANES, F32),
            (512, BF16), (512, BF16), (512, BF16), (512, F32)]
    return pl.pallas_call(
        _even_in_kernel,
        grid=(T // TM,),
        in_specs=[tok(D), pl.BlockSpec((None, 8, D), lambda i: (_group_of_tile(i, TM), 0, 0)),
                  full(w_in), full(qg), full(kvg), full(wq), full(wkv), full(ek)] + [tab] * 6,
        out_specs=[tok(w) for w, _ in outs],
        out_shape=[jax.ShapeDtypeStruct((T, w), dt) for w, dt in outs],
        compiler_params=_cparams(("parallel",)),
        name="even_in",
    )(x, mod_l, w_in, qg, kvg, wq, wkv, ek, *tabs)


def _ctx_kv_kernel(ckv_ref, kpe_ref, wkv_ref, ek_ref, k_ref, v_ref):
    kv = _bdot(ckv_ref[...].astype(BF16), wkv_ref[...])
    k_ref[...] = (kv[:, :HEADS * LANES] + _bdot(kpe_ref[...].astype(BF16), ek_ref[...])).astype(BF16)
    v_ref[...] = kv[:, HEADS * LANES:].astype(BF16)


def _ctx_kv(ckv_c, kpe_c, wkv, ek):
    n = ckv_c.shape[0]
    full = lambda a: pl.BlockSpec(a.shape, lambda i: (0,) * a.ndim)
    return pl.pallas_call(
        _ctx_kv_kernel,
        grid=(n // PAST,),
        in_specs=[pl.BlockSpec((PAST, KV_LORA), lambda i: (i, 0)), pl.BlockSpec((PAST, LANES), lambda i: (i, 0)),
                  full(wkv), full(ek)],
        out_specs=[pl.BlockSpec((PAST, HEADS * LANES), lambda i: (i, 0))] * 2,
        out_shape=[jax.ShapeDtypeStruct((n, HEADS * LANES), BF16)] * 2,
        compiler_params=_cparams(("parallel",)),
        name="ctx_kv",
    )(ckv_c, kpe_c, wkv, ek)


def _attn_kernel(n_kv, q_ref, *refs):
    k_refs = refs[0:2 * n_kv:2]
    v_refs = refs[1:2 * n_kv:2]
    o_ref = refs[-1]
    nt = (((1,), (1,)), ((), ()))
    for pair in range(HEADS // 2):
        acc = None
        for sub in range(2):
            sl = slice((2 * pair + sub) * LANES, (2 * pair + sub + 1) * LANES)
            qh = q_ref[:, sl]
            s = [lax.dot_general(qh, k[:, sl], nt, preferred_element_type=F32) for k in k_refs]
            m = functools.reduce(jnp.maximum, [jnp.max(a, axis=-1, keepdims=True) for a in s])
            e = [jnp.exp(a - m) for a in s]
            den = functools.reduce(jnp.add, [jnp.sum(a, axis=-1, keepdims=True) for a in e])
            o = functools.reduce(jnp.add, [_bdot(a.astype(BF16), v[:, sl]) for a, v in zip(e, v_refs)])
            o = o / den
            acc = o if acc is None else acc + o
        o_ref[:, pair * LANES:(pair + 1) * LANES] = acc.astype(BF16)


def _attention(q, k, v, kc, vc):
    w = HEADS * LANES
    out_shape = jax.ShapeDtypeStruct((T, HEADS * VDIM), BF16)
    blk = lambda i: (i, 0)
    attn = pl.pallas_call(
        functools.partial(_attn_kernel, 1),
        grid=(N_PROMPT_SEQ,),
        in_specs=[pl.BlockSpec((PROMPT_LEN, w), blk)] * 3,
        out_specs=pl.BlockSpec((PROMPT_LEN, HEADS * VDIM), blk),
        out_shape=out_shape,
        compiler_params=_cparams(("parallel",)),
        name="attn_context",
    )(q, k, v)
    qpb = LAT_LEN // TQ
    qmap = lambda b, i: (TP // TQ + b * qpb + i, 0)
    own = lambda b, i: (TP // LAT_LEN + b, 0)
    ctx = lambda b, i: (b, 0)
    return pl.pallas_call(
        functools.partial(_attn_kernel, 2),
        grid=(N_LAT_SEQ, qpb),
        in_specs=[pl.BlockSpec((TQ, w), qmap),
                  pl.BlockSpec((PAST, w), ctx), pl.BlockSpec((PAST, w), ctx),
                  pl.BlockSpec((LAT_LEN, w), own), pl.BlockSpec((LAT_LEN, w), own),
                  pl.BlockSpec(memory_space=pl.ANY)],
        out_specs=pl.BlockSpec((TQ, HEADS * VDIM), qmap),
        out_shape=out_shape,
        input_output_aliases={5: 0},
        compiler_params=_cparams(("parallel", "arbitrary")),
        name="attn_latent",
    )(q, kc, vc, k, v, attn)


def _retention_kernel(latent, lg_ref, rq_ref, rk_ref, rv_ref, rg_ref, *refs):
    if latent:
        s0f_ref, s0b_ref, _, o_ref = refs
        q0 = pl.program_id(1) * TQ
        seq_len = LAT_LEN
    else:
        unperm_ref = refs[0]
        o_ref, sf_ref, sb_ref = refs[-3:]
        q0 = 0
        seq_len = PROMPT_LEN
    tq, tk = rq_ref.shape[0], rk_ref.shape[0]
    nt = (((1,), (1,)), ((), ()))
    n_idx = (q0 + lax.broadcasted_iota(jnp.int32, (tq, tk), 0)).astype(F32)
    m_idx = lax.broadcasted_iota(jnp.int32, (tq, tk), 1).astype(F32)
    dist = n_idx - m_idx
    adist = jnp.abs(dist)
    fwd = dist > 0.0
    diag = jnp.where(dist == 0.0, 1.0, 0.0)
    lane = lax.broadcasted_iota(jnp.int32, (1, LANES), 1)
    lo = lane < RDK
    n_col = (q0 + lax.broadcasted_iota(jnp.int32, (tq, 1), 0)).astype(F32)
    m_col = lax.broadcasted_iota(jnp.int32, (tk, 1), 0).astype(F32)
    for pair in range(HEADS // 2):
        sl = slice(pair * LANES, (pair + 1) * LANES)
        qb, kb, vb = rq_ref[:, sl], rk_ref[:, sl], rv_ref[:, sl]
        acc = jnp.zeros((tq, LANES), F32)
        for sub in range(2):
            h = 2 * pair + sub
            lgf, lgb = lg_ref[h], lg_ref[HEADS + h]
            half = lo if sub == 0 else jnp.logical_not(lo)
            qm = jnp.where(half, qb, jnp.zeros_like(qb))
            vm = jnp.where(half, vb, jnp.zeros_like(vb))
            s = lax.dot_general(qm, kb, nt, preferred_element_type=F32)
            w = jnp.exp(adist * jnp.where(fwd, lgf, lgb)) + diag
            acc = acc + _bdot((s * w).astype(BF16), vm)
        lgf_l = jnp.where(lo, lg_ref[2 * pair], lg_ref[2 * pair + 1])
        lgb_l = jnp.where(lo, lg_ref[HEADS + 2 * pair], lg_ref[HEADS + 2 * pair + 1])
        if latent:
            acc = acc + _bdot(qb, s0f_ref[pair]) * jnp.exp((n_col + 1.0) * lgf_l)
            acc = acc + _bdot(qb, s0b_ref[pair]) * jnp.exp((seq_len - n_col) * lgb_l)
        else:
            v_swapped = pltpu.roll(vb.astype(F32), RDK, axis=1).astype(BF16)
            for st_ref, dec in ((sf_ref, jnp.exp((seq_len - 1.0 - m_col) * lgf_l)),
                                (sb_ref, jnp.exp(m_col * lgb_l))):
                kt = (kb.astype(F32) * dec).T.astype(BF16)
                kt = _bdot(unperm_ref[...], kt).astype(BF16)
                st_ref[2 * pair] = _bdot(kt, vb)[0:RDK, 0:RDK]
                st_ref[2 * pair + 1] = _bdot(kt, v_swapped)[RDK:, 0:RDK]
        inv = 1.0 / RDK
        mu = jnp.where(lo, jnp.sum(jnp.where(lo, acc, 0.0), axis=-1, keepdims=True),
                       jnp.sum(jnp.where(lo, 0.0, acc), axis=-1, keepdims=True)) * inv
        dlt = acc - mu
        d2 = dlt * dlt
        var = jnp.where(lo, jnp.sum(jnp.where(lo, d2, 0.0), axis=-1, keepdims=True),
                        jnp.sum(jnp.where(lo, 0.0, d2), axis=-1, keepdims=True)) * inv
        g = rg_ref[:, sl]
        o_ref[:, sl] = (dlt * lax.rsqrt(var + 1e-5) * (g * _sigmoid(g))).astype(BF16)


def _retention(lg, rq, rk, rv, rg, s0f, s0b, unperm, j, prev_states):
    w = HEADS * RDK
    n_mix = DEPTH // 2
    out_shape = jax.ShapeDtypeStruct((T, w), BF16)
    st_shape = jax.ShapeDtypeStruct((N_PROMPT_SEQ, n_mix, HEADS, RDK, RDK), F32)
    blk = lambda i, lg: (i, 0)
    st_blk = pl.BlockSpec((None, None, HEADS, RDK, RDK), lambda i, lg: (i, j, 0, 0, 0))
    prev = () if prev_states is None else tuple(prev_states)
    ret, sf, sb = pl.pallas_call(
        functools.partial(_retention_kernel, False),
        grid_spec=pltpu.PrefetchScalarGridSpec(
            num_scalar_prefetch=1, grid=(N_PROMPT_SEQ,),
            in_specs=[pl.BlockSpec((PROMPT_LEN, w), blk)] * 4 + [pl.BlockSpec((LANES, LANES), lambda i, lg: (0, 0))]
                     + [pl.BlockSpec(memory_space=pl.ANY)] * len(prev),
            out_specs=[pl.BlockSpec((PROMPT_LEN, w), blk), st_blk, st_blk]),
        out_shape=[out_shape, st_shape, st_shape],
        input_output_aliases={6: 1, 7: 2} if prev else {},
        compiler_params=_cparams(("parallel",)),
        name="retention_context",
    )(lg, rq, rk, rv, rg, unperm, *prev)
    qpb = LAT_LEN // TQ
    qmap = lambda b, i, lg: (TP // TQ + b * qpb + i, 0)
    own = lambda b, i, lg: (TP // LAT_LEN + b, 0)
    s0_blk = pl.BlockSpec((None, HEADS // 2, LANES, LANES), lambda b, i, lg: (b, 0, 0, 0))
    ret = pl.pallas_call(
        functools.partial(_retention_kernel, True),
        grid_spec=pltpu.PrefetchScalarGridSpec(
            num_scalar_prefetch=1, grid=(N_LAT_SEQ, qpb),
            in_specs=[pl.BlockSpec((TQ, w), qmap), pl.BlockSpec((LAT_LEN, w), own), pl.BlockSpec((LAT_LEN, w), own),
                      pl.BlockSpec((TQ, w), qmap), s0_blk, s0_blk, pl.BlockSpec(memory_space=pl.ANY)],
            out_specs=pl.BlockSpec((TQ, w), qmap)),
        out_shape=out_shape,
        input_output_aliases={7: 0},
        compiler_params=_cparams(("parallel", "arbitrary")),
        name="retention_latent",
    )(lg, rq, rk, rv, rg, s0f, s0b, ret)
    return ret, sf, sb


def _mix_out_kernel(x_ref, a_ref, r_ref, mod_ref, w_ref, ln_ref, o_ref):
    half = HEADS * VDIM
    y = _bdot(a_ref[...], w_ref[0:half, :]) + _bdot(r_ref[...], w_ref[half:, :])
    o_ref[...] = _layer_norm(ALPHA * x_ref[...] + mod_ref[2:3, :] * y, ln_ref[0:1, :], ln_ref[1:2, :])


def _mix_out(x, attn, ret, mod_l, w_out, ln, j):
    tok = lambda w: pl.BlockSpec((TM, w), lambda i: (i, 0))
    return pl.pallas_call(
        _mix_out_kernel,
        grid=(T // TM,),
        in_specs=[tok(D), tok(512), tok(512), pl.BlockSpec((None, 8, D), lambda i: (_group_of_tile(i, TM), 0, 0)),
                  pl.BlockSpec((None,) + w_out.shape[1:], lambda i: (j, 0, 0)), pl.BlockSpec((8, D), lambda i: (0, 0))],
        out_specs=tok(D),
        out_shape=jax.ShapeDtypeStruct((T, D), F32),
        compiler_params=_cparams(("parallel",)),
        name="mix_out",
    )(x, attn, ret, mod_l, w_out, ln)


def _ffn_kernel(x_ref, mod_ref, wg_ref, wu_ref, wd_ref, ln_ref, o_ref, h_scr, acc_scr):
    f = pl.program_id(1)

    @pl.when(f == 0)
    def _():
        h_scr[...] = (x_ref[...] * (1.0 + mod_ref[4:5, :]) + mod_ref[3:4, :]).astype(BF16)
        acc_scr[...] = jnp.zeros_like(acc_scr)

    h = h_scr[...]
    g = _bdot(h, wg_ref[...])
    u = _bdot(h, wu_ref[...])
    acc_scr[...] += _bdot((g * _sigmoid(g) * u).astype(BF16), wd_ref[...])

    @pl.when(f == pl.num_programs(1) - 1)
    def _():
        o_ref[...] = _layer_norm(ALPHA * x_ref[...] + mod_ref[5:6, :] * acc_scr[...], ln_ref[2:3, :], ln_ref[3:4, :])


def _ffn(x, mod_l, wg, wu, wd, ln, j):
    tok = pl.BlockSpec((TM_FF, D), lambda i, f: (i, 0))
    return pl.pallas_call(
        _ffn_kernel,
        grid=(T // TM_FF, D_FF // TF),
        in_specs=[tok, pl.BlockSpec((None, 8, D), lambda i, f: (_group_of_tile(i, TM_FF), 0, 0)),
                  pl.BlockSpec((None, D, TF), lambda i, f: (j, 0, f)),
                  pl.BlockSpec((None, D, TF), lambda i, f: (j, 0, f)),
                  pl.BlockSpec((None, TF, D), lambda i, f: (j, f, 0)), pl.BlockSpec((8, D), lambda i, f: (0, 0))],
        out_specs=tok,
        out_shape=jax.ShapeDtypeStruct((T, D), F32),
        scratch_shapes=[pltpu.VMEM((TM_FF, D), BF16), pltpu.VMEM((TM_FF, D), F32)],
        compiler_params=_cparams(("parallel", "arbitrary")),
        name="ffn",
    )(x, mod_l, wg, wu, wd, ln)


def _conv_in_kernel(x_ref, mod_ref, w_ref, b_ref, z_ref):
    h = (x_ref[...] * (1.0 + mod_ref[1:2, :]) + mod_ref[0:1, :]).astype(BF16)
    p = _bdot(h, w_ref[...])
    b_ref[...] = p[:, 0:D]
    z_ref[...] = p[:, D:2 * D] * p[:, 2 * D:3 * D]


def _conv_in(x, mod_l, w_in, j):
    tok = pl.BlockSpec((TM, D), lambda i: (i, 0))
    return pl.pallas_call(
        _conv_in_kernel,
        grid=(T // TM,),
        in_specs=[tok, pl.BlockSpec((None, 8, D), lambda i: (_group_of_tile(i, TM), 0, 0)),
                  pl.BlockSpec((None,) + w_in.shape[1:], lambda i: (j, 0, 0))],
        out_specs=[tok, tok],
        out_shape=[jax.ShapeDtypeStruct((T, D), F32)] * 2,
        compiler_params=_cparams(("parallel",)),
        name="conv_in",
    )(x, mod_l, w_in)


def _conv_out_kernel(x_ref, b_ref, z_ref, zp_ref, zn_ref, mod_ref, cw_ref, w_ref, ln_ref, rw_ref, rb_ref,
                     o_ref, h_ref, route_ref):
    i = pl.program_id(0)
    z = z_ref[...]
    row = lax.broadcasted_iota(jnp.int32, (TM, 1), 0)
    seq_len = jnp.where(i < TP // TM, PROMPT_LEN, LAT_LEN)
    pos = (i * TM + row) & (seq_len - 1)
    prev = jnp.where(row == 0, zp_ref[7:8, :], pltpu.roll(z, 1, axis=0))
    prev = jnp.where(pos == 0, 0.0, prev)
    nxt = jnp.where(row == TM - 1, zn_ref[0:1, :], pltpu.roll(z, TM - 1, axis=0))
    nxt = jnp.where(pos == seq_len - 1, 0.0, nxt)
    y = prev * cw_ref[0:1, :] + z * cw_ref[1:2, :] + nxt * cw_ref[2:3, :]
    t = _bdot((b_ref[...] * y).astype(BF16), w_ref[...])
    x1 = _layer_norm(ALPHA * x_ref[...] + mod_ref[2:3, :] * t, ln_ref[0:1, :], ln_ref[1:2, :])
    o_ref[...] = x1
    h = x1 * (1.0 + mod_ref[4:5, :]) + mod_ref[3:4, :]
    h_ref[...] = h
    logits = jnp.dot(h, rw_ref[...], precision=lax.Precision.HIGHEST, preferred_element_type=F32) + rb_ref[...]
    lane = lax.broadcasted_iota(jnp.int32, logits.shape, 1).astype(F32)
    t1 = jnp.max(logits, axis=-1, keepdims=True)
    i1 = jnp.min(jnp.where(logits == t1, lane, float(LANES)), axis=-1, keepdims=True)
    rest = jnp.where(lane == i1, -jnp.inf, logits)
    t2 = jnp.max(rest, axis=-1, keepdims=True)
    i2 = jnp.min(jnp.where(rest == t2, lane, float(LANES)), axis=-1, keepdims=True)
    e = jnp.exp(t2 - t1)
    den = 1.0 + e
    route_ref[...] = jnp.where(lane == 0.0, i1, jnp.where(lane == 1.0, i2,
                               jnp.where(lane == 2.0, 1.0 / den, jnp.where(lane == 3.0, e / den, 0.0))))


def _conv_out(x, b, z, mod_l, cw, w_out, ln, rw, rb, j):
    tok = pl.BlockSpec((TM, D), lambda i: (i, 0))
    sub = TM // 8
    return pl.pallas_call(
        _conv_out_kernel,
        grid=(T // TM,),
        in_specs=[tok, tok, tok,
                  pl.BlockSpec((8, D), lambda i: (jnp.maximum(i * sub - 1, 0), 0)),
                  pl.BlockSpec((8, D), lambda i: (jnp.minimum((i + 1) * sub, T // 8 - 1), 0)),
                  pl.BlockSpec((None, 8, D), lambda i: (_group_of_tile(i, TM), 0, 0)),
                  pl.BlockSpec((8, D), lambda i: (0, 0)), pl.BlockSpec((None, D, D), lambda i: (j, 0, 0)),
                  pl.BlockSpec((8, D), lambda i: (0, 0)),
                  pl.BlockSpec((D, LANES), lambda i: (0, 0)), pl.BlockSpec((1, LANES), lambda i: (0, 0))],
        out_specs=[tok, tok, pl.BlockSpec((TM, LANES), lambda i: (i, 0))],
        out_shape=[jax.ShapeDtypeStruct((T, D), F32), jax.ShapeDtypeStruct((T, D), F32),
                   jax.ShapeDtypeStruct((T, LANES), F32)],
        compiler_params=_cparams(("parallel",)),
        name="conv_out",
    )(x, b, z, z, z, mod_l, cw, w_out, ln, rw, rb)


def _moe_kernel(te_ref, nv_ref, x_ref, wg_ref, wu_ref, wd_ref, o_ref, h_scr, acc_scr):
    i, f = pl.program_id(0), pl.program_id(1)
    valid = i < nv_ref[0]

    @pl.when(f == 0)
    def _():
        acc_scr[...] = jnp.zeros_like(acc_scr)
        h_scr[...] = x_ref[...].astype(BF16)

    @pl.when(valid)
    def _():
        h = h_scr[...]
        g = _bdot(h, wg_ref[...])
        u = _bdot(h, wu_ref[...])
        acc_scr[...] += _bdot((g * _sigmoid(g) * u).astype(BF16), wd_ref[...])

    @pl.when(f == pl.num_programs(1) - 1)
    def _():
        o_ref[...] = acc_scr[...]


def _moe(tile_expert, n_valid, xs, wg, wu, wd, j):
    def wcol(i, f, te, nv):
        return (j, te[i], 0, jnp.where(i < nv[0], f, D_FF // TF - 1))

    def wrow(i, f, te, nv):
        return (j, te[i], jnp.where(i < nv[0], f, D_FF // TF - 1), 0)

    tok = pl.BlockSpec((TM_FF, D), lambda i, f, te, nv: (i, 0))
    return pl.pallas_call(
        _moe_kernel,
        grid_spec=pltpu.PrefetchScalarGridSpec(
            num_scalar_prefetch=2, grid=(NP_ROWS // TM_FF, D_FF // TF),
            in_specs=[tok, pl.BlockSpec((None, None, D, TF), wcol), pl.BlockSpec((None, None, D, TF), wcol),
                      pl.BlockSpec((None, None, TF, D), wrow)],
            out_specs=tok,
            scratch_shapes=[pltpu.VMEM((TM_FF, D), BF16), pltpu.VMEM((TM_FF, D), F32)]),
        out_shape=jax.ShapeDtypeStruct((NP_ROWS, D), F32),
        compiler_params=_cparams(("arbitrary", "arbitrary")),
        name="moe",
    )(tile_expert, n_valid, xs, wg, wu, wd)


def _combine_kernel(x_ref, o0_ref, o1_ref, route_ref, mod_ref, ln_ref, o_ref):
    y = route_ref[:, 2:3] * o0_ref[...] + route_ref[:, 3:4] * o1_ref[...]
    o_ref[...] = _layer_norm(ALPHA * x_ref[...] + mod_ref[5:6, :] * y, ln_ref[2:3, :], ln_ref[3:4, :])


def _combine(x, o0, o1, route, mod_l, ln):
    tok = pl.BlockSpec((TM, D), lambda i: (i, 0))
    return pl.pallas_call(
        _combine_kernel,
        grid=(T // TM,),
        in_specs=[tok, tok, tok, pl.BlockSpec((TM, LANES), lambda i: (i, 0)),
                  pl.BlockSpec((None, 8, D), lambda i: (_group_of_tile(i, TM), 0, 0)),
                  pl.BlockSpec((8, D), lambda i: (0, 0))],
        out_specs=tok,
        out_shape=jax.ShapeDtypeStruct((T, D), F32),
        compiler_params=_cparams(("parallel",)),
        name="moe_combine",
    )(x, o0, o1, route, mod_l, ln)


def _routing_plan(route):
    e = route[:, 0:2].astype(jnp.int32).reshape(-1)
    onehot = (e[:, None] == jnp.arange(N_EXP, dtype=jnp.int32)[None, :]).astype(jnp.int32)
    csum = jnp.cumsum(onehot, axis=0)
    counts = csum[-1]
    rank = jnp.sum((csum - onehot) * onehot, axis=1)
    padded = (counts + TM_FF - 1) // TM_FF * TM_FF
    pend = jnp.cumsum(padded)
    dest = jnp.sum(onehot * (pend - padded)[None, :], axis=1) + rank
    order = jnp.argsort(e, stable=True).astype(jnp.int32)
    rows = jnp.arange(NP_ROWS, dtype=jnp.int32)
    before = (rows[:, None] >= pend[None, :]).astype(jnp.int32)
    row_e = jnp.minimum(jnp.sum(before, axis=1), N_EXP - 1)
    row_cnt = jnp.sum((row_e[:, None] == jnp.arange(N_EXP, dtype=jnp.int32)[None, :]) * counts[None, :], axis=1)
    q = jnp.clip(rows - jnp.sum(before * padded[None, :], axis=1), 0, jnp.maximum(row_cnt - 1, 0))
    src = jnp.minimum(jnp.sum(before * counts[None, :], axis=1) + q, 2 * T - 1)
    row_token = order[src] // 2
    n_valid = (pend[-1] // TM_FF).astype(jnp.int32)
    tile_start = jnp.minimum(jnp.arange(NP_ROWS // TM_FF, dtype=jnp.int32), n_valid - 1) * TM_FF
    tile_expert = jnp.minimum(jnp.sum((tile_start[:, None] >= pend[None, :]).astype(jnp.int32), axis=1), N_EXP - 1)
    return dest.reshape(T, 2), row_token, tile_expert.astype(jnp.int32), n_valid.reshape(1)


_INV_PERM_RDK = np.argsort(np.concatenate([np.arange(0, RDK, 2), np.arange(1, RDK, 2)]))


def _pad_cols(a, width):
    return jnp.pad(a, ((0, 0), (0, width - a.shape[1])))


def _deinterleave(a):
    n = a.shape[-1]
    return jnp.swapaxes(a.reshape(a.shape[:-1] + (n // 2, 2)), -1, -2).reshape(a.shape)


def _prep_even_weights(w_in, w_q_b, w_kv_b):
    o_kpe, o_rq, o_rk, o_rv = 640, 672, 1184, 1696
    heads = lambda a: _deinterleave(a.reshape(D, HEADS, RDK)).reshape(D, HEADS * RDK)
    kpe = w_in[:, o_kpe:o_kpe + ROPE]
    w_in_p = jnp.concatenate([
        w_in[:, :640], heads(w_in[:, o_rq:o_rk]), heads(w_in[:, o_rk:o_rv]) * (RDK ** -0.5), w_in[:, o_rv:],
        _pad_cols(kpe, LANES), _pad_cols(_deinterleave(kpe), LANES),
        _pad_cols(jnp.concatenate([-kpe[:, 1::2], kpe[:, 0::2]], axis=1), LANES)], axis=1).astype(BF16)
    wq = w_q_b.reshape(Q_LORA, HEADS, NOPE + ROPE)
    wq = jnp.concatenate([wq[:, :, :NOPE], _deinterleave(wq[:, :, NOPE:]),
                          jnp.zeros((Q_LORA, HEADS, LANES - NOPE - ROPE), F32)], axis=2)
    wq = wq.reshape(Q_LORA, HEADS * LANES).astype(BF16)
    wkv = w_kv_b.reshape(KV_LORA, HEADS, NOPE + VDIM)
    zero = jnp.zeros((KV_LORA, HEADS, LANES - NOPE), F32)
    wk = jnp.concatenate([wkv[:, :, :NOPE], zero], axis=2).reshape(KV_LORA, HEADS * LANES)
    wv = wkv[:, :, NOPE:].reshape(KV_LORA, HEADS // 2, 2, VDIM)
    zv = jnp.zeros((KV_LORA, HEADS // 2, VDIM), F32)
    wv = jnp.stack([jnp.concatenate([wv[:, :, 0], zv], axis=2), jnp.concatenate([zv, wv[:, :, 1]], axis=2)], axis=2)
    wkv_p = jnp.concatenate([wk, wv.reshape(KV_LORA, HEADS * LANES)], axis=1).astype(BF16)
    return w_in_p, wq, wkv_p


def _placement():
    ek = np.zeros((LANES, HEADS * LANES), np.float32)
    for h in range(HEADS):
        ek[np.arange(ROPE), h * LANES + NOPE + np.arange(ROPE)] = 1.0
    return jnp.asarray(ek, BF16)


def _rotary_tables():
    rows = LAT_LEN // GRID_W
    r, col = jnp.meshgrid(jnp.arange(rows, dtype=F32), jnp.arange(GRID_W, dtype=F32), indexing='ij')
    n_freq = ROPE // 4
    freqs = 1.0 / (10000.0 ** (jnp.arange(n_freq, dtype=F32) / n_freq))
    ang = jnp.concatenate([r.reshape(-1)[:, None] * freqs, col.reshape(-1)[:, None] * freqs], axis=-1)
    cos, sin = jnp.cos(ang), jnp.sin(ang)
    theta = 1.0 / (10000.0 ** jnp.linspace(0.0, 1.0, RDK // 2, dtype=F32))
    rang = jnp.arange(LAT_LEN, dtype=F32)[:, None] * theta
    rcos, rsin = jnp.cos(rang), jnp.sin(rang)
    one = lambda w: jnp.ones((LAT_LEN, w), F32)
    zero = lambda w: jnp.zeros((LAT_LEN, w), F32)
    lat = [jnp.concatenate([one(NOPE), cos, cos, one(LANES - NOPE - ROPE)], axis=1),
           jnp.concatenate([zero(NOPE), sin, sin, zero(LANES - NOPE - ROPE)], axis=1),
           jnp.concatenate([cos, cos, zero(LANES - ROPE)], axis=1),
           jnp.concatenate([sin, sin, zero(LANES - ROPE)], axis=1),
           jnp.concatenate([rcos] * 4, axis=1), jnp.concatenate([rsin] * 4, axis=1)]
    return lat


def _block_diag_states(s0):
    s = jnp.swapaxes(_deinterleave(jnp.swapaxes(s0, -1, -2)), -1, -2)
    s = s.reshape(s0.shape[0], HEADS // 2, 2, RDK, RDK)
    z = jnp.zeros_like(s[:, :, 0])
    top = jnp.concatenate([s[:, :, 0], z], axis=-1)
    bot = jnp.concatenate([z, s[:, :, 1]], axis=-1)
    return jnp.concatenate([top, bot], axis=-2).astype(BF16)


def _unpermute_matrix():
    m = np.zeros((LANES, LANES), np.float32)
    for blk in range(LANES // RDK):
        m[blk * RDK + np.arange(RDK), blk * RDK + _INV_PERM_RDK] = 1.0
    return jnp.asarray(m, BF16)


def kernel(x_prompt, x_sample, c, cache_ckv, cache_kpe, state_ret_fwd, state_ret_bwd, c_ctx, w_mod, b_mod, ln_g, ln_b, w_in_mix, q_a_gain, kv_a_gain, w_q_b, w_kv_b, ret_decay_fwd, ret_decay_bwd, w_out_mix, w_in_conv, conv_w, w_out_conv, ffn_gate, ffn_up, ffn_down, router_w, router_b, exp_gate, exp_up, exp_down):
    x = jnp.concatenate([x_prompt.reshape(TP, D), x_sample.reshape(TS, D)], axis=0)
    cond8 = jnp.concatenate([c_ctx[None], c, jnp.zeros((8 - 1 - N_LAT_SEQ, D), F32)], axis=0)
    mods = _modulation(cond8, w_mod, b_mod)
    mods = jnp.pad(mods.reshape(DEPTH, N_GROUPS, 6, D), ((0, 0), (0, 0), (0, 2), (0, 0)))
    ln = jnp.pad(jnp.concatenate([ln_g, ln_b], axis=1)[:, jnp.array([0, 2, 1, 3])], ((0, 0), (0, 4), (0, 0)))
    tabs = _rotary_tables()
    ek = _placement()
    unperm = _unpermute_matrix()
    bf = lambda a: a.astype(BF16)
    w_out_mix_b, w_in_conv_b, w_out_conv_b = bf(w_out_mix), bf(w_in_conv), bf(w_out_conv)
    ffn_b = (bf(ffn_gate), bf(ffn_up), bf(ffn_down))
    exp_b = (bf(exp_gate), bf(exp_up), bf(exp_down))
    new_ckv, new_kpe, states = [], [], None
    for layer in range(DEPTH):
        j = layer // 2
        mod_l, ln_l = mods[layer], ln[layer]
        if layer % 2 == 0:
            w_in_p, wq, wkv = _prep_even_weights(w_in_mix[j], w_q_b[j], w_kv_b[j])
            q, k, v, ckv, kpe, rq, rk, rv, rg = _even_in(
                x, mod_l, w_in_p, q_a_gain[j][None], kv_a_gain[j][None], wq, wkv, ek, tabs)
            kpe_c = _pad_cols(_deinterleave(cache_kpe[:, j]).reshape(N_LAT_SEQ * PAST, ROPE), LANES)
            kc, vc = _ctx_kv(cache_ckv[:, j].reshape(N_LAT_SEQ * PAST, KV_LORA), kpe_c, wkv, ek)
            attn = _attention(q, k, v, kc, vc)
            lg = jnp.concatenate([jax.nn.log_sigmoid(ret_decay_fwd[j].astype(F32)),
                                  jax.nn.log_sigmoid(ret_decay_bwd[j].astype(F32))])
            ret, sf, sb = _retention(lg, rq, rk, rv, rg, _block_diag_states(state_ret_fwd[:, j]),
                                     _block_diag_states(state_ret_bwd[:, j]), unperm, j, states)
            states = (sf, sb)
            x = _mix_out(x, attn, ret, mod_l, w_out_mix_b, ln_l, j)
            x = _ffn(x, mod_l, *ffn_b, ln_l, j)
            new_ckv.append(ckv[:TP].reshape(N_PROMPT_SEQ, PROMPT_LEN, KV_LORA))
            new_kpe.append(kpe[:TP, :ROPE].reshape(N_PROMPT_SEQ, PROMPT_LEN, ROPE))
        else:
            b, z = _conv_in(x, mod_l, w_in_conv_b, j)
            cw = jnp.pad(conv_w[j], ((0, 5), (0, 0)))
            rw = _pad_cols(router_w[j], LANES)
            rb = jnp.concatenate([router_b[j].astype(F32), jnp.full((LANES - N_EXP,), -1e30, F32)])[None]
            x, h, route = _conv_out(x, b, z, mod_l, cw, w_out_conv_b, ln_l, rw, rb, j)
            dest, row_token, tile_expert, n_valid = _routing_plan(route)
            out_sorted = _moe(tile_expert, n_valid, h[row_token], *exp_b, j)
            x = _combine(x, out_sorted[dest[:, 0]], out_sorted[dest[:, 1]], route, mod_l, ln_l)
    y_prompt = x[:TP].reshape(N_PROMPT_SEQ, PROMPT_LEN, D)
    y_sample = x[TP:].reshape(N_LAT_SEQ, LAT_LEN, D)
    return (y_prompt, y_sample, jnp.stack(new_ckv, axis=1), jnp.stack(new_kpe, axis=1), states[0], states[1])
```

```python
import functools

import numpy as np
import jax
import jax.numpy as jnp
from jax import lax
from jax.experimental import pallas as pl
from jax.experimental.pallas import tpu as pltpu

F32 = jnp.float32
BF16 = jnp.bfloat16

D = 1024
DEPTH = 4
N_PROMPT_SEQ, PROMPT_LEN = 32, 256
N_LAT_SEQ, LAT_LEN = 2, 2048
PAST = 512
GRID_W = 64
TP = N_PROMPT_SEQ * PROMPT_LEN
TS = N_LAT_SEQ * LAT_LEN
T = TP + TS
HEADS = 8
NOPE, ROPE, VDIM = 64, 32, 64
Q_LORA, KV_LORA = 384, 256
RDK = 64
D_FF = 2816
N_EXP = 8
ALPHA = (2.0 * DEPTH) ** 0.25
Q_SCALE = float((NOPE + ROPE) ** -0.5)
LANES = 128
N_GROUPS = 8

TM = 256
TM_FF = 512
TF = D_FF // 2
TQ = 256
NP_ROWS = 2 * T + N_EXP * TM_FF
VMEM_LIMIT = 56 * 1024 * 1024

IN_COLS = 3072


def _cparams(sem):
    return pltpu.CompilerParams(dimension_semantics=sem, vmem_limit_bytes=VMEM_LIMIT)


def _group_of_tile(i, tm):
    per_seq = LAT_LEN // tm
    return jnp.maximum(i - TP // tm + per_seq, 0) // per_seq


def _bdot(a, b):
    return jnp.dot(a, b, preferred_element_type=F32)


def _sigmoid(v):
    return 1.0 / (1.0 + jnp.exp(-v))


def _layer_norm(v, g, b):
    mu = jnp.mean(v, axis=-1, keepdims=True)
    d = v - mu
    var = jnp.mean(d * d, axis=-1, keepdims=True)
    return d * lax.rsqrt(var + 1e-5) * g + b


def _rms(v, g):
    return v * lax.rsqrt(jnp.mean(v * v, axis=-1, keepdims=True) + 1e-6) * g


def _mod_kernel(c_ref, w_ref, b_ref, o_ref):
    c = c_ref[...]
    s = (c * _sigmoid(c)).astype(BF16)
    o_ref[...] = _bdot(s, w_ref[...].astype(BF16)) + b_ref[...]


def _modulation(cond8, w_mod, b_mod):
    tn = 1536
    return pl.pallas_call(
        _mod_kernel,
        grid=(DEPTH, 6 * D // tn),
        in_specs=[pl.BlockSpec((8, D), lambda l, n: (0, 0)),
                  pl.BlockSpec((None, D, tn), lambda l, n: (l, 0, n)),
                  pl.BlockSpec((None, 1, tn), lambda l, n: (l, 0, n))],
        out_specs=pl.BlockSpec((None, 8, tn), lambda l, n: (l, 0, n)),
        out_shape=jax.ShapeDtypeStruct((DEPTH, 8, 6 * D), F32),
        compiler_params=_cparams(("arbitrary", "arbitrary")),
        name="modulation",
    )(cond8, w_mod, b_mod.reshape(DEPTH, 1, 6 * D))


def _swap_halves(a, half):
    n = a.shape[-1]
    lane = lax.broadcasted_iota(jnp.int32, a.shape, 1)
    first = (lane & (2 * half - 1)) < half
    return jnp.where(first, -pltpu.roll(a, n - half, axis=1), pltpu.roll(a, half, axis=1))


def _even_in_kernel(x_ref, mod_ref, w_in_ref, qg_ref, kvg_ref, wq_ref, wkv_ref, ek_ref,
                    cq_ref, sq_ref, ck_ref, sk_ref, cr_ref, sr_ref,
                    q_ref, k_ref, v_ref, ckv_ref, kpe_ref, rq_ref, rk_ref, rv_ref, rg_ref):
    x = x_ref[...]
    h = (x * (1.0 + mod_ref[1:2, :]) + mod_ref[0:1, :]).astype(BF16)
    p = _bdot(h, w_in_ref[...])
    qn = _rms(p[:, 0:Q_LORA], qg_ref[...]).astype(BF16)
    qa = _bdot(qn, wq_ref[...])
    ckv = _rms(p[:, Q_LORA:Q_LORA + KV_LORA], kvg_ref[...])
    ckv_ref[...] = ckv
    kv = _bdot(ckv.astype(BF16), wkv_ref[...])
    v_ref[...] = kv[:, HEADS * LANES:].astype(BF16)
    base = 2688
    kpe_ref[...] = p[:, base:base + LANES]
    ka = p[:, base + LANES:base + 2 * LANES]
    kb = p[:, base + 2 * LANES:base + 3 * LANES]
    rq = p[:, 640:1152]
    rk = p[:, 1152:1664]
    rv_ref[...] = p[:, 1664:2176].astype(BF16)
    rg_ref[...] = p[:, 2176:2688]
    lane = lax.broadcasted_iota(jnp.int32, qa.shape, 1) & (LANES - 1)
    qb = jnp.where(lane < NOPE + ROPE // 2,
                   -pltpu.roll(qa, qa.shape[1] - ROPE // 2, axis=1),
                   pltpu.roll(qa, ROPE // 2, axis=1))
    cq = jnp.concatenate([cq_ref[...]] * HEADS, axis=1)
    sq = jnp.concatenate([sq_ref[...]] * HEADS, axis=1)
    q_ref[...] = ((qa * cq + qb * sq) * Q_SCALE).astype(BF16)
    kpe_rot = ka * ck_ref[...] + kb * sk_ref[...]
    k_ref[...] = (kv[:, :HEADS * LANES] + _bdot(kpe_rot.astype(BF16), ek_ref[...])).astype(BF16)
    cr = jnp.concatenate([cr_ref[...]] * 4, axis=1)
    sr = jnp.concatenate([sr_ref[...]] * 4, axis=1)
    rq_ref[...] = (rq * cr + _swap_halves(rq, RDK // 2) * sr).astype(BF16)
    rk_ref[...] = (rk * cr + _swap_halves(rk, RDK // 2) * sr).astype(BF16)


def _even_in(x, mod_l, w_in, qg, kvg, wq, wkv, ek, tabs):
    tok = lambda w: pl.BlockSpec((TM, w), lambda i: (i, 0))
    full = lambda a: pl.BlockSpec(a.shape, lambda i: (0,) * a.ndim)
    lat_tiles = LAT_LEN // TM
    tab = pl.BlockSpec((TM, LANES), lambda i: (
        jnp.where(i < TP // TM, lat_tiles, jnp.maximum(i - TP // TM, 0) % lat_tiles), 0))
    outs = [(HEADS * LANES, BF16), (HEADS * LANES, BF16), (HEADS * LANES, BF16), (KV_LORA, F32), (LANES, F32),
            (512, BF16), (512, BF16), (512, BF16), (512, F32)]
    return pl.pallas_call(
        _even_in_kernel,
        grid=(T // TM,),
        in_specs=[tok(D), pl.BlockSpec((None, 8, D), lambda i: (_group_of_tile(i, TM), 0, 0)),
                  full(w_in), full(qg), full(kvg), full(wq), full(wkv), full(ek)] + [tab] * 6,
        out_specs=[tok(w) for w, _ in outs],
        out_shape=[jax.ShapeDtypeStruct((T, w), dt) for w, dt in outs],
        compiler_params=_cparams(("parallel",)),
        name="even_in",
    )(x, mod_l, w_in, qg, kvg, wq, wkv, ek, *tabs)


def _ctx_kv_kernel(ckv_ref, kpe_ref, wkv_ref, ek_ref, k_ref, v_ref):
    kv = _bdot(ckv_ref[...].astype(BF16), wkv_ref[...])
    k_ref[...] = (kv[:, :HEADS * LANES] + _bdot(kpe_ref[...].astype(BF16), ek_ref[...])).astype(BF16)
    v_ref[...] = kv[:, HEADS * LANES:].astype(BF16)


def _ctx_kv(ckv_c, kpe_c, wkv, ek):
    n = ckv_c.shape[0]
    full = lambda a: pl.BlockSpec(a.shape, lambda i: (0,) * a.ndim)
    return pl.pallas_call(
        _ctx_kv_kernel,
        grid=(n // PAST,),
        in_specs=[pl.BlockSpec((PAST, KV_LORA), lambda i: (i, 0)), pl.BlockSpec((PAST, LANES), lambda i: (i, 0)),
                  full(wkv), full(ek)],
        out_specs=[pl.BlockSpec((PAST, HEADS * LANES), lambda i: (i, 0))] * 2,
        out_shape=[jax.ShapeDtypeStruct((n, HEADS * LANES), BF16)] * 2,
        compiler_params=_cparams(("parallel",)),
        name="ctx_kv",
    )(ckv_c, kpe_c, wkv, ek)


def _attn_kernel(n_kv, q_ref, *refs):
    k_refs = refs[0:2 * n_kv:2]
    v_refs = refs[1:2 * n_kv:2]
    o_ref = refs[2 * n_kv]
    nt = (((1,), (1,)), ((), ()))
    for pair in range(HEADS // 2):
        acc = None
        for sub in range(2):
            sl = slice((2 * pair + sub) * LANES, (2 * pair + sub + 1) * LANES)
            qh = q_ref[:, sl]
            s = [lax.dot_general(qh, k[:, sl], nt, preferred_element_type=F32) for k in k_refs]
            m = functools.reduce(jnp.maximum, [jnp.max(a, axis=-1, keepdims=True) for a in s])
            e = [jnp.exp(a - m) for a in s]
            den = functools.reduce(jnp.add, [jnp.sum(a, axis=-1, keepdims=True) for a in e])
            o = functools.reduce(jnp.add, [_bdot(a.astype(BF16), v[:, sl]) for a, v in zip(e, v_refs)])
            o = o / den
            acc = o if acc is None else acc + o
        o_ref[:, pair * LANES:(pair + 1) * LANES] = acc.astype(BF16)


def _latent_seq(i):
    return jnp.maximum(i - TP // TQ, 0) // (LAT_LEN // TQ)


def _attn_tiles_kernel(q_ref, kp_ref, vp_ref, kc_ref, vc_ref, kl_ref, vl_ref, o_ref):
    is_latent = pl.program_id(0) >= TP // TQ

    @pl.when(jnp.logical_not(is_latent))
    def _():
        _attn_kernel(1, q_ref, kp_ref, vp_ref, o_ref)

    @pl.when(is_latent)
    def _():
        _attn_kernel(2, q_ref, kc_ref, vc_ref, kl_ref, vl_ref, o_ref)


def _attention(q, k, v, kc, vc):
    w = HEADS * LANES
    tile = lambda i: (i, 0)
    ctx_own = lambda i: (jnp.minimum(i, N_PROMPT_SEQ - 1), 0)
    cache = lambda i: (_latent_seq(i), 0)
    lat_own = lambda i: (TP // LAT_LEN + _latent_seq(i), 0)
    return pl.pallas_call(
        _attn_tiles_kernel,
        grid=(T // TQ,),
        in_specs=[pl.BlockSpec((TQ, w), tile),
                  pl.BlockSpec((PROMPT_LEN, w), ctx_own), pl.BlockSpec((PROMPT_LEN, w), ctx_own),
                  pl.BlockSpec((PAST, w), cache), pl.BlockSpec((PAST, w), cache),
                  pl.BlockSpec((LAT_LEN, w), lat_own), pl.BlockSpec((LAT_LEN, w), lat_own)],
        out_specs=pl.BlockSpec((TQ, HEADS * VDIM), tile),
        out_shape=jax.ShapeDtypeStruct((T, HEADS * VDIM), BF16),
        compiler_params=_cparams(("arbitrary",)),
        name="attention",
    )(q, k, v, kc, vc, k, v)


def _retention_kernel(latent, lg_ref, rq_ref, rk_ref, rv_ref, rg_ref, *refs):
    if latent:
        s0f_ref, s0b_ref, o_ref = refs
        q0 = (pl.program_id(0) - TP // TQ) % (LAT_LEN // TQ) * TQ
        seq_len = LAT_LEN
    else:
        unperm_ref, o_ref, sf_ref, sb_ref = refs
        q0 = 0
        seq_len = PROMPT_LEN
    tq, tk = rq_ref.shape[0], rk_ref.shape[0]
    nt = (((1,), (1,)), ((), ()))
    n_idx = (q0 + lax.broadcasted_iota(jnp.int32, (tq, tk), 0)).astype(F32)
    m_idx = lax.broadcasted_iota(jnp.int32, (tq, tk), 1).astype(F32)
    dist = n_idx - m_idx
    adist = jnp.abs(dist)
    fwd = dist > 0.0
    diag = jnp.where(dist == 0.0, 1.0, 0.0)
    lane = lax.broadcasted_iota(jnp.int32, (1, LANES), 1)
    lo = lane < RDK
    n_col = (q0 + lax.broadcasted_iota(jnp.int32, (tq, 1), 0)).astype(F32)
    m_col = lax.broadcasted_iota(jnp.int32, (tk, 1), 0).astype(F32)
    for pair in range(HEADS // 2):
        sl = slice(pair * LANES, (pair + 1) * LANES)
        qb, kb, vb = rq_ref[:, sl], rk_ref[:, sl], rv_ref[:, sl]
        acc = jnp.zeros((tq, LANES), F32)
        for sub in range(2):
            h = 2 * pair + sub
            lgf, lgb = lg_ref[h], lg_ref[HEADS + h]
            half = lo if sub == 0 else jnp.logical_not(lo)
            qm = jnp.where(half, qb, jnp.zeros_like(qb))
            vm = jnp.where(half, vb, jnp.zeros_like(vb))
            s = lax.dot_general(qm, kb, nt, preferred_element_type=F32)
            w = jnp.exp(adist * jnp.where(fwd, lgf, lgb)) + diag
            acc = acc + _bdot((s * w).astype(BF16), vm)
        lgf_l = jnp.where(lo, lg_ref[2 * pair], lg_ref[2 * pair + 1])
        lgb_l = jnp.where(lo, lg_ref[HEADS + 2 * pair], lg_ref[HEADS + 2 * pair + 1])
        if latent:
            acc = acc + _bdot(qb, s0f_ref[pair]) * jnp.exp((n_col + 1.0) * lgf_l)
            acc = acc + _bdot(qb, s0b_ref[pair]) * jnp.exp((seq_len - n_col) * lgb_l)
        else:
            v_swapped = pltpu.roll(vb.astype(F32), RDK, axis=1).astype(BF16)
            for st_ref, dec in ((sf_ref, jnp.exp((seq_len - 1.0 - m_col) * lgf_l)),
                                (sb_ref, jnp.exp(m_col * lgb_l))):
                kt = (kb.astype(F32) * dec).T.astype(BF16)
                kt = _bdot(unperm_ref[...], kt).astype(BF16)
                st_ref[2 * pair] = _bdot(kt, vb)[0:RDK, 0:RDK]
                st_ref[2 * pair + 1] = _bdot(kt, v_swapped)[RDK:, 0:RDK]
        inv = 1.0 / RDK
        mu = jnp.where(lo, jnp.sum(jnp.where(lo, acc, 0.0), axis=-1, keepdims=True),
                       jnp.sum(jnp.where(lo, 0.0, acc), axis=-1, keepdims=True)) * inv
        dlt = acc - mu
        d2 = dlt * dlt
        var = jnp.where(lo, jnp.sum(jnp.where(lo, d2, 0.0), axis=-1, keepdims=True),
                        jnp.sum(jnp.where(lo, 0.0, d2), axis=-1, keepdims=True)) * inv
        g = rg_ref[:, sl]
        o_ref[:, sl] = (dlt * lax.rsqrt(var + 1e-5) * (g * _sigmoid(g))).astype(BF16)


def _retention_tiles_kernel(lg_ref, rq_ref, rkp_ref, rvp_ref, rkl_ref, rvl_ref, rg_ref, s0f_ref, s0b_ref, unperm_ref,
                            o_ref, sf_ref, sb_ref):
    is_latent = pl.program_id(0) >= TP // TQ

    @pl.when(jnp.logical_not(is_latent))
    def _():
        _retention_kernel(False, lg_ref, rq_ref, rkp_ref, rvp_ref, rg_ref, unperm_ref, o_ref, sf_ref, sb_ref)

    @pl.when(is_latent)
    def _():
        _retention_kernel(True, lg_ref, rq_ref, rkl_ref, rvl_ref, rg_ref, s0f_ref, s0b_ref, o_ref)


def _retention(lg, rq, rk, rv, rg, s0f, s0b, unperm):
    w = HEADS * RDK
    tile = lambda i, lg: (i, 0)
    ctx_own = lambda i, lg: (jnp.minimum(i, N_PROMPT_SEQ - 1), 0)
    lat_own = lambda i, lg: (TP // LAT_LEN + _latent_seq(i), 0)
    s0_blk = pl.BlockSpec((None, HEADS // 2, LANES, LANES), lambda i, lg: (_latent_seq(i), 0, 0, 0))
    st_blk = pl.BlockSpec((None, HEADS, RDK, RDK), lambda i, lg: (jnp.minimum(i, N_PROMPT_SEQ - 1), 0, 0, 0))
    st_shape = jax.ShapeDtypeStruct((N_PROMPT_SEQ, HEADS, RDK, RDK), F32)
    return pl.pallas_call(
        _retention_tiles_kernel,
        grid_spec=pltpu.PrefetchScalarGridSpec(
            num_scalar_prefetch=1, grid=(T // TQ,),
            in_specs=[pl.BlockSpec((TQ, w), tile),
                      pl.BlockSpec((PROMPT_LEN, w), ctx_own), pl.BlockSpec((PROMPT_LEN, w), ctx_own),
                      pl.BlockSpec((LAT_LEN, w), lat_own), pl.BlockSpec((LAT_LEN, w), lat_own),
                      pl.BlockSpec((TQ, w), tile), s0_blk, s0_blk,
                      pl.BlockSpec((LANES, LANES), lambda i, lg: (0, 0))],
            out_specs=[pl.BlockSpec((TQ, w), tile), st_blk, st_blk]),
        out_shape=[jax.ShapeDtypeStruct((T, w), BF16), st_shape, st_shape],
        compiler_params=_cparams(("arbitrary",)),
        name="retention",
    )(lg, rq, rk, rv, rk, rv, rg, s0f, s0b, unperm)


def _mix_ffn_kernel(x_ref, a_ref, r_ref, mod_ref, wo_ref, wg_ref, wu_ref, wd_ref, ln_ref, o_ref,
                    x1_scr, h_scr, acc_scr):
    f = pl.program_id(1)

    @pl.when(f == 0)
    def _():
        half = HEADS * VDIM
        y = _bdot(a_ref[...], wo_ref[0:half, :]) + _bdot(r_ref[...], wo_ref[half:, :])
        x1 = _layer_norm(ALPHA * x_ref[...] + mod_ref[2:3, :] * y, ln_ref[0:1, :], ln_ref[1:2, :])
        x1_scr[...] = x1
        h_scr[...] = (x1 * (1.0 + mod_ref[4:5, :]) + mod_ref[3:4, :]).astype(BF16)
        acc_scr[...] = jnp.zeros_like(acc_scr)

    h = h_scr[...]
    g = _bdot(h, wg_ref[...])
    u = _bdot(h, wu_ref[...])
    acc_scr[...] += _bdot((g * _sigmoid(g) * u).astype(BF16), wd_ref[...])

    @pl.when(f == pl.num_programs(1) - 1)
    def _():
        o_ref[...] = _layer_norm(ALPHA * x1_scr[...] + mod_ref[5:6, :] * acc_scr[...], ln_ref[2:3, :], ln_ref[3:4, :])


def _mix_ffn(x, attn, ret, mod_l, w_out, wg, wu, wd, ln, j):
    tok = lambda w: pl.BlockSpec((TM_FF, w), lambda i, f: (i, 0))
    return pl.pallas_call(
        _mix_ffn_kernel,
        grid=(T // TM_FF, D_FF // TF),
        in_specs=[tok(D), tok(HEADS * VDIM), tok(HEADS * RDK),
                  pl.BlockSpec((None, 8, D), lambda i, f: (_group_of_tile(i, TM_FF), 0, 0)),
                  pl.BlockSpec((None,) + w_out.shape[1:], lambda i, f: (j, 0, 0)),
                  pl.BlockSpec((None, D, TF), lambda i, f: (j, 0, f)),
                  pl.BlockSpec((None, D, TF), lambda i, f: (j, 0, f)),
                  pl.BlockSpec((None, TF, D), lambda i, f: (j, f, 0)), pl.BlockSpec((8, D), lambda i, f: (0, 0))],
        out_specs=tok(D),
        out_shape=jax.ShapeDtypeStruct((T, D), F32),
        scratch_shapes=[pltpu.VMEM((TM_FF, D), F32), pltpu.VMEM((TM_FF, D), BF16), pltpu.VMEM((TM_FF, D), F32)],
        compiler_params=_cparams(("parallel", "arbitrary")),
        name="mix_ffn",
    )(x, attn, ret, mod_l, w_out, wg, wu, wd, ln)


def _conv_in_kernel(x_ref, mod_ref, w_ref, b_ref, z_ref):
    h = (x_ref[...] * (1.0 + mod_ref[1:2, :]) + mod_ref[0:1, :]).astype(BF16)
    p = _bdot(h, w_ref[...])
    b_ref[...] = p[:, 0:D]
    z_ref[...] = p[:, D:2 * D] * p[:, 2 * D:3 * D]


def _conv_in(x, mod_l, w_in, j):
    tok = pl.BlockSpec((TM, D), lambda i: (i, 0))
    return pl.pallas_call(
        _conv_in_kernel,
        grid=(T // TM,),
        in_specs=[tok, pl.BlockSpec((None, 8, D), lambda i: (_group_of_tile(i, TM), 0, 0)),
                  pl.BlockSpec((None,) + w_in.shape[1:], lambda i: (j, 0, 0))],
        out_specs=[tok, tok],
        out_shape=[jax.ShapeDtypeStruct((T, D), F32)] * 2,
        compiler_params=_cparams(("parallel",)),
        name="conv_in",
    )(x, mod_l, w_in)


def _conv_out_kernel(x_ref, b_ref, z_ref, zp_ref, zn_ref, mod_ref, cw_ref, w_ref, ln_ref, rw_ref, rb_ref,
                     o_ref, h_ref, route_ref):
    i = pl.program_id(0)
    z = z_ref[...]
    row = lax.broadcasted_iota(jnp.int32, (TM, 1), 0)
    seq_len = jnp.where(i < TP // TM, PROMPT_LEN, LAT_LEN)
    pos = (i * TM + row) & (seq_len - 1)
    prev = jnp.where(row == 0, zp_ref[7:8, :], pltpu.roll(z, 1, axis=0))
    prev = jnp.where(pos == 0, 0.0, prev)
    nxt = jnp.where(row == TM - 1, zn_ref[0:1, :], pltpu.roll(z, TM - 1, axis=0))
    nxt = jnp.where(pos == seq_len - 1, 0.0, nxt)
    y = prev * cw_ref[0:1, :] + z * cw_ref[1:2, :] + nxt * cw_ref[2:3, :]
    t = _bdot((b_ref[...] * y).astype(BF16), w_ref[...])
    x1 = _layer_norm(ALPHA * x_ref[...] + mod_ref[2:3, :] * t, ln_ref[0:1, :], ln_ref[1:2, :])
    o_ref[...] = x1
    h = x1 * (1.0 + mod_ref[4:5, :]) + mod_ref[3:4, :]
    h_ref[...] = h
    h_hi = h.astype(BF16)
    h_lo = (h - h_hi.astype(F32)).astype(BF16)
    both = _bdot(h_hi, rw_ref[...])
    logits = both[:, :LANES] + both[:, LANES:] + _bdot(h_lo, rw_ref[:, :LANES]) + rb_ref[...]
    lane = lax.broadcasted_iota(jnp.int32, logits.shape, 1).astype(F32)
    t1 = jnp.max(logits, axis=-1, keepdims=True)
    i1 = jnp.min(jnp.where(logits == t1, lane, float(LANES)), axis=-1, keepdims=True)
    rest = jnp.where(lane == i1, -jnp.inf, logits)
    t2 = jnp.max(rest, axis=-1, keepdims=True)
    i2 = jnp.min(jnp.where(rest == t2, lane, float(LANES)), axis=-1, keepdims=True)
    e = jnp.exp(t2 - t1)
    den = 1.0 + e
    route_ref[...] = jnp.where(lane == 0.0, i1, jnp.where(lane == 1.0, i2,
                               jnp.where(lane == 2.0, 1.0 / den, jnp.where(lane == 3.0, e / den, 0.0))))


def _conv_out(x, b, z, mod_l, cw, w_out, ln, rw, rb, j):
    tok = pl.BlockSpec((TM, D), lambda i: (i, 0))
    sub = TM // 8
    return pl.pallas_call(
        _conv_out_kernel,
        grid=(T // TM,),
        in_specs=[tok, tok, tok,
                  pl.BlockSpec((8, D), lambda i: (jnp.maximum(i * sub - 1, 0), 0)),
                  pl.BlockSpec((8, D), lambda i: (jnp.minimum((i + 1) * sub, T // 8 - 1), 0)),
                  pl.BlockSpec((None, 8, D), lambda i: (_group_of_tile(i, TM), 0, 0)),
                  pl.BlockSpec((8, D), lambda i: (0, 0)), pl.BlockSpec((None, D, D), lambda i: (j, 0, 0)),
                  pl.BlockSpec((8, D), lambda i: (0, 0)),
                  pl.BlockSpec((D, 2 * LANES), lambda i: (0, 0)), pl.BlockSpec((1, LANES), lambda i: (0, 0))],
        out_specs=[tok, tok, pl.BlockSpec((TM, LANES), lambda i: (i, 0))],
        out_shape=[jax.ShapeDtypeStruct((T, D), F32), jax.ShapeDtypeStruct((T, D), F32),
                   jax.ShapeDtypeStruct((T, LANES), F32)],
        compiler_params=_cparams(("parallel",)),
        name="conv_out",
    )(x, b, z, z, z, mod_l, cw, w_out, ln, rw, rb)


def _moe_kernel(te_ref, nv_ref, x_ref, wg_ref, wu_ref, wd_ref, o_ref, h_scr, acc_scr):
    i, f = pl.program_id(0), pl.program_id(1)
    valid = i < nv_ref[0]

    @pl.when(f == 0)
    def _():
        acc_scr[...] = jnp.zeros_like(acc_scr)
        h_scr[...] = x_ref[...].astype(BF16)

    @pl.when(valid)
    def _():
        h = h_scr[...]
        g = _bdot(h, wg_ref[...])
        u = _bdot(h, wu_ref[...])
        acc_scr[...] += _bdot((g * _sigmoid(g) * u).astype(BF16), wd_ref[...])

    @pl.when(f == pl.num_programs(1) - 1)
    def _():
        o_ref[...] = acc_scr[...]


def _moe(tile_expert, n_valid, xs, wg, wu, wd, j):
    def wcol(i, f, te, nv):
        return (j, te[i], 0, jnp.where(i < nv[0], f, D_FF // TF - 1))

    def wrow(i, f, te, nv):
        return (j, te[i], jnp.where(i < nv[0], f, D_FF // TF - 1), 0)

    tok = pl.BlockSpec((TM_FF, D), lambda i, f, te, nv: (i, 0))
    return pl.pallas_call(
        _moe_kernel,
        grid_spec=pltpu.PrefetchScalarGridSpec(
            num_scalar_prefetch=2, grid=(NP_ROWS // TM_FF, D_FF // TF),
            in_specs=[tok, pl.BlockSpec((None, None, D, TF), wcol), pl.BlockSpec((None, None, D, TF), wcol),
                      pl.BlockSpec((None, None, TF, D), wrow)],
            out_specs=tok,
            scratch_shapes=[pltpu.VMEM((TM_FF, D), BF16), pltpu.VMEM((TM_FF, D), F32)]),
        out_shape=jax.ShapeDtypeStruct((NP_ROWS, D), F32),
        compiler_params=_cparams(("arbitrary", "arbitrary")),
        name="moe",
    )(tile_expert, n_valid, xs, wg, wu, wd)


def _combine_kernel(x_ref, o0_ref, o1_ref, route_ref, mod_ref, ln_ref, o_ref):
    y = route_ref[:, 2:3] * o0_ref[...] + route_ref[:, 3:4] * o1_ref[...]
    o_ref[...] = _layer_norm(ALPHA * x_ref[...] + mod_ref[5:6, :] * y, ln_ref[2:3, :], ln_ref[3:4, :])


def _combine(x, o0, o1, route, mod_l, ln):
    tok = pl.BlockSpec((TM_FF, D), lambda i: (i, 0))
    return pl.pallas_call(
        _combine_kernel,
        grid=(T // TM_FF,),
        in_specs=[tok, tok, tok, pl.BlockSpec((TM_FF, LANES), lambda i: (i, 0)),
                  pl.BlockSpec((None, 8, D), lambda i: (_group_of_tile(i, TM_FF), 0, 0)),
                  pl.BlockSpec((8, D), lambda i: (0, 0))],
        out_specs=tok,
        out_shape=jax.ShapeDtypeStruct((T, D), F32),
        compiler_params=_cparams(("parallel",)),
        name="moe_combine",
    )(x, o0, o1, route, mod_l, ln)


def _routing_plan(route):
    e = route[:, 0:2].astype(jnp.int32).reshape(-1)
    onehot = (e[:, None] == jnp.arange(N_EXP, dtype=jnp.int32)[None, :]).astype(jnp.int32)
    csum = jnp.cumsum(onehot, axis=0)
    counts = csum[-1]
    rank = jnp.sum((csum - onehot) * onehot, axis=1)
    padded = (counts + TM_FF - 1) // TM_FF * TM_FF
    pend = jnp.cumsum(padded)
    dest = jnp.sum(onehot * (pend - padded)[None, :], axis=1) + rank
    order = jnp.argsort(e, stable=True).astype(jnp.int32)
    rows = jnp.arange(NP_ROWS, dtype=jnp.int32)
    before = (rows[:, None] >= pend[None, :]).astype(jnp.int32)
    row_e = jnp.minimum(jnp.sum(before, axis=1), N_EXP - 1)
    row_cnt = jnp.sum((row_e[:, None] == jnp.arange(N_EXP, dtype=jnp.int32)[None, :]) * counts[None, :], axis=1)
    q = jnp.clip(rows - jnp.sum(before * padded[None, :], axis=1), 0, jnp.maximum(row_cnt - 1, 0))
    src = jnp.minimum(jnp.sum(before * counts[None, :], axis=1) + q, 2 * T - 1)
    row_token = order[src] // 2
    n_valid = (pend[-1] // TM_FF).astype(jnp.int32)
    tile_start = jnp.minimum(jnp.arange(NP_ROWS // TM_FF, dtype=jnp.int32), n_valid - 1) * TM_FF
    tile_expert = jnp.minimum(jnp.sum((tile_start[:, None] >= pend[None, :]).astype(jnp.int32), axis=1), N_EXP - 1)
    return dest.reshape(T, 2), row_token, tile_expert.astype(jnp.int32), n_valid.reshape(1)


_INV_PERM_RDK = np.argsort(np.concatenate([np.arange(0, RDK, 2), np.arange(1, RDK, 2)]))


def _pad_cols(a, width):
    return jnp.pad(a, ((0, 0), (0, width - a.shape[1])))


def _deinterleave(a):
    n = a.shape[-1]
    return jnp.swapaxes(a.reshape(a.shape[:-1] + (n // 2, 2)), -1, -2).reshape(a.shape)


def _prep_even_weights(w_in, w_q_b, w_kv_b):
    o_kpe, o_rq, o_rk, o_rv = 640, 672, 1184, 1696
    heads = lambda a: _deinterleave(a.reshape(D, HEADS, RDK)).reshape(D, HEADS * RDK)
    kpe = w_in[:, o_kpe:o_kpe + ROPE]
    w_in_p = jnp.concatenate([
        w_in[:, :640], heads(w_in[:, o_rq:o_rk]), heads(w_in[:, o_rk:o_rv]) * (RDK ** -0.5), w_in[:, o_rv:],
        _pad_cols(kpe, LANES), _pad_cols(_deinterleave(kpe), LANES),
        _pad_cols(jnp.concatenate([-kpe[:, 1::2], kpe[:, 0::2]], axis=1), LANES)], axis=1).astype(BF16)
    wq = w_q_b.reshape(Q_LORA, HEADS, NOPE + ROPE)
    wq = jnp.concatenate([wq[:, :, :NOPE], _deinterleave(wq[:, :, NOPE:]),
                          jnp.zeros((Q_LORA, HEADS, LANES - NOPE - ROPE), F32)], axis=2)
    wq = wq.reshape(Q_LORA, HEADS * LANES).astype(BF16)
    wkv = w_kv_b.reshape(KV_LORA, HEADS, NOPE + VDIM)
    zero = jnp.zeros((KV_LORA, HEADS, LANES - NOPE), F32)
    wk = jnp.concatenate([wkv[:, :, :NOPE], zero], axis=2).reshape(KV_LORA, HEADS * LANES)
    wv = wkv[:, :, NOPE:].reshape(KV_LORA, HEADS // 2, 2, VDIM)
    zv = jnp.zeros((KV_LORA, HEADS // 2, VDIM), F32)
    wv = jnp.stack([jnp.concatenate([wv[:, :, 0], zv], axis=2), jnp.concatenate([zv, wv[:, :, 1]], axis=2)], axis=2)
    wkv_p = jnp.concatenate([wk, wv.reshape(KV_LORA, HEADS * LANES)], axis=1).astype(BF16)
    return w_in_p, wq, wkv_p


def _placement():
    ek = np.zeros((LANES, HEADS * LANES), np.float32)
    for h in range(HEADS):
        ek[np.arange(ROPE), h * LANES + NOPE + np.arange(ROPE)] = 1.0
    return jnp.asarray(ek, BF16)


def _rotary_tables():
    rows = LAT_LEN // GRID_W
    r, col = jnp.meshgrid(jnp.arange(rows, dtype=F32), jnp.arange(GRID_W, dtype=F32), indexing='ij')
    n_freq = ROPE // 4
    freqs = 1.0 / (10000.0 ** (jnp.arange(n_freq, dtype=F32) / n_freq))
    ang = jnp.concatenate([r.reshape(-1)[:, None] * freqs, col.reshape(-1)[:, None] * freqs], axis=-1)
    cos, sin = jnp.cos(ang), jnp.sin(ang)
    theta = 1.0 / (10000.0 ** jnp.linspace(0.0, 1.0, RDK // 2, dtype=F32))
    rang = jnp.arange(LAT_LEN, dtype=F32)[:, None] * theta
    rcos, rsin = jnp.cos(rang), jnp.sin(rang)
    one = lambda w: jnp.ones((LAT_LEN, w), F32)
    zero = lambda w: jnp.zeros((LAT_LEN, w), F32)
    lat = [jnp.concatenate([one(NOPE), cos, cos, one(LANES - NOPE - ROPE)], axis=1),
           jnp.concatenate([zero(NOPE), sin, sin, zero(LANES - NOPE - ROPE)], axis=1),
           jnp.concatenate([cos, cos, zero(LANES - ROPE)], axis=1),
           jnp.concatenate([sin, sin, zero(LANES - ROPE)], axis=1),
           jnp.concatenate([rcos] * 4, axis=1), jnp.concatenate([rsin] * 4, axis=1)]
    ident = [np.ones((TM, LANES), np.float32), np.zeros((TM, LANES), np.float32)]
    ident_k = np.concatenate([np.ones((TM, ROPE), np.float32), np.zeros((TM, LANES - ROPE), np.float32)], axis=1)
    ident = [ident[0], ident[1], ident_k, ident[1], ident[0], ident[1]]
    return [jnp.concatenate([l, jnp.asarray(c)], axis=0) for l, c in zip(lat, ident)]


def _block_diag_states(s0):
    s = jnp.swapaxes(_deinterleave(jnp.swapaxes(s0, -1, -2)), -1, -2)
    s = s.reshape(s0.shape[0], HEADS // 2, 2, RDK, RDK)
    z = jnp.zeros_like(s[:, :, 0])
    top = jnp.concatenate([s[:, :, 0], z], axis=-1)
    bot = jnp.concatenate([z, s[:, :, 1]], axis=-1)
    return jnp.concatenate([top, bot], axis=-2).astype(BF16)


def _unpermute_matrix():
    m = np.zeros((LANES, LANES), np.float32)
    for blk in range(LANES // RDK):
        m[blk * RDK + np.arange(RDK), blk * RDK + _INV_PERM_RDK] = 1.0
    return jnp.asarray(m, BF16)


def kernel(x_prompt, x_sample, c, cache_ckv, cache_kpe, state_ret_fwd, state_ret_bwd, c_ctx, w_mod, b_mod, ln_g, ln_b, w_in_mix, q_a_gain, kv_a_gain, w_q_b, w_kv_b, ret_decay_fwd, ret_decay_bwd, w_out_mix, w_in_conv, conv_w, w_out_conv, ffn_gate, ffn_up, ffn_down, router_w, router_b, exp_gate, exp_up, exp_down):
    x = jnp.concatenate([x_prompt.reshape(TP, D), x_sample.reshape(TS, D)], axis=0)
    cond8 = jnp.concatenate([c_ctx[None], c, jnp.zeros((8 - 1 - N_LAT_SEQ, D), F32)], axis=0)
    mods = _modulation(cond8, w_mod, b_mod)
    mods = jnp.pad(mods.reshape(DEPTH, N_GROUPS, 6, D), ((0, 0), (0, 0), (0, 2), (0, 0)))
    ln = jnp.pad(jnp.concatenate([ln_g, ln_b], axis=1)[:, jnp.array([0, 2, 1, 3])], ((0, 0), (0, 4), (0, 0)))
    tabs = _rotary_tables()
    ek = _placement()
    unperm = _unpermute_matrix()
    bf = lambda a: a.astype(BF16)
    w_out_mix_b, w_in_conv_b, w_out_conv_b = bf(w_out_mix), bf(w_in_conv), bf(w_out_conv)
    ffn_b = (bf(ffn_gate), bf(ffn_up), bf(ffn_down))
    exp_b = (bf(exp_gate), bf(exp_up), bf(exp_down))
    new_ckv, new_kpe, new_sf, new_sb = [], [], [], []
    for layer in range(DEPTH):
        j = layer // 2
        mod_l, ln_l = mods[layer], ln[layer]
        if layer % 2 == 0:
            w_in_p, wq, wkv = _prep_even_weights(w_in_mix[j], w_q_b[j], w_kv_b[j])
            q, k, v, ckv, kpe, rq, rk, rv, rg = _even_in(
                x, mod_l, w_in_p, q_a_gain[j][None], kv_a_gain[j][None], wq, wkv, ek, tabs)
            kpe_c = _pad_cols(_deinterleave(cache_kpe[:, j]).reshape(N_LAT_SEQ * PAST, ROPE), LANES)
            kc, vc = _ctx_kv(cache_ckv[:, j].reshape(N_LAT_SEQ * PAST, KV_LORA), kpe_c, wkv, ek)
            attn = _attention(q, k, v, kc, vc)
            lg = jnp.concatenate([jax.nn.log_sigmoid(ret_decay_fwd[j].astype(F32)),
                                  jax.nn.log_sigmoid(ret_decay_bwd[j].astype(F32))])
            ret, sf, sb = _retention(lg, rq, rk, rv, rg, _block_diag_states(state_ret_fwd[:, j]),
                                     _block_diag_states(state_ret_bwd[:, j]), unperm)
            x = _mix_ffn(x, attn, ret, mod_l, w_out_mix_b, *ffn_b, ln_l, j)
            new_ckv.append(ckv[:TP].reshape(N_PROMPT_SEQ, PROMPT_LEN, KV_LORA))
            new_kpe.append(kpe[:TP, :ROPE].reshape(N_PROMPT_SEQ, PROMPT_LEN, ROPE))
            new_sf.append(sf)
            new_sb.append(sb)
        else:
            b, z = _conv_in(x, mod_l, w_in_conv_b, j)
            cw = jnp.pad(conv_w[j], ((0, 5), (0, 0)))
            rw = _pad_cols(router_w[j], LANES)
            rw_hi = rw.astype(BF16)
            rw = jnp.concatenate([rw_hi, (rw - rw_hi.astype(F32)).astype(BF16)], axis=1)
            rb =jnp.concatenate([router_b[j].astype(F32), jnp.full((LANES - N_EXP,), -1e30, F32)])[None]
            x, h, route = _conv_out(x, b, z, mod_l, cw, w_out_conv_b, ln_l, rw, rb, j)
            dest, row_token, tile_expert, n_valid = _routing_plan(route)
            out_sorted = _moe(tile_expert, n_valid, h[row_token], *exp_b, j)
            x = _combine(x, out_sorted[dest[:, 0]], out_sorted[dest[:, 1]], route, mod_l, ln_l)
    y_prompt = x[:TP].reshape(N_PROMPT_SEQ, PROMPT_LEN, D)
    y_sample = x[TP:].reshape(N_LAT_SEQ, LAT_LEN, D)
    return (y_prompt, y_sample, jnp.stack(new_ckv, axis=1), jnp.stack(new_kpe, axis=1),
            jnp.stack(new_sf, axis=1), jnp.stack(new_sb, axis=1))
```

```python
import functools

import numpy as np
import jax
import jax.numpy as jnp
from jax import lax
from jax.experimental import pallas as pl
from jax.experimental.pallas import tpu as pltpu

F32 = jnp.float32
BF16 = jnp.bfloat16

D = 1024
DEPTH = 4
N_PROMPT_SEQ, PROMPT_LEN = 32, 256
N_LAT_SEQ, LAT_LEN = 2, 2048
PAST = 512
GRID_W = 64
TP = N_PROMPT_SEQ * PROMPT_LEN
TS = N_LAT_SEQ * LAT_LEN
T = TP + TS
HEADS = 8
NOPE, ROPE, VDIM = 64, 32, 64
Q_LORA, KV_LORA = 384, 256
RDK = 64
D_FF = 2816
N_EXP = 8
ALPHA = (2.0 * DEPTH) ** 0.25
Q_SCALE = float((NOPE + ROPE) ** -0.5)
LANES = 128
N_GROUPS = 8

TM = 256
TM_FF = 512
TF = D_FF // 2
TQ = 256
NP_ROWS = 2 * T + N_EXP * TM_FF
VMEM_LIMIT = 56 * 1024 * 1024

IN_COLS = 3072


def _cparams(sem):
    return pltpu.CompilerParams(dimension_semantics=sem, vmem_limit_bytes=VMEM_LIMIT)


def _group_of_tile(i, tm):
    per_seq = LAT_LEN // tm
    return jnp.maximum(i - TP // tm + per_seq, 0) // per_seq


def _bdot(a, b):
    return jnp.dot(a, b, preferred_element_type=F32)


def _wdot(a, w):
    return lax.dot_general(a, w, (((1,), (0,)), ((), ())), preferred_element_type=F32)


def _sigmoid(v):
    return 1.0 / (1.0 + jnp.exp(-v))


def _layer_norm(v, g, b):
    mu = jnp.mean(v, axis=-1, keepdims=True)
    d = v - mu
    var = jnp.mean(d * d, axis=-1, keepdims=True)
    return d * lax.rsqrt(var + 1e-5) * g + b


def _rms(v, g):
    return v * lax.rsqrt(jnp.mean(v * v, axis=-1, keepdims=True) + 1e-6) * g


def _mod_kernel(c_ref, w_ref, b_ref, o_ref):
    c = c_ref[...]
    s = (c * _sigmoid(c)).astype(BF16)
    o_ref[...] = _bdot(s, w_ref[...].astype(BF16)) + b_ref[...]


def _modulation(cond8, w_mod, b_mod):
    tn = 1536
    return pl.pallas_call(
        _mod_kernel,
        grid=(DEPTH, 6 * D // tn),
        in_specs=[pl.BlockSpec((8, D), lambda l, n: (0, 0)),
                  pl.BlockSpec((None, D, tn), lambda l, n: (l, 0, n)),
                  pl.BlockSpec((None, 1, tn), lambda l, n: (l, 0, n))],
        out_specs=pl.BlockSpec((None, 8, tn), lambda l, n: (l, 0, n)),
        out_shape=jax.ShapeDtypeStruct((DEPTH, 8, 6 * D), F32),
        compiler_params=_cparams(("arbitrary", "arbitrary")),
        name="modulation",
    )(cond8, w_mod, b_mod.reshape(DEPTH, 1, 6 * D))


def _swap_halves(a, half):
    n = a.shape[-1]
    lane = lax.broadcasted_iota(jnp.int32, a.shape, 1)
    first = (lane & (2 * half - 1)) < half
    return jnp.where(first, -pltpu.roll(a, n - half, axis=1), pltpu.roll(a, half, axis=1))


def _even_in_kernel(x_ref, mod_ref, w_in_ref, qg_ref, kvg_ref, wq_ref, wkv_ref, ek_ref,
                    cq_ref, sq_ref, ck_ref, sk_ref, cr_ref, sr_ref,
                    q_ref, k_ref, v_ref, ckv_ref, kpe_ref, rq_ref, rk_ref, rv_ref, rg_ref):
    x = x_ref[...]
    h = (x * (1.0 + mod_ref[1:2, :]) + mod_ref[0:1, :]).astype(BF16)
    p = _bdot(h, w_in_ref[...])
    qn = _rms(p[:, 0:Q_LORA], qg_ref[...]).astype(BF16)
    qa = _bdot(qn, wq_ref[...])
    ckv = _rms(p[:, Q_LORA:Q_LORA + KV_LORA], kvg_ref[...])
    ckv_ref[...] = ckv
    kv = _bdot(ckv.astype(BF16), wkv_ref[...])
    v_ref[...] = kv[:, HEADS * LANES:].astype(BF16)
    base = 2688
    kpe_ref[...] = p[:, base:base + LANES]
    ka = p[:, base + LANES:base + 2 * LANES]
    kb = p[:, base + 2 * LANES:base + 3 * LANES]
    rq = p[:, 640:1152]
    rk = p[:, 1152:1664]
    rv_ref[...] = p[:, 1664:2176].astype(BF16)
    rg_ref[...] = p[:, 2176:2688]
    lane = lax.broadcasted_iota(jnp.int32, qa.shape, 1) & (LANES - 1)
    qb = jnp.where(lane < NOPE + ROPE // 2,
                   -pltpu.roll(qa, qa.shape[1] - ROPE // 2, axis=1),
                   pltpu.roll(qa, ROPE // 2, axis=1))
    cq = jnp.concatenate([cq_ref[...]] * HEADS, axis=1)
    sq = jnp.concatenate([sq_ref[...]] * HEADS, axis=1)
    q_ref[...] = ((qa * cq + qb * sq) * Q_SCALE).astype(BF16)
    kpe_rot = ka * ck_ref[...] + kb * sk_ref[...]
    k_ref[...] = (kv[:, :HEADS * LANES] + _bdot(kpe_rot.astype(BF16), ek_ref[...])).astype(BF16)
    cr = jnp.concatenate([cr_ref[...]] * 4, axis=1)
    sr = jnp.concatenate([sr_ref[...]] * 4, axis=1)
    rq_ref[...] = (rq * cr + _swap_halves(rq, RDK // 2) * sr).astype(BF16)
    rk_ref[...] = (rk * cr + _swap_halves(rk, RDK // 2) * sr).astype(BF16)


def _even_in(x, mod_l, w_in, qg, kvg, wq, wkv, ek, tabs):
    tok = lambda w: pl.BlockSpec((TM, w), lambda i: (i, 0))
    full = lambda a: pl.BlockSpec(a.shape, lambda i: (0,) * a.ndim)
    lat_tiles = LAT_LEN // TM
    tab = pl.BlockSpec((TM, LANES), lambda i: (
        jnp.where(i < TP // TM, lat_tiles, jnp.maximum(i - TP // TM, 0) % lat_tiles), 0))
    outs = [(HEADS * LANES, BF16), (HEADS * LANES, BF16), (HEADS * LANES, BF16), (KV_LORA, F32), (LANES, F32),
            (512, BF16), (512, BF16), (512, BF16), (512, F32)]
    return pl.pallas_call(
        _even_in_kernel,
        grid=(T // TM,),
        in_specs=[tok(D), pl.BlockSpec((None, 8, D), lambda i: (_group_of_tile(i, TM), 0, 0)),
                  full(w_in), full(qg), full(kvg), full(wq), full(wkv), full(ek)] + [tab] * 6,
        out_specs=[tok(w) for w, _ in outs],
        out_shape=[jax.ShapeDtypeStruct((T, w), dt) for w, dt in outs],
        compiler_params=_cparams(("parallel",)),
        name="even_in",
    )(x, mod_l, w_in, qg, kvg, wq, wkv, ek, *tabs)


def _ctx_kv_kernel(ckv_ref, kpe_ref, wkv_ref, ek_ref, k_ref, v_ref):
    kv = _bdot(ckv_ref[...].astype(BF16), wkv_ref[...])
    k_ref[...] = (kv[:, :HEADS * LANES] + _bdot(kpe_ref[...].astype(BF16), ek_ref[...])).astype(BF16)
    v_ref[...] = kv[:, HEADS * LANES:].astype(BF16)


def _ctx_kv(ckv_c, kpe_c, wkv, ek):
    n = ckv_c.shape[0]
    full = lambda a: pl.BlockSpec(a.shape, lambda i: (0,) * a.ndim)
    return pl.pallas_call(
        _ctx_kv_kernel,
        grid=(n // PAST,),
        in_specs=[pl.BlockSpec((PAST, KV_LORA), lambda i: (i, 0)), pl.BlockSpec((PAST, LANES), lambda i: (i, 0)),
                  full(wkv), full(ek)],
        out_specs=[pl.BlockSpec((PAST, HEADS * LANES), lambda i: (i, 0))] * 2,
        out_shape=[jax.ShapeDtypeStruct((n, HEADS * LANES), BF16)] * 2,
        compiler_params=_cparams(("parallel",)),
        name="ctx_kv",
    )(ckv_c, kpe_c, wkv, ek)


def _attn_kernel(n_kv, q_ref, *refs):
    k_refs = refs[0:2 * n_kv:2]
    v_refs = refs[1:2 * n_kv:2]
    o_ref = refs[2 * n_kv]
    nt = (((1,), (1,)), ((), ()))
    for pair in range(HEADS // 2):
        acc = None
        for sub in range(2):
            sl = slice((2 * pair + sub) * LANES, (2 * pair + sub + 1) * LANES)
            qh = q_ref[:, sl]
            s = [lax.dot_general(qh, k[:, sl], nt, preferred_element_type=F32) for k in k_refs]
            m = functools.reduce(jnp.maximum, [jnp.max(a, axis=-1, keepdims=True) for a in s])
            e = [jnp.exp(a - m) for a in s]
            den = functools.reduce(jnp.add, [jnp.sum(a, axis=-1, keepdims=True) for a in e])
            o = functools.reduce(jnp.add, [_bdot(a.astype(BF16), v[:, sl]) for a, v in zip(e, v_refs)])
            o = o / den
            acc = o if acc is None else acc + o
        o_ref[:, pair * LANES:(pair + 1) * LANES] = acc.astype(BF16)


def _latent_seq(i):
    return jnp.maximum(i - TP // TQ, 0) // (LAT_LEN // TQ)


def _attn_tiles_kernel(q_ref, kp_ref, vp_ref, kc_ref, vc_ref, kl_ref, vl_ref, o_ref):
    is_latent = pl.program_id(0) >= TP // TQ

    @pl.when(jnp.logical_not(is_latent))
    def _():
        _attn_kernel(1, q_ref, kp_ref, vp_ref, o_ref)

    @pl.when(is_latent)
    def _():
        _attn_kernel(2, q_ref, kc_ref, vc_ref, kl_ref, vl_ref, o_ref)


def _attention(q, k, v, kc, vc):
    w = HEADS * LANES
    tile = lambda i: (i, 0)
    ctx_own = lambda i: (jnp.minimum(i, N_PROMPT_SEQ - 1), 0)
    cache = lambda i: (_latent_seq(i), 0)
    lat_own = lambda i: (TP // LAT_LEN + _latent_seq(i), 0)
    return pl.pallas_call(
        _attn_tiles_kernel,
        grid=(T // TQ,),
        in_specs=[pl.BlockSpec((TQ, w), tile),
                  pl.BlockSpec((PROMPT_LEN, w), ctx_own), pl.BlockSpec((PROMPT_LEN, w), ctx_own),
                  pl.BlockSpec((PAST, w), cache), pl.BlockSpec((PAST, w), cache),
                  pl.BlockSpec((LAT_LEN, w), lat_own), pl.BlockSpec((LAT_LEN, w), lat_own)],
        out_specs=pl.BlockSpec((TQ, HEADS * VDIM), tile),
        out_shape=jax.ShapeDtypeStruct((T, HEADS * VDIM), BF16),
        compiler_params=_cparams(("arbitrary",)),
        name="attention",
    )(q, k, v, kc, vc, k, v)


def _retention_kernel(latent, lg_ref, rq_ref, rk_ref, rv_ref, rg_ref, *refs):
    if latent:
        s0f_ref, s0b_ref, o_ref = refs
        q0 = (pl.program_id(0) - TP // TQ) % (LAT_LEN // TQ) * TQ
        seq_len = LAT_LEN
    else:
        unperm_ref, o_ref, sf_ref, sb_ref = refs
        q0 = 0
        seq_len = PROMPT_LEN
    tq, tk = rq_ref.shape[0], rk_ref.shape[0]
    nt = (((1,), (1,)), ((), ()))
    n_idx = (q0 + lax.broadcasted_iota(jnp.int32, (tq, tk), 0)).astype(F32)
    m_idx = lax.broadcasted_iota(jnp.int32, (tq, tk), 1).astype(F32)
    dist = n_idx - m_idx
    adist = jnp.abs(dist)
    fwd = dist > 0.0
    diag = jnp.where(dist == 0.0, 1.0, 0.0)
    lane = lax.broadcasted_iota(jnp.int32, (1, LANES), 1)
    lo = lane < RDK
    n_col = (q0 + lax.broadcasted_iota(jnp.int32, (tq, 1), 0)).astype(F32)
    m_col = lax.broadcasted_iota(jnp.int32, (tk, 1), 0).astype(F32)
    for pair in range(HEADS // 2):
        sl = slice(pair * LANES, (pair + 1) * LANES)
        qb, kb, vb = rq_ref[:, sl], rk_ref[:, sl], rv_ref[:, sl]
        acc = jnp.zeros((tq, LANES), F32)
        for sub in range(2):
            h = 2 * pair + sub
            lgf, lgb = lg_ref[h], lg_ref[HEADS + h]
            half = lo if sub == 0 else jnp.logical_not(lo)
            qm = jnp.where(half, qb, jnp.zeros_like(qb))
            vm = jnp.where(half, vb, jnp.zeros_like(vb))
            s = lax.dot_general(qm, kb, nt, preferred_element_type=F32)
            w = jnp.exp(adist * jnp.where(fwd, lgf, lgb)) + diag
            acc = acc + _bdot((s * w).astype(BF16), vm)
        lgf_l = jnp.where(lo, lg_ref[2 * pair], lg_ref[2 * pair + 1])
        lgb_l = jnp.where(lo, lg_ref[HEADS + 2 * pair], lg_ref[HEADS + 2 * pair + 1])
        if latent:
            acc = acc + _bdot(qb, s0f_ref[pair]) * jnp.exp((n_col + 1.0) * lgf_l)
            acc = acc + _bdot(qb, s0b_ref[pair]) * jnp.exp((seq_len - n_col) * lgb_l)
        else:
            v_swapped = pltpu.roll(vb.astype(F32), RDK, axis=1).astype(BF16)
            for st_ref, dec in ((sf_ref, jnp.exp((seq_len - 1.0 - m_col) * lgf_l)),
                                (sb_ref, jnp.exp(m_col * lgb_l))):
                kt = (kb.astype(F32) * dec).T.astype(BF16)
                kt = _bdot(unperm_ref[...], kt).astype(BF16)
                st_ref[2 * pair] = _bdot(kt, vb)[0:RDK, 0:RDK]
                st_ref[2 * pair + 1] = _bdot(kt, v_swapped)[RDK:, 0:RDK]
        inv = 1.0 / RDK
        mu = jnp.where(lo, jnp.sum(jnp.where(lo, acc, 0.0), axis=-1, keepdims=True),
                       jnp.sum(jnp.where(lo, 0.0, acc), axis=-1, keepdims=True)) * inv
        dlt = acc - mu
        d2 = dlt * dlt
        var = jnp.where(lo, jnp.sum(jnp.where(lo, d2, 0.0), axis=-1, keepdims=True),
                        jnp.sum(jnp.where(lo, 0.0, d2), axis=-1, keepdims=True)) * inv
        g = rg_ref[:, sl]
        o_ref[:, sl] = (dlt * lax.rsqrt(var + 1e-5) * (g * _sigmoid(g))).astype(BF16)


def _retention_tiles_kernel(n_prev, lg_ref, rq_ref, rkp_ref, rvp_ref, rkl_ref, rvl_ref, rg_ref, s0f_ref, s0b_ref,
                            unperm_ref, *refs):
    prev_refs, (o_ref, sf_ref, sb_ref) = refs[:2 * n_prev], refs[2 * n_prev:]
    is_latent = pl.program_id(0) >= TP // TQ

    @pl.when(jnp.logical_not(is_latent))
    def _():
        for k in range(n_prev):
            sf_ref[k] = prev_refs[2 * k][...]
            sb_ref[k] = prev_refs[2 * k + 1][...]
        _retention_kernel(False, lg_ref, rq_ref, rkp_ref, rvp_ref, rg_ref, unperm_ref, o_ref,
                          sf_ref.at[n_prev], sb_ref.at[n_prev])

    @pl.when(is_latent)
    def _():
        _retention_kernel(True, lg_ref, rq_ref, rkl_ref, rvl_ref, rg_ref, s0f_ref, s0b_ref, o_ref)


def _retention(lg, rq, rk, rv, rg, s0f, s0b, unperm, prev_states):
    w = HEADS * RDK
    n_prev = 0 if prev_states is None else prev_states[0].shape[1]
    tile = lambda i, lg: (i, 0)
    ctx_own = lambda i, lg: (jnp.minimum(i, N_PROMPT_SEQ - 1), 0)
    lat_own = lambda i, lg: (TP // LAT_LEN + _latent_seq(i), 0)
    s0_blk = pl.BlockSpec((None, HEADS // 2, LANES, LANES), lambda i, lg: (_latent_seq(i), 0, 0, 0))
    seq = lambda i, lg: (jnp.minimum(i, N_PROMPT_SEQ - 1), 0, 0, 0, 0)
    st_blk = pl.BlockSpec((None, n_prev + 1, HEADS, RDK, RDK), seq)
    st_shape = jax.ShapeDtypeStruct((N_PROMPT_SEQ, n_prev + 1, HEADS, RDK, RDK), F32)
    prev = () if prev_states is None else tuple(prev_states)
    prev_specs = [pl.BlockSpec((None, None, HEADS, RDK, RDK),
                               functools.partial(lambda k, i, lg: (jnp.minimum(i, N_PROMPT_SEQ - 1), k, 0, 0, 0), k))
                  for k in range(n_prev) for _ in range(2)]
    prev_args = [p for k in range(n_prev) for p in prev]
    return pl.pallas_call(
        functools.partial(_retention_tiles_kernel, n_prev),
        grid_spec=pltpu.PrefetchScalarGridSpec(
            num_scalar_prefetch=1, grid=(T // TQ,),
            in_specs=[pl.BlockSpec((TQ, w), tile),
                      pl.BlockSpec((PROMPT_LEN, w), ctx_own), pl.BlockSpec((PROMPT_LEN, w), ctx_own),
                      pl.BlockSpec((LAT_LEN, w), lat_own), pl.BlockSpec((LAT_LEN, w), lat_own),
                      pl.BlockSpec((TQ, w), tile), s0_blk, s0_blk,
                      pl.BlockSpec((LANES, LANES), lambda i, lg: (0, 0))] + prev_specs,
            out_specs=[pl.BlockSpec((TQ, w), tile), st_blk, st_blk]),
        out_shape=[jax.ShapeDtypeStruct((T, w), BF16), st_shape, st_shape],
        compiler_params=_cparams(("arbitrary",)),
        name="retention",
    )(lg, rq, rk, rv, rk, rv, rg, s0f, s0b, unperm, *prev_args)


def _mix_ffn_kernel(x_ref, a_ref, r_ref, mod_ref, wo_ref, wg_ref, wu_ref, wd_ref, ln_ref, o_ref,
                    x1_scr, h_scr, acc_scr):
    f = pl.program_id(1)

    @pl.when(f == 0)
    def _():
        half = HEADS * VDIM
        y = _bdot(a_ref[...], wo_ref[0:half, :]) + _bdot(r_ref[...], wo_ref[half:, :])
        x1 = _layer_norm(ALPHA * x_ref[...] + mod_ref[2:3, :] * y, ln_ref[0:1, :], ln_ref[1:2, :])
        x1_scr[...] = x1
        h_scr[...] = (x1 * (1.0 + mod_ref[4:5, :]) + mod_ref[3:4, :]).astype(BF16)
        acc_scr[...] = jnp.zeros_like(acc_scr)

    h = h_scr[...]
    g = _bdot(h, wg_ref[...])
    u = _bdot(h, wu_ref[...])
    acc_scr[...] += _bdot((g * _sigmoid(g) * u).astype(BF16), wd_ref[...])

    @pl.when(f == pl.num_programs(1) - 1)
    def _():
        o_ref[...] = _layer_norm(ALPHA * x1_scr[...] + mod_ref[5:6, :] * acc_scr[...], ln_ref[2:3, :], ln_ref[3:4, :])


def _mix_ffn(x, attn, ret, mod_l, w_out, wg, wu, wd, ln, j):
    tok = lambda w: pl.BlockSpec((TM_FF, w), lambda i, f: (i, 0))
    return pl.pallas_call(
        _mix_ffn_kernel,
        grid=(T // TM_FF, D_FF // TF),
        in_specs=[tok(D), tok(HEADS * VDIM), tok(HEADS * RDK),
                  pl.BlockSpec((None, 8, D), lambda i, f: (_group_of_tile(i, TM_FF), 0, 0)),
                  pl.BlockSpec((None,) + w_out.shape[1:], lambda i, f: (j, 0, 0)),
                  pl.BlockSpec((None, D, TF), lambda i, f: (j, 0, f)),
                  pl.BlockSpec((None, D, TF), lambda i, f: (j, 0, f)),
                  pl.BlockSpec((None, TF, D), lambda i, f: (j, f, 0)), pl.BlockSpec((8, D), lambda i, f: (0, 0))],
        out_specs=tok(D),
        out_shape=jax.ShapeDtypeStruct((T, D), F32),
        scratch_shapes=[pltpu.VMEM((TM_FF, D), F32), pltpu.VMEM((TM_FF, D), BF16), pltpu.VMEM((TM_FF, D), F32)],
        compiler_params=_cparams(("parallel", "arbitrary")),
        name="mix_ffn",
    )(x, attn, ret, mod_l, w_out, wg, wu, wd, ln)


def _conv_in_kernel(x_ref, mod_ref, w_ref, b_ref, z_ref):
    h = (x_ref[...] * (1.0 + mod_ref[1:2, :]) + mod_ref[0:1, :]).astype(BF16)
    p = _bdot(h, w_ref[...])
    b_ref[...] = p[:, 0:D]
    z_ref[...] = p[:, D:2 * D] * p[:, 2 * D:3 * D]


def _conv_in(x, mod_l, w_in, j):
    tok = pl.BlockSpec((TM, D), lambda i: (i, 0))
    return pl.pallas_call(
        _conv_in_kernel,
        grid=(T // TM,),
        in_specs=[tok, pl.BlockSpec((None, 8, D), lambda i: (_group_of_tile(i, TM), 0, 0)),
                  pl.BlockSpec((None,) + w_in.shape[1:], lambda i: (j, 0, 0))],
        out_specs=[tok, tok],
        out_shape=[jax.ShapeDtypeStruct((T, D), F32)] * 2,
        compiler_params=_cparams(("parallel",)),
        name="conv_in",
    )(x, mod_l, w_in)


def _conv_out_kernel(x_ref, b_ref, z_ref, zp_ref, zn_ref, mod_ref, cw_ref, w_ref, ln_ref, rw_ref, rb_ref,
                     o_ref, h_ref, route_ref):
    i = pl.program_id(0)
    z = z_ref[...]
    row = lax.broadcasted_iota(jnp.int32, (TM, 1), 0)
    seq_len = jnp.where(i < TP // TM, PROMPT_LEN, LAT_LEN)
    pos = (i * TM + row) & (seq_len - 1)
    prev = jnp.where(row == 0, zp_ref[7:8, :], pltpu.roll(z, 1, axis=0))
    prev = jnp.where(pos == 0, 0.0, prev)
    nxt = jnp.where(row == TM - 1, zn_ref[0:1, :], pltpu.roll(z, TM - 1, axis=0))
    nxt = jnp.where(pos == seq_len - 1, 0.0, nxt)
    y = prev * cw_ref[0:1, :] + z * cw_ref[1:2, :] + nxt * cw_ref[2:3, :]
    t = _bdot((b_ref[...] * y).astype(BF16), w_ref[...])
    x1 = _layer_norm(ALPHA * x_ref[...] + mod_ref[2:3, :] * t, ln_ref[0:1, :], ln_ref[1:2, :])
    o_ref[...] = x1
    h = x1 * (1.0 + mod_ref[4:5, :]) + mod_ref[3:4, :]
    h_ref[...] = h
    h_hi = h.astype(BF16)
    h_lo = (h - h_hi.astype(F32)).astype(BF16)
    both = _bdot(h_hi, rw_ref[...])
    logits = both[:, :LANES] + both[:, LANES:] + _bdot(h_lo, rw_ref[:, :LANES]) + rb_ref[...]
    lane = lax.broadcasted_iota(jnp.int32, logits.shape, 1).astype(F32)
    t1 = jnp.max(logits, axis=-1, keepdims=True)
    i1 = jnp.min(jnp.where(logits == t1, lane, float(LANES)), axis=-1, keepdims=True)
    rest = jnp.where(lane == i1, -jnp.inf, logits)
    t2 = jnp.max(rest, axis=-1, keepdims=True)
    i2 = jnp.min(jnp.where(rest == t2, lane, float(LANES)), axis=-1, keepdims=True)
    e = jnp.exp(t2 - t1)
    den = 1.0 + e
    route_ref[...] = jnp.where(lane == 0.0, i1, jnp.where(lane == 1.0, i2,
                               jnp.where(lane == 2.0, 1.0 / den, jnp.where(lane == 3.0, e / den, 0.0))))


def _conv_out(x, b, z, mod_l, cw, w_out, ln, rw, rb, j):
    tok = pl.BlockSpec((TM, D), lambda i: (i, 0))
    sub = TM // 8
    return pl.pallas_call(
        _conv_out_kernel,
        grid=(T // TM,),
        in_specs=[tok, tok, tok,
                  pl.BlockSpec((8, D), lambda i: (jnp.maximum(i * sub - 1, 0), 0)),
                  pl.BlockSpec((8, D), lambda i: (jnp.minimum((i + 1) * sub, T // 8 - 1), 0)),
                  pl.BlockSpec((None, 8, D), lambda i: (_group_of_tile(i, TM), 0, 0)),
                  pl.BlockSpec((8, D), lambda i: (0, 0)), pl.BlockSpec((None, D, D), lambda i: (j, 0, 0)),
                  pl.BlockSpec((8, D), lambda i: (0, 0)),
                  pl.BlockSpec((D, 2 * LANES), lambda i: (0, 0)), pl.BlockSpec((1, LANES), lambda i: (0, 0))],
        out_specs=[tok, tok, pl.BlockSpec((TM, LANES), lambda i: (i, 0))],
        out_shape=[jax.ShapeDtypeStruct((T, D), F32), jax.ShapeDtypeStruct((T, D), F32),
                   jax.ShapeDtypeStruct((T, LANES), F32)],
        compiler_params=_cparams(("parallel",)),
        name="conv_out",
    )(x, b, z, z, z, mod_l, cw, w_out, ln, rw, rb)


def _moe_up_kernel(te_ref, nv_ref, x_ref, wg_ref, wu_ref, a_ref):
    @pl.when(pl.program_id(1) < nv_ref[0])
    def _():
        h = x_ref[...].astype(BF16)
        g = _wdot(h, wg_ref[...])
        u = _wdot(h, wu_ref[...])
        a_ref[...] = (g * _sigmoid(g) * u).astype(BF16)

    @pl.when(pl.program_id(1) >= nv_ref[0])
    def _():
        a_ref[...] = jnp.zeros_like(a_ref)


def _moe_down_kernel(te_ref, nv_ref, a_ref, wd_ref, o_ref):
    @pl.when(pl.program_id(0) < nv_ref[0])
    def _():
        o_ref[...] = _wdot(a_ref[...], wd_ref[...])

    @pl.when(pl.program_id(0) >= nv_ref[0])
    def _():
        o_ref[...] = jnp.zeros_like(o_ref)


def _moe(tile_expert, n_valid, xs, wg, wu, wd, j):
    n_tiles = NP_ROWS // TM_FF
    act = pl.pallas_call(
        _moe_up_kernel,
        grid_spec=pltpu.PrefetchScalarGridSpec(
            num_scalar_prefetch=2, grid=(D_FF // TF, n_tiles),
            in_specs=[pl.BlockSpec((TM_FF, D), lambda f, i, te, nv: (i, 0)),
                      pl.BlockSpec((None, None, D, TF), lambda f, i, te, nv: (j, te[i], 0, f)),
                      pl.BlockSpec((None, None, D, TF), lambda f, i, te, nv: (j, te[i], 0, f))],
            out_specs=pl.BlockSpec((TM_FF, TF), lambda f, i, te, nv: (i, f))),
        out_shape=jax.ShapeDtypeStruct((NP_ROWS, D_FF), BF16),
        compiler_params=_cparams(("arbitrary", "arbitrary")),
        name="moe_up",
    )(tile_expert, n_valid, xs, wg, wu)
    return pl.pallas_call(
        _moe_down_kernel,
        grid_spec=pltpu.PrefetchScalarGridSpec(
            num_scalar_prefetch=2, grid=(n_tiles,),
            in_specs=[pl.BlockSpec((TM_FF, D_FF), lambda i, te, nv: (i, 0)),
                      pl.BlockSpec((None, None, D_FF, D), lambda i, te, nv: (j, te[i], 0, 0))],
            out_specs=pl.BlockSpec((TM_FF, D), lambda i, te, nv: (i, 0))),
        out_shape=jax.ShapeDtypeStruct((NP_ROWS, D), F32),
        compiler_params=_cparams(("arbitrary",)),
        name="moe_down",
    )(tile_expert, n_valid, act, wd)


def _combine_kernel(x_ref, o0_ref, o1_ref, route_ref, mod_ref, ln_ref, o_ref):
    y = route_ref[:, 2:3] * o0_ref[...] + route_ref[:, 3:4] * o1_ref[...]
    o_ref[...] = _layer_norm(ALPHA * x_ref[...] + mod_ref[5:6, :] * y, ln_ref[2:3, :], ln_ref[3:4, :])


def _combine(x, o0, o1, route, mod_l, ln):
    tok = pl.BlockSpec((TM_FF, D), lambda i: (i, 0))
    return pl.pallas_call(
        _combine_kernel,
        grid=(T // TM_FF,),
        in_specs=[tok, tok, tok, pl.BlockSpec((TM_FF, LANES), lambda i: (i, 0)),
                  pl.BlockSpec((None, 8, D), lambda i: (_group_of_tile(i, TM_FF), 0, 0)),
                  pl.BlockSpec((8, D), lambda i: (0, 0))],
        out_specs=tok,
        out_shape=jax.ShapeDtypeStruct((T, D), F32),
        compiler_params=_cparams(("parallel",)),
        name="moe_combine",
    )(x, o0, o1, route, mod_l, ln)


def _routing_plan(route):
    e = route[:, 0:2].astype(jnp.int32).reshape(-1)
    onehot = (e[:, None] == jnp.arange(N_EXP, dtype=jnp.int32)[None, :]).astype(jnp.int32)
    csum = jnp.cumsum(onehot, axis=0)
    counts = csum[-1]
    rank = jnp.sum((csum - onehot) * onehot, axis=1)
    padded = (counts + TM_FF - 1) // TM_FF * TM_FF
    pend = jnp.cumsum(padded)
    dest = jnp.sum(onehot * (pend - padded)[None, :], axis=1) + rank
    order = jnp.argsort(e, stable=True).astype(jnp.int32)
    rows = jnp.arange(NP_ROWS, dtype=jnp.int32)
    before = (rows[:, None] >= pend[None, :]).astype(jnp.int32)
    row_e = jnp.minimum(jnp.sum(before, axis=1), N_EXP - 1)
    row_cnt = jnp.sum((row_e[:, None] == jnp.arange(N_EXP, dtype=jnp.int32)[None, :]) * counts[None, :], axis=1)
    q = jnp.clip(rows - jnp.sum(before * padded[None, :], axis=1), 0, jnp.maximum(row_cnt - 1, 0))
    src = jnp.minimum(jnp.sum(before * counts[None, :], axis=1) + q, 2 * T - 1)
    row_token = order[src] // 2
    n_valid = (pend[-1] // TM_FF).astype(jnp.int32)
    tile_start = jnp.minimum(jnp.arange(NP_ROWS // TM_FF, dtype=jnp.int32), n_valid - 1) * TM_FF
    tile_expert = jnp.minimum(jnp.sum((tile_start[:, None] >= pend[None, :]).astype(jnp.int32), axis=1), N_EXP - 1)
    return dest.reshape(T, 2), row_token, tile_expert.astype(jnp.int32), n_valid.reshape(1)


_INV_PERM_RDK = np.argsort(np.concatenate([np.arange(0, RDK, 2), np.arange(1, RDK, 2)]))


def _pad_cols(a, width):
    return jnp.pad(a, ((0, 0), (0, width - a.shape[1])))


def _deinterleave(a):
    n = a.shape[-1]
    return jnp.swapaxes(a.reshape(a.shape[:-1] + (n // 2, 2)), -1, -2).reshape(a.shape)


def _prep_even_weights(w_in, w_q_b, w_kv_b):
    o_kpe, o_rq, o_rk, o_rv = 640, 672, 1184, 1696
    heads = lambda a: _deinterleave(a.reshape(D, HEADS, RDK)).reshape(D, HEADS * RDK)
    kpe = w_in[:, o_kpe:o_kpe + ROPE]
    w_in_p = jnp.concatenate([
        w_in[:, :640], heads(w_in[:, o_rq:o_rk]), heads(w_in[:, o_rk:o_rv]) * (RDK ** -0.5), w_in[:, o_rv:],
        _pad_cols(kpe, LANES), _pad_cols(_deinterleave(kpe), LANES),
        _pad_cols(jnp.concatenate([-kpe[:, 1::2], kpe[:, 0::2]], axis=1), LANES)], axis=1).astype(BF16)
    wq = w_q_b.reshape(Q_LORA, HEADS, NOPE + ROPE)
    wq = jnp.concatenate([wq[:, :, :NOPE], _deinterleave(wq[:, :, NOPE:]),
                          jnp.zeros((Q_LORA, HEADS, LANES - NOPE - ROPE), F32)], axis=2)
    wq = wq.reshape(Q_LORA, HEADS * LANES).astype(BF16)
    wkv = w_kv_b.reshape(KV_LORA, HEADS, NOPE + VDIM)
    zero = jnp.zeros((KV_LORA, HEADS, LANES - NOPE), F32)
    wk = jnp.concatenate([wkv[:, :, :NOPE], zero], axis=2).reshape(KV_LORA, HEADS * LANES)
    wv = wkv[:, :, NOPE:].reshape(KV_LORA, HEADS // 2, 2, VDIM)
    zv = jnp.zeros((KV_LORA, HEADS // 2, VDIM), F32)
    wv = jnp.stack([jnp.concatenate([wv[:, :, 0], zv], axis=2), jnp.concatenate([zv, wv[:, :, 1]], axis=2)], axis=2)
    wkv_p = jnp.concatenate([wk, wv.reshape(KV_LORA, HEADS * LANES)], axis=1).astype(BF16)
    return w_in_p, wq, wkv_p


def _placement():
    ek = np.zeros((LANES, HEADS * LANES), np.float32)
    for h in range(HEADS):
        ek[np.arange(ROPE), h * LANES + NOPE + np.arange(ROPE)] = 1.0
    return jnp.asarray(ek, BF16)


def _rotary_tables():
    rows = LAT_LEN // GRID_W
    r, col = jnp.meshgrid(jnp.arange(rows, dtype=F32), jnp.arange(GRID_W, dtype=F32), indexing='ij')
    n_freq = ROPE // 4
    freqs = 1.0 / (10000.0 ** (jnp.arange(n_freq, dtype=F32) / n_freq))
    ang = jnp.concatenate([r.reshape(-1)[:, None] * freqs, col.reshape(-1)[:, None] * freqs], axis=-1)
    cos, sin = jnp.cos(ang), jnp.sin(ang)
    theta = 1.0 / (10000.0 ** jnp.linspace(0.0, 1.0, RDK // 2, dtype=F32))
    rang = jnp.arange(LAT_LEN, dtype=F32)[:, None] * theta
    rcos, rsin = jnp.cos(rang), jnp.sin(rang)
    one = lambda w: jnp.ones((LAT_LEN, w), F32)
    zero = lambda w: jnp.zeros((LAT_LEN, w), F32)
    lat = [jnp.concatenate([one(NOPE), cos, cos, one(LANES - NOPE - ROPE)], axis=1),
           jnp.concatenate([zero(NOPE), sin, sin, zero(LANES - NOPE - ROPE)], axis=1),
           jnp.concatenate([cos, cos, zero(LANES - ROPE)], axis=1),
           jnp.concatenate([sin, sin, zero(LANES - ROPE)], axis=1),
           jnp.concatenate([rcos] * 4, axis=1), jnp.concatenate([rsin] * 4, axis=1)]
    ident = [np.ones((TM, LANES), np.float32), np.zeros((TM, LANES), np.float32)]
    ident_k = np.concatenate([np.ones((TM, ROPE), np.float32), np.zeros((TM, LANES - ROPE), np.float32)], axis=1)
    ident = [ident[0], ident[1], ident_k, ident[1], ident[0], ident[1]]
    return [jnp.concatenate([l, jnp.asarray(c)], axis=0) for l, c in zip(lat, ident)]


def _block_diag_states(s0):
    s = jnp.swapaxes(_deinterleave(jnp.swapaxes(s0, -1, -2)), -1, -2)
    s = s.reshape(s0.shape[0], HEADS // 2, 2, RDK, RDK)
    z = jnp.zeros_like(s[:, :, 0])
    top = jnp.concatenate([s[:, :, 0], z], axis=-1)
    bot = jnp.concatenate([z, s[:, :, 1]], axis=-1)
    return jnp.concatenate([top, bot], axis=-2).astype(BF16)


def _unpermute_matrix():
    m = np.zeros((LANES, LANES), np.float32)
    for blk in range(LANES // RDK):
        m[blk * RDK + np.arange(RDK), blk * RDK + _INV_PERM_RDK] = 1.0
    return jnp.asarray(m, BF16)


def kernel(x_prompt, x_sample, c, cache_ckv, cache_kpe, state_ret_fwd, state_ret_bwd, c_ctx, w_mod, b_mod, ln_g, ln_b, w_in_mix, q_a_gain, kv_a_gain, w_q_b, w_kv_b, ret_decay_fwd, ret_decay_bwd, w_out_mix, w_in_conv, conv_w, w_out_conv, ffn_gate, ffn_up, ffn_down, router_w, router_b, exp_gate, exp_up, exp_down):
    x = jnp.concatenate([x_prompt.reshape(TP, D), x_sample.reshape(TS, D)], axis=0)
    cond8 = jnp.concatenate([c_ctx[None], c, jnp.zeros((8 - 1 - N_LAT_SEQ, D), F32)], axis=0)
    mods = _modulation(cond8, w_mod, b_mod)
    mods = jnp.pad(mods.reshape(DEPTH, N_GROUPS, 6, D), ((0, 0), (0, 0), (0, 2), (0, 0)))
    ln = jnp.pad(jnp.concatenate([ln_g, ln_b], axis=1)[:, jnp.array([0, 2, 1, 3])], ((0, 0), (0, 4), (0, 0)))
    tabs = _rotary_tables()
    ek = _placement()
    unperm = _unpermute_matrix()
    bf = lambda a: a.astype(BF16)
    w_out_mix_b, w_in_conv_b, w_out_conv_b = bf(w_out_mix), bf(w_in_conv), bf(w_out_conv)
    ffn_b = (bf(ffn_gate), bf(ffn_up), bf(ffn_down))
    exp_b = (exp_gate, exp_up, exp_down)
    new_ckv, new_kpe, states = [], [], None
    for layer in range(DEPTH):
        j = layer // 2
        mod_l, ln_l = mods[layer], ln[layer]
        if layer % 2 == 0:
            w_in_p, wq, wkv = _prep_even_weights(w_in_mix[j], w_q_b[j], w_kv_b[j])
            q, k, v, ckv, kpe, rq, rk, rv, rg = _even_in(
                x, mod_l, w_in_p, q_a_gain[j][None], kv_a_gain[j][None], wq, wkv, ek, tabs)
            kpe_c = _pad_cols(_deinterleave(cache_kpe[:, j]).reshape(N_LAT_SEQ * PAST, ROPE), LANES)
            kc, vc = _ctx_kv(cache_ckv[:, j].reshape(N_LAT_SEQ * PAST, KV_LORA), kpe_c, wkv, ek)
            attn = _attention(q, k, v, kc, vc)
            lg = jnp.concatenate([jax.nn.log_sigmoid(ret_decay_fwd[j].astype(F32)),
                                  jax.nn.log_sigmoid(ret_decay_bwd[j].astype(F32))])
            ret, sf, sb = _retention(lg, rq, rk, rv, rg, _block_diag_states(state_ret_fwd[:, j]),
                                     _block_diag_states(state_ret_bwd[:, j]), unperm, states)
            states = (sf, sb)
            x = _mix_ffn(x, attn, ret, mod_l, w_out_mix_b, *ffn_b, ln_l, j)
            new_ckv.append(ckv[:TP].reshape(N_PROMPT_SEQ, PROMPT_LEN, KV_LORA))
            new_kpe.append(kpe[:TP, :ROPE].reshape(N_PROMPT_SEQ, PROMPT_LEN, ROPE))
        else:
            b, z = _conv_in(x, mod_l, w_in_conv_b, j)
            cw = jnp.pad(conv_w[j], ((0, 5), (0, 0)))
            rw = _pad_cols(router_w[j], LANES)
            rw_hi = rw.astype(BF16)
            rw = jnp.concatenate([rw_hi, (rw - rw_hi.astype(F32)).astype(BF16)], axis=1)
            rb =jnp.concatenate([router_b[j].astype(F32), jnp.full((LANES - N_EXP,), -1e30, F32)])[None]
            x, h, route = _conv_out(x, b, z, mod_l, cw, w_out_conv_b, ln_l, rw, rb, j)
            dest, row_token, tile_expert, n_valid = _routing_plan(route)
            out_sorted = _moe(tile_expert, n_valid, h[row_token], *exp_b, j)
            x = _combine(x, out_sorted[dest[:, 0]], out_sorted[dest[:, 1]], route, mod_l, ln_l)
    y_prompt = x[:TP].reshape(N_PROMPT_SEQ, PROMPT_LEN, D)
    y_sample = x[TP:].reshape(N_LAT_SEQ, LAT_LEN, D)
    return (y_prompt, y_sample, jnp.stack(new_ckv, axis=1), jnp.stack(new_kpe, axis=1), states[0], states[1])
```

```python
import functools

import numpy as np
import jax
import jax.numpy as jnp
from jax import lax
from jax.experimental import pallas as pl
from jax.experimental.pallas import tpu as pltpu

F32 = jnp.float32
BF16 = jnp.bfloat16

D = 1024
DEPTH = 4
N_PROMPT_SEQ, PROMPT_LEN = 32, 256
N_LAT_SEQ, LAT_LEN = 2, 2048
PAST = 512
GRID_W = 64
TP = N_PROMPT_SEQ * PROMPT_LEN
TS = N_LAT_SEQ * LAT_LEN
T = TP + TS
HEADS = 8
NOPE, ROPE, VDIM = 64, 32, 64
Q_LORA, KV_LORA = 384, 256
RDK = 64
D_FF = 2816
N_EXP = 8
ALPHA = (2.0 * DEPTH) ** 0.25
Q_SCALE = float((NOPE + ROPE) ** -0.5)
LANES = 128
N_GROUPS = 8

TM = 256
TM_FF = 512
TF = D_FF // 2
TQ = 256
NP_ROWS = 2 * T + N_EXP * TM_FF
VMEM_LIMIT = 56 * 1024 * 1024

IN_COLS = 3072


def _cparams(sem):
    return pltpu.CompilerParams(dimension_semantics=sem, vmem_limit_bytes=VMEM_LIMIT)


def _group_of_tile(i, tm):
    per_seq = LAT_LEN // tm
    return jnp.maximum(i - TP // tm + per_seq, 0) // per_seq


def _bdot(a, b):
    return jnp.dot(a, b, preferred_element_type=F32)


def _wdot(a, w):
    return lax.dot_general(a, w, (((1,), (0,)), ((), ())), preferred_element_type=F32)


def _sigmoid(v):
    return 1.0 / (1.0 + jnp.exp(-v))


def _layer_norm(v, g, b):
    mu = jnp.mean(v, axis=-1, keepdims=True)
    d = v - mu
    var = jnp.mean(d * d, axis=-1, keepdims=True)
    return d * lax.rsqrt(var + 1e-5) * g + b


def _rms(v, g):
    return v * lax.rsqrt(jnp.mean(v * v, axis=-1, keepdims=True) + 1e-6) * g


def _mod_kernel(c_ref, w_ref, b_ref, o_ref):
    c = c_ref[...]
    s = (c * _sigmoid(c)).astype(BF16)
    o_ref[...] = _bdot(s, w_ref[...].astype(BF16)) + b_ref[...]


def _modulation(cond8, w_mod, b_mod):
    tn = 1536
    return pl.pallas_call(
        _mod_kernel,
        grid=(DEPTH, 6 * D // tn),
        in_specs=[pl.BlockSpec((8, D), lambda l, n: (0, 0)),
                  pl.BlockSpec((None, D, tn), lambda l, n: (l, 0, n)),
                  pl.BlockSpec((None, 1, tn), lambda l, n: (l, 0, n))],
        out_specs=pl.BlockSpec((None, 8, tn), lambda l, n: (l, 0, n)),
        out_shape=jax.ShapeDtypeStruct((DEPTH, 8, 6 * D), F32),
        compiler_params=_cparams(("arbitrary", "arbitrary")),
        name="modulation",
    )(cond8, w_mod, b_mod.reshape(DEPTH, 1, 6 * D))


def _swap_halves(a, half):
    n = a.shape[-1]
    lane = lax.broadcasted_iota(jnp.int32, a.shape, 1)
    first = (lane & (2 * half - 1)) < half
    return jnp.where(first, -pltpu.roll(a, n - half, axis=1), pltpu.roll(a, half, axis=1))


def _tile_rows(x_refs, tm):
    if len(x_refs) == 1:
        return x_refs[0][...]
    return jnp.where(pl.program_id(0) >= TP // tm, x_refs[1][...], x_refs[0][...])


def _tile_specs(x, tm):
    if not isinstance(x, tuple):
        return (x,), [pl.BlockSpec((tm, D), lambda i, *_: (i, 0))]
    return x, [pl.BlockSpec((tm, D), lambda i, *_: (jnp.minimum(i, TP // tm - 1), 0)),
               pl.BlockSpec((tm, D), lambda i, *_: (jnp.maximum(i - TP // tm, 0), 0))]


def _even_in_kernel(n_x, n_prev, *refs):
    x_refs, refs = refs[:n_x], refs[n_x:]
    (mod_ref, w_in_ref, qg_ref, kvg_ref, wq_ref, wkv_ref, ek_ref,
     cq_ref, sq_ref, ck_ref, sk_ref, cr_ref, sr_ref), refs = refs[:13], refs[13:]
    prev_refs, refs = refs[:2 * n_prev], refs[2 * n_prev:]
    q_ref, k_ref, v_ref, ckv_ref, kpe_ref, rq_ref, rk_ref, rv_ref, rg_ref = refs
    x = _tile_rows(x_refs, TM)
    h = (x * (1.0 + mod_ref[1:2, :]) + mod_ref[0:1, :]).astype(BF16)
    p = _bdot(h, w_in_ref[...])
    qn = _rms(p[:, 0:Q_LORA], qg_ref[...]).astype(BF16)
    qa = _bdot(qn, wq_ref[...])
    ckv = _rms(p[:, Q_LORA:Q_LORA + KV_LORA], kvg_ref[...])
    kv = _bdot(ckv.astype(BF16), wkv_ref[...])
    v_ref[...] = kv[:, HEADS * LANES:].astype(BF16)
    base = 2688

    @pl.when(pl.program_id(0) < TP // TM)
    def _():
        for k in range(n_prev):
            ckv_ref[k] = prev_refs[2 * k][...]
            kpe_ref[k] = prev_refs[2 * k + 1][...]
        ckv_ref[n_prev] = ckv
        kpe_ref[n_prev] = p[:, base:base + ROPE]

    ka = p[:, base + LANES:base + 2 * LANES]
    kb = p[:, base + 2 * LANES:base + 3 * LANES]
    rq = p[:, 640:1152]
    rk = p[:, 1152:1664]
    rv_ref[...] = p[:, 1664:2176].astype(BF16)
    rg_ref[...] = p[:, 2176:2688]
    lane = lax.broadcasted_iota(jnp.int32, qa.shape, 1) & (LANES - 1)
    qb = jnp.where(lane < NOPE + ROPE // 2,
                   -pltpu.roll(qa, qa.shape[1] - ROPE // 2, axis=1),
                   pltpu.roll(qa, ROPE // 2, axis=1))
    cq = jnp.concatenate([cq_ref[...]] * HEADS, axis=1)
    sq = jnp.concatenate([sq_ref[...]] * HEADS, axis=1)
    q_ref[...] = ((qa * cq + qb * sq) * Q_SCALE).astype(BF16)
    kpe_rot = ka * ck_ref[...] + kb * sk_ref[...]
    k_ref[...] = (kv[:, :HEADS * LANES] + _bdot(kpe_rot.astype(BF16), ek_ref[...])).astype(BF16)
    cr = jnp.concatenate([cr_ref[...]] * 4, axis=1)
    sr = jnp.concatenate([sr_ref[...]] * 4, axis=1)
    rq_ref[...] = (rq * cr + _swap_halves(rq, RDK // 2) * sr).astype(BF16)
    rk_ref[...] = (rk * cr + _swap_halves(rk, RDK // 2) * sr).astype(BF16)


def _even_in(x, mod_l, w_in, qg, kvg, wq, wkv, ek, tabs, prev_cache):
    assert TM == PROMPT_LEN
    x_args, x_specs = _tile_specs(x, TM)
    n_prev = 0 if prev_cache is None else prev_cache[0].shape[1]
    tok = lambda w: pl.BlockSpec((TM, w), lambda i: (i, 0))
    full = lambda a: pl.BlockSpec(a.shape, lambda i: (0,) * a.ndim)
    lat_tiles = LAT_LEN // TM
    tab = pl.BlockSpec((TM, LANES), lambda i: (
        jnp.where(i < TP // TM, lat_tiles, jnp.maximum(i - TP // TM, 0) % lat_tiles), 0))
    seq = lambda i: (jnp.minimum(i, N_PROMPT_SEQ - 1), 0, 0, 0)
    seq_k = lambda k, i: (jnp.minimum(i, N_PROMPT_SEQ - 1), k, 0, 0)
    prev_args, prev_specs = [], []
    for k in range(n_prev):
        for a, w in zip(prev_cache, (KV_LORA, ROPE)):
            prev_args.append(a)
            prev_specs.append(pl.BlockSpec((None, None, PROMPT_LEN, w), functools.partial(seq_k, k)))
    tok_outs = lambda dims: ([tok(w) for w, _ in dims], [jax.ShapeDtypeStruct((T, w), dt) for w, dt in dims])
    qkv_specs, qkv_shapes = tok_outs([(HEADS * LANES, BF16)] * 3)
    ret_specs, ret_shapes = tok_outs([(512, BF16), (512, BF16), (512, BF16), (512, F32)])
    cache_specs = [pl.BlockSpec((None, n_prev + 1, PROMPT_LEN, w), seq) for w in (KV_LORA, ROPE)]
    cache_shapes = [jax.ShapeDtypeStruct((N_PROMPT_SEQ, n_prev + 1, PROMPT_LEN, w), F32) for w in (KV_LORA, ROPE)]
    q, k, v, ckv, kpe, rq, rk, rv, rg = pl.pallas_call(
        functools.partial(_even_in_kernel, len(x_args), n_prev),
        grid=(T // TM,),
        in_specs=x_specs + [pl.BlockSpec((None, 8, D), lambda i: (_group_of_tile(i, TM), 0, 0)),
                            full(w_in), full(qg), full(kvg), full(wq), full(wkv), full(ek)] + [tab] * 6 + prev_specs,
        out_specs=qkv_specs + cache_specs + ret_specs,
        out_shape=qkv_shapes + cache_shapes + ret_shapes,
        compiler_params=_cparams(("arbitrary",)),
        name="even_in",
    )(*x_args, mod_l, w_in, qg, kvg, wq, wkv, ek, *tabs, *prev_args)
    return q, k, v, (ckv, kpe), rq, rk, rv, rg


def _ctx_kv_kernel(ckv_ref, kpe_ref, wkv_ref, ek_ref, k_ref, v_ref):
    kv = _bdot(ckv_ref[...].astype(BF16), wkv_ref[...])
    k_ref[...] = (kv[:, :HEADS * LANES] + _bdot(kpe_ref[...].astype(BF16), ek_ref[...])).astype(BF16)
    v_ref[...] = kv[:, HEADS * LANES:].astype(BF16)


def _ctx_kv(ckv_c, kpe_c, wkv, ek):
    n = ckv_c.shape[0]
    full = lambda a: pl.BlockSpec(a.shape, lambda i: (0,) * a.ndim)
    return pl.pallas_call(
        _ctx_kv_kernel,
        grid=(n // PAST,),
        in_specs=[pl.BlockSpec((PAST, KV_LORA), lambda i: (i, 0)), pl.BlockSpec((PAST, LANES), lambda i: (i, 0)),
                  full(wkv), full(ek)],
        out_specs=[pl.BlockSpec((PAST, HEADS * LANES), lambda i: (i, 0))] * 2,
        out_shape=[jax.ShapeDtypeStruct((n, HEADS * LANES), BF16)] * 2,
        compiler_params=_cparams(("parallel",)),
        name="ctx_kv",
    )(ckv_c, kpe_c, wkv, ek)


def _attn_kernel(n_kv, q_ref, *refs):
    k_refs = refs[0:2 * n_kv:2]
    v_refs = refs[1:2 * n_kv:2]
    o_ref = refs[2 * n_kv]
    nt = (((1,), (1,)), ((), ()))
    for pair in range(HEADS // 2):
        acc = None
        for sub in range(2):
            sl = slice((2 * pair + sub) * LANES, (2 * pair + sub + 1) * LANES)
            qh = q_ref[:, sl]
            s = [lax.dot_general(qh, k[:, sl], nt, preferred_element_type=F32) for k in k_refs]
            m = functools.reduce(jnp.maximum, [jnp.max(a, axis=-1, keepdims=True) for a in s])
            e = [jnp.exp(a - m) for a in s]
            den = functools.reduce(jnp.add, [jnp.sum(a, axis=-1, keepdims=True) for a in e])
            o = functools.reduce(jnp.add, [_bdot(a.astype(BF16), v[:, sl]) for a, v in zip(e, v_refs)])
            o = o / den
            acc = o if acc is None else acc + o
        o_ref[:, pair * LANES:(pair + 1) * LANES] = acc.astype(BF16)


def _latent_seq(i):
    return jnp.maximum(i - TP // TQ, 0) // (LAT_LEN // TQ)


def _attn_tiles_kernel(q_ref, kp_ref, vp_ref, kc_ref, vc_ref, kl_ref, vl_ref, o_ref):
    is_latent = pl.program_id(0) >= TP // TQ

    @pl.when(jnp.logical_not(is_latent))
    def _():
        _attn_kernel(1, q_ref, kp_ref, vp_ref, o_ref)

    @pl.when(is_latent)
    def _():
        _attn_kernel(2, q_ref, kc_ref, vc_ref, kl_ref, vl_ref, o_ref)


def _attention(q, k, v, kc, vc):
    w = HEADS * LANES
    tile = lambda i: (i, 0)
    ctx_own = lambda i: (jnp.minimum(i, N_PROMPT_SEQ - 1), 0)
    cache = lambda i: (_latent_seq(i), 0)
    lat_own = lambda i: (TP // LAT_LEN + _latent_seq(i), 0)
    return pl.pallas_call(
        _attn_tiles_kernel,
        grid=(T // TQ,),
        in_specs=[pl.BlockSpec((TQ, w), tile),
                  pl.BlockSpec((PROMPT_LEN, w), ctx_own), pl.BlockSpec((PROMPT_LEN, w), ctx_own),
                  pl.BlockSpec((PAST, w), cache), pl.BlockSpec((PAST, w), cache),
                  pl.BlockSpec((LAT_LEN, w), lat_own), pl.BlockSpec((LAT_LEN, w), lat_own)],
        out_specs=pl.BlockSpec((TQ, HEADS * VDIM), tile),
        out_shape=jax.ShapeDtypeStruct((T, HEADS * VDIM), BF16),
        compiler_params=_cparams(("arbitrary",)),
        name="attention",
    )(q, k, v, kc, vc, k, v)


def _ret_prefix_kernel(lg_ref, rk_ref, rv_ref, s0f_ref, s0b_ref, pf_ref, qb_ref, sf_scr, sb_scr):
    s = pl.program_id(0)
    n_tiles, per_seq = TS // TQ, LAT_LEN // TQ
    row = lax.broadcasted_iota(jnp.int32, (LANES, 1), 0)
    lane = lax.broadcasted_iota(jnp.int32, (1, LANES), 1)
    top, lo = row < RDK, lane < RDK
    same_head = top == lo
    m_col = lax.broadcasted_iota(jnp.int32, (TQ, 1), 0).astype(F32)

    def scan_step(first, lg_off, s0_ref, scr, out_ref, pos):
        @pl.when(first)
        def _():
            scr[...] = s0_ref[...]

        for pair in range(HEADS // 2):
            sl = slice(pair * LANES, (pair + 1) * LANES)
            lg_even, lg_odd = lg_ref[lg_off + 2 * pair], lg_ref[lg_off + 2 * pair + 1]
            dec = jnp.exp(pos * jnp.where(lo, lg_even, lg_odd))
            local = _bdot((rk_ref[:, sl].astype(F32) * dec).T.astype(BF16), rv_ref[:, sl])
            out_ref[pair] = scr[pair].astype(BF16)
            scr[pair] = scr[pair] * jnp.exp(float(TQ) * jnp.where(top, lg_even, lg_odd)) + jnp.where(same_head, local, 0.0)

    @pl.when(s < n_tiles)
    def _():
        scan_step(s % per_seq == 0, 0, s0f_ref, sf_scr, pf_ref, TQ - 1.0 - m_col)

    @pl.when(s >= n_tiles)
    def _():
        scan_step((2 * n_tiles - 1 - s) % per_seq == per_seq - 1, HEADS, s0b_ref, sb_scr, qb_ref, m_col)


def _ret_prefix(lg, rk, rv, s0f, s0b):
    n_tiles, per_seq = TS // TQ, LAT_LEN // TQ
    tile_of = lambda s: jnp.where(s < n_tiles, s, 2 * n_tiles - 1 - s)
    st = pl.BlockSpec((None, HEADS // 2, LANES, LANES), lambda s, lg: (tile_of(s) // per_seq, 0, 0, 0))
    kv = pl.BlockSpec((TQ, HEADS * RDK), lambda s, lg: (TP // TQ + tile_of(s), 0))
    pf_blk = pl.BlockSpec((None, HEADS // 2, LANES, LANES), lambda s, lg: (jnp.minimum(s, n_tiles - 1), 0, 0, 0))
    qb_blk = pl.BlockSpec((None, HEADS // 2, LANES, LANES),
                          lambda s, lg: (jnp.minimum(2 * n_tiles - 1 - s, n_tiles - 1), 0, 0, 0))
    shape = jax.ShapeDtypeStruct((n_tiles, HEADS // 2, LANES, LANES), BF16)
    return pl.pallas_call(
        _ret_prefix_kernel,
        grid_spec=pltpu.PrefetchScalarGridSpec(
            num_scalar_prefetch=1, grid=(2 * n_tiles,),
            in_specs=[kv, kv, st, st], out_specs=[pf_blk, qb_blk],
            scratch_shapes=[pltpu.VMEM((HEADS // 2, LANES, LANES), F32)] * 2),
        out_shape=[shape, shape],
        compiler_params=_cparams(("arbitrary",)),
        name="ret_prefix",
    )(lg, rk, rv, s0f, s0b)


def _retention_kernel(latent, lg_ref, rq_ref, rk_ref, rv_ref, rg_ref, *refs):
    q0 = 0
    seq_len = rq_ref.shape[0]
    if latent:
        s0f_ref, s0b_ref, o_ref = refs
    else:
        unperm_ref, o_ref, sf_ref, sb_ref = refs
    tq, tk = rq_ref.shape[0], rk_ref.shape[0]
    nt = (((1,), (1,)), ((), ()))
    n_idx = (q0 + lax.broadcasted_iota(jnp.int32, (tq, tk), 0)).astype(F32)
    m_idx = lax.broadcasted_iota(jnp.int32, (tq, tk), 1).astype(F32)
    dist = n_idx - m_idx
    adist = jnp.abs(dist)
    fwd = dist > 0.0
    diag = jnp.where(dist == 0.0, 1.0, 0.0)
    lane = lax.broadcasted_iota(jnp.int32, (1, LANES), 1)
    lo = lane < RDK
    n_col = (q0 + lax.broadcasted_iota(jnp.int32, (tq, 1), 0)).astype(F32)
    m_col = lax.broadcasted_iota(jnp.int32, (tk, 1), 0).astype(F32)
    for pair in range(HEADS // 2):
        sl = slice(pair * LANES, (pair + 1) * LANES)
        qb, kb, vb = rq_ref[:, sl], rk_ref[:, sl], rv_ref[:, sl]
        acc = jnp.zeros((tq, LANES), F32)
        for sub in range(2):
            h = 2 * pair + sub
            lgf, lgb = lg_ref[h], lg_ref[HEADS + h]
            half = lo if sub == 0 else jnp.logical_not(lo)
            qm = jnp.where(half, qb, jnp.zeros_like(qb))
            vm = jnp.where(half, vb, jnp.zeros_like(vb))
            s = lax.dot_general(qm, kb, nt, preferred_element_type=F32)
            w = jnp.exp(adist * jnp.where(fwd, lgf, lgb)) + diag
            acc = acc + _bdot((s * w).astype(BF16), vm)
        lgf_l = jnp.where(lo, lg_ref[2 * pair], lg_ref[2 * pair + 1])
        lgb_l = jnp.where(lo, lg_ref[HEADS + 2 * pair], lg_ref[HEADS + 2 * pair + 1])
        if latent:
            acc = acc + _bdot(qb, s0f_ref[pair]) * jnp.exp((n_col + 1.0) * lgf_l)
            acc = acc + _bdot(qb, s0b_ref[pair]) * jnp.exp((seq_len - n_col) * lgb_l)
        else:
            v_swapped = pltpu.roll(vb.astype(F32), RDK, axis=1).astype(BF16)
            for st_ref, dec in ((sf_ref, jnp.exp((seq_len - 1.0 - m_col) * lgf_l)),
                                (sb_ref, jnp.exp(m_col * lgb_l))):
                kt = (kb.astype(F32) * dec).T.astype(BF16)
                kt = _bdot(unperm_ref[...], kt).astype(BF16)
                st_ref[2 * pair] = _bdot(kt, vb)[0:RDK, 0:RDK]
                st_ref[2 * pair + 1] = _bdot(kt, v_swapped)[RDK:, 0:RDK]
        inv = 1.0 / RDK
        mu = jnp.where(lo, jnp.sum(jnp.where(lo, acc, 0.0), axis=-1, keepdims=True),
                       jnp.sum(jnp.where(lo, 0.0, acc), axis=-1, keepdims=True)) * inv
        dlt = acc - mu
        d2 = dlt * dlt
        var = jnp.where(lo, jnp.sum(jnp.where(lo, d2, 0.0), axis=-1, keepdims=True),
                        jnp.sum(jnp.where(lo, 0.0, d2), axis=-1, keepdims=True)) * inv
        g = rg_ref[:, sl]
        o_ref[:, sl] = (dlt * lax.rsqrt(var + 1e-5) * (g * _sigmoid(g))).astype(BF16)


def _retention_tiles_kernel(n_prev, lg_ref, rq_ref, rk_ref, rv_ref, rg_ref, pf_ref, qb_ref, unperm_ref, *refs):
    prev_refs, (o_ref, sf_ref, sb_ref) = refs[:2 * n_prev], refs[2 * n_prev:]
    is_latent = pl.program_id(0) >= TP // TQ

    @pl.when(jnp.logical_not(is_latent))
    def _():
        for k in range(n_prev):
            sf_ref[k] = prev_refs[2 * k][...]
            sb_ref[k] = prev_refs[2 * k + 1][...]
        _retention_kernel(False, lg_ref, rq_ref, rk_ref, rv_ref, rg_ref, unperm_ref, o_ref,
                          sf_ref.at[n_prev], sb_ref.at[n_prev])

    @pl.when(is_latent)
    def _():
        _retention_kernel(True, lg_ref, rq_ref, rk_ref, rv_ref, rg_ref, pf_ref, qb_ref, o_ref)


def _retention(lg, rq, rk, rv, rg, s0f, s0b, unperm, prev_states):
    w = HEADS * RDK
    n_prev = 0 if prev_states is None else prev_states[0].shape[1]
    pf, qb = _ret_prefix(lg, rk, rv, s0f, s0b)
    tile = lambda i, lg: (i, 0)
    s0_blk = pl.BlockSpec((None, HEADS // 2, LANES, LANES), lambda i, lg: (jnp.maximum(i - TP // TQ, 0), 0, 0, 0))
    seq = lambda i, lg: (jnp.minimum(i, N_PROMPT_SEQ - 1), 0, 0, 0, 0)
    st_blk = pl.BlockSpec((None, n_prev + 1, HEADS, RDK, RDK), seq)
    st_shape = jax.ShapeDtypeStruct((N_PROMPT_SEQ, n_prev + 1, HEADS, RDK, RDK), F32)
    prev = () if prev_states is None else tuple(prev_states)
    prev_specs = [pl.BlockSpec((None, None, HEADS, RDK, RDK),
                               functools.partial(lambda k, i, lg: (jnp.minimum(i, N_PROMPT_SEQ - 1), k, 0, 0, 0), k))
                  for k in range(n_prev) for _ in range(2)]
    prev_args = [p for k in range(n_prev) for p in prev]
    return pl.pallas_call(
        functools.partial(_retention_tiles_kernel, n_prev),
        grid_spec=pltpu.PrefetchScalarGridSpec(
            num_scalar_prefetch=1, grid=(T // TQ,),
            in_specs=[pl.BlockSpec((TQ, w), tile)] * 4 + [s0_blk, s0_blk,
                      pl.BlockSpec((LANES, LANES), lambda i, lg: (0, 0))] + prev_specs,
            out_specs=[pl.BlockSpec((TQ, w), tile), st_blk, st_blk]),
        out_shape=[jax.ShapeDtypeStruct((T, w), BF16), st_shape, st_shape],
        compiler_params=_cparams(("arbitrary",)),
        name="retention",
    )(lg, rq, rk, rv, rg, pf, qb, unperm, *prev_args)


def _mix_ffn_kernel(n_x, *refs):
    x_refs, refs = refs[:n_x], refs[n_x:]
    a_ref, r_ref, mod_ref, wo_ref, wg_ref, wu_ref, wd_ref, ln_ref, o_ref, x1_scr, h_scr, acc_scr = refs
    f = pl.program_id(1)

    @pl.when(f == 0)
    def _():
        half = HEADS * VDIM
        y = _bdot(a_ref[...], wo_ref[0:half, :]) + _bdot(r_ref[...], wo_ref[half:, :])
        x1 = _layer_norm(ALPHA * _tile_rows(x_refs, TM_FF) + mod_ref[2:3, :] * y, ln_ref[0:1, :], ln_ref[1:2, :])
        x1_scr[...] = x1
        h_scr[...] = (x1 * (1.0 + mod_ref[4:5, :]) + mod_ref[3:4, :]).astype(BF16)
        acc_scr[...] = jnp.zeros_like(acc_scr)

    h = h_scr[...]
    g = _bdot(h, wg_ref[...])
    u = _bdot(h, wu_ref[...])
    acc_scr[...] += _bdot((g * _sigmoid(g) * u).astype(BF16), wd_ref[...])

    @pl.when(f == pl.num_programs(1) - 1)
    def _():
        o_ref[...] = _layer_norm(ALPHA * x1_scr[...] + mod_ref[5:6, :] * acc_scr[...], ln_ref[2:3, :], ln_ref[3:4, :])


def _mix_ffn(x, attn, ret, mod_l, w_out, wg, wu, wd, ln, j):
    x_args, x_specs = _tile_specs(x, TM_FF)
    tok = lambda w: pl.BlockSpec((TM_FF, w), lambda i, f: (i, 0))
    return pl.pallas_call(
        functools.partial(_mix_ffn_kernel, len(x_args)),
        grid=(T // TM_FF, D_FF // TF),
        in_specs=x_specs + [tok(HEADS * VDIM), tok(HEADS * RDK),
                  pl.BlockSpec((None, 8, D), lambda i, f: (_group_of_tile(i, TM_FF), 0, 0)),
                  pl.BlockSpec((None,) + w_out.shape[1:], lambda i, f: (j, 0, 0)),
                  pl.BlockSpec((None, D, TF), lambda i, f: (j, 0, f)),
                  pl.BlockSpec((None, D, TF), lambda i, f: (j, 0, f)),
                  pl.BlockSpec((None, TF, D), lambda i, f: (j, f, 0)), pl.BlockSpec((8, D), lambda i, f: (0, 0))],
        out_specs=tok(D),
        out_shape=jax.ShapeDtypeStruct((T, D), F32),
        scratch_shapes=[pltpu.VMEM((TM_FF, D), F32), pltpu.VMEM((TM_FF, D), BF16), pltpu.VMEM((TM_FF, D), F32)],
        compiler_params=_cparams(("parallel", "arbitrary")),
        name="mix_ffn",
    )(*x_args, attn, ret, mod_l, w_out, wg, wu, wd, ln)


def _conv_in_kernel(x_ref, mod_ref, w_ref, b_ref, z_ref):
    h = (x_ref[...] * (1.0 + mod_ref[1:2, :]) + mod_ref[0:1, :]).astype(BF16)
    p = _bdot(h, w_ref[...])
    b_ref[...] = p[:, 0:D]
    z_ref[...] = p[:, D:2 * D] * p[:, 2 * D:3 * D]


def _conv_in(x, mod_l, w_in, j):
    tok = pl.BlockSpec((TM, D), lambda i: (i, 0))
    return pl.pallas_call(
        _conv_in_kernel,
        grid=(T // TM,),
        in_specs=[tok, pl.BlockSpec((None, 8, D), lambda i: (_group_of_tile(i, TM), 0, 0)),
                  pl.BlockSpec((None,) + w_in.shape[1:], lambda i: (j, 0, 0))],
        out_specs=[tok, tok],
        out_shape=[jax.ShapeDtypeStruct((T, D), F32)] * 2,
        compiler_params=_cparams(("parallel",)),
        name="conv_in",
    )(x, mod_l, w_in)


def _conv_out_kernel(x_ref, b_ref, z_ref, zp_ref, zn_ref, mod_ref, cw_ref, w_ref, ln_ref, rw_ref, rb_ref,
                     o_ref, h_ref, route_ref):
    i = pl.program_id(0)
    z = z_ref[...]
    row = lax.broadcasted_iota(jnp.int32, (TM, 1), 0)
    seq_len = jnp.where(i < TP // TM, PROMPT_LEN, LAT_LEN)
    pos = (i * TM + row) & (seq_len - 1)
    prev = jnp.where(row == 0, zp_ref[7:8, :], pltpu.roll(z, 1, axis=0))
    prev = jnp.where(pos == 0, 0.0, prev)
    nxt = jnp.where(row == TM - 1, zn_ref[0:1, :], pltpu.roll(z, TM - 1, axis=0))
    nxt = jnp.where(pos == seq_len - 1, 0.0, nxt)
    y = prev * cw_ref[0:1, :] + z * cw_ref[1:2, :] + nxt * cw_ref[2:3, :]
    t = _bdot((b_ref[...] * y).astype(BF16), w_ref[...])
    x1 = _layer_norm(ALPHA * x_ref[...] + mod_ref[2:3, :] * t, ln_ref[0:1, :], ln_ref[1:2, :])
    o_ref[...] = x1
    h = x1 * (1.0 + mod_ref[4:5, :]) + mod_ref[3:4, :]
    h_ref[...] = h
    h_hi = h.astype(BF16)
    h_lo = (h - h_hi.astype(F32)).astype(BF16)
    both = _bdot(h_hi, rw_ref[...])
    logits = both[:, :LANES] + both[:, LANES:] + _bdot(h_lo, rw_ref[:, :LANES]) + rb_ref[...]
    lane = lax.broadcasted_iota(jnp.int32, logits.shape, 1).astype(F32)
    t1 = jnp.max(logits, axis=-1, keepdims=True)
    i1 = jnp.min(jnp.where(logits == t1, lane, float(LANES)), axis=-1, keepdims=True)
    rest = jnp.where(lane == i1, -jnp.inf, logits)
    t2 = jnp.max(rest, axis=-1, keepdims=True)
    i2 = jnp.min(jnp.where(rest == t2, lane, float(LANES)), axis=-1, keepdims=True)
    e = jnp.exp(t2 - t1)
    den = 1.0 + e
    route_ref[...] = jnp.where(lane == 0.0, i1, jnp.where(lane == 1.0, i2,
                               jnp.where(lane == 2.0, 1.0 / den, jnp.where(lane == 3.0, e / den, 0.0))))


def _conv_out(x, b, z, mod_l, cw, w_out, ln, rw, rb, j):
    tok = pl.BlockSpec((TM, D), lambda i: (i, 0))
    sub = TM // 8
    return pl.pallas_call(
        _conv_out_kernel,
        grid=(T // TM,),
        in_specs=[tok, tok, tok,
                  pl.BlockSpec((8, D), lambda i: (jnp.maximum(i * sub - 1, 0), 0)),
                  pl.BlockSpec((8, D), lambda i: (jnp.minimum((i + 1) * sub, T // 8 - 1), 0)),
                  pl.BlockSpec((None, 8, D), lambda i: (_group_of_tile(i, TM), 0, 0)),
                  pl.BlockSpec((8, D), lambda i: (0, 0)), pl.BlockSpec((None, D, D), lambda i: (j, 0, 0)),
                  pl.BlockSpec((8, D), lambda i: (0, 0)),
                  pl.BlockSpec((D, 2 * LANES), lambda i: (0, 0)), pl.BlockSpec((1, LANES), lambda i: (0, 0))],
        out_specs=[tok, tok, pl.BlockSpec((TM, LANES), lambda i: (i, 0))],
        out_shape=[jax.ShapeDtypeStruct((T, D), F32), jax.ShapeDtypeStruct((T, D), F32),
                   jax.ShapeDtypeStruct((T, LANES), F32)],
        compiler_params=_cparams(("parallel",)),
        name="conv_out",
    )(x, b, z, z, z, mod_l, cw, w_out, ln, rw, rb)


def _moe_up_kernel(te_ref, nv_ref, x_ref, wg_ref, wu_ref, a_ref):
    @pl.when(pl.program_id(1) < nv_ref[0])
    def _():
        h = x_ref[...].astype(BF16)
        g = _wdot(h, wg_ref[...])
        u = _wdot(h, wu_ref[...])
        a_ref[...] = (g * _sigmoid(g) * u).astype(BF16)

    @pl.when(pl.program_id(1) >= nv_ref[0])
    def _():
        a_ref[...] = jnp.zeros_like(a_ref)


def _moe_down_kernel(te_ref, nv_ref, a_ref, wd_ref, o_ref):
    @pl.when(pl.program_id(0) < nv_ref[0])
    def _():
        o_ref[...] = _wdot(a_ref[...], wd_ref[...])

    @pl.when(pl.program_id(0) >= nv_ref[0])
    def _():
        o_ref[...] = jnp.zeros_like(o_ref)


def _moe(tile_expert, n_valid, xs, wg, wu, wd, j):
    n_tiles = NP_ROWS // TM_FF
    act = pl.pallas_call(
        _moe_up_kernel,
        grid_spec=pltpu.PrefetchScalarGridSpec(
            num_scalar_prefetch=2, grid=(D_FF // TF, n_tiles),
            in_specs=[pl.BlockSpec((TM_FF, D), lambda f, i, te, nv: (i, 0)),
                      pl.BlockSpec((None, None, D, TF), lambda f, i, te, nv: (j, te[i], 0, f)),
                      pl.BlockSpec((None, None, D, TF), lambda f, i, te, nv: (j, te[i], 0, f))],
            out_specs=pl.BlockSpec((TM_FF, TF), lambda f, i, te, nv: (i, f))),
        out_shape=jax.ShapeDtypeStruct((NP_ROWS, D_FF), BF16),
        compiler_params=_cparams(("arbitrary", "arbitrary")),
        name="moe_up",
    )(tile_expert, n_valid, xs, wg, wu)
    return pl.pallas_call(
        _moe_down_kernel,
        grid_spec=pltpu.PrefetchScalarGridSpec(
            num_scalar_prefetch=2, grid=(n_tiles,),
            in_specs=[pl.BlockSpec((TM_FF, D_FF), lambda i, te, nv: (i, 0)),
                      pl.BlockSpec((None, None, D_FF, D), lambda i, te, nv: (j, te[i], 0, 0))],
            out_specs=pl.BlockSpec((TM_FF, D), lambda i, te, nv: (i, 0))),
        out_shape=jax.ShapeDtypeStruct((NP_ROWS, D), F32),
        compiler_params=_cparams(("arbitrary",)),
        name="moe_down",
    )(tile_expert, n_valid, act, wd)


def _combine_kernel(split, x_ref, o0_ref, o1_ref, route_ref, mod_ref, ln_ref, *o_refs):
    y = route_ref[:, 2:3] * o0_ref[...] + route_ref[:, 3:4] * o1_ref[...]
    out = _layer_norm(ALPHA * x_ref[...] + mod_ref[5:6, :] * y, ln_ref[2:3, :], ln_ref[3:4, :])
    if not split:
        o_refs[0][...] = out
        return
    is_latent = pl.program_id(0) >= TP // TM_FF

    @pl.when(jnp.logical_not(is_latent))
    def _():
        o_refs[0][...] = out

    @pl.when(is_latent)
    def _():
        o_refs[1][...] = out


def _combine(x, o0, o1, route, mod_l, ln, split):
    tok = pl.BlockSpec((TM_FF, D), lambda i: (i, 0))
    if split:
        out_specs = [pl.BlockSpec((TM_FF, D), lambda i: (jnp.minimum(i, TP // TM_FF - 1), 0)),
                     pl.BlockSpec((TM_FF, D), lambda i: (jnp.maximum(i - TP // TM_FF, 0), 0))]
        out_shape = [jax.ShapeDtypeStruct((TP, D), F32), jax.ShapeDtypeStruct((TS, D), F32)]
    else:
        out_specs, out_shape = tok, jax.ShapeDtypeStruct((T, D), F32)
    return pl.pallas_call(
        functools.partial(_combine_kernel, split),
        grid=(T // TM_FF,),
        in_specs=[tok, tok, tok, pl.BlockSpec((TM_FF, LANES), lambda i: (i, 0)),
                  pl.BlockSpec((None, 8, D), lambda i: (_group_of_tile(i, TM_FF), 0, 0)),
                  pl.BlockSpec((8, D), lambda i: (0, 0))],
        out_specs=out_specs,
        out_shape=out_shape,
        compiler_params=_cparams(("arbitrary",)),
        name="moe_combine",
    )(x, o0, o1, route, mod_l, ln)


def _routing_plan(route):
    e = route[:, 0:2].astype(jnp.int32).reshape(-1)
    onehot = (e[:, None] == jnp.arange(N_EXP, dtype=jnp.int32)[None, :]).astype(jnp.int32)
    csum = jnp.cumsum(onehot, axis=0)
    counts = csum[-1]
    rank = jnp.sum((csum - onehot) * onehot, axis=1)
    padded = (counts + TM_FF - 1) // TM_FF * TM_FF
    pend = jnp.cumsum(padded)
    dest = jnp.sum(onehot * (pend - padded)[None, :], axis=1) + rank
    order = jnp.argsort(e, stable=True).astype(jnp.int32)
    rows = jnp.arange(NP_ROWS, dtype=jnp.int32)
    before = (rows[:, None] >= pend[None, :]).astype(jnp.int32)
    row_e = jnp.minimum(jnp.sum(before, axis=1), N_EXP - 1)
    row_cnt = jnp.sum((row_e[:, None] == jnp.arange(N_EXP, dtype=jnp.int32)[None, :]) * counts[None, :], axis=1)
    q = jnp.clip(rows - jnp.sum(before * padded[None, :], axis=1), 0, jnp.maximum(row_cnt - 1, 0))
    src = jnp.minimum(jnp.sum(before * counts[None, :], axis=1) + q, 2 * T - 1)
    row_token = order[src] // 2
    n_valid = (pend[-1] // TM_FF).astype(jnp.int32)
    tile_start = jnp.minimum(jnp.arange(NP_ROWS // TM_FF, dtype=jnp.int32), n_valid - 1) * TM_FF
    tile_expert = jnp.minimum(jnp.sum((tile_start[:, None] >= pend[None, :]).astype(jnp.int32), axis=1), N_EXP - 1)
    return dest.reshape(T, 2), row_token, tile_expert.astype(jnp.int32), n_valid.reshape(1)


_INV_PERM_RDK = np.argsort(np.concatenate([np.arange(0, RDK, 2), np.arange(1, RDK, 2)]))


def _pad_cols(a, width):
    return jnp.pad(a, ((0, 0), (0, width - a.shape[1])))


def _deinterleave(a):
    n = a.shape[-1]
    return jnp.swapaxes(a.reshape(a.shape[:-1] + (n // 2, 2)), -1, -2).reshape(a.shape)


def _prep_even_weights(w_in, w_q_b, w_kv_b):
    o_kpe, o_rq, o_rk, o_rv = 640, 672, 1184, 1696
    heads = lambda a: _deinterleave(a.reshape(D, HEADS, RDK)).reshape(D, HEADS * RDK)
    kpe = w_in[:, o_kpe:o_kpe + ROPE]
    w_in_p = jnp.concatenate([
        w_in[:, :640], heads(w_in[:, o_rq:o_rk]), heads(w_in[:, o_rk:o_rv]) * (RDK ** -0.5), w_in[:, o_rv:],
        _pad_cols(kpe, LANES), _pad_cols(_deinterleave(kpe), LANES),
        _pad_cols(jnp.concatenate([-kpe[:, 1::2], kpe[:, 0::2]], axis=1), LANES)], axis=1).astype(BF16)
    wq = w_q_b.reshape(Q_LORA, HEADS, NOPE + ROPE)
    wq = jnp.concatenate([wq[:, :, :NOPE], _deinterleave(wq[:, :, NOPE:]),
                          jnp.zeros((Q_LORA, HEADS, LANES - NOPE - ROPE), F32)], axis=2)
    wq = wq.reshape(Q_LORA, HEADS * LANES).astype(BF16)
    wkv = w_kv_b.reshape(KV_LORA, HEADS, NOPE + VDIM)
    zero = jnp.zeros((KV_LORA, HEADS, LANES - NOPE), F32)
    wk = jnp.concatenate([wkv[:, :, :NOPE], zero], axis=2).reshape(KV_LORA, HEADS * LANES)
    wv = wkv[:, :, NOPE:].reshape(KV_LORA, HEADS // 2, 2, VDIM)
    zv = jnp.zeros((KV_LORA, HEADS // 2, VDIM), F32)
    wv = jnp.stack([jnp.concatenate([wv[:, :, 0], zv], axis=2), jnp.concatenate([zv, wv[:, :, 1]], axis=2)], axis=2)
    wkv_p = jnp.concatenate([wk, wv.reshape(KV_LORA, HEADS * LANES)], axis=1).astype(BF16)
    return w_in_p, wq, wkv_p


def _placement():
    ek = np.zeros((LANES, HEADS * LANES), np.float32)
    for h in range(HEADS):
        ek[np.arange(ROPE), h * LANES + NOPE + np.arange(ROPE)] = 1.0
    return jnp.asarray(ek, BF16)


def _rotary_tables():
    rows = LAT_LEN // GRID_W
    r, col = jnp.meshgrid(jnp.arange(rows, dtype=F32), jnp.arange(GRID_W, dtype=F32), indexing='ij')
    n_freq = ROPE // 4
    freqs = 1.0 / (10000.0 ** (jnp.arange(n_freq, dtype=F32) / n_freq))
    ang = jnp.concatenate([r.reshape(-1)[:, None] * freqs, col.reshape(-1)[:, None] * freqs], axis=-1)
    cos, sin = jnp.cos(ang), jnp.sin(ang)
    theta = 1.0 / (10000.0 ** jnp.linspace(0.0, 1.0, RDK // 2, dtype=F32))
    rang = jnp.arange(LAT_LEN, dtype=F32)[:, None] * theta
    rcos, rsin = jnp.cos(rang), jnp.sin(rang)
    one = lambda w: jnp.ones((LAT_LEN, w), F32)
    zero = lambda w: jnp.zeros((LAT_LEN, w), F32)
    lat = [jnp.concatenate([one(NOPE), cos, cos, one(LANES - NOPE - ROPE)], axis=1),
           jnp.concatenate([zero(NOPE), sin, sin, zero(LANES - NOPE - ROPE)], axis=1),
           jnp.concatenate([cos, cos, zero(LANES - ROPE)], axis=1),
           jnp.concatenate([sin, sin, zero(LANES - ROPE)], axis=1),
           jnp.concatenate([rcos] * 4, axis=1), jnp.concatenate([rsin] * 4, axis=1)]
    ident = [np.ones((TM, LANES), np.float32), np.zeros((TM, LANES), np.float32)]
    ident_k = np.concatenate([np.ones((TM, ROPE), np.float32), np.zeros((TM, LANES - ROPE), np.float32)], axis=1)
    ident = [ident[0], ident[1], ident_k, ident[1], ident[0], ident[1]]
    return [jnp.concatenate([l, jnp.asarray(c)], axis=0) for l, c in zip(lat, ident)]


def _block_diag_states(s0):
    s = jnp.swapaxes(_deinterleave(jnp.swapaxes(s0, -1, -2)), -1, -2)
    s = s.reshape(s0.shape[0], HEADS // 2, 2, RDK, RDK)
    z = jnp.zeros_like(s[:, :, 0])
    top = jnp.concatenate([s[:, :, 0], z], axis=-1)
    bot = jnp.concatenate([z, s[:, :, 1]], axis=-1)
    return jnp.concatenate([top, bot], axis=-2)


def _unpermute_matrix():
    m = np.zeros((LANES, LANES), np.float32)
    for blk in range(LANES // RDK):
        m[blk * RDK + np.arange(RDK), blk * RDK + _INV_PERM_RDK] = 1.0
    return jnp.asarray(m, BF16)


def kernel(x_prompt, x_sample, c, cache_ckv, cache_kpe, state_ret_fwd, state_ret_bwd, c_ctx, w_mod, b_mod, ln_g, ln_b, w_in_mix, q_a_gain, kv_a_gain, w_q_b, w_kv_b, ret_decay_fwd, ret_decay_bwd, w_out_mix, w_in_conv, conv_w, w_out_conv, ffn_gate, ffn_up, ffn_down, router_w, router_b, exp_gate, exp_up, exp_down):
    x = (x_prompt.reshape(TP, D), x_sample.reshape(TS, D))
    cond8 = jnp.concatenate([c_ctx[None], c, jnp.zeros((8 - 1 - N_LAT_SEQ, D), F32)], axis=0)
    mods = _modulation(cond8, w_mod, b_mod)
    mods = jnp.pad(mods.reshape(DEPTH, N_GROUPS, 6, D), ((0, 0), (0, 0), (0, 2), (0, 0)))
    ln = jnp.pad(jnp.concatenate([ln_g, ln_b], axis=1)[:, jnp.array([0, 2, 1, 3])], ((0, 0), (0, 4), (0, 0)))
    tabs = _rotary_tables()
    ek = _placement()
    unperm = _unpermute_matrix()
    bf = lambda a: a.astype(BF16)
    w_out_mix_b, w_in_conv_b, w_out_conv_b = bf(w_out_mix), bf(w_in_conv), bf(w_out_conv)
    ffn_b = (bf(ffn_gate), bf(ffn_up), bf(ffn_down))
    exp_b = (exp_gate, exp_up, exp_down)
    cache, states = None, None
    for layer in range(DEPTH):
        j = layer // 2
        mod_l, ln_l = mods[layer], ln[layer]
        if layer % 2 == 0:
            w_in_p, wq, wkv = _prep_even_weights(w_in_mix[j], w_q_b[j], w_kv_b[j])
            q, k, v, cache, rq, rk, rv, rg = _even_in(
                x, mod_l, w_in_p, q_a_gain[j][None], kv_a_gain[j][None], wq, wkv, ek, tabs, cache)
            kpe_c = _pad_cols(_deinterleave(cache_kpe[:, j]).reshape(N_LAT_SEQ * PAST, ROPE), LANES)
            kc, vc = _ctx_kv(cache_ckv[:, j].reshape(N_LAT_SEQ * PAST, KV_LORA), kpe_c, wkv, ek)
            attn = _attention(q, k, v, kc, vc)
            lg = jnp.concatenate([jax.nn.log_sigmoid(ret_decay_fwd[j].astype(F32)),
                                  jax.nn.log_sigmoid(ret_decay_bwd[j].astype(F32))])
            ret, sf, sb = _retention(lg, rq, rk, rv, rg, _block_diag_states(state_ret_fwd[:, j]),
                                     _block_diag_states(state_ret_bwd[:, j]), unperm, states)
            states = (sf, sb)
            x = _mix_ffn(x, attn, ret, mod_l, w_out_mix_b, *ffn_b, ln_l, j)
        else:
            b, z = _conv_in(x, mod_l, w_in_conv_b, j)
            cw = jnp.pad(conv_w[j], ((0, 5), (0, 0)))
            rw = _pad_cols(router_w[j], LANES)
            rw_hi = rw.astype(BF16)
            rw = jnp.concatenate([rw_hi, (rw - rw_hi.astype(F32)).astype(BF16)], axis=1)
            rb = jnp.concatenate([router_b[j].astype(F32), jnp.full((LANES - N_EXP,), -1e30, F32)])[None]
            x, h, route = _conv_out(x, b, z, mod_l, cw, w_out_conv_b, ln_l, rw, rb, j)
            dest, row_token, tile_expert, n_valid = _routing_plan(route)
            out_sorted = _moe(tile_expert, n_valid, h[row_token], *exp_b, j)
            x = _combine(x, out_sorted[dest[:, 0]], out_sorted[dest[:, 1]], route, mod_l, ln_l,
                         split=layer == DEPTH - 1)
    y_prompt = x[0].reshape(N_PROMPT_SEQ, PROMPT_LEN, D)
    y_sample = x[1].reshape(N_LAT_SEQ, LAT_LEN, D)
    return (y_prompt, y_sample, cache[0], cache[1], states[0], states[1])
```

```python
import functools

import numpy as np
import jax
import jax.numpy as jnp
from jax import lax
from jax.experimental import pallas as pl
from jax.experimental.pallas import tpu as pltpu

F32 = jnp.float32
BF16 = jnp.bfloat16

D = 1024
DEPTH = 4
N_PROMPT_SEQ, PROMPT_LEN = 32, 256
N_LAT_SEQ, LAT_LEN = 2, 2048
PAST = 512
GRID_W = 64
TP = N_PROMPT_SEQ * PROMPT_LEN
TS = N_LAT_SEQ * LAT_LEN
T = TP + TS
HEADS = 8
NOPE, ROPE, VDIM = 64, 32, 64
Q_LORA, KV_LORA = 384, 256
RDK = 64
D_FF = 2816
N_EXP = 8
ALPHA = (2.0 * DEPTH) ** 0.25
Q_SCALE = float((NOPE + ROPE) ** -0.5)
LANES = 128
N_GROUPS = 8

TM = 256
TM_FF = 512
TF = D_FF // 2
TQ = 256
NP_ROWS = 2 * T + N_EXP * TM_FF
X_PARTS = 4
VMEM_LIMIT = 56 * 1024 * 1024

IN_COLS = 3072


def _cparams(sem):
    return pltpu.CompilerParams(dimension_semantics=sem, vmem_limit_bytes=VMEM_LIMIT)


def _group_of_tile(i, tm):
    per_seq = LAT_LEN // tm
    return jnp.maximum(i - TP // tm + per_seq, 0) // per_seq


def _bdot(a, b):
    return jnp.dot(a, b, preferred_element_type=F32)


def _wdot(a, w):
    return lax.dot_general(a, w, (((1,), (0,)), ((), ())), preferred_element_type=F32)


def _sigmoid(v):
    return 1.0 / (1.0 + jnp.exp(-v))


def _layer_norm(v, g, b):
    mu = jnp.mean(v, axis=-1, keepdims=True)
    d = v - mu
    var = jnp.mean(d * d, axis=-1, keepdims=True)
    return d * lax.rsqrt(var + 1e-5) * g + b


def _rms(v, g):
    return v * lax.rsqrt(jnp.mean(v * v, axis=-1, keepdims=True) + 1e-6) * g


def _mod_kernel(c_ref, w_ref, b_ref, o_ref):
    c = c_ref[...]
    s = (c * _sigmoid(c)).astype(BF16)
    o_ref[...] = _bdot(s, w_ref[...].astype(BF16)) + b_ref[...]


def _modulation(cond8, w_mod, b_mod):
    tn = 1536
    return pl.pallas_call(
        _mod_kernel,
        grid=(DEPTH, 6 * D // tn),
        in_specs=[pl.BlockSpec((8, D), lambda l, n: (0, 0)),
                  pl.BlockSpec((None, D, tn), lambda l, n: (l, 0, n)),
                  pl.BlockSpec((None, 1, tn), lambda l, n: (l, 0, n))],
        out_specs=pl.BlockSpec((None, 8, tn), lambda l, n: (l, 0, n)),
        out_shape=jax.ShapeDtypeStruct((DEPTH, 8, 6 * D), F32),
        compiler_params=_cparams(("arbitrary", "arbitrary")),
        name="modulation",
    )(cond8, w_mod, b_mod.reshape(DEPTH, 1, 6 * D))


def _swap_halves(a, half):
    n = a.shape[-1]
    lane = lax.broadcasted_iota(jnp.int32, a.shape, 1)
    first = (lane & (2 * half - 1)) < half
    return jnp.where(first, -pltpu.roll(a, n - half, axis=1), pltpu.roll(a, half, axis=1))


def _tile_rows(x_refs, tm):
    if len(x_refs) == 1:
        return x_refs[0][...]
    return jnp.where(pl.program_id(0) >= TP // tm, x_refs[1][...], x_refs[0][...])


def _tile_specs(x, tm):
    if not isinstance(x, tuple):
        return (x,), [pl.BlockSpec((tm, D), lambda i, *_: (i, 0))]
    return x, [pl.BlockSpec((tm, D), lambda i, *_: (jnp.minimum(i, TP // tm - 1), 0)),
               pl.BlockSpec((tm, D), lambda i, *_: (jnp.maximum(i - TP // tm, 0), 0))]


def _even_in_kernel(n_x, n_prev, *refs):
    x_refs, refs = refs[:n_x], refs[n_x:]
    (mod_ref, w_in_ref, qg_ref, kvg_ref, wq_ref, wkv_ref, ek_ref,
     cq_ref, sq_ref, ck_ref, sk_ref, cr_ref, sr_ref), refs = refs[:13], refs[13:]
    prev_refs, refs = refs[:2 * n_prev], refs[2 * n_prev:]
    q_ref, k_ref, v_ref, ckv_ref, kpe_ref, rq_ref, rk_ref, rv_ref, rg_ref = refs
    x = _tile_rows(x_refs, TM)
    h = (x * (1.0 + mod_ref[1:2, :]) + mod_ref[0:1, :]).astype(BF16)
    p = _bdot(h, w_in_ref[...])
    qn = _rms(p[:, 0:Q_LORA], qg_ref[...]).astype(BF16)
    qa = _bdot(qn, wq_ref[...])
    ckv = _rms(p[:, Q_LORA:Q_LORA + KV_LORA], kvg_ref[...])
    kv = _bdot(ckv.astype(BF16), wkv_ref[...])
    v_ref[...] = kv[:, HEADS * LANES:].astype(BF16)
    base = 2688
    ka = p[:, base + LANES:base + 2 * LANES]
    kb = p[:, base + 2 * LANES:base + 3 * LANES]
    rq = p[:, 640:1152]
    rk = p[:, 1152:1664]
    rv_ref[...] = p[:, 1664:2176].astype(BF16)
    rg_ref[...] = p[:, 2176:2688]
    lane = lax.broadcasted_iota(jnp.int32, qa.shape, 1) & (LANES - 1)
    qb = jnp.where(lane < NOPE + ROPE // 2,
                   -pltpu.roll(qa, qa.shape[1] - ROPE // 2, axis=1),
                   pltpu.roll(qa, ROPE // 2, axis=1))
    cq = jnp.concatenate([cq_ref[...]] * HEADS, axis=1)
    sq = jnp.concatenate([sq_ref[...]] * HEADS, axis=1)
    q_ref[...] = ((qa * cq + qb * sq) * Q_SCALE).astype(BF16)
    kpe_rot = ka * ck_ref[...] + kb * sk_ref[...]
    k_ref[...] = (kv[:, :HEADS * LANES] + _bdot(kpe_rot.astype(BF16), ek_ref[...])).astype(BF16)
    cr = jnp.concatenate([cr_ref[...]] * 4, axis=1)
    sr = jnp.concatenate([sr_ref[...]] * 4, axis=1)
    rq_ref[...] = (rq * cr + _swap_halves(rq, RDK // 2) * sr).astype(BF16)
    rk_ref[...] = (rk * cr + _swap_halves(rk, RDK // 2) * sr).astype(BF16)

    @pl.when(pl.program_id(0) < TP // TM)
    def _():
        for k in range(n_prev):
            ckv_ref[k] = prev_refs[2 * k][...]
            kpe_ref[k] = prev_refs[2 * k + 1][...]
        ckv_ref[n_prev] = ckv
        kpe_ref[n_prev] = p[:, base:base + ROPE]


def _even_in(x, mod_l, w_in, qg, kvg, wq, wkv, ek, tabs, prev_cache):
    assert TM == PROMPT_LEN
    x_args, x_specs = _tile_specs(x, TM)
    n_prev = 0 if prev_cache is None else prev_cache[0].shape[1]
    tok = lambda w: pl.BlockSpec((TM, w), lambda i: (i, 0))
    full = lambda a: pl.BlockSpec(a.shape, lambda i: (0,) * a.ndim)
    lat_tiles = LAT_LEN // TM
    tab = pl.BlockSpec((TM, LANES), lambda i: (
        jnp.where(i < TP // TM, lat_tiles, jnp.maximum(i - TP // TM, 0) % lat_tiles), 0))
    seq = lambda i: (jnp.minimum(i, N_PROMPT_SEQ - 1), 0, 0, 0)
    seq_k = lambda k, i: (jnp.minimum(i, N_PROMPT_SEQ - 1), k, 0, 0)
    prev_args, prev_specs = [], []
    for k in range(n_prev):
        for a, w in zip(prev_cache, (KV_LORA, ROPE)):
            prev_args.append(a)
            prev_specs.append(pl.BlockSpec((None, None, PROMPT_LEN, w), functools.partial(seq_k, k)))
    tok_outs = lambda dims: ([tok(w) for w, _ in dims], [jax.ShapeDtypeStruct((T, w), dt) for w, dt in dims])
    qkv_specs, qkv_shapes = tok_outs([(HEADS * LANES, BF16)] * 3)
    ret_specs, ret_shapes = tok_outs([(512, BF16), (512, BF16), (512, BF16), (512, F32)])
    cache_specs = [pl.BlockSpec((None, n_prev + 1, PROMPT_LEN, w), seq) for w in (KV_LORA, ROPE)]
    cache_shapes = [jax.ShapeDtypeStruct((N_PROMPT_SEQ, n_prev + 1, PROMPT_LEN, w), F32) for w in (KV_LORA, ROPE)]
    q, k, v, ckv, kpe, rq, rk, rv, rg = pl.pallas_call(
        functools.partial(_even_in_kernel, len(x_args), n_prev),
        grid=(T // TM,),
        in_specs=x_specs + [pl.BlockSpec((None, 8, D), lambda i: (_group_of_tile(i, TM), 0, 0)),
                            full(w_in), full(qg), full(kvg), full(wq), full(wkv), full(ek)] + [tab] * 6 + prev_specs,
        out_specs=qkv_specs + cache_specs + ret_specs,
        out_shape=qkv_shapes + cache_shapes + ret_shapes,
        compiler_params=_cparams(("arbitrary",)),
        name="even_in",
    )(*x_args, mod_l, w_in, qg, kvg, wq, wkv, ek, *tabs, *prev_args)
    return q, k, v, (ckv, kpe), rq, rk, rv, rg


def _ctx_kv_kernel(ckv_ref, kpe_ref, wkv_ref, ek_ref, k_ref, v_ref):
    kv = _bdot(ckv_ref[...].astype(BF16), wkv_ref[...])
    k_ref[...] = (kv[:, :HEADS * LANES] + _bdot(kpe_ref[...].astype(BF16), ek_ref[...])).astype(BF16)
    v_ref[...] = kv[:, HEADS * LANES:].astype(BF16)


def _ctx_kv(ckv_c, kpe_c, wkv, ek):
    n = ckv_c.shape[0]
    full = lambda a: pl.BlockSpec(a.shape, lambda i: (0,) * a.ndim)
    return pl.pallas_call(
        _ctx_kv_kernel,
        grid=(n // PAST,),
        in_specs=[pl.BlockSpec((PAST, KV_LORA), lambda i: (i, 0)), pl.BlockSpec((PAST, LANES), lambda i: (i, 0)),
                  full(wkv), full(ek)],
        out_specs=[pl.BlockSpec((PAST, HEADS * LANES), lambda i: (i, 0))] * 2,
        out_shape=[jax.ShapeDtypeStruct((n, HEADS * LANES), BF16)] * 2,
        compiler_params=_cparams(("parallel",)),
        name="ctx_kv",
    )(ckv_c, kpe_c, wkv, ek)


def _attn_kernel(n_kv, q_ref, *refs):
    k_refs = refs[0:2 * n_kv:2]
    v_refs = refs[1:2 * n_kv:2]
    o_ref = refs[2 * n_kv]
    nt = (((1,), (1,)), ((), ()))
    for pair in range(HEADS // 2):
        acc = None
        for sub in range(2):
            sl = slice((2 * pair + sub) * LANES, (2 * pair + sub + 1) * LANES)
            qh = q_ref[:, sl]
            s = [lax.dot_general(qh, k[:, sl], nt, preferred_element_type=F32) for k in k_refs]
            m = functools.reduce(jnp.maximum, [jnp.max(a, axis=-1, keepdims=True) for a in s])
            e = [jnp.exp(a - m) for a in s]
            den = functools.reduce(jnp.add, [jnp.sum(a, axis=-1, keepdims=True) for a in e])
            o = functools.reduce(jnp.add, [_bdot(a.astype(BF16), v[:, sl]) for a, v in zip(e, v_refs)])
            o = o / den
            acc = o if acc is None else acc + o
        o_ref[:, pair * LANES:(pair + 1) * LANES] = acc.astype(BF16)


def _latent_seq(i):
    return jnp.maximum(i - TP // TQ, 0) // (LAT_LEN // TQ)


def _attn_tiles_kernel(q_ref, kp_ref, vp_ref, kc_ref, vc_ref, kl_ref, vl_ref, o_ref):
    is_latent = pl.program_id(0) >= TP // TQ

    @pl.when(jnp.logical_not(is_latent))
    def _():
        _attn_kernel(1, q_ref, kp_ref, vp_ref, o_ref)

    @pl.when(is_latent)
    def _():
        _attn_kernel(2, q_ref, kc_ref, vc_ref, kl_ref, vl_ref, o_ref)


def _attention(q, k, v, kc, vc):
    w = HEADS * LANES
    tile = lambda i: (i, 0)
    ctx_own = lambda i: (jnp.minimum(i, N_PROMPT_SEQ - 1), 0)
    cache = lambda i: (_latent_seq(i), 0)
    lat_own = lambda i: (TP // LAT_LEN + _latent_seq(i), 0)
    return pl.pallas_call(
        _attn_tiles_kernel,
        grid=(T // TQ,),
        in_specs=[pl.BlockSpec((TQ, w), tile),
                  pl.BlockSpec((PROMPT_LEN, w), ctx_own), pl.BlockSpec((PROMPT_LEN, w), ctx_own),
                  pl.BlockSpec((PAST, w), cache), pl.BlockSpec((PAST, w), cache),
                  pl.BlockSpec((LAT_LEN, w), lat_own), pl.BlockSpec((LAT_LEN, w), lat_own)],
        out_specs=pl.BlockSpec((TQ, HEADS * VDIM), tile),
        out_shape=jax.ShapeDtypeStruct((T, HEADS * VDIM), BF16),
        compiler_params=_cparams(("arbitrary",)),
        name="attention",
    )(q, k, v, kc, vc, k, v)


def _ret_prefix_kernel(lg_ref, rk_ref, rv_ref, s0f_ref, s0b_ref, pf_ref, qb_ref, sf_scr, sb_scr):
    s = pl.program_id(0)
    n_tiles, per_seq = TS // TQ, LAT_LEN // TQ
    row = lax.broadcasted_iota(jnp.int32, (LANES, 1), 0)
    lane = lax.broadcasted_iota(jnp.int32, (1, LANES), 1)
    top, lo = row < RDK, lane < RDK
    same_head = top == lo
    m_col = lax.broadcasted_iota(jnp.int32, (TQ, 1), 0).astype(F32)

    def scan_step(first, lg_off, s0_ref, scr, out_ref, pos):
        @pl.when(first)
        def _():
            scr[...] = s0_ref[...]

        for pair in range(HEADS // 2):
            sl = slice(pair * LANES, (pair + 1) * LANES)
            lg_even, lg_odd = lg_ref[lg_off + 2 * pair], lg_ref[lg_off + 2 * pair + 1]
            dec = jnp.exp(pos * jnp.where(lo, lg_even, lg_odd))
            local = _bdot((rk_ref[:, sl].astype(F32) * dec).T.astype(BF16), rv_ref[:, sl])
            out_ref[pair] = scr[pair].astype(BF16)
            scr[pair] = scr[pair] * jnp.exp(float(TQ) * jnp.where(top, lg_even, lg_odd)) + jnp.where(same_head, local, 0.0)

    @pl.when(s < n_tiles)
    def _():
        scan_step(s % per_seq == 0, 0, s0f_ref, sf_scr, pf_ref, TQ - 1.0 - m_col)

    @pl.when(s >= n_tiles)
    def _():
        scan_step((2 * n_tiles - 1 - s) % per_seq == per_seq - 1, HEADS, s0b_ref, sb_scr, qb_ref, m_col)


def _ret_prefix(lg, rk, rv, s0f, s0b):
    n_tiles, per_seq = TS // TQ, LAT_LEN // TQ
    tile_of = lambda s: jnp.where(s < n_tiles, s, 2 * n_tiles - 1 - s)
    st = pl.BlockSpec((None, HEADS // 2, LANES, LANES), lambda s, lg: (tile_of(s) // per_seq, 0, 0, 0))
    kv = pl.BlockSpec((TQ, HEADS * RDK), lambda s, lg: (TP // TQ + tile_of(s), 0))
    pf_blk = pl.BlockSpec((None, HEADS // 2, LANES, LANES), lambda s, lg: (jnp.minimum(s, n_tiles - 1), 0, 0, 0))
    qb_blk = pl.BlockSpec((None, HEADS // 2, LANES, LANES),
                          lambda s, lg: (jnp.minimum(2 * n_tiles - 1 - s, n_tiles - 1), 0, 0, 0))
    shape = jax.ShapeDtypeStruct((n_tiles, HEADS // 2, LANES, LANES), BF16)
    return pl.pallas_call(
        _ret_prefix_kernel,
        grid_spec=pltpu.PrefetchScalarGridSpec(
            num_scalar_prefetch=1, grid=(2 * n_tiles,),
            in_specs=[kv, kv, st, st], out_specs=[pf_blk, qb_blk],
            scratch_shapes=[pltpu.VMEM((HEADS // 2, LANES, LANES), F32)] * 2),
        out_shape=[shape, shape],
        compiler_params=_cparams(("arbitrary",)),
        name="ret_prefix",
    )(lg, rk, rv, s0f, s0b)


def _retention_kernel(latent, lg_ref, rq_ref, rk_ref, rv_ref, rg_ref, *refs):
    q0 = 0
    seq_len = rq_ref.shape[0]
    if latent:
        s0f_ref, s0b_ref, o_ref = refs
    else:
        unperm_ref, o_ref, sf_ref, sb_ref = refs
    tq, tk = rq_ref.shape[0], rk_ref.shape[0]
    nt = (((1,), (1,)), ((), ()))
    n_idx = (q0 + lax.broadcasted_iota(jnp.int32, (tq, tk), 0)).astype(F32)
    m_idx = lax.broadcasted_iota(jnp.int32, (tq, tk), 1).astype(F32)
    dist = n_idx - m_idx
    adist = jnp.abs(dist)
    fwd = dist > 0.0
    diag = jnp.where(dist == 0.0, 1.0, 0.0)
    lane = lax.broadcasted_iota(jnp.int32, (1, LANES), 1)
    lo = lane < RDK
    n_col = (q0 + lax.broadcasted_iota(jnp.int32, (tq, 1), 0)).astype(F32)
    m_col = lax.broadcasted_iota(jnp.int32, (tk, 1), 0).astype(F32)
    for pair in range(HEADS // 2):
        sl = slice(pair * LANES, (pair + 1) * LANES)
        qb, kb, vb = rq_ref[:, sl], rk_ref[:, sl], rv_ref[:, sl]
        acc = jnp.zeros((tq, LANES), F32)
        for sub in range(2):
            h = 2 * pair + sub
            lgf, lgb = lg_ref[h], lg_ref[HEADS + h]
            half = lo if sub == 0 else jnp.logical_not(lo)
            qm = jnp.where(half, qb, jnp.zeros_like(qb))
            vm = jnp.where(half, vb, jnp.zeros_like(vb))
            s = lax.dot_general(qm, kb, nt, preferred_element_type=F32)
            w = jnp.exp(adist * jnp.where(fwd, lgf, lgb)) + diag
            acc = acc + _bdot((s * w).astype(BF16), vm)
        lgf_l = jnp.where(lo, lg_ref[2 * pair], lg_ref[2 * pair + 1])
        lgb_l = jnp.where(lo, lg_ref[HEADS + 2 * pair], lg_ref[HEADS + 2 * pair + 1])
        if latent:
            acc = acc + _bdot(qb, s0f_ref[pair]) * jnp.exp((n_col + 1.0) * lgf_l)
            acc = acc + _bdot(qb, s0b_ref[pair]) * jnp.exp((seq_len - n_col) * lgb_l)
        else:
            v_swapped = pltpu.roll(vb.astype(F32), RDK, axis=1).astype(BF16)
            for st_ref, dec in ((sf_ref, jnp.exp((seq_len - 1.0 - m_col) * lgf_l)),
                                (sb_ref, jnp.exp(m_col * lgb_l))):
                kt = (kb.astype(F32) * dec).T.astype(BF16)
                kt = _bdot(unperm_ref[...], kt).astype(BF16)
                st_ref[2 * pair] = _bdot(kt, vb)[0:RDK, 0:RDK]
                st_ref[2 * pair + 1] = _bdot(kt, v_swapped)[RDK:, 0:RDK]
        inv = 1.0 / RDK
        mu = jnp.where(lo, jnp.sum(jnp.where(lo, acc, 0.0), axis=-1, keepdims=True),
                       jnp.sum(jnp.where(lo, 0.0, acc), axis=-1, keepdims=True)) * inv
        dlt = acc - mu
        d2 = dlt * dlt
        var = jnp.where(lo, jnp.sum(jnp.where(lo, d2, 0.0), axis=-1, keepdims=True),
                        jnp.sum(jnp.where(lo, 0.0, d2), axis=-1, keepdims=True)) * inv
        g = rg_ref[:, sl]
        o_ref[:, sl] = (dlt * lax.rsqrt(var + 1e-5) * (g * _sigmoid(g))).astype(BF16)


def _retention_tiles_kernel(n_prev, lg_ref, rq_ref, rk_ref, rv_ref, rg_ref, pf_ref, qb_ref, unperm_ref, *refs):
    prev_refs, (o_ref, sf_ref, sb_ref) = refs[:2 * n_prev], refs[2 * n_prev:]
    is_latent = pl.program_id(0) >= TP // TQ

    @pl.when(jnp.logical_not(is_latent))
    def _():
        for k in range(n_prev):
            sf_ref[k] = prev_refs[2 * k][...]
            sb_ref[k] = prev_refs[2 * k + 1][...]
        _retention_kernel(False, lg_ref, rq_ref, rk_ref, rv_ref, rg_ref, unperm_ref, o_ref,
                          sf_ref.at[n_prev], sb_ref.at[n_prev])

    @pl.when(is_latent)
    def _():
        _retention_kernel(True, lg_ref, rq_ref, rk_ref, rv_ref, rg_ref, pf_ref, qb_ref, o_ref)


def _retention(lg, rq, rk, rv, rg, s0f, s0b, unperm, prev_states):
    w = HEADS * RDK
    n_prev = 0 if prev_states is None else prev_states[0].shape[1]
    pf, qb = _ret_prefix(lg, rk, rv, s0f, s0b)
    tile = lambda i, lg: (i, 0)
    s0_blk = pl.BlockSpec((None, HEADS // 2, LANES, LANES), lambda i, lg: (jnp.maximum(i - TP // TQ, 0), 0, 0, 0))
    seq = lambda i, lg: (jnp.minimum(i, N_PROMPT_SEQ - 1), 0, 0, 0, 0)
    st_blk = pl.BlockSpec((None, n_prev + 1, HEADS, RDK, RDK), seq)
    st_shape = jax.ShapeDtypeStruct((N_PROMPT_SEQ, n_prev + 1, HEADS, RDK, RDK), F32)
    prev = () if prev_states is None else tuple(prev_states)
    prev_specs = [pl.BlockSpec((None, None, HEADS, RDK, RDK),
                               functools.partial(lambda k, i, lg: (jnp.minimum(i, N_PROMPT_SEQ - 1), k, 0, 0, 0), k))
                  for k in range(n_prev) for _ in range(2)]
    prev_args = [p for k in range(n_prev) for p in prev]
    return pl.pallas_call(
        functools.partial(_retention_tiles_kernel, n_prev),
        grid_spec=pltpu.PrefetchScalarGridSpec(
            num_scalar_prefetch=1, grid=(T // TQ,),
            in_specs=[pl.BlockSpec((TQ, w), tile)] * 4 + [s0_blk, s0_blk,
                      pl.BlockSpec((LANES, LANES), lambda i, lg: (0, 0))] + prev_specs,
            out_specs=[pl.BlockSpec((TQ, w), tile), st_blk, st_blk]),
        out_shape=[jax.ShapeDtypeStruct((T, w), BF16), st_shape, st_shape],
        compiler_params=_cparams(("arbitrary",)),
        name="retention",
    )(lg, rq, rk, rv, rg, pf, qb, unperm, *prev_args)


def _mix_ffn_kernel(n_x, *refs):
    x_refs, refs = refs[:n_x], refs[n_x:]
    a_ref, r_ref, mod_ref, wo_ref, wg_ref, wu_ref, wd_ref, ln_ref, o_ref, x1_scr, h_scr, acc_scr = refs
    f = pl.program_id(1)

    @pl.when(f == 0)
    def _():
        half = HEADS * VDIM
        y = _bdot(a_ref[...], wo_ref[0:half, :]) + _bdot(r_ref[...], wo_ref[half:, :])
        x1 = _layer_norm(ALPHA * _tile_rows(x_refs, TM_FF) + mod_ref[2:3, :] * y, ln_ref[0:1, :], ln_ref[1:2, :])
        x1_scr[...] = x1
        h_scr[...] = (x1 * (1.0 + mod_ref[4:5, :]) + mod_ref[3:4, :]).astype(BF16)
        acc_scr[...] = jnp.zeros_like(acc_scr)

    h = h_scr[...]
    g = _bdot(h, wg_ref[...])
    u = _bdot(h, wu_ref[...])
    acc_scr[...] += _bdot((g * _sigmoid(g) * u).astype(BF16), wd_ref[...])

    @pl.when(f == pl.num_programs(1) - 1)
    def _():
        o_ref[...] = _layer_norm(ALPHA * x1_scr[...] + mod_ref[5:6, :] * acc_scr[...], ln_ref[2:3, :], ln_ref[3:4, :])


def _mix_ffn(x, attn, ret, mod_l, w_out, wg, wu, wd, ln, j):
    x_args, x_specs = _tile_specs(x, TM_FF)
    tok = lambda w: pl.BlockSpec((TM_FF, w), lambda i, f: (i, 0))
    return pl.pallas_call(
        functools.partial(_mix_ffn_kernel, len(x_args)),
        grid=(T // TM_FF, D_FF // TF),
        in_specs=x_specs + [tok(HEADS * VDIM), tok(HEADS * RDK),
                  pl.BlockSpec((None, 8, D), lambda i, f: (_group_of_tile(i, TM_FF), 0, 0)),
                  pl.BlockSpec((None,) + w_out.shape[1:], lambda i, f: (j, 0, 0)),
                  pl.BlockSpec((None, D, TF), lambda i, f: (j, 0, f)),
                  pl.BlockSpec((None, D, TF), lambda i, f: (j, 0, f)),
                  pl.BlockSpec((None, TF, D), lambda i, f: (j, f, 0)), pl.BlockSpec((8, D), lambda i, f: (0, 0))],
        out_specs=tok(D),
        out_shape=jax.ShapeDtypeStruct((T, D), F32),
        scratch_shapes=[pltpu.VMEM((TM_FF, D), F32), pltpu.VMEM((TM_FF, D), BF16), pltpu.VMEM((TM_FF, D), F32)],
        compiler_params=_cparams(("parallel", "arbitrary")),
        name="mix_ffn",
    )(*x_args, attn, ret, mod_l, w_out, wg, wu, wd, ln)


def _conv_in_kernel(x_ref, mod_ref, w_ref, b_ref, z_ref):
    h = (x_ref[...] * (1.0 + mod_ref[1:2, :]) + mod_ref[0:1, :]).astype(BF16)
    p = _bdot(h, w_ref[...])
    b_ref[...] = p[:, 0:D]
    z_ref[...] = p[:, D:2 * D] * p[:, 2 * D:3 * D]


def _conv_in(x, mod_l, w_in, j):
    tok = pl.BlockSpec((TM, D), lambda i: (i, 0))
    return pl.pallas_call(
        _conv_in_kernel,
        grid=(T // TM,),
        in_specs=[tok, pl.BlockSpec((None, 8, D), lambda i: (_group_of_tile(i, TM), 0, 0)),
                  pl.BlockSpec((None,) + w_in.shape[1:], lambda i: (j, 0, 0))],
        out_specs=[tok, tok],
        out_shape=[jax.ShapeDtypeStruct((T, D), F32)] * 2,
        compiler_params=_cparams(("parallel",)),
        name="conv_in",
    )(x, mod_l, w_in)


def _conv_out_kernel(x_ref, b_ref, z_ref, zp_ref, zn_ref, mod_ref, cw_ref, w_ref, ln_ref, rw_ref, rb_ref,
                     o_ref, h_ref, route_ref):
    i = pl.program_id(0)
    z = z_ref[...]
    row = lax.broadcasted_iota(jnp.int32, (TM, 1), 0)
    seq_len = jnp.where(i < TP // TM, PROMPT_LEN, LAT_LEN)
    pos = (i * TM + row) & (seq_len - 1)
    prev = jnp.where(row == 0, zp_ref[7:8, :], pltpu.roll(z, 1, axis=0))
    prev = jnp.where(pos == 0, 0.0, prev)
    nxt = jnp.where(row == TM - 1, zn_ref[0:1, :], pltpu.roll(z, TM - 1, axis=0))
    nxt = jnp.where(pos == seq_len - 1, 0.0, nxt)
    y = prev * cw_ref[0:1, :] + z * cw_ref[1:2, :] + nxt * cw_ref[2:3, :]
    t = _bdot((b_ref[...] * y).astype(BF16), w_ref[...])
    x1 = _layer_norm(ALPHA * x_ref[...] + mod_ref[2:3, :] * t, ln_ref[0:1, :], ln_ref[1:2, :])
    o_ref[...] = x1
    h = x1 * (1.0 + mod_ref[4:5, :]) + mod_ref[3:4, :]
    h_ref[...] = h
    h_hi = h.astype(BF16)
    h_lo = (h - h_hi.astype(F32)).astype(BF16)
    both = _bdot(h_hi, rw_ref[...])
    logits = both[:, :LANES] + both[:, LANES:] + _bdot(h_lo, rw_ref[:, :LANES]) + rb_ref[...]
    lane = lax.broadcasted_iota(jnp.int32, logits.shape, 1).astype(F32)
    t1 = jnp.max(logits, axis=-1, keepdims=True)
    i1 = jnp.min(jnp.where(logits == t1, lane, float(LANES)), axis=-1, keepdims=True)
    rest = jnp.where(lane == i1, -jnp.inf, logits)
    t2 = jnp.max(rest, axis=-1, keepdims=True)
    i2 = jnp.min(jnp.where(rest == t2, lane, float(LANES)), axis=-1, keepdims=True)
    e = jnp.exp(t2 - t1)
    den = 1.0 + e
    route_ref[...] = jnp.where(lane == 0.0, i1, jnp.where(lane == 1.0, i2,
                               jnp.where(lane == 2.0, 1.0 / den, jnp.where(lane == 3.0, e / den, 0.0))))


def _conv_out(x, b, z, mod_l, cw, w_out, ln, rw, rb, j):
    tok = pl.BlockSpec((TM, D), lambda i: (i, 0))
    sub = TM // 8
    return pl.pallas_call(
        _conv_out_kernel,
        grid=(T // TM,),
        in_specs=[tok, tok, tok,
                  pl.BlockSpec((8, D), lambda i: (jnp.maximum(i * sub - 1, 0), 0)),
                  pl.BlockSpec((8, D), lambda i: (jnp.minimum((i + 1) * sub, T // 8 - 1), 0)),
                  pl.BlockSpec((None, 8, D), lambda i: (_group_of_tile(i, TM), 0, 0)),
                  pl.BlockSpec((8, D), lambda i: (0, 0)), pl.BlockSpec((None, D, D), lambda i: (j, 0, 0)),
                  pl.BlockSpec((8, D), lambda i: (0, 0)),
                  pl.BlockSpec((D, 2 * LANES), lambda i: (0, 0)), pl.BlockSpec((1, LANES), lambda i: (0, 0))],
        out_specs=[tok, tok, pl.BlockSpec((TM, LANES), lambda i: (i, 0))],
        out_shape=[jax.ShapeDtypeStruct((T, D), F32), jax.ShapeDtypeStruct((T, D), F32),
                   jax.ShapeDtypeStruct((T, LANES), F32)],
        compiler_params=_cparams(("parallel",)),
        name="conv_out",
    )(x, b, z, z, z, mod_l, cw, w_out, ln, rw, rb)


def _moe_up_kernel(te_ref, nv_ref, *refs):
    x_refs, (wg_ref, wu_ref, a_ref, h_scr) = refs[:X_PARTS], refs[X_PARTS:]
    i = pl.program_id(1)
    for part, x_ref in enumerate(x_refs):
        @pl.when(i // (NP_ROWS // TM_FF // X_PARTS) == part)
        def _():
            h_scr[...] = x_ref[...].astype(BF16)

    @pl.when(i < nv_ref[0])
    def _():
        h = h_scr[...]
        g = _wdot(h, wg_ref[...])
        u = _wdot(h, wu_ref[...])
        a_ref[...] = (g * _sigmoid(g) * u).astype(BF16)

    @pl.when(pl.program_id(1) >= nv_ref[0])
    def _():
        a_ref[...] = jnp.zeros_like(a_ref)


def _moe_down_kernel(te_ref, nv_ref, a_ref, wd_ref, o_ref):
    @pl.when(pl.program_id(0) < nv_ref[0])
    def _():
        o_ref[...] = _wdot(a_ref[...], wd_ref[...])

    @pl.when(pl.program_id(0) >= nv_ref[0])
    def _():
        o_ref[...] = jnp.zeros_like(o_ref)


def _moe(tile_expert, n_valid, xs, wg, wu, wd, j):
    n_tiles = NP_ROWS // TM_FF
    per_part = n_tiles // X_PARTS
    part_spec = lambda c: pl.BlockSpec(
        (TM_FF, D), lambda f, i, te, nv: (jnp.clip(i - c * per_part, 0, per_part - 1), 0))
    act = pl.pallas_call(
        _moe_up_kernel,
        grid_spec=pltpu.PrefetchScalarGridSpec(
            num_scalar_prefetch=2, grid=(D_FF // TF, n_tiles),
            in_specs=[part_spec(c) for c in range(X_PARTS)] + [
                      pl.BlockSpec((None, None, D, TF), lambda f, i, te, nv: (j, te[i], 0, f)),
                      pl.BlockSpec((None, None, D, TF), lambda f, i, te, nv: (j, te[i], 0, f))],
            out_specs=pl.BlockSpec((TM_FF, TF), lambda f, i, te, nv: (i, f)),
            scratch_shapes=[pltpu.VMEM((TM_FF, D), BF16)]),
        out_shape=jax.ShapeDtypeStruct((NP_ROWS, D_FF), BF16),
        compiler_params=_cparams(("arbitrary", "arbitrary")),
        name="moe_up",
    )(tile_expert, n_valid, *xs, wg, wu)
    return pl.pallas_call(
        _moe_down_kernel,
        grid_spec=pltpu.PrefetchScalarGridSpec(
            num_scalar_prefetch=2, grid=(n_tiles,),
            in_specs=[pl.BlockSpec((TM_FF, D_FF), lambda i, te, nv: (i, 0)),
                      pl.BlockSpec((None, None, D_FF, D), lambda i, te, nv: (j, te[i], 0, 0))],
            out_specs=pl.BlockSpec((TM_FF, D), lambda i, te, nv: (i, 0))),
        out_shape=jax.ShapeDtypeStruct((NP_ROWS, D), F32),
        compiler_params=_cparams(("arbitrary",)),
        name="moe_down",
    )(tile_expert, n_valid, act, wd)


def _combine_kernel(split, x_ref, o0_ref, o1_ref, route_ref, mod_ref, ln_ref, *o_refs):
    y = route_ref[:, 2:3] * o0_ref[...] + route_ref[:, 3:4] * o1_ref[...]
    out = _layer_norm(ALPHA * x_ref[...] + mod_ref[5:6, :] * y, ln_ref[2:3, :], ln_ref[3:4, :])
    if not split:
        o_refs[0][...] = out
        return
    is_latent = pl.program_id(0) >= TP // TM_FF

    @pl.when(jnp.logical_not(is_latent))
    def _():
        o_refs[0][...] = out

    @pl.when(is_latent)
    def _():
        o_refs[1][...] = out


def _combine(x, o0, o1, route, mod_l, ln, split):
    tok = pl.BlockSpec((TM_FF, D), lambda i: (i, 0))
    if split:
        out_specs = [pl.BlockSpec((TM_FF, D), lambda i: (jnp.minimum(i, TP // TM_FF - 1), 0)),
                     pl.BlockSpec((TM_FF, D), lambda i: (jnp.maximum(i - TP // TM_FF, 0), 0))]
        out_shape = [jax.ShapeDtypeStruct((TP, D), F32), jax.ShapeDtypeStruct((TS, D), F32)]
    else:
        out_specs, out_shape = tok, jax.ShapeDtypeStruct((T, D), F32)
    return pl.pallas_call(
        functools.partial(_combine_kernel, split),
        grid=(T // TM_FF,),
        in_specs=[tok, tok, tok, pl.BlockSpec((TM_FF, LANES), lambda i: (i, 0)),
                  pl.BlockSpec((None, 8, D), lambda i: (_group_of_tile(i, TM_FF), 0, 0)),
                  pl.BlockSpec((8, D), lambda i: (0, 0))],
        out_specs=out_specs,
        out_shape=out_shape,
        compiler_params=_cparams(("arbitrary",)),
        name="moe_combine",
    )(x, o0, o1, route, mod_l, ln)


def _routing_plan(route):
    e = route[:, 0:2].astype(jnp.int32).reshape(-1)
    onehot = (e[:, None] == jnp.arange(N_EXP, dtype=jnp.int32)[None, :]).astype(jnp.int32)
    csum = jnp.cumsum(onehot, axis=0)
    counts = csum[-1]
    rank = jnp.sum((csum - onehot) * onehot, axis=1)
    padded = (counts + TM_FF - 1) // TM_FF * TM_FF
    pend = jnp.cumsum(padded)
    dest = jnp.sum(onehot * (pend - padded)[None, :], axis=1) + rank
    order = jnp.argsort(e, stable=True).astype(jnp.int32)
    rows = jnp.arange(NP_ROWS, dtype=jnp.int32)
    before = (rows[:, None] >= pend[None, :]).astype(jnp.int32)
    row_e = jnp.minimum(jnp.sum(before, axis=1), N_EXP - 1)
    row_cnt = jnp.sum((row_e[:, None] == jnp.arange(N_EXP, dtype=jnp.int32)[None, :]) * counts[None, :], axis=1)
    q = jnp.clip(rows - jnp.sum(before * padded[None, :], axis=1), 0, jnp.maximum(row_cnt - 1, 0))
    src = jnp.minimum(jnp.sum(before * counts[None, :], axis=1) + q, 2 * T - 1)
    row_token = order[src] // 2
    n_valid = (pend[-1] // TM_FF).astype(jnp.int32)
    tile_start = jnp.minimum(jnp.arange(NP_ROWS // TM_FF, dtype=jnp.int32), n_valid - 1) * TM_FF
    tile_expert = jnp.minimum(jnp.sum((tile_start[:, None] >= pend[None, :]).astype(jnp.int32), axis=1), N_EXP - 1)
    return dest.reshape(T, 2), row_token, tile_expert.astype(jnp.int32), n_valid.reshape(1)


_INV_PERM_RDK = np.argsort(np.concatenate([np.arange(0, RDK, 2), np.arange(1, RDK, 2)]))


def _pad_cols(a, width):
    return jnp.pad(a, ((0, 0), (0, width - a.shape[1])))


def _deinterleave(a):
    n = a.shape[-1]
    return jnp.swapaxes(a.reshape(a.shape[:-1] + (n // 2, 2)), -1, -2).reshape(a.shape)


def _prep_even_weights(w_in, w_q_b, w_kv_b):
    o_kpe, o_rq, o_rk, o_rv = 640, 672, 1184, 1696
    heads = lambda a: _deinterleave(a.reshape(D, HEADS, RDK)).reshape(D, HEADS * RDK)
    kpe = w_in[:, o_kpe:o_kpe + ROPE]
    w_in_p = jnp.concatenate([
        w_in[:, :640], heads(w_in[:, o_rq:o_rk]), heads(w_in[:, o_rk:o_rv]) * (RDK ** -0.5), w_in[:, o_rv:],
        _pad_cols(kpe, LANES), _pad_cols(_deinterleave(kpe), LANES),
        _pad_cols(jnp.concatenate([-kpe[:, 1::2], kpe[:, 0::2]], axis=1), LANES)], axis=1).astype(BF16)
    wq = w_q_b.reshape(Q_LORA, HEADS, NOPE + ROPE)
    wq = jnp.concatenate([wq[:, :, :NOPE], _deinterleave(wq[:, :, NOPE:]),
                          jnp.zeros((Q_LORA, HEADS, LANES - NOPE - ROPE), F32)], axis=2)
    wq = wq.reshape(Q_LORA, HEADS * LANES).astype(BF16)
    wkv = w_kv_b.reshape(KV_LORA, HEADS, NOPE + VDIM)
    zero = jnp.zeros((KV_LORA, HEADS, LANES - NOPE), F32)
    wk = jnp.concatenate([wkv[:, :, :NOPE], zero], axis=2).reshape(KV_LORA, HEADS * LANES)
    wv = wkv[:, :, NOPE:].reshape(KV_LORA, HEADS // 2, 2, VDIM)
    zv = jnp.zeros((KV_LORA, HEADS // 2, VDIM), F32)
    wv = jnp.stack([jnp.concatenate([wv[:, :, 0], zv], axis=2), jnp.concatenate([zv, wv[:, :, 1]], axis=2)], axis=2)
    wkv_p = jnp.concatenate([wk, wv.reshape(KV_LORA, HEADS * LANES)], axis=1).astype(BF16)
    return w_in_p, wq, wkv_p


def _placement():
    ek = np.zeros((LANES, HEADS * LANES), np.float32)
    for h in range(HEADS):
        ek[np.arange(ROPE), h * LANES + NOPE + np.arange(ROPE)] = 1.0
    return jnp.asarray(ek, BF16)


def _rotary_tables():
    rows = LAT_LEN // GRID_W
    r, col = jnp.meshgrid(jnp.arange(rows, dtype=F32), jnp.arange(GRID_W, dtype=F32), indexing='ij')
    n_freq = ROPE // 4
    freqs = 1.0 / (10000.0 ** (jnp.arange(n_freq, dtype=F32) / n_freq))
    ang = jnp.concatenate([r.reshape(-1)[:, None] * freqs, col.reshape(-1)[:, None] * freqs], axis=-1)
    cos, sin = jnp.cos(ang), jnp.sin(ang)
    theta = 1.0 / (10000.0 ** jnp.linspace(0.0, 1.0, RDK // 2, dtype=F32))
    rang = jnp.arange(LAT_LEN, dtype=F32)[:, None] * theta
    rcos, rsin = jnp.cos(rang), jnp.sin(rang)
    one = lambda w: jnp.ones((LAT_LEN, w), F32)
    zero = lambda w: jnp.zeros((LAT_LEN, w), F32)
    lat = [jnp.concatenate([one(NOPE), cos, cos, one(LANES - NOPE - ROPE)], axis=1),
           jnp.concatenate([zero(NOPE), sin, sin, zero(LANES - NOPE - ROPE)], axis=1),
           jnp.concatenate([cos, cos, zero(LANES - ROPE)], axis=1),
           jnp.concatenate([sin, sin, zero(LANES - ROPE)], axis=1),
           jnp.concatenate([rcos] * 4, axis=1), jnp.concatenate([rsin] * 4, axis=1)]
    ident = [np.ones((TM, LANES), np.float32), np.zeros((TM, LANES), np.float32)]
    ident_k = np.concatenate([np.ones((TM, ROPE), np.float32), np.zeros((TM, LANES - ROPE), np.float32)], axis=1)
    ident = [ident[0], ident[1], ident_k, ident[1], ident[0], ident[1]]
    return [jnp.concatenate([l, jnp.asarray(c)], axis=0) for l, c in zip(lat, ident)]


def _block_diag_states(s0):
    s = jnp.swapaxes(_deinterleave(jnp.swapaxes(s0, -1, -2)), -1, -2)
    s = s.reshape(s0.shape[0], HEADS // 2, 2, RDK, RDK)
    z = jnp.zeros_like(s[:, :, 0])
    top = jnp.concatenate([s[:, :, 0], z], axis=-1)
    bot = jnp.concatenate([z, s[:, :, 1]], axis=-1)
    return jnp.concatenate([top, bot], axis=-2)


def _unpermute_matrix():
    m = np.zeros((LANES, LANES), np.float32)
    for blk in range(LANES // RDK):
        m[blk * RDK + np.arange(RDK), blk * RDK + _INV_PERM_RDK] = 1.0
    return jnp.asarray(m, BF16)


def kernel(x_prompt, x_sample, c, cache_ckv, cache_kpe, state_ret_fwd, state_ret_bwd, c_ctx, w_mod, b_mod, ln_g, ln_b, w_in_mix, q_a_gain, kv_a_gain, w_q_b, w_kv_b, ret_decay_fwd, ret_decay_bwd, w_out_mix, w_in_conv, conv_w, w_out_conv, ffn_gate, ffn_up, ffn_down, router_w, router_b, exp_gate, exp_up, exp_down):
    x = (x_prompt.reshape(TP, D), x_sample.reshape(TS, D))
    cond8 = jnp.concatenate([c_ctx[None], c, jnp.zeros((8 - 1 - N_LAT_SEQ, D), F32)], axis=0)
    mods = _modulation(cond8, w_mod, b_mod)
    mods = jnp.pad(mods.reshape(DEPTH, N_GROUPS, 6, D), ((0, 0), (0, 0), (0, 2), (0, 0)))
    ln = jnp.pad(jnp.concatenate([ln_g, ln_b], axis=1)[:, jnp.array([0, 2, 1, 3])], ((0, 0), (0, 4), (0, 0)))
    tabs = _rotary_tables()
    ek = _placement()
    unperm = _unpermute_matrix()
    bf = lambda a: a.astype(BF16)
    w_out_mix_b, w_in_conv_b, w_out_conv_b = bf(w_out_mix), bf(w_in_conv), bf(w_out_conv)
    ffn_b = (bf(ffn_gate), bf(ffn_up), bf(ffn_down))
    exp_b = (exp_gate, exp_up, exp_down)
    cache, states = None, None
    for layer in range(DEPTH):
        j = layer // 2
        mod_l, ln_l = mods[layer], ln[layer]
        if layer % 2 == 0:
            w_in_p, wq, wkv = _prep_even_weights(w_in_mix[j], w_q_b[j], w_kv_b[j])
            q, k, v, cache, rq, rk, rv, rg = _even_in(
                x, mod_l, w_in_p, q_a_gain[j][None], kv_a_gain[j][None], wq, wkv, ek, tabs, cache)
            kpe_c = _pad_cols(_deinterleave(cache_kpe[:, j]).reshape(N_LAT_SEQ * PAST, ROPE), LANES)
            kc, vc = _ctx_kv(cache_ckv[:, j].reshape(N_LAT_SEQ * PAST, KV_LORA), kpe_c, wkv, ek)
            attn = _attention(q, k, v, kc, vc)
            lg = jnp.concatenate([jax.nn.log_sigmoid(ret_decay_fwd[j].astype(F32)),
                                  jax.nn.log_sigmoid(ret_decay_bwd[j].astype(F32))])
            ret, sf, sb = _retention(lg, rq, rk, rv, rg, _block_diag_states(state_ret_fwd[:, j]),
                                     _block_diag_states(state_ret_bwd[:, j]), unperm, states)
            states = (sf, sb)
            x = _mix_ffn(x, attn, ret, mod_l, w_out_mix_b, *ffn_b, ln_l, j)
        else:
            b, z = _conv_in(x, mod_l, w_in_conv_b, j)
            cw = jnp.pad(conv_w[j], ((0, 5), (0, 0)))
            rw = _pad_cols(router_w[j], LANES)
            rw_hi = rw.astype(BF16)
            rw = jnp.concatenate([rw_hi, (rw - rw_hi.astype(F32)).astype(BF16)], axis=1)
            rb = jnp.concatenate([router_b[j].astype(F32), jnp.full((LANES - N_EXP,), -1e30, F32)])[None]
            x, h, route = _conv_out(x, b, z, mod_l, cw, w_out_conv_b, ln_l, rw, rb, j)
            dest, row_token, tile_expert, n_valid = _routing_plan(route)
            part = NP_ROWS // X_PARTS
            xs = [h[row_token[a:a + part]] for a in range(0, NP_ROWS, part)]
            out_sorted = _moe(tile_expert, n_valid, xs, *exp_b, j)
            x = _combine(x, out_sorted[dest[:, 0]], out_sorted[dest[:, 1]], route, mod_l, ln_l,
                         split=layer == DEPTH - 1)
    y_prompt = x[0].reshape(N_PROMPT_SEQ, PROMPT_LEN, D)
    y_sample = x[1].reshape(N_LAT_SEQ, LAT_LEN, D)
    return (y_prompt, y_sample, cache[0], cache[1], states[0], states[1])
```

```python
import functools

import numpy as np
import jax
import jax.numpy as jnp
from jax import lax
from jax.experimental import pallas as pl
from jax.experimental.pallas import tpu as pltpu

F32 = jnp.float32
BF16 = jnp.bfloat16

D = 1024
DEPTH = 4
N_PROMPT_SEQ, PROMPT_LEN = 32, 256
N_LAT_SEQ, LAT_LEN = 2, 2048
PAST = 512
GRID_W = 64
TP = N_PROMPT_SEQ * PROMPT_LEN
TS = N_LAT_SEQ * LAT_LEN
T = TP + TS
HEADS = 8
NOPE, ROPE, VDIM = 64, 32, 64
Q_LORA, KV_LORA = 384, 256
RDK = 64
D_FF = 2816
N_EXP = 8
ALPHA = (2.0 * DEPTH) ** 0.25
Q_SCALE = float((NOPE + ROPE) ** -0.5)
LANES = 128
N_GROUPS = 8

TM = 256
TM_FF = 512
TF = D_FF // 2
TQ = 256
NP_ROWS = 2 * T + N_EXP * TM_FF
VMEM_LIMIT = 56 * 1024 * 1024

IN_COLS = 3072


def _cparams(sem):
    return pltpu.CompilerParams(dimension_semantics=sem, vmem_limit_bytes=VMEM_LIMIT)


def _group_of_tile(i, tm):
    per_seq = LAT_LEN // tm
    return jnp.maximum(i - TP // tm + per_seq, 0) // per_seq


def _bdot(a, b):
    return jnp.dot(a, b, preferred_element_type=F32)


def _wdot(a, w):
    return lax.dot_general(a, w, (((1,), (0,)), ((), ())), preferred_element_type=F32)


def _sigmoid(v):
    return 1.0 / (1.0 + jnp.exp(-v))


def _layer_norm(v, g, b):
    mu = jnp.mean(v, axis=-1, keepdims=True)
    d = v - mu
    var = jnp.mean(d * d, axis=-1, keepdims=True)
    return d * lax.rsqrt(var + 1e-5) * g + b


def _rms(v, g):
    return v * lax.rsqrt(jnp.mean(v * v, axis=-1, keepdims=True) + 1e-6) * g


def _mod_kernel(c_ref, w_ref, b_ref, o_ref):
    c = c_ref[...]
    s = (c * _sigmoid(c)).astype(BF16)
    o_ref[...] = _bdot(s, w_ref[...].astype(BF16)) + b_ref[...]


def _modulation(cond8, w_mod, b_mod):
    tn = 1536
    return pl.pallas_call(
        _mod_kernel,
        grid=(DEPTH, 6 * D // tn),
        in_specs=[pl.BlockSpec((8, D), lambda l, n: (0, 0)),
                  pl.BlockSpec((None, D, tn), lambda l, n: (l, 0, n)),
                  pl.BlockSpec((None, 1, tn), lambda l, n: (l, 0, n))],
        out_specs=pl.BlockSpec((None, 8, tn), lambda l, n: (l, 0, n)),
        out_shape=jax.ShapeDtypeStruct((DEPTH, 8, 6 * D), F32),
        compiler_params=_cparams(("arbitrary", "arbitrary")),
        name="modulation",
    )(cond8, w_mod, b_mod.reshape(DEPTH, 1, 6 * D))


def _swap_halves(a, half):
    n = a.shape[-1]
    lane = lax.broadcasted_iota(jnp.int32, a.shape, 1)
    first = (lane & (2 * half - 1)) < half
    return jnp.where(first, -pltpu.roll(a, n - half, axis=1), pltpu.roll(a, half, axis=1))


def _tile_rows(x_refs, tm):
    if len(x_refs) == 1:
        return x_refs[0][...]
    return jnp.where(pl.program_id(0) >= TP // tm, x_refs[1][...], x_refs[0][...])


def _tile_specs(x, tm):
    if not isinstance(x, tuple):
        return (x,), [pl.BlockSpec((tm, D), lambda i, *_: (i, 0))]
    return x, [pl.BlockSpec((tm, D), lambda i, *_: (jnp.minimum(i, TP // tm - 1), 0)),
               pl.BlockSpec((tm, D), lambda i, *_: (jnp.maximum(i - TP // tm, 0), 0))]


def _even_in_kernel(n_x, n_prev, *refs):
    x_refs, refs = refs[:n_x], refs[n_x:]
    (mod_ref, w_in_ref, qg_ref, kvg_ref, wq_ref, wkv_ref, ek_ref,
     cq_ref, sq_ref, ck_ref, sk_ref, cr_ref, sr_ref), refs = refs[:13], refs[13:]
    prev_refs, refs = refs[:2 * n_prev], refs[2 * n_prev:]
    q_ref, k_ref, v_ref, ckv_ref, kpe_ref, rq_ref, rk_ref, rv_ref, rg_ref = refs
    x = _tile_rows(x_refs, TM)
    h = (x * (1.0 + mod_ref[1:2, :]) + mod_ref[0:1, :]).astype(BF16)
    p = _bdot(h, w_in_ref[...])
    qn = _rms(p[:, 0:Q_LORA], qg_ref[...]).astype(BF16)
    qa = _bdot(qn, wq_ref[...])
    ckv = _rms(p[:, Q_LORA:Q_LORA + KV_LORA], kvg_ref[...])
    kv = _bdot(ckv.astype(BF16), wkv_ref[...])
    v_ref[...] = kv[:, HEADS * LANES:].astype(BF16)
    base = 2688
    ka = p[:, base + LANES:base + 2 * LANES]
    kb = p[:, base + 2 * LANES:base + 3 * LANES]
    rq = p[:, 640:1152]
    rk = p[:, 1152:1664]
    rv_ref[...] = p[:, 1664:2176].astype(BF16)
    rg_ref[...] = p[:, 2176:2688]
    lane = lax.broadcasted_iota(jnp.int32, qa.shape, 1) & (LANES - 1)
    qb = jnp.where(lane < NOPE + ROPE // 2,
                   -pltpu.roll(qa, qa.shape[1] - ROPE // 2, axis=1),
                   pltpu.roll(qa, ROPE // 2, axis=1))
    cq = jnp.concatenate([cq_ref[...]] * HEADS, axis=1)
    sq = jnp.concatenate([sq_ref[...]] * HEADS, axis=1)
    q_ref[...] = ((qa * cq + qb * sq) * Q_SCALE).astype(BF16)
    kpe_rot = ka * ck_ref[...] + kb * sk_ref[...]
    k_ref[...] = (kv[:, :HEADS * LANES] + _bdot(kpe_rot.astype(BF16), ek_ref[...])).astype(BF16)
    cr = jnp.concatenate([cr_ref[...]] * 4, axis=1)
    sr = jnp.concatenate([sr_ref[...]] * 4, axis=1)
    rq_ref[...] = (rq * cr + _swap_halves(rq, RDK // 2) * sr).astype(BF16)
    rk_ref[...] = (rk * cr + _swap_halves(rk, RDK // 2) * sr).astype(BF16)

    @pl.when(pl.program_id(0) < TP // TM)
    def _():
        for k in range(n_prev):
            ckv_ref[k] = prev_refs[2 * k][...]
            kpe_ref[k] = prev_refs[2 * k + 1][...]
        ckv_ref[n_prev] = ckv
        kpe_ref[n_prev] = p[:, base:base + ROPE]


def _even_in(x, mod_l, w_in, qg, kvg, wq, wkv, ek, tabs, prev_cache):
    assert TM == PROMPT_LEN
    x_args, x_specs = _tile_specs(x, TM)
    n_prev = 0 if prev_cache is None else prev_cache[0].shape[1]
    tok = lambda w: pl.BlockSpec((TM, w), lambda i: (i, 0))
    full = lambda a: pl.BlockSpec(a.shape, lambda i: (0,) * a.ndim)
    lat_tiles = LAT_LEN // TM
    tab = pl.BlockSpec((TM, LANES), lambda i: (
        jnp.where(i < TP // TM, lat_tiles, jnp.maximum(i - TP // TM, 0) % lat_tiles), 0))
    seq = lambda i: (jnp.minimum(i, N_PROMPT_SEQ - 1), 0, 0, 0)
    seq_k = lambda k, i: (jnp.minimum(i, N_PROMPT_SEQ - 1), k, 0, 0)
    prev_args, prev_specs = [], []
    for k in range(n_prev):
        for a, w in zip(prev_cache, (KV_LORA, ROPE)):
            prev_args.append(a)
            prev_specs.append(pl.BlockSpec((None, None, PROMPT_LEN, w), functools.partial(seq_k, k)))
    tok_outs = lambda dims: ([tok(w) for w, _ in dims], [jax.ShapeDtypeStruct((T, w), dt) for w, dt in dims])
    qkv_specs, qkv_shapes = tok_outs([(HEADS * LANES, BF16)] * 3)
    ret_specs, ret_shapes = tok_outs([(512, BF16), (512, BF16), (512, BF16), (512, F32)])
    cache_specs = [pl.BlockSpec((None, n_prev + 1, PROMPT_LEN, w), seq) for w in (KV_LORA, ROPE)]
    cache_shapes = [jax.ShapeDtypeStruct((N_PROMPT_SEQ, n_prev + 1, PROMPT_LEN, w), F32) for w in (KV_LORA, ROPE)]
    q, k, v, ckv, kpe, rq, rk, rv, rg = pl.pallas_call(
        functools.partial(_even_in_kernel, len(x_args), n_prev),
        grid=(T // TM,),
        in_specs=x_specs + [pl.BlockSpec((None, 8, D), lambda i: (_group_of_tile(i, TM), 0, 0)),
                            full(w_in), full(qg), full(kvg), full(wq), full(wkv), full(ek)] + [tab] * 6 + prev_specs,
        out_specs=qkv_specs + cache_specs + ret_specs,
        out_shape=qkv_shapes + cache_shapes + ret_shapes,
        compiler_params=_cparams(("arbitrary",)),
        name="even_in",
    )(*x_args, mod_l, w_in, qg, kvg, wq, wkv, ek, *tabs, *prev_args)
    return q, k, v, (ckv, kpe), rq, rk, rv, rg


def _ctx_kv_kernel(ckv_ref, kpe_ref, wkv_ref, ek_ref, k_ref, v_ref):
    kv = _bdot(ckv_ref[...].astype(BF16), wkv_ref[...])
    k_ref[...] = (kv[:, :HEADS * LANES] + _bdot(kpe_ref[...].astype(BF16), ek_ref[...])).astype(BF16)
    v_ref[...] = kv[:, HEADS * LANES:].astype(BF16)


def _ctx_kv(ckv_c, kpe_c, wkv, ek):
    n = ckv_c.shape[0]
    full = lambda a: pl.BlockSpec(a.shape, lambda i: (0,) * a.ndim)
    return pl.pallas_call(
        _ctx_kv_kernel,
        grid=(n // PAST,),
        in_specs=[pl.BlockSpec((PAST, KV_LORA), lambda i: (i, 0)), pl.BlockSpec((PAST, LANES), lambda i: (i, 0)),
                  full(wkv), full(ek)],
        out_specs=[pl.BlockSpec((PAST, HEADS * LANES), lambda i: (i, 0))] * 2,
        out_shape=[jax.ShapeDtypeStruct((n, HEADS * LANES), BF16)] * 2,
        compiler_params=_cparams(("parallel",)),
        name="ctx_kv",
    )(ckv_c, kpe_c, wkv, ek)


def _attn_kernel(n_kv, q_ref, *refs):
    k_refs = refs[0:2 * n_kv:2]
    v_refs = refs[1:2 * n_kv:2]
    o_ref = refs[2 * n_kv]
    nt = (((1,), (1,)), ((), ()))
    for pair in range(HEADS // 2):
        acc = None
        for sub in range(2):
            sl = slice((2 * pair + sub) * LANES, (2 * pair + sub + 1) * LANES)
            qh = q_ref[:, sl]
            s = [lax.dot_general(qh, k[:, sl], nt, preferred_element_type=F32) for k in k_refs]
            m = functools.reduce(jnp.maximum, [jnp.max(a, axis=-1, keepdims=True) for a in s])
            e = [jnp.exp(a - m) for a in s]
            den = functools.reduce(jnp.add, [jnp.sum(a, axis=-1, keepdims=True) for a in e])
            o = functools.reduce(jnp.add, [_bdot(a.astype(BF16), v[:, sl]) for a, v in zip(e, v_refs)])
            o = o / den
            acc = o if acc is None else acc + o
        o_ref[:, pair * LANES:(pair + 1) * LANES] = acc.astype(BF16)


def _latent_seq(i):
    return jnp.maximum(i - TP // TQ, 0) // (LAT_LEN // TQ)


def _attn_tiles_kernel(q_ref, kp_ref, vp_ref, kc_ref, vc_ref, kl_ref, vl_ref, o_ref):
    is_latent = pl.program_id(0) >= TP // TQ

    @pl.when(jnp.logical_not(is_latent))
    def _():
        _attn_kernel(1, q_ref, kp_ref, vp_ref, o_ref)

    @pl.when(is_latent)
    def _():
        _attn_kernel(2, q_ref, kc_ref, vc_ref, kl_ref, vl_ref, o_ref)


def _attention(q, k, v, kc, vc):
    w = HEADS * LANES
    tile = lambda i: (i, 0)
    ctx_own = lambda i: (jnp.minimum(i, N_PROMPT_SEQ - 1), 0)
    cache = lambda i: (_latent_seq(i), 0)
    lat_own = lambda i: (TP // LAT_LEN + _latent_seq(i), 0)
    return pl.pallas_call(
        _attn_tiles_kernel,
        grid=(T // TQ,),
        in_specs=[pl.BlockSpec((TQ, w), tile),
                  pl.BlockSpec((PROMPT_LEN, w), ctx_own), pl.BlockSpec((PROMPT_LEN, w), ctx_own),
                  pl.BlockSpec((PAST, w), cache), pl.BlockSpec((PAST, w), cache),
                  pl.BlockSpec((LAT_LEN, w), lat_own), pl.BlockSpec((LAT_LEN, w), lat_own)],
        out_specs=pl.BlockSpec((TQ, HEADS * VDIM), tile),
        out_shape=jax.ShapeDtypeStruct((T, HEADS * VDIM), BF16),
        compiler_params=_cparams(("arbitrary",)),
        name="attention",
    )(q, k, v, kc, vc, k, v)


def _ret_prefix_kernel(lg_ref, rk_ref, rv_ref, s0f_ref, s0b_ref, pf_ref, qb_ref, sf_scr, sb_scr):
    s = pl.program_id(0)
    n_tiles, per_seq = TS // TQ, LAT_LEN // TQ
    row = lax.broadcasted_iota(jnp.int32, (LANES, 1), 0)
    lane = lax.broadcasted_iota(jnp.int32, (1, LANES), 1)
    top, lo = row < RDK, lane < RDK
    same_head = top == lo
    m_col = lax.broadcasted_iota(jnp.int32, (TQ, 1), 0).astype(F32)

    def scan_step(first, lg_off, s0_ref, scr, out_ref, pos):
        @pl.when(first)
        def _():
            scr[...] = s0_ref[...]

        for pair in range(HEADS // 2):
            sl = slice(pair * LANES, (pair + 1) * LANES)
            lg_even, lg_odd = lg_ref[lg_off + 2 * pair], lg_ref[lg_off + 2 * pair + 1]
            dec = jnp.exp(pos * jnp.where(lo, lg_even, lg_odd))
            local = _bdot((rk_ref[:, sl].astype(F32) * dec).T.astype(BF16), rv_ref[:, sl])
            out_ref[pair] = scr[pair].astype(BF16)
            scr[pair] = scr[pair] * jnp.exp(float(TQ) * jnp.where(top, lg_even, lg_odd)) + jnp.where(same_head, local, 0.0)

    @pl.when(s < n_tiles)
    def _():
        scan_step(s % per_seq == 0, 0, s0f_ref, sf_scr, pf_ref, TQ - 1.0 - m_col)

    @pl.when(s >= n_tiles)
    def _():
        scan_step((2 * n_tiles - 1 - s) % per_seq == per_seq - 1, HEADS, s0b_ref, sb_scr, qb_ref, m_col)


def _ret_prefix(lg, rk, rv, s0f, s0b):
    n_tiles, per_seq = TS // TQ, LAT_LEN // TQ
    tile_of = lambda s: jnp.where(s < n_tiles, s, 2 * n_tiles - 1 - s)
    st = pl.BlockSpec((None, HEADS // 2, LANES, LANES), lambda s, lg: (tile_of(s) // per_seq, 0, 0, 0))
    kv = pl.BlockSpec((TQ, HEADS * RDK), lambda s, lg: (TP // TQ + tile_of(s), 0))
    pf_blk = pl.BlockSpec((None, HEADS // 2, LANES, LANES), lambda s, lg: (jnp.minimum(s, n_tiles - 1), 0, 0, 0))
    qb_blk = pl.BlockSpec((None, HEADS // 2, LANES, LANES),
                          lambda s, lg: (jnp.minimum(2 * n_tiles - 1 - s, n_tiles - 1), 0, 0, 0))
    shape = jax.ShapeDtypeStruct((n_tiles, HEADS // 2, LANES, LANES), BF16)
    return pl.pallas_call(
        _ret_prefix_kernel,
        grid_spec=pltpu.PrefetchScalarGridSpec(
            num_scalar_prefetch=1, grid=(2 * n_tiles,),
            in_specs=[kv, kv, st, st], out_specs=[pf_blk, qb_blk],
            scratch_shapes=[pltpu.VMEM((HEADS // 2, LANES, LANES), F32)] * 2),
        out_shape=[shape, shape],
        compiler_params=_cparams(("arbitrary",)),
        name="ret_prefix",
    )(lg, rk, rv, s0f, s0b)


def _retention_kernel(latent, lg_ref, rq_ref, rk_ref, rv_ref, rg_ref, *refs):
    q0 = 0
    seq_len = rq_ref.shape[0]
    if latent:
        s0f_ref, s0b_ref, o_ref = refs
    else:
        unperm_ref, o_ref, sf_ref, sb_ref = refs
    tq, tk = rq_ref.shape[0], rk_ref.shape[0]
    nt = (((1,), (1,)), ((), ()))
    n_idx = (q0 + lax.broadcasted_iota(jnp.int32, (tq, tk), 0)).astype(F32)
    m_idx = lax.broadcasted_iota(jnp.int32, (tq, tk), 1).astype(F32)
    dist = n_idx - m_idx
    adist = jnp.abs(dist)
    fwd = dist > 0.0
    diag = jnp.where(dist == 0.0, 1.0, 0.0)
    lane = lax.broadcasted_iota(jnp.int32, (1, LANES), 1)
    lo = lane < RDK
    n_col = (q0 + lax.broadcasted_iota(jnp.int32, (tq, 1), 0)).astype(F32)
    m_col = lax.broadcasted_iota(jnp.int32, (tk, 1), 0).astype(F32)
    for pair in range(HEADS // 2):
        sl = slice(pair * LANES, (pair + 1) * LANES)
        qb, kb, vb = rq_ref[:, sl], rk_ref[:, sl], rv_ref[:, sl]
        acc = jnp.zeros((tq, LANES), F32)
        for sub in range(2):
            h = 2 * pair + sub
            lgf, lgb = lg_ref[h], lg_ref[HEADS + h]
            half = lo if sub == 0 else jnp.logical_not(lo)
            qm = jnp.where(half, qb, jnp.zeros_like(qb))
            vm = jnp.where(half, vb, jnp.zeros_like(vb))
            s = lax.dot_general(qm, kb, nt, preferred_element_type=F32)
            w = jnp.exp(adist * jnp.where(fwd, lgf, lgb)) + diag
            acc = acc + _bdot((s * w).astype(BF16), vm)
        lgf_l = jnp.where(lo, lg_ref[2 * pair], lg_ref[2 * pair + 1])
        lgb_l = jnp.where(lo, lg_ref[HEADS + 2 * pair], lg_ref[HEADS + 2 * pair + 1])
        if latent:
            acc = acc + _bdot(qb, s0f_ref[pair]) * jnp.exp((n_col + 1.0) * lgf_l)
            acc = acc + _bdot(qb, s0b_ref[pair]) * jnp.exp((seq_len - n_col) * lgb_l)
        else:
            v_swapped = pltpu.roll(vb.astype(F32), RDK, axis=1).astype(BF16)
            for st_ref, dec in ((sf_ref, jnp.exp((seq_len - 1.0 - m_col) * lgf_l)),
                                (sb_ref, jnp.exp(m_col * lgb_l))):
                kt = (kb.astype(F32) * dec).T.astype(BF16)
                kt = _bdot(unperm_ref[...], kt).astype(BF16)
                st_ref[2 * pair] = _bdot(kt, vb)[0:RDK, 0:RDK]
                st_ref[2 * pair + 1] = _bdot(kt, v_swapped)[RDK:, 0:RDK]
        inv = 1.0 / RDK
        mu = jnp.where(lo, jnp.sum(jnp.where(lo, acc, 0.0), axis=-1, keepdims=True),
                       jnp.sum(jnp.where(lo, 0.0, acc), axis=-1, keepdims=True)) * inv
        dlt = acc - mu
        d2 = dlt * dlt
        var = jnp.where(lo, jnp.sum(jnp.where(lo, d2, 0.0), axis=-1, keepdims=True),
                        jnp.sum(jnp.where(lo, 0.0, d2), axis=-1, keepdims=True)) * inv
        g = rg_ref[:, sl]
        o_ref[:, sl] = (dlt * lax.rsqrt(var + 1e-5) * (g * _sigmoid(g))).astype(BF16)


def _retention_tiles_kernel(n_prev, lg_ref, rq_ref, rk_ref, rv_ref, rg_ref, pf_ref, qb_ref, unperm_ref, *refs):
    prev_refs, (o_ref, sf_ref, sb_ref) = refs[:2 * n_prev], refs[2 * n_prev:]
    is_latent = pl.program_id(0) >= TP // TQ

    @pl.when(jnp.logical_not(is_latent))
    def _():
        for k in range(n_prev):
            sf_ref[k] = prev_refs[2 * k][...]
            sb_ref[k] = prev_refs[2 * k + 1][...]
        _retention_kernel(False, lg_ref, rq_ref, rk_ref, rv_ref, rg_ref, unperm_ref, o_ref,
                          sf_ref.at[n_prev], sb_ref.at[n_prev])

    @pl.when(is_latent)
    def _():
        _retention_kernel(True, lg_ref, rq_ref, rk_ref, rv_ref, rg_ref, pf_ref, qb_ref, o_ref)


def _retention(lg, rq, rk, rv, rg, s0f, s0b, unperm, prev_states):
    w = HEADS * RDK
    n_prev = 0 if prev_states is None else prev_states[0].shape[1]
    pf, qb = _ret_prefix(lg, rk, rv, s0f, s0b)
    tile = lambda i, lg: (i, 0)
    s0_blk = pl.BlockSpec((None, HEADS // 2, LANES, LANES), lambda i, lg: (jnp.maximum(i - TP // TQ, 0), 0, 0, 0))
    seq = lambda i, lg: (jnp.minimum(i, N_PROMPT_SEQ - 1), 0, 0, 0, 0)
    st_blk = pl.BlockSpec((None, n_prev + 1, HEADS, RDK, RDK), seq)
    st_shape = jax.ShapeDtypeStruct((N_PROMPT_SEQ, n_prev + 1, HEADS, RDK, RDK), F32)
    prev = () if prev_states is None else tuple(prev_states)
    prev_specs = [pl.BlockSpec((None, None, HEADS, RDK, RDK),
                               functools.partial(lambda k, i, lg: (jnp.minimum(i, N_PROMPT_SEQ - 1), k, 0, 0, 0), k))
                  for k in range(n_prev) for _ in range(2)]
    prev_args = [p for k in range(n_prev) for p in prev]
    return pl.pallas_call(
        functools.partial(_retention_tiles_kernel, n_prev),
        grid_spec=pltpu.PrefetchScalarGridSpec(
            num_scalar_prefetch=1, grid=(T // TQ,),
            in_specs=[pl.BlockSpec((TQ, w), tile)] * 4 + [s0_blk, s0_blk,
                      pl.BlockSpec((LANES, LANES), lambda i, lg: (0, 0))] + prev_specs,
            out_specs=[pl.BlockSpec((TQ, w), tile), st_blk, st_blk]),
        out_shape=[jax.ShapeDtypeStruct((T, w), BF16), st_shape, st_shape],
        compiler_params=_cparams(("arbitrary",)),
        name="retention",
    )(lg, rq, rk, rv, rg, pf, qb, unperm, *prev_args)


def _mix_ffn_kernel(n_x, *refs):
    x_refs, refs = refs[:n_x], refs[n_x:]
    a_ref, r_ref, mod_ref, wo_ref, wg_ref, wu_ref, wd_ref, ln_ref, o_ref, x1_scr, h_scr, acc_scr = refs
    f = pl.program_id(1)

    @pl.when(f == 0)
    def _():
        half = HEADS * VDIM
        y = _bdot(a_ref[...], wo_ref[0:half, :]) + _bdot(r_ref[...], wo_ref[half:, :])
        x1 = _layer_norm(ALPHA * _tile_rows(x_refs, TM_FF) + mod_ref[2:3, :] * y, ln_ref[0:1, :], ln_ref[1:2, :])
        x1_scr[...] = x1
        h_scr[...] = (x1 * (1.0 + mod_ref[4:5, :]) + mod_ref[3:4, :]).astype(BF16)
        acc_scr[...] = jnp.zeros_like(acc_scr)

    h = h_scr[...]
    g = _bdot(h, wg_ref[...])
    u = _bdot(h, wu_ref[...])
    acc_scr[...] += _bdot((g * _sigmoid(g) * u).astype(BF16), wd_ref[...])

    @pl.when(f == pl.num_programs(1) - 1)
    def _():
        o_ref[...] = _layer_norm(ALPHA * x1_scr[...] + mod_ref[5:6, :] * acc_scr[...], ln_ref[2:3, :], ln_ref[3:4, :])


def _mix_ffn(x, attn, ret, mod_l, w_out, wg, wu, wd, ln, j):
    x_args, x_specs = _tile_specs(x, TM_FF)
    tok = lambda w: pl.BlockSpec((TM_FF, w), lambda i, f: (i, 0))
    return pl.pallas_call(
        functools.partial(_mix_ffn_kernel, len(x_args)),
        grid=(T // TM_FF, D_FF // TF),
        in_specs=x_specs + [tok(HEADS * VDIM), tok(HEADS * RDK),
                  pl.BlockSpec((None, 8, D), lambda i, f: (_group_of_tile(i, TM_FF), 0, 0)),
                  pl.BlockSpec((None,) + w_out.shape[1:], lambda i, f: (j, 0, 0)),
                  pl.BlockSpec((None, D, TF), lambda i, f: (j, 0, f)),
                  pl.BlockSpec((None, D, TF), lambda i, f: (j, 0, f)),
                  pl.BlockSpec((None, TF, D), lambda i, f: (j, f, 0)), pl.BlockSpec((8, D), lambda i, f: (0, 0))],
        out_specs=tok(D),
        out_shape=jax.ShapeDtypeStruct((T, D), F32),
        scratch_shapes=[pltpu.VMEM((TM_FF, D), F32), pltpu.VMEM((TM_FF, D), BF16), pltpu.VMEM((TM_FF, D), F32)],
        compiler_params=_cparams(("parallel", "arbitrary")),
        name="mix_ffn",
    )(*x_args, attn, ret, mod_l, w_out, wg, wu, wd, ln)


def _conv_in_kernel(x_ref, mod_ref, w_ref, b_ref, z_ref):
    h = (x_ref[...] * (1.0 + mod_ref[1:2, :]) + mod_ref[0:1, :]).astype(BF16)
    p = _bdot(h, w_ref[...])
    b_ref[...] = p[:, 0:D]
    z_ref[...] = p[:, D:2 * D] * p[:, 2 * D:3 * D]


def _conv_in(x, mod_l, w_in, j):
    tok = pl.BlockSpec((TM_FF, D), lambda i: (i, 0))
    return pl.pallas_call(
        _conv_in_kernel,
        grid=(T // TM_FF,),
        in_specs=[tok, pl.BlockSpec((None, 8, D), lambda i: (_group_of_tile(i, TM_FF), 0, 0)),
                  pl.BlockSpec((None,) + w_in.shape[1:], lambda i: (j, 0, 0))],
        out_specs=[tok, tok],
        out_shape=[jax.ShapeDtypeStruct((T, D), F32)] * 2,
        compiler_params=_cparams(("parallel",)),
        name="conv_in",
    )(x, mod_l, w_in)


def _conv_out_kernel(x_ref, b_ref, z_ref, zp_ref, zn_ref, mod_ref, cw_ref, w_ref, ln_ref, rw_ref, rb_ref,
                     o_ref, h_ref, route_ref):
    i = pl.program_id(0)
    z = z_ref[...]
    tm = z.shape[0]
    row = lax.broadcasted_iota(jnp.int32, (tm, 1), 0)
    seq_len = jnp.where(i < TP // tm, PROMPT_LEN, LAT_LEN)
    pos = (i * tm + row) & (seq_len - 1)
    prev = jnp.where(row == 0, zp_ref[7:8, :], pltpu.roll(z, 1, axis=0))
    prev = jnp.where(pos == 0, 0.0, prev)
    nxt = jnp.where(row == tm - 1, zn_ref[0:1, :], pltpu.roll(z, tm - 1, axis=0))
    nxt = jnp.where(pos == seq_len - 1, 0.0, nxt)
    y = prev * cw_ref[0:1, :] + z * cw_ref[1:2, :] + nxt * cw_ref[2:3, :]
    t = _bdot((b_ref[...] * y).astype(BF16), w_ref[...])
    x1 = _layer_norm(ALPHA * x_ref[...] + mod_ref[2:3, :] * t, ln_ref[0:1, :], ln_ref[1:2, :])
    o_ref[...] = x1
    h = x1 * (1.0 + mod_ref[4:5, :]) + mod_ref[3:4, :]
    h_hi = h.astype(BF16)
    h_ref[...] = h_hi
    h_lo = (h - h_hi.astype(F32)).astype(BF16)
    both = _bdot(h_hi, rw_ref[...])
    logits = both[:, :LANES] + both[:, LANES:] + _bdot(h_lo, rw_ref[:, :LANES]) + rb_ref[...]
    lane = lax.broadcasted_iota(jnp.int32, logits.shape, 1).astype(F32)
    t1 = jnp.max(logits, axis=-1, keepdims=True)
    i1 = jnp.min(jnp.where(logits == t1, lane, float(LANES)), axis=-1, keepdims=True)
    rest = jnp.where(lane == i1, -jnp.inf, logits)
    t2 = jnp.max(rest, axis=-1, keepdims=True)
    i2 = jnp.min(jnp.where(rest == t2, lane, float(LANES)), axis=-1, keepdims=True)
    e = jnp.exp(t2 - t1)
    den = 1.0 + e
    route_ref[...] = jnp.where(lane == 0.0, i1, jnp.where(lane == 1.0, i2,
                               jnp.where(lane == 2.0, 1.0 / den, jnp.where(lane == 3.0, e / den, 0.0))))


def _conv_out(x, b, z, mod_l, cw, w_out, ln, rw, rb, j):
    tok = pl.BlockSpec((TM_FF, D), lambda i: (i, 0))
    sub = TM_FF // 8
    return pl.pallas_call(
        _conv_out_kernel,
        grid=(T // TM_FF,),
        in_specs=[tok, tok, tok,
                  pl.BlockSpec((8, D), lambda i: (jnp.maximum(i * sub - 1, 0), 0)),
                  pl.BlockSpec((8, D), lambda i: (jnp.minimum((i + 1) * sub, T // 8 - 1), 0)),
                  pl.BlockSpec((None, 8, D), lambda i: (_group_of_tile(i, TM_FF), 0, 0)),
                  pl.BlockSpec((8, D), lambda i: (0, 0)), pl.BlockSpec((None, D, D), lambda i: (j, 0, 0)),
                  pl.BlockSpec((8, D), lambda i: (0, 0)),
                  pl.BlockSpec((D, 2 * LANES), lambda i: (0, 0)), pl.BlockSpec((1, LANES), lambda i: (0, 0))],
        out_specs=[tok, tok, pl.BlockSpec((TM_FF, LANES), lambda i: (i, 0))],
        out_shape=[jax.ShapeDtypeStruct((T, D), F32), jax.ShapeDtypeStruct((T, D), BF16),
                   jax.ShapeDtypeStruct((T, LANES), F32)],
        compiler_params=_cparams(("parallel",)),
        name="conv_out",
    )(x, b, z, z, z, mod_l, cw, w_out, ln, rw, rb)


def _moe_up_kernel(te_ref, nv_ref, x_ref, wg_ref, wu_ref, a_ref):
    @pl.when(pl.program_id(1) < nv_ref[0])
    def _():
        h = x_ref[...]
        g = _wdot(h, wg_ref[...])
        u = _wdot(h, wu_ref[...])
        a_ref[...] = (g * _sigmoid(g) * u).astype(BF16)

    @pl.when(pl.program_id(1) >= nv_ref[0])
    def _():
        a_ref[...] = jnp.zeros_like(a_ref)


def _moe_down_kernel(te_ref, nv_ref, a_ref, wd_ref, o_ref):
    @pl.when(pl.program_id(0) < nv_ref[0])
    def _():
        o_ref[...] = _wdot(a_ref[...], wd_ref[...])

    @pl.when(pl.program_id(0) >= nv_ref[0])
    def _():
        o_ref[...] = jnp.zeros_like(o_ref)


def _moe(tile_expert, n_valid, xs, wg, wu, wd, j):
    n_tiles = NP_ROWS // TM_FF
    act = pl.pallas_call(
        _moe_up_kernel,
        grid_spec=pltpu.PrefetchScalarGridSpec(
            num_scalar_prefetch=2, grid=(D_FF // TF, n_tiles),
            in_specs=[pl.BlockSpec((TM_FF, D), lambda f, i, te, nv: (i, 0)),
                      pl.BlockSpec((None, None, D, TF), lambda f, i, te, nv: (j, te[i], 0, f)),
                      pl.BlockSpec((None, None, D, TF), lambda f, i, te, nv: (j, te[i], 0, f))],
            out_specs=pl.BlockSpec((TM_FF, TF), lambda f, i, te, nv: (i, f))),
        out_shape=jax.ShapeDtypeStruct((NP_ROWS, D_FF), BF16),
        compiler_params=_cparams(("arbitrary", "arbitrary")),
        name="moe_up",
    )(tile_expert, n_valid, xs, wg, wu)
    return pl.pallas_call(
        _moe_down_kernel,
        grid_spec=pltpu.PrefetchScalarGridSpec(
            num_scalar_prefetch=2, grid=(n_tiles,),
            in_specs=[pl.BlockSpec((TM_FF, D_FF), lambda i, te, nv: (i, 0)),
                      pl.BlockSpec((None, None, D_FF, D), lambda i, te, nv: (j, te[i], 0, 0))],
            out_specs=pl.BlockSpec((TM_FF, D), lambda i, te, nv: (i, 0))),
        out_shape=jax.ShapeDtypeStruct((NP_ROWS, D), F32),
        compiler_params=_cparams(("arbitrary",)),
        name="moe_down",
    )(tile_expert, n_valid, act, wd)


def _combine_kernel(split, x_ref, o0_ref, o1_ref, route_ref, mod_ref, ln_ref, *o_refs):
    y = route_ref[:, 2:3] * o0_ref[...] + route_ref[:, 3:4] * o1_ref[...]
    out = _layer_norm(ALPHA * x_ref[...] + mod_ref[5:6, :] * y, ln_ref[2:3, :], ln_ref[3:4, :])
    if not split:
        o_refs[0][...] = out
        return
    is_latent = pl.program_id(0) >= TP // TM_FF

    @pl.when(jnp.logical_not(is_latent))
    def _():
        o_refs[0][...] = out

    @pl.when(is_latent)
    def _():
        o_refs[1][...] = out


def _combine(x, o0, o1, route, mod_l, ln, split):
    tok = pl.BlockSpec((TM_FF, D), lambda i: (i, 0))
    if split:
        out_specs = [pl.BlockSpec((TM_FF, D), lambda i: (jnp.minimum(i, TP // TM_FF - 1), 0)),
                     pl.BlockSpec((TM_FF, D), lambda i: (jnp.maximum(i - TP // TM_FF, 0), 0))]
        out_shape = [jax.ShapeDtypeStruct((TP, D), F32), jax.ShapeDtypeStruct((TS, D), F32)]
    else:
        out_specs, out_shape = tok, jax.ShapeDtypeStruct((T, D), F32)
    return pl.pallas_call(
        functools.partial(_combine_kernel, split),
        grid=(T // TM_FF,),
        in_specs=[tok, tok, tok, pl.BlockSpec((TM_FF, LANES), lambda i: (i, 0)),
                  pl.BlockSpec((None, 8, D), lambda i: (_group_of_tile(i, TM_FF), 0, 0)),
                  pl.BlockSpec((8, D), lambda i: (0, 0))],
        out_specs=out_specs,
        out_shape=out_shape,
        compiler_params=_cparams(("arbitrary",)),
        name="moe_combine",
    )(x, o0, o1, route, mod_l, ln)


def _routing_plan(route):
    e = route[:, 0:2].astype(jnp.int32).reshape(-1)
    onehot = (e[:, None] == jnp.arange(N_EXP, dtype=jnp.int32)[None, :]).astype(jnp.int32)
    csum = jnp.cumsum(onehot, axis=0)
    counts = csum[-1]
    rank = jnp.sum((csum - onehot) * onehot, axis=1)
    padded = (counts + TM_FF - 1) // TM_FF * TM_FF
    pend = jnp.cumsum(padded)
    dest = jnp.sum(onehot * (pend - padded)[None, :], axis=1) + rank
    order = jnp.argsort(e, stable=True).astype(jnp.int32)
    rows = jnp.arange(NP_ROWS, dtype=jnp.int32)
    before = (rows[:, None] >= pend[None, :]).astype(jnp.int32)
    row_e = jnp.minimum(jnp.sum(before, axis=1), N_EXP - 1)
    row_cnt = jnp.sum((row_e[:, None] == jnp.arange(N_EXP, dtype=jnp.int32)[None, :]) * counts[None, :], axis=1)
    q = jnp.clip(rows - jnp.sum(before * padded[None, :], axis=1), 0, jnp.maximum(row_cnt - 1, 0))
    src = jnp.minimum(jnp.sum(before * counts[None, :], axis=1) + q, 2 * T - 1)
    row_token = order[src] // 2
    n_valid = (pend[-1] // TM_FF).astype(jnp.int32)
    tile_start = jnp.minimum(jnp.arange(NP_ROWS // TM_FF, dtype=jnp.int32), n_valid - 1) * TM_FF
    tile_expert = jnp.minimum(jnp.sum((tile_start[:, None] >= pend[None, :]).astype(jnp.int32), axis=1), N_EXP - 1)
    return dest.reshape(T, 2), row_token, tile_expert.astype(jnp.int32), n_valid.reshape(1)


_INV_PERM_RDK = np.argsort(np.concatenate([np.arange(0, RDK, 2), np.arange(1, RDK, 2)]))


def _pad_cols(a, width):
    return jnp.pad(a, ((0, 0), (0, width - a.shape[1])))


def _deinterleave(a):
    n = a.shape[-1]
    return jnp.swapaxes(a.reshape(a.shape[:-1] + (n // 2, 2)), -1, -2).reshape(a.shape)


def _prep_even_weights(w_in, w_q_b, w_kv_b):
    o_kpe, o_rq, o_rk, o_rv = 640, 672, 1184, 1696
    heads = lambda a: _deinterleave(a.reshape(D, HEADS, RDK)).reshape(D, HEADS * RDK)
    kpe = w_in[:, o_kpe:o_kpe + ROPE]
    w_in_p = jnp.concatenate([
        w_in[:, :640], heads(w_in[:, o_rq:o_rk]), heads(w_in[:, o_rk:o_rv]) * (RDK ** -0.5), w_in[:, o_rv:],
        _pad_cols(kpe, LANES), _pad_cols(_deinterleave(kpe), LANES),
        _pad_cols(jnp.concatenate([-kpe[:, 1::2], kpe[:, 0::2]], axis=1), LANES)], axis=1).astype(BF16)
    wq = w_q_b.reshape(Q_LORA, HEADS, NOPE + ROPE)
    wq = jnp.concatenate([wq[:, :, :NOPE], _deinterleave(wq[:, :, NOPE:]),
                          jnp.zeros((Q_LORA, HEADS, LANES - NOPE - ROPE), F32)], axis=2)
    wq = wq.reshape(Q_LORA, HEADS * LANES).astype(BF16)
    wkv = w_kv_b.reshape(KV_LORA, HEADS, NOPE + VDIM)
    zero = jnp.zeros((KV_LORA, HEADS, LANES - NOPE), F32)
    wk = jnp.concatenate([wkv[:, :, :NOPE], zero], axis=2).reshape(KV_LORA, HEADS * LANES)
    wv = wkv[:, :, NOPE:].reshape(KV_LORA, HEADS // 2, 2, VDIM)
    zv = jnp.zeros((KV_LORA, HEADS // 2, VDIM), F32)
    wv = jnp.stack([jnp.concatenate([wv[:, :, 0], zv], axis=2), jnp.concatenate([zv, wv[:, :, 1]], axis=2)], axis=2)
    wkv_p = jnp.concatenate([wk, wv.reshape(KV_LORA, HEADS * LANES)], axis=1).astype(BF16)
    return w_in_p, wq, wkv_p


def _placement():
    ek = np.zeros((LANES, HEADS * LANES), np.float32)
    for h in range(HEADS):
        ek[np.arange(ROPE), h * LANES + NOPE + np.arange(ROPE)] = 1.0
    return jnp.asarray(ek, BF16)


def _rotary_tables():
    rows = LAT_LEN // GRID_W
    r, col = jnp.meshgrid(jnp.arange(rows, dtype=F32), jnp.arange(GRID_W, dtype=F32), indexing='ij')
    n_freq = ROPE // 4
    freqs = 1.0 / (10000.0 ** (jnp.arange(n_freq, dtype=F32) / n_freq))
    ang = jnp.concatenate([r.reshape(-1)[:, None] * freqs, col.reshape(-1)[:, None] * freqs], axis=-1)
    cos, sin = jnp.cos(ang), jnp.sin(ang)
    theta = 1.0 / (10000.0 ** jnp.linspace(0.0, 1.0, RDK // 2, dtype=F32))
    rang = jnp.arange(LAT_LEN, dtype=F32)[:, None] * theta
    rcos, rsin = jnp.cos(rang), jnp.sin(rang)
    one = lambda w: jnp.ones((LAT_LEN, w), F32)
    zero = lambda w: jnp.zeros((LAT_LEN, w), F32)
    lat = [jnp.concatenate([one(NOPE), cos, cos, one(LANES - NOPE - ROPE)], axis=1),
           jnp.concatenate([zero(NOPE), sin, sin, zero(LANES - NOPE - ROPE)], axis=1),
           jnp.concatenate([cos, cos, zero(LANES - ROPE)], axis=1),
           jnp.concatenate([sin, sin, zero(LANES - ROPE)], axis=1),
           jnp.concatenate([rcos] * 4, axis=1), jnp.concatenate([rsin] * 4, axis=1)]
    ident = [np.ones((TM, LANES), np.float32), np.zeros((TM, LANES), np.float32)]
    ident_k = np.concatenate([np.ones((TM, ROPE), np.float32), np.zeros((TM, LANES - ROPE), np.float32)], axis=1)
    ident = [ident[0], ident[1], ident_k, ident[1], ident[0], ident[1]]
    return [jnp.concatenate([l, jnp.asarray(c)], axis=0) for l, c in zip(lat, ident)]


def _block_diag_states(s0):
    s = jnp.swapaxes(_deinterleave(jnp.swapaxes(s0, -1, -2)), -1, -2)
    s = s.reshape(s0.shape[0], HEADS // 2, 2, RDK, RDK)
    z = jnp.zeros_like(s[:, :, 0])
    top = jnp.concatenate([s[:, :, 0], z], axis=-1)
    bot = jnp.concatenate([z, s[:, :, 1]], axis=-1)
    return jnp.concatenate([top, bot], axis=-2)


def _unpermute_matrix():
    m = np.zeros((LANES, LANES), np.float32)
    for blk in range(LANES // RDK):
        m[blk * RDK + np.arange(RDK), blk * RDK + _INV_PERM_RDK] = 1.0
    return jnp.asarray(m, BF16)


def kernel(x_prompt, x_sample, c, cache_ckv, cache_kpe, state_ret_fwd, state_ret_bwd, c_ctx, w_mod, b_mod, ln_g, ln_b, w_in_mix, q_a_gain, kv_a_gain, w_q_b, w_kv_b, ret_decay_fwd, ret_decay_bwd, w_out_mix, w_in_conv, conv_w, w_out_conv, ffn_gate, ffn_up, ffn_down, router_w, router_b, exp_gate, exp_up, exp_down):
    x = (x_prompt.reshape(TP, D), x_sample.reshape(TS, D))
    cond8 = jnp.concatenate([c_ctx[None], c, jnp.zeros((8 - 1 - N_LAT_SEQ, D), F32)], axis=0)
    mods = _modulation(cond8, w_mod, b_mod)
    mods = jnp.pad(mods.reshape(DEPTH, N_GROUPS, 6, D), ((0, 0), (0, 0), (0, 2), (0, 0)))
    ln = jnp.pad(jnp.concatenate([ln_g, ln_b], axis=1)[:, jnp.array([0, 2, 1, 3])], ((0, 0), (0, 4), (0, 0)))
    tabs = _rotary_tables()
    ek = _placement()
    unperm = _unpermute_matrix()
    bf = lambda a: a.astype(BF16)
    w_out_mix_b, w_in_conv_b, w_out_conv_b = bf(w_out_mix), bf(w_in_conv), bf(w_out_conv)
    ffn_b = (bf(ffn_gate), bf(ffn_up), bf(ffn_down))
    exp_b = (exp_gate, exp_up, exp_down)
    cache, states = None, None
    for layer in range(DEPTH):
        j = layer // 2
        mod_l, ln_l = mods[layer], ln[layer]
        if layer % 2 == 0:
            w_in_p, wq, wkv = _prep_even_weights(w_in_mix[j], w_q_b[j], w_kv_b[j])
            q, k, v, cache, rq, rk, rv, rg = _even_in(
                x, mod_l, w_in_p, q_a_gain[j][None], kv_a_gain[j][None], wq, wkv, ek, tabs, cache)
            kpe_c = _pad_cols(_deinterleave(cache_kpe[:, j]).reshape(N_LAT_SEQ * PAST, ROPE), LANES)
            kc, vc = _ctx_kv(cache_ckv[:, j].reshape(N_LAT_SEQ * PAST, KV_LORA), kpe_c, wkv, ek)
            attn = _attention(q, k, v, kc, vc)
            lg = jnp.concatenate([jax.nn.log_sigmoid(ret_decay_fwd[j].astype(F32)),
                                  jax.nn.log_sigmoid(ret_decay_bwd[j].astype(F32))])
            ret, sf, sb = _retention(lg, rq, rk, rv, rg, _block_diag_states(state_ret_fwd[:, j]),
                                     _block_diag_states(state_ret_bwd[:, j]), unperm, states)
            states = (sf, sb)
            x = _mix_ffn(x, attn, ret, mod_l, w_out_mix_b, *ffn_b, ln_l, j)
        else:
            b, z = _conv_in(x, mod_l, w_in_conv_b, j)
            cw = jnp.pad(conv_w[j], ((0, 5), (0, 0)))
            rw = _pad_cols(router_w[j], LANES)
            rw_hi = rw.astype(BF16)
            rw = jnp.concatenate([rw_hi, (rw - rw_hi.astype(F32)).astype(BF16)], axis=1)
            rb = jnp.concatenate([router_b[j].astype(F32), jnp.full((LANES - N_EXP,), -1e30, F32)])[None]
            x, h, route = _conv_out(x, b, z, mod_l, cw, w_out_conv_b, ln_l, rw, rb, j)
            dest, row_token, tile_expert, n_valid = _routing_plan(route)
            out_sorted = _moe(tile_expert, n_valid, h[row_token], *exp_b, j)
            x = _combine(x, out_sorted[dest[:, 0]], out_sorted[dest[:, 1]], route, mod_l, ln_l,
                         split=layer == DEPTH - 1)
    y_prompt = x[0].reshape(N_PROMPT_SEQ, PROMPT_LEN, D)
    y_sample = x[1].reshape(N_LAT_SEQ, LAT_LEN, D)
    return (y_prompt, y_sample, cache[0], cache[1], states[0], states[1])
```

```python
import functools

import numpy as np
import jax
import jax.numpy as jnp
from jax import lax
from jax.experimental import pallas as pl
from jax.experimental.pallas import tpu as pltpu

F32 = jnp.float32
BF16 = jnp.bfloat16

D = 1024
DEPTH = 4
N_PROMPT_SEQ, PROMPT_LEN = 32, 256
N_LAT_SEQ, LAT_LEN = 2, 2048
PAST = 512
GRID_W = 64
TP = N_PROMPT_SEQ * PROMPT_LEN
TS = N_LAT_SEQ * LAT_LEN
T = TP + TS
HEADS = 8
NOPE, ROPE, VDIM = 64, 32, 64
Q_LORA, KV_LORA = 384, 256
RDK = 64
D_FF = 2816
N_EXP = 8
ALPHA = (2.0 * DEPTH) ** 0.25
Q_SCALE = float((NOPE + ROPE) ** -0.5)
LANES = 128
N_GROUPS = 8

TM = 256
TM_FF = 512
TF = D_FF // 2
TQ = 256
NP_ROWS = 2 * T + N_EXP * TM_FF
VMEM_LIMIT = 56 * 1024 * 1024

IN_COLS = 3072


def _cparams(sem):
    return pltpu.CompilerParams(dimension_semantics=sem, vmem_limit_bytes=VMEM_LIMIT)


def _group_of_tile(i, tm):
    per_seq = LAT_LEN // tm
    return jnp.maximum(i - TP // tm + per_seq, 0) // per_seq


def _bdot(a, b):
    return jnp.dot(a, b, preferred_element_type=F32)


def _wdot(a, w):
    return lax.dot_general(a, w, (((1,), (0,)), ((), ())), preferred_element_type=F32)


def _sigmoid(v):
    return 1.0 / (1.0 + jnp.exp(-v))


def _layer_norm(v, g, b):
    mu = jnp.mean(v, axis=-1, keepdims=True)
    d = v - mu
    var = jnp.mean(d * d, axis=-1, keepdims=True)
    return d * lax.rsqrt(var + 1e-5) * g + b


def _rms(v, g):
    return v * lax.rsqrt(jnp.mean(v * v, axis=-1, keepdims=True) + 1e-6) * g


def _mod_kernel(c_ref, w_ref, b_ref, o_ref):
    c = c_ref[...]
    s = (c * _sigmoid(c)).astype(BF16)
    o_ref[...] = _bdot(s, w_ref[...].astype(BF16)) + b_ref[...]


def _modulation(cond8, w_mod, b_mod):
    tn = 1536
    return pl.pallas_call(
        _mod_kernel,
        grid=(DEPTH, 6 * D // tn),
        in_specs=[pl.BlockSpec((8, D), lambda l, n: (0, 0)),
                  pl.BlockSpec((None, D, tn), lambda l, n: (l, 0, n)),
                  pl.BlockSpec((None, 1, tn), lambda l, n: (l, 0, n))],
        out_specs=pl.BlockSpec((None, 8, tn), lambda l, n: (l, 0, n)),
        out_shape=jax.ShapeDtypeStruct((DEPTH, 8, 6 * D), F32),
        compiler_params=_cparams(("arbitrary", "arbitrary")),
        name="modulation",
    )(cond8, w_mod, b_mod.reshape(DEPTH, 1, 6 * D))


def _swap_halves(a, half):
    n = a.shape[-1]
    lane = lax.broadcasted_iota(jnp.int32, a.shape, 1)
    first = (lane & (2 * half - 1)) < half
    return jnp.where(first, -pltpu.roll(a, n - half, axis=1), pltpu.roll(a, half, axis=1))


def _tile_rows(x_refs, tm):
    if len(x_refs) == 1:
        return x_refs[0][...]
    return jnp.where(pl.program_id(0) >= TP // tm, x_refs[1][...], x_refs[0][...])


def _tile_specs(x, tm):
    if not isinstance(x, tuple):
        return (x,), [pl.BlockSpec((tm, D), lambda i, *_: (i, 0))]
    return x, [pl.BlockSpec((tm, D), lambda i, *_: (jnp.minimum(i, TP // tm - 1), 0)),
               pl.BlockSpec((tm, D), lambda i, *_: (jnp.maximum(i - TP // tm, 0), 0))]


def _even_in_kernel(n_x, n_prev, *refs):
    x_refs, refs = refs[:n_x], refs[n_x:]
    (mod_ref, w_in_ref, qg_ref, kvg_ref, wq_ref, wkv_ref, ek_ref,
     cq_ref, sq_ref, ck_ref, sk_ref, cr_ref, sr_ref), refs = refs[:13], refs[13:]
    prev_refs, refs = refs[:2 * n_prev], refs[2 * n_prev:]
    q_ref, k_ref, v_ref, ckv_ref, kpe_ref, rq_ref, rk_ref, rv_ref, rg_ref = refs
    x = _tile_rows(x_refs, TM)
    h = (x * (1.0 + mod_ref[1:2, :]) + mod_ref[0:1, :]).astype(BF16)
    p = _bdot(h, w_in_ref[...])
    qn = _rms(p[:, 0:Q_LORA], qg_ref[...]).astype(BF16)
    qa = _bdot(qn, wq_ref[...])
    ckv = _rms(p[:, Q_LORA:Q_LORA + KV_LORA], kvg_ref[...])
    kv = _bdot(ckv.astype(BF16), wkv_ref[...])
    v_ref[...] = kv[:, HEADS * LANES:].astype(BF16)
    base = 2688
    ka = p[:, base + LANES:base + 2 * LANES]
    kb = p[:, base + 2 * LANES:base + 3 * LANES]
    rq = p[:, 640:1152]
    rk = p[:, 1152:1664]
    rv_ref[...] = p[:, 1664:2176].astype(BF16)
    rg_ref[...] = p[:, 2176:2688]
    lane = lax.broadcasted_iota(jnp.int32, qa.shape, 1) & (LANES - 1)
    qb = jnp.where(lane < NOPE + ROPE // 2,
                   -pltpu.roll(qa, qa.shape[1] - ROPE // 2, axis=1),
                   pltpu.roll(qa, ROPE // 2, axis=1))
    cq = jnp.concatenate([cq_ref[...]] * HEADS, axis=1)
    sq = jnp.concatenate([sq_ref[...]] * HEADS, axis=1)
    q_ref[...] = ((qa * cq + qb * sq) * Q_SCALE).astype(BF16)
    kpe_rot = ka * ck_ref[...] + kb * sk_ref[...]
    k_ref[...] = (kv[:, :HEADS * LANES] + _bdot(kpe_rot.astype(BF16), ek_ref[...])).astype(BF16)
    cr = jnp.concatenate([cr_ref[...]] * 4, axis=1)
    sr = jnp.concatenate([sr_ref[...]] * 4, axis=1)
    rq_ref[...] = (rq * cr + _swap_halves(rq, RDK // 2) * sr).astype(BF16)
    rk_ref[...] = (rk * cr + _swap_halves(rk, RDK // 2) * sr).astype(BF16)

    @pl.when(pl.program_id(0) < TP // TM)
    def _():
        for k in range(n_prev):
            ckv_ref[k] = prev_refs[2 * k][...]
            kpe_ref[k] = prev_refs[2 * k + 1][...]
        ckv_ref[n_prev] = ckv
        kpe_ref[n_prev] = p[:, base:base + ROPE]


def _even_in(x, mod_l, w_in, qg, kvg, wq, wkv, ek, tabs, prev_cache):
    assert TM == PROMPT_LEN
    x_args, x_specs = _tile_specs(x, TM)
    n_prev = 0 if prev_cache is None else prev_cache[0].shape[1]
    tok = lambda w: pl.BlockSpec((TM, w), lambda i: (i, 0))
    full = lambda a: pl.BlockSpec(a.shape, lambda i: (0,) * a.ndim)
    lat_tiles = LAT_LEN // TM
    tab = pl.BlockSpec((TM, LANES), lambda i: (
        jnp.where(i < TP // TM, lat_tiles, jnp.maximum(i - TP // TM, 0) % lat_tiles), 0))
    seq = lambda i: (jnp.minimum(i, N_PROMPT_SEQ - 1), 0, 0, 0)
    seq_k = lambda k, i: (jnp.minimum(i, N_PROMPT_SEQ - 1), k, 0, 0)
    prev_args, prev_specs = [], []
    for k in range(n_prev):
        for a, w in zip(prev_cache, (KV_LORA, ROPE)):
            prev_args.append(a)
            prev_specs.append(pl.BlockSpec((None, None, PROMPT_LEN, w), functools.partial(seq_k, k)))
    tok_outs = lambda dims: ([tok(w) for w, _ in dims], [jax.ShapeDtypeStruct((T, w), dt) for w, dt in dims])
    qkv_specs, qkv_shapes = tok_outs([(HEADS * LANES, BF16)] * 3)
    ret_specs, ret_shapes = tok_outs([(512, BF16), (512, BF16), (512, BF16), (512, F32)])
    cache_specs = [pl.BlockSpec((None, n_prev + 1, PROMPT_LEN, w), seq) for w in (KV_LORA, ROPE)]
    cache_shapes = [jax.ShapeDtypeStruct((N_PROMPT_SEQ, n_prev + 1, PROMPT_LEN, w), F32) for w in (KV_LORA, ROPE)]
    q, k, v, ckv, kpe, rq, rk, rv, rg = pl.pallas_call(
        functools.partial(_even_in_kernel, len(x_args), n_prev),
        grid=(T // TM,),
        in_specs=x_specs + [pl.BlockSpec((None, 8, D), lambda i: (_group_of_tile(i, TM), 0, 0)),
                            full(w_in), full(qg), full(kvg), full(wq), full(wkv), full(ek)] + [tab] * 6 + prev_specs,
        out_specs=qkv_specs + cache_specs + ret_specs,
        out_shape=qkv_shapes + cache_shapes + ret_shapes,
        compiler_params=_cparams(("arbitrary",)),
        name="even_in",
    )(*x_args, mod_l, w_in, qg, kvg, wq, wkv, ek, *tabs, *prev_args)
    return q, k, v, (ckv, kpe), rq, rk, rv, rg


def _ctx_kv_kernel(ckv_ref, kpe_ref, wkv_ref, ek_ref, k_ref, v_ref):
    kv = _bdot(ckv_ref[...].astype(BF16), wkv_ref[...])
    k_ref[...] = (kv[:, :HEADS * LANES] + _bdot(kpe_ref[...].astype(BF16), ek_ref[...])).astype(BF16)
    v_ref[...] = kv[:, HEADS * LANES:].astype(BF16)


def _ctx_kv(ckv_c, kpe_c, wkv, ek):
    n = ckv_c.shape[0]
    full = lambda a: pl.BlockSpec(a.shape, lambda i: (0,) * a.ndim)
    return pl.pallas_call(
        _ctx_kv_kernel,
        grid=(n // PAST,),
        in_specs=[pl.BlockSpec((PAST, KV_LORA), lambda i: (i, 0)), pl.BlockSpec((PAST, LANES), lambda i: (i, 0)),
                  full(wkv), full(ek)],
        out_specs=[pl.BlockSpec((PAST, HEADS * LANES), lambda i: (i, 0))] * 2,
        out_shape=[jax.ShapeDtypeStruct((n, HEADS * LANES), BF16)] * 2,
        compiler_params=_cparams(("parallel",)),
        name="ctx_kv",
    )(ckv_c, kpe_c, wkv, ek)


def _attn_kernel(n_kv, q_ref, *refs):
    k_refs = refs[0:2 * n_kv:2]
    v_refs = refs[1:2 * n_kv:2]
    o_ref = refs[2 * n_kv]
    nt = (((1,), (1,)), ((), ()))
    for pair in range(HEADS // 2):
        acc = None
        for sub in range(2):
            sl = slice((2 * pair + sub) * LANES, (2 * pair + sub + 1) * LANES)
            qh = q_ref[:, sl]
            s = [lax.dot_general(qh, k[:, sl], nt, preferred_element_type=F32) for k in k_refs]
            m = functools.reduce(jnp.maximum, [jnp.max(a, axis=-1, keepdims=True) for a in s])
            e = [jnp.exp(a - m) for a in s]
            den = functools.reduce(jnp.add, [jnp.sum(a, axis=-1, keepdims=True) for a in e])
            o = functools.reduce(jnp.add, [_bdot(a.astype(BF16), v[:, sl]) for a, v in zip(e, v_refs)])
            o = o / den
            acc = o if acc is None else acc + o
        o_ref[:, pair * LANES:(pair + 1) * LANES] = acc.astype(BF16)


def _latent_seq(i):
    return jnp.maximum(i - TP // TQ, 0) // (LAT_LEN // TQ)


def _attn_tiles_kernel(q_ref, kp_ref, vp_ref, kc_ref, vc_ref, kl_ref, vl_ref, o_ref):
    is_latent = pl.program_id(0) >= TP // TQ

    @pl.when(jnp.logical_not(is_latent))
    def _():
        _attn_kernel(1, q_ref, kp_ref, vp_ref, o_ref)

    @pl.when(is_latent)
    def _():
        _attn_kernel(2, q_ref, kc_ref, vc_ref, kl_ref, vl_ref, o_ref)


def _attention(q, k, v, kc, vc):
    w = HEADS * LANES
    tile = lambda i: (i, 0)
    ctx_own = lambda i: (jnp.minimum(i, N_PROMPT_SEQ - 1), 0)
    cache = lambda i: (_latent_seq(i), 0)
    lat_own = lambda i: (TP // LAT_LEN + _latent_seq(i), 0)
    return pl.pallas_call(
        _attn_tiles_kernel,
        grid=(T // TQ,),
        in_specs=[pl.BlockSpec((TQ, w), tile),
                  pl.BlockSpec((PROMPT_LEN, w), ctx_own), pl.BlockSpec((PROMPT_LEN, w), ctx_own),
                  pl.BlockSpec((PAST, w), cache), pl.BlockSpec((PAST, w), cache),
                  pl.BlockSpec((LAT_LEN, w), lat_own), pl.BlockSpec((LAT_LEN, w), lat_own)],
        out_specs=pl.BlockSpec((TQ, HEADS * VDIM), tile),
        out_shape=jax.ShapeDtypeStruct((T, HEADS * VDIM), BF16),
        compiler_params=_cparams(("arbitrary",)),
        name="attention",
    )(q, k, v, kc, vc, k, v)


def _ret_prefix_kernel(lg_ref, rk_ref, rv_ref, s0f_ref, s0b_ref, pf_ref, qb_ref, sf_scr, sb_scr):
    s = pl.program_id(0)
    n_tiles, per_seq = TS // TQ, LAT_LEN // TQ
    row = lax.broadcasted_iota(jnp.int32, (LANES, 1), 0)
    lane = lax.broadcasted_iota(jnp.int32, (1, LANES), 1)
    top, lo = row < RDK, lane < RDK
    same_head = top == lo
    m_col = lax.broadcasted_iota(jnp.int32, (TQ, 1), 0).astype(F32)

    def scan_step(first, lg_off, s0_ref, scr, out_ref, pos):
        @pl.when(first)
        def _():
            scr[...] = s0_ref[...]

        for pair in range(HEADS // 2):
            sl = slice(pair * LANES, (pair + 1) * LANES)
            lg_even, lg_odd = lg_ref[lg_off + 2 * pair], lg_ref[lg_off + 2 * pair + 1]
            dec = jnp.exp(pos * jnp.where(lo, lg_even, lg_odd))
            local = _bdot((rk_ref[:, sl].astype(F32) * dec).T.astype(BF16), rv_ref[:, sl])
            out_ref[pair] = scr[pair].astype(BF16)
            scr[pair] = scr[pair] * jnp.exp(float(TQ) * jnp.where(top, lg_even, lg_odd)) + jnp.where(same_head, local, 0.0)

    @pl.when(s < n_tiles)
    def _():
        scan_step(s % per_seq == 0, 0, s0f_ref, sf_scr, pf_ref, TQ - 1.0 - m_col)

    @pl.when(s >= n_tiles)
    def _():
        scan_step((2 * n_tiles - 1 - s) % per_seq == per_seq - 1, HEADS, s0b_ref, sb_scr, qb_ref, m_col)


def _ret_prefix(lg, rk, rv, s0f, s0b):
    n_tiles, per_seq = TS // TQ, LAT_LEN // TQ
    tile_of = lambda s: jnp.where(s < n_tiles, s, 2 * n_tiles - 1 - s)
    st = pl.BlockSpec((None, HEADS // 2, LANES, LANES), lambda s, lg: (tile_of(s) // per_seq, 0, 0, 0))
    kv = pl.BlockSpec((TQ, HEADS * RDK), lambda s, lg: (TP // TQ + tile_of(s), 0))
    pf_blk = pl.BlockSpec((None, HEADS // 2, LANES, LANES), lambda s, lg: (jnp.minimum(s, n_tiles - 1), 0, 0, 0))
    qb_blk = pl.BlockSpec((None, HEADS // 2, LANES, LANES),
                          lambda s, lg: (jnp.minimum(2 * n_tiles - 1 - s, n_tiles - 1), 0, 0, 0))
    shape = jax.ShapeDtypeStruct((n_tiles, HEADS // 2, LANES, LANES), BF16)
    return pl.pallas_call(
        _ret_prefix_kernel,
        grid_spec=pltpu.PrefetchScalarGridSpec(
            num_scalar_prefetch=1, grid=(2 * n_tiles,),
            in_specs=[kv, kv, st, st], out_specs=[pf_blk, qb_blk],
            scratch_shapes=[pltpu.VMEM((HEADS // 2, LANES, LANES), F32)] * 2),
        out_shape=[shape, shape],
        compiler_params=_cparams(("arbitrary",)),
        name="ret_prefix",
    )(lg, rk, rv, s0f, s0b)


def _retention_kernel(latent, lg_ref, rq_ref, rk_ref, rv_ref, rg_ref, *refs):
    q0 = 0
    seq_len = rq_ref.shape[0]
    if latent:
        s0f_ref, s0b_ref, o_ref = refs
    else:
        unperm_ref, o_ref, sf_ref, sb_ref = refs
    tq, tk = rq_ref.shape[0], rk_ref.shape[0]
    nt = (((1,), (1,)), ((), ()))
    n_idx = (q0 + lax.broadcasted_iota(jnp.int32, (tq, tk), 0)).astype(F32)
    m_idx = lax.broadcasted_iota(jnp.int32, (tq, tk), 1).astype(F32)
    dist = n_idx - m_idx
    adist = jnp.abs(dist)
    fwd = dist > 0.0
    diag = jnp.where(dist == 0.0, 1.0, 0.0)
    lane = lax.broadcasted_iota(jnp.int32, (1, LANES), 1)
    lo = lane < RDK
    n_col = (q0 + lax.broadcasted_iota(jnp.int32, (tq, 1), 0)).astype(F32)
    m_col = lax.broadcasted_iota(jnp.int32, (tk, 1), 0).astype(F32)
    for pair in range(HEADS // 2):
        sl = slice(pair * LANES, (pair + 1) * LANES)
        qb, kb, vb = rq_ref[:, sl], rk_ref[:, sl], rv_ref[:, sl]
        acc = jnp.zeros((tq, LANES), F32)
        for sub in range(2):
            h = 2 * pair + sub
            lgf, lgb = lg_ref[h], lg_ref[HEADS + h]
            half = lo if sub == 0 else jnp.logical_not(lo)
            qm = jnp.where(half, qb, jnp.zeros_like(qb))
            vm = jnp.where(half, vb, jnp.zeros_like(vb))
            s = lax.dot_general(qm, kb, nt, preferred_element_type=F32)
            w = jnp.exp(adist * jnp.where(fwd, lgf, lgb)) + diag
            acc = acc + _bdot((s * w).astype(BF16), vm)
        lgf_l = jnp.where(lo, lg_ref[2 * pair], lg_ref[2 * pair + 1])
        lgb_l = jnp.where(lo, lg_ref[HEADS + 2 * pair], lg_ref[HEADS + 2 * pair + 1])
        if latent:
            acc = acc + _bdot(qb, s0f_ref[pair]) * jnp.exp((n_col + 1.0) * lgf_l)
            acc = acc + _bdot(qb, s0b_ref[pair]) * jnp.exp((seq_len - n_col) * lgb_l)
        else:
            v_swapped = pltpu.roll(vb.astype(F32), RDK, axis=1).astype(BF16)
            for st_ref, dec in ((sf_ref, jnp.exp((seq_len - 1.0 - m_col) * lgf_l)),
                                (sb_ref, jnp.exp(m_col * lgb_l))):
                kt = (kb.astype(F32) * dec).T.astype(BF16)
                kt = _bdot(unperm_ref[...], kt).astype(BF16)
                st_ref[2 * pair] = _bdot(kt, vb)[0:RDK, 0:RDK]
                st_ref[2 * pair + 1] = _bdot(kt, v_swapped)[RDK:, 0:RDK]
        inv = 1.0 / RDK
        mu = jnp.where(lo, jnp.sum(jnp.where(lo, acc, 0.0), axis=-1, keepdims=True),
                       jnp.sum(jnp.where(lo, 0.0, acc), axis=-1, keepdims=True)) * inv
        dlt = acc - mu
        d2 = dlt * dlt
        var = jnp.where(lo, jnp.sum(jnp.where(lo, d2, 0.0), axis=-1, keepdims=True),
                        jnp.sum(jnp.where(lo, 0.0, d2), axis=-1, keepdims=True)) * inv
        g = rg_ref[:, sl]
        o_ref[:, sl] = (dlt * lax.rsqrt(var + 1e-5) * (g * _sigmoid(g))).astype(BF16)


def _retention_tiles_kernel(n_prev, lg_ref, rq_ref, rk_ref, rv_ref, rg_ref, pf_ref, qb_ref, unperm_ref, *refs):
    prev_refs, (o_ref, sf_ref, sb_ref) = refs[:2 * n_prev], refs[2 * n_prev:]
    is_latent = pl.program_id(0) >= TP // TQ

    @pl.when(jnp.logical_not(is_latent))
    def _():
        for k in range(n_prev):
            sf_ref[k] = prev_refs[2 * k][...]
            sb_ref[k] = prev_refs[2 * k + 1][...]
        _retention_kernel(False, lg_ref, rq_ref, rk_ref, rv_ref, rg_ref, unperm_ref, o_ref,
                          sf_ref.at[n_prev], sb_ref.at[n_prev])

    @pl.when(is_latent)
    def _():
        _retention_kernel(True, lg_ref, rq_ref, rk_ref, rv_ref, rg_ref, pf_ref, qb_ref, o_ref)


def _retention(lg, rq, rk, rv, rg, s0f, s0b, unperm, prev_states):
    w = HEADS * RDK
    n_prev = 0 if prev_states is None else prev_states[0].shape[1]
    pf, qb = _ret_prefix(lg, rk, rv, s0f, s0b)
    tile = lambda i, lg: (i, 0)
    s0_blk = pl.BlockSpec((None, HEADS // 2, LANES, LANES), lambda i, lg: (jnp.maximum(i - TP // TQ, 0), 0, 0, 0))
    seq = lambda i, lg: (jnp.minimum(i, N_PROMPT_SEQ - 1), 0, 0, 0, 0)
    st_blk = pl.BlockSpec((None, n_prev + 1, HEADS, RDK, RDK), seq)
    st_shape = jax.ShapeDtypeStruct((N_PROMPT_SEQ, n_prev + 1, HEADS, RDK, RDK), F32)
    prev = () if prev_states is None else tuple(prev_states)
    prev_specs = [pl.BlockSpec((None, None, HEADS, RDK, RDK),
                               functools.partial(lambda k, i, lg: (jnp.minimum(i, N_PROMPT_SEQ - 1), k, 0, 0, 0), k))
                  for k in range(n_prev) for _ in range(2)]
    prev_args = [p for k in range(n_prev) for p in prev]
    return pl.pallas_call(
        functools.partial(_retention_tiles_kernel, n_prev),
        grid_spec=pltpu.PrefetchScalarGridSpec(
            num_scalar_prefetch=1, grid=(T // TQ,),
            in_specs=[pl.BlockSpec((TQ, w), tile)] * 4 + [s0_blk, s0_blk,
                      pl.BlockSpec((LANES, LANES), lambda i, lg: (0, 0))] + prev_specs,
            out_specs=[pl.BlockSpec((TQ, w), tile), st_blk, st_blk]),
        out_shape=[jax.ShapeDtypeStruct((T, w), BF16), st_shape, st_shape],
        compiler_params=_cparams(("arbitrary",)),
        name="retention",
    )(lg, rq, rk, rv, rg, pf, qb, unperm, *prev_args)


def _mix_ffn_kernel(n_x, *refs):
    x_refs, refs = refs[:n_x], refs[n_x:]
    a_ref, r_ref, mod_ref, wo_ref, wg_ref, wu_ref, wd_ref, ln_ref, o_ref, x1_scr, h_scr, acc_scr = refs
    f = pl.program_id(1)

    @pl.when(f == 0)
    def _():
        half = HEADS * VDIM
        y = _bdot(a_ref[...], wo_ref[0:half, :]) + _bdot(r_ref[...], wo_ref[half:, :])
        x1 = _layer_norm(ALPHA * _tile_rows(x_refs, TM_FF) + mod_ref[2:3, :] * y, ln_ref[0:1, :], ln_ref[1:2, :])
        x1_scr[...] = x1
        h_scr[...] = (x1 * (1.0 + mod_ref[4:5, :]) + mod_ref[3:4, :]).astype(BF16)
        acc_scr[...] = jnp.zeros_like(acc_scr)

    h = h_scr[...]
    g = _bdot(h, wg_ref[...])
    u = _bdot(h, wu_ref[...])
    acc_scr[...] += _bdot((g * _sigmoid(g) * u).astype(BF16), wd_ref[...])

    @pl.when(f == pl.num_programs(1) - 1)
    def _():
        o_ref[...] = _layer_norm(ALPHA * x1_scr[...] + mod_ref[5:6, :] * acc_scr[...], ln_ref[2:3, :], ln_ref[3:4, :])


def _mix_ffn(x, attn, ret, mod_l, w_out, wg, wu, wd, ln, j):
    x_args, x_specs = _tile_specs(x, TM_FF)
    tok = lambda w: pl.BlockSpec((TM_FF, w), lambda i, f: (i, 0))
    return pl.pallas_call(
        functools.partial(_mix_ffn_kernel, len(x_args)),
        grid=(T // TM_FF, D_FF // TF),
        in_specs=x_specs + [tok(HEADS * VDIM), tok(HEADS * RDK),
                  pl.BlockSpec((None, 8, D), lambda i, f: (_group_of_tile(i, TM_FF), 0, 0)),
                  pl.BlockSpec((None,) + w_out.shape[1:], lambda i, f: (j, 0, 0)),
                  pl.BlockSpec((None, D, TF), lambda i, f: (j, 0, f)),
                  pl.BlockSpec((None, D, TF), lambda i, f: (j, 0, f)),
                  pl.BlockSpec((None, TF, D), lambda i, f: (j, f, 0)), pl.BlockSpec((8, D), lambda i, f: (0, 0))],
        out_specs=tok(D),
        out_shape=jax.ShapeDtypeStruct((T, D), F32),
        scratch_shapes=[pltpu.VMEM((TM_FF, D), F32), pltpu.VMEM((TM_FF, D), BF16), pltpu.VMEM((TM_FF, D), F32)],
        compiler_params=_cparams(("parallel", "arbitrary")),
        name="mix_ffn",
    )(*x_args, attn, ret, mod_l, w_out, wg, wu, wd, ln)


def _conv_in_kernel(x_ref, mod_ref, w_ref, b_ref, z_ref):
    h = (x_ref[...] * (1.0 + mod_ref[1:2, :]) + mod_ref[0:1, :]).astype(BF16)
    p = _bdot(h, w_ref[...])
    b_ref[...] = p[:, 0:D]
    z_ref[...] = p[:, D:2 * D] * p[:, 2 * D:3 * D]


def _conv_in(x, mod_l, w_in, j):
    tok = pl.BlockSpec((TM_FF, D), lambda i: (i, 0))
    return pl.pallas_call(
        _conv_in_kernel,
        grid=(T // TM_FF,),
        in_specs=[tok, pl.BlockSpec((None, 8, D), lambda i: (_group_of_tile(i, TM_FF), 0, 0)),
                  pl.BlockSpec((None,) + w_in.shape[1:], lambda i: (j, 0, 0))],
        out_specs=[tok, tok],
        out_shape=[jax.ShapeDtypeStruct((T, D), F32)] * 2,
        compiler_params=_cparams(("parallel",)),
        name="conv_in",
    )(x, mod_l, w_in)


def _conv_out_kernel(x_ref, b_ref, z_ref, zp_ref, zn_ref, mod_ref, cw_ref, w_ref, ln_ref, rw_ref, rb_ref,
                     o_ref, h_ref, route_ref):
    i = pl.program_id(0)
    z = z_ref[...]
    tm = z.shape[0]
    row = lax.broadcasted_iota(jnp.int32, (tm, 1), 0)
    seq_len = jnp.where(i < TP // tm, PROMPT_LEN, LAT_LEN)
    pos = (i * tm + row) & (seq_len - 1)
    prev = jnp.where(row == 0, zp_ref[7:8, :], pltpu.roll(z, 1, axis=0))
    prev = jnp.where(pos == 0, 0.0, prev)
    nxt = jnp.where(row == tm - 1, zn_ref[0:1, :], pltpu.roll(z, tm - 1, axis=0))
    nxt = jnp.where(pos == seq_len - 1, 0.0, nxt)
    y = prev * cw_ref[0:1, :] + z * cw_ref[1:2, :] + nxt * cw_ref[2:3, :]
    t = _bdot((b_ref[...] * y).astype(BF16), w_ref[...])
    x1 = _layer_norm(ALPHA * x_ref[...] + mod_ref[2:3, :] * t, ln_ref[0:1, :], ln_ref[1:2, :])
    o_ref[...] = x1
    h = x1 * (1.0 + mod_ref[4:5, :]) + mod_ref[3:4, :]
    h_hi = h.astype(BF16)
    h_ref[...] = h_hi
    h_lo = (h - h_hi.astype(F32)).astype(BF16)
    both = _bdot(h_hi, rw_ref[...])
    logits = both[:, :LANES] + both[:, LANES:] + _bdot(h_lo, rw_ref[:, :LANES]) + rb_ref[...]
    lane = lax.broadcasted_iota(jnp.int32, logits.shape, 1).astype(F32)
    t1 = jnp.max(logits, axis=-1, keepdims=True)
    i1 = jnp.min(jnp.where(logits == t1, lane, float(LANES)), axis=-1, keepdims=True)
    rest = jnp.where(lane == i1, -jnp.inf, logits)
    t2 = jnp.max(rest, axis=-1, keepdims=True)
    i2 = jnp.min(jnp.where(rest == t2, lane, float(LANES)), axis=-1, keepdims=True)
    e = jnp.exp(t2 - t1)
    den = 1.0 + e
    route_ref[...] = jnp.where(lane == 0.0, i1, jnp.where(lane == 1.0, i2,
                               jnp.where(lane == 2.0, 1.0 / den, jnp.where(lane == 3.0, e / den, 0.0))))


def _conv_out(x, b, z, mod_l, cw, w_out, ln, rw, rb, j):
    tok = pl.BlockSpec((TM_FF, D), lambda i: (i, 0))
    sub = TM_FF // 8
    return pl.pallas_call(
        _conv_out_kernel,
        grid=(T // TM_FF,),
        in_specs=[tok, tok, tok,
                  pl.BlockSpec((8, D), lambda i: (jnp.maximum(i * sub - 1, 0), 0)),
                  pl.BlockSpec((8, D), lambda i: (jnp.minimum((i + 1) * sub, T // 8 - 1), 0)),
                  pl.BlockSpec((None, 8, D), lambda i: (_group_of_tile(i, TM_FF), 0, 0)),
                  pl.BlockSpec((8, D), lambda i: (0, 0)), pl.BlockSpec((None, D, D), lambda i: (j, 0, 0)),
                  pl.BlockSpec((8, D), lambda i: (0, 0)),
                  pl.BlockSpec((D, 2 * LANES), lambda i: (0, 0)), pl.BlockSpec((1, LANES), lambda i: (0, 0))],
        out_specs=[tok, tok, pl.BlockSpec((TM_FF, LANES), lambda i: (i, 0))],
        out_shape=[jax.ShapeDtypeStruct((T, D), F32), jax.ShapeDtypeStruct((T, D), BF16),
                   jax.ShapeDtypeStruct((T, LANES), F32)],
        compiler_params=_cparams(("parallel",)),
        name="conv_out",
    )(x, b, z, z, z, mod_l, cw, w_out, ln, rw, rb)


def _moe_up_kernel(te_ref, nv_ref, x_ref, wg_ref, wu_ref, a_ref):
    @pl.when(pl.program_id(1) < nv_ref[0])
    def _():
        h = x_ref[...]
        g = _wdot(h, wg_ref[...])
        u = _wdot(h, wu_ref[...])
        a_ref[...] = (g * _sigmoid(g) * u).astype(BF16)

    @pl.when(pl.program_id(1) >= nv_ref[0])
    def _():
        a_ref[...] = jnp.zeros_like(a_ref)


def _moe_down_kernel(te_ref, nv_ref, a_ref, wd_ref, o_ref):
    @pl.when(pl.program_id(0) < nv_ref[0])
    def _():
        o_ref[...] = _wdot(a_ref[...], wd_ref[...])

    @pl.when(pl.program_id(0) >= nv_ref[0])
    def _():
        o_ref[...] = jnp.zeros_like(o_ref)


def _moe(tile_expert, n_valid, xs, wg, wu, wd, j):
    n_tiles = NP_ROWS // TM_FF
    act = pl.pallas_call(
        _moe_up_kernel,
        grid_spec=pltpu.PrefetchScalarGridSpec(
            num_scalar_prefetch=2, grid=(D_FF // TF, n_tiles),
            in_specs=[pl.BlockSpec((TM_FF, D), lambda f, i, te, nv: (i, 0)),
                      pl.BlockSpec((None, None, D, TF), lambda f, i, te, nv: (j, te[i], 0, f)),
                      pl.BlockSpec((None, None, D, TF), lambda f, i, te, nv: (j, te[i], 0, f))],
            out_specs=pl.BlockSpec((TM_FF, TF), lambda f, i, te, nv: (i, f))),
        out_shape=jax.ShapeDtypeStruct((NP_ROWS, D_FF), BF16),
        compiler_params=_cparams(("arbitrary", "arbitrary")),
        name="moe_up",
    )(tile_expert, n_valid, xs, wg, wu)
    return pl.pallas_call(
        _moe_down_kernel,
        grid_spec=pltpu.PrefetchScalarGridSpec(
            num_scalar_prefetch=2, grid=(n_tiles,),
            in_specs=[pl.BlockSpec((TM_FF, D_FF), lambda i, te, nv: (i, 0)),
                      pl.BlockSpec((None, None, D_FF, D), lambda i, te, nv: (j, te[i], 0, 0))],
            out_specs=pl.BlockSpec((TM_FF, D), lambda i, te, nv: (i, 0))),
        out_shape=jax.ShapeDtypeStruct((NP_ROWS, D), F32),
        compiler_params=_cparams(("arbitrary",)),
        name="moe_down",
    )(tile_expert, n_valid, act, wd)


def _combine_kernel(split, x_ref, o0_ref, o1_ref, route_ref, mod_ref, ln_ref, *o_refs):
    y = route_ref[:, 2:3] * o0_ref[...] + route_ref[:, 3:4] * o1_ref[...]
    out = _layer_norm(ALPHA * x_ref[...] + mod_ref[5:6, :] * y, ln_ref[2:3, :], ln_ref[3:4, :])
    if not split:
        o_refs[0][...] = out
        return
    is_latent = pl.program_id(0) >= TP // TM_FF

    @pl.when(jnp.logical_not(is_latent))
    def _():
        o_refs[0][...] = out

    @pl.when(is_latent)
    def _():
        o_refs[1][...] = out


def _combine(x, o0, o1, route, mod_l, ln, split):
    tok = pl.BlockSpec((TM_FF, D), lambda i: (i, 0))
    if split:
        out_specs = [pl.BlockSpec((TM_FF, D), lambda i: (jnp.minimum(i, TP // TM_FF - 1), 0)),
                     pl.BlockSpec((TM_FF, D), lambda i: (jnp.maximum(i - TP // TM_FF, 0), 0))]
        out_shape = [jax.ShapeDtypeStruct((TP, D), F32), jax.ShapeDtypeStruct((TS, D), F32)]
    else:
        out_specs, out_shape = tok, jax.ShapeDtypeStruct((T, D), F32)
    return pl.pallas_call(
        functools.partial(_combine_kernel, split),
        grid=(T // TM_FF,),
        in_specs=[tok, tok, tok, pl.BlockSpec((TM_FF, LANES), lambda i: (i, 0)),
                  pl.BlockSpec((None, 8, D), lambda i: (_group_of_tile(i, TM_FF), 0, 0)),
                  pl.BlockSpec((8, D), lambda i: (0, 0))],
        out_specs=out_specs,
        out_shape=out_shape,
        compiler_params=_cparams(("arbitrary",)),
        name="moe_combine",
    )(x, o0, o1, route, mod_l, ln)


def _routing_plan(route):
    e = jnp.concatenate([route[:, 0], route[:, 1]]).astype(jnp.int32)
    onehot = (e[:, None] == jnp.arange(N_EXP, dtype=jnp.int32)[None, :]).astype(jnp.int32)
    csum = jnp.cumsum(onehot, axis=0)
    counts = csum[-1]
    rank = jnp.sum((csum - onehot) * onehot, axis=1)
    padded = (counts + TM_FF - 1) // TM_FF * TM_FF
    pend = jnp.cumsum(padded)
    dest = jnp.sum(onehot * (pend - padded)[None, :], axis=1) + rank
    order = jnp.argsort(e, stable=True).astype(jnp.int32)
    rows = jnp.arange(NP_ROWS, dtype=jnp.int32)
    before = (rows[:, None] >= pend[None, :]).astype(jnp.int32)
    row_e = jnp.minimum(jnp.sum(before, axis=1), N_EXP - 1)
    row_cnt = jnp.sum((row_e[:, None] == jnp.arange(N_EXP, dtype=jnp.int32)[None, :]) * counts[None, :], axis=1)
    q = jnp.clip(rows - jnp.sum(before * padded[None, :], axis=1), 0, jnp.maximum(row_cnt - 1, 0))
    src = jnp.minimum(jnp.sum(before * counts[None, :], axis=1) + q, 2 * T - 1)
    row_token = order[src] % T
    n_valid = (pend[-1] // TM_FF).astype(jnp.int32)
    tile_start = jnp.minimum(jnp.arange(NP_ROWS // TM_FF, dtype=jnp.int32), n_valid - 1) * TM_FF
    tile_expert = jnp.minimum(jnp.sum((tile_start[:, None] >= pend[None, :]).astype(jnp.int32), axis=1), N_EXP - 1)
    return dest[:T], dest[T:], row_token, tile_expert.astype(jnp.int32), n_valid.reshape(1)


_INV_PERM_RDK = np.argsort(np.concatenate([np.arange(0, RDK, 2), np.arange(1, RDK, 2)]))


def _pad_cols(a, width):
    return jnp.pad(a, ((0, 0), (0, width - a.shape[1])))


def _deinterleave(a):
    n = a.shape[-1]
    return jnp.swapaxes(a.reshape(a.shape[:-1] + (n // 2, 2)), -1, -2).reshape(a.shape)


def _prep_even_weights(w_in, w_q_b, w_kv_b):
    o_kpe, o_rq, o_rk, o_rv = 640, 672, 1184, 1696
    heads = lambda a: _deinterleave(a.reshape(D, HEADS, RDK)).reshape(D, HEADS * RDK)
    kpe = w_in[:, o_kpe:o_kpe + ROPE]
    w_in_p = jnp.concatenate([
        w_in[:, :640], heads(w_in[:, o_rq:o_rk]), heads(w_in[:, o_rk:o_rv]) * (RDK ** -0.5), w_in[:, o_rv:],
        _pad_cols(kpe, LANES), _pad_cols(_deinterleave(kpe), LANES),
        _pad_cols(jnp.concatenate([-kpe[:, 1::2], kpe[:, 0::2]], axis=1), LANES)], axis=1).astype(BF16)
    wq = w_q_b.reshape(Q_LORA, HEADS, NOPE + ROPE)
    wq = jnp.concatenate([wq[:, :, :NOPE], _deinterleave(wq[:, :, NOPE:]),
                          jnp.zeros((Q_LORA, HEADS, LANES - NOPE - ROPE), F32)], axis=2)
    wq = wq.reshape(Q_LORA, HEADS * LANES).astype(BF16)
    wkv = w_kv_b.reshape(KV_LORA, HEADS, NOPE + VDIM)
    zero = jnp.zeros((KV_LORA, HEADS, LANES - NOPE), F32)
    wk = jnp.concatenate([wkv[:, :, :NOPE], zero], axis=2).reshape(KV_LORA, HEADS * LANES)
    wv = wkv[:, :, NOPE:].reshape(KV_LORA, HEADS // 2, 2, VDIM)
    zv = jnp.zeros((KV_LORA, HEADS // 2, VDIM), F32)
    wv = jnp.stack([jnp.concatenate([wv[:, :, 0], zv], axis=2), jnp.concatenate([zv, wv[:, :, 1]], axis=2)], axis=2)
    wkv_p = jnp.concatenate([wk, wv.reshape(KV_LORA, HEADS * LANES)], axis=1).astype(BF16)
    return w_in_p, wq, wkv_p


def _placement():
    ek = np.zeros((LANES, HEADS * LANES), np.float32)
    for h in range(HEADS):
        ek[np.arange(ROPE), h * LANES + NOPE + np.arange(ROPE)] = 1.0
    return jnp.asarray(ek, BF16)


def _rotary_tables():
    rows = LAT_LEN // GRID_W
    r, col = jnp.meshgrid(jnp.arange(rows, dtype=F32), jnp.arange(GRID_W, dtype=F32), indexing='ij')
    n_freq = ROPE // 4
    freqs = 1.0 / (10000.0 ** (jnp.arange(n_freq, dtype=F32) / n_freq))
    ang = jnp.concatenate([r.reshape(-1)[:, None] * freqs, col.reshape(-1)[:, None] * freqs], axis=-1)
    cos, sin = jnp.cos(ang), jnp.sin(ang)
    theta = 1.0 / (10000.0 ** jnp.linspace(0.0, 1.0, RDK // 2, dtype=F32))
    rang = jnp.arange(LAT_LEN, dtype=F32)[:, None] * theta
    rcos, rsin = jnp.cos(rang), jnp.sin(rang)
    one = lambda w: jnp.ones((LAT_LEN, w), F32)
    zero = lambda w: jnp.zeros((LAT_LEN, w), F32)
    lat = [jnp.concatenate([one(NOPE), cos, cos, one(LANES - NOPE - ROPE)], axis=1),
           jnp.concatenate([zero(NOPE), sin, sin, zero(LANES - NOPE - ROPE)], axis=1),
           jnp.concatenate([cos, cos, zero(LANES - ROPE)], axis=1),
           jnp.concatenate([sin, sin, zero(LANES - ROPE)], axis=1),
           jnp.concatenate([rcos] * 4, axis=1), jnp.concatenate([rsin] * 4, axis=1)]
    ident = [np.ones((TM, LANES), np.float32), np.zeros((TM, LANES), np.float32)]
    ident_k = np.concatenate([np.ones((TM, ROPE), np.float32), np.zeros((TM, LANES - ROPE), np.float32)], axis=1)
    ident = [ident[0], ident[1], ident_k, ident[1], ident[0], ident[1]]
    return [jnp.concatenate([l, jnp.asarray(c)], axis=0) for l, c in zip(lat, ident)]


def _block_diag_states(s0):
    s = jnp.swapaxes(_deinterleave(jnp.swapaxes(s0, -1, -2)), -1, -2)
    s = s.reshape(s0.shape[0], HEADS // 2, 2, RDK, RDK)
    z = jnp.zeros_like(s[:, :, 0])
    top = jnp.concatenate([s[:, :, 0], z], axis=-1)
    bot = jnp.concatenate([z, s[:, :, 1]], axis=-1)
    return jnp.concatenate([top, bot], axis=-2)


def _unpermute_matrix():
    m = np.zeros((LANES, LANES), np.float32)
    for blk in range(LANES // RDK):
        m[blk * RDK + np.arange(RDK), blk * RDK + _INV_PERM_RDK] = 1.0
    return jnp.asarray(m, BF16)


def kernel(x_prompt, x_sample, c, cache_ckv, cache_kpe, state_ret_fwd, state_ret_bwd, c_ctx, w_mod, b_mod, ln_g, ln_b, w_in_mix, q_a_gain, kv_a_gain, w_q_b, w_kv_b, ret_decay_fwd, ret_decay_bwd, w_out_mix, w_in_conv, conv_w, w_out_conv, ffn_gate, ffn_up, ffn_down, router_w, router_b, exp_gate, exp_up, exp_down):
    x = (x_prompt.reshape(TP, D), x_sample.reshape(TS, D))
    cond8 = jnp.concatenate([c_ctx[None], c, jnp.zeros((8 - 1 - N_LAT_SEQ, D), F32)], axis=0)
    mods = _modulation(cond8, w_mod, b_mod)
    mods = jnp.pad(mods.reshape(DEPTH, N_GROUPS, 6, D), ((0, 0), (0, 0), (0, 2), (0, 0)))
    ln = jnp.pad(jnp.concatenate([ln_g, ln_b], axis=1)[:, jnp.array([0, 2, 1, 3])], ((0, 0), (0, 4), (0, 0)))
    tabs = _rotary_tables()
    ek = _placement()
    unperm = _unpermute_matrix()
    bf = lambda a: a.astype(BF16)
    w_out_mix_b, w_in_conv_b, w_out_conv_b = bf(w_out_mix), bf(w_in_conv), bf(w_out_conv)
    ffn_b = (bf(ffn_gate), bf(ffn_up), bf(ffn_down))
    exp_b = (exp_gate, exp_up, exp_down)
    cache, states = None, None
    for layer in range(DEPTH):
        j = layer // 2
        mod_l, ln_l = mods[layer], ln[layer]
        if layer % 2 == 0:
            w_in_p, wq, wkv = _prep_even_weights(w_in_mix[j], w_q_b[j], w_kv_b[j])
            q, k, v, cache, rq, rk, rv, rg = _even_in(
                x, mod_l, w_in_p, q_a_gain[j][None], kv_a_gain[j][None], wq, wkv, ek, tabs, cache)
            kpe_c = _pad_cols(_deinterleave(cache_kpe[:, j]).reshape(N_LAT_SEQ * PAST, ROPE), LANES)
            kc, vc = _ctx_kv(cache_ckv[:, j].reshape(N_LAT_SEQ * PAST, KV_LORA), kpe_c, wkv, ek)
            attn = _attention(q, k, v, kc, vc)
            lg = jnp.concatenate([jax.nn.log_sigmoid(ret_decay_fwd[j].astype(F32)),
                                  jax.nn.log_sigmoid(ret_decay_bwd[j].astype(F32))])
            ret, sf, sb = _retention(lg, rq, rk, rv, rg, _block_diag_states(state_ret_fwd[:, j]),
                                     _block_diag_states(state_ret_bwd[:, j]), unperm, states)
            states = (sf, sb)
            x = _mix_ffn(x, attn, ret, mod_l, w_out_mix_b, *ffn_b, ln_l, j)
        else:
            b, z = _conv_in(x, mod_l, w_in_conv_b, j)
            cw = jnp.pad(conv_w[j], ((0, 5), (0, 0)))
            rw = _pad_cols(router_w[j], LANES)
            rw_hi = rw.astype(BF16)
            rw = jnp.concatenate([rw_hi, (rw - rw_hi.astype(F32)).astype(BF16)], axis=1)
            rb = jnp.concatenate([router_b[j].astype(F32), jnp.full((LANES - N_EXP,), -1e30, F32)])[None]
            x, h, route = _conv_out(x, b, z, mod_l, cw, w_out_conv_b, ln_l, rw, rb, j)
            dest0, dest1, row_token, tile_expert, n_valid = _routing_plan(route)
            out_sorted = _moe(tile_expert, n_valid, h[row_token], *exp_b, j)
            x = _combine(x, out_sorted[dest0], out_sorted[dest1], route, mod_l, ln_l, split=layer == DEPTH - 1)
    y_prompt = x[0].reshape(N_PROMPT_SEQ, PROMPT_LEN, D)
    y_sample = x[1].reshape(N_LAT_SEQ, LAT_LEN, D)
    return (y_prompt, y_sample, cache[0], cache[1], states[0], states[1])
```

```python
import functools

import numpy as np
import jax
import jax.numpy as jnp
from jax import lax
from jax.experimental import pallas as pl
from jax.experimental.pallas import tpu as pltpu

F32 = jnp.float32
BF16 = jnp.bfloat16

D = 1024
DEPTH = 4
N_PROMPT_SEQ, PROMPT_LEN = 32, 256
N_LAT_SEQ, LAT_LEN = 2, 2048
PAST = 512
GRID_W = 64
TP = N_PROMPT_SEQ * PROMPT_LEN
TS = N_LAT_SEQ * LAT_LEN
T = TP + TS
HEADS = 8
NOPE, ROPE, VDIM = 64, 32, 64
Q_LORA, KV_LORA = 384, 256
RDK = 64
D_FF = 2816
N_EXP = 8
ALPHA = (2.0 * DEPTH) ** 0.25
Q_SCALE = float((NOPE + ROPE) ** -0.5)
LANES = 128
N_GROUPS = 8

TM = 512
TM_FF = 512
TF = D_FF // 2
TQ = 256
NP_ROWS = 2 * T + N_EXP * TM_FF
VMEM_LIMIT = 56 * 1024 * 1024

IN_COLS = 3072


def _cparams(sem):
    return pltpu.CompilerParams(dimension_semantics=sem, vmem_limit_bytes=VMEM_LIMIT)


def _group_of_tile(i, tm):
    per_seq = LAT_LEN // tm
    return jnp.maximum(i - TP // tm + per_seq, 0) // per_seq


def _bdot(a, b):
    return jnp.dot(a, b, preferred_element_type=F32)


def _wdot(a, w):
    return lax.dot_general(a, w, (((1,), (0,)), ((), ())), preferred_element_type=F32)


def _sigmoid(v):
    return 1.0 / (1.0 + jnp.exp(-v))


def _layer_norm(v, g, b):
    mu = jnp.mean(v, axis=-1, keepdims=True)
    d = v - mu
    var = jnp.mean(d * d, axis=-1, keepdims=True)
    return d * lax.rsqrt(var + 1e-5) * g + b


def _rms(v, g):
    return v * lax.rsqrt(jnp.mean(v * v, axis=-1, keepdims=True) + 1e-6) * g


def _mod_kernel(c_ref, w_ref, b_ref, o_ref):
    c = c_ref[...]
    s = (c * _sigmoid(c)).astype(BF16)
    o_ref[...] = _bdot(s, w_ref[...].astype(BF16)) + b_ref[...]


def _modulation(cond8, w_mod, b_mod):
    tn = 1536
    return pl.pallas_call(
        _mod_kernel,
        grid=(DEPTH, 6 * D // tn),
        in_specs=[pl.BlockSpec((8, D), lambda l, n: (0, 0)),
                  pl.BlockSpec((None, D, tn), lambda l, n: (l, 0, n)),
                  pl.BlockSpec((None, 1, tn), lambda l, n: (l, 0, n))],
        out_specs=pl.BlockSpec((None, 8, tn), lambda l, n: (l, 0, n)),
        out_shape=jax.ShapeDtypeStruct((DEPTH, 8, 6 * D), F32),
        compiler_params=_cparams(("arbitrary", "arbitrary")),
        name="modulation",
    )(cond8, w_mod, b_mod.reshape(DEPTH, 1, 6 * D))


def _swap_halves(a, half):
    n = a.shape[-1]
    lane = lax.broadcasted_iota(jnp.int32, a.shape, 1)
    first = (lane & (2 * half - 1)) < half
    return jnp.where(first, -pltpu.roll(a, n - half, axis=1), pltpu.roll(a, half, axis=1))


def _tile_rows(x_refs, tm):
    if len(x_refs) == 1:
        return x_refs[0][...]
    return jnp.where(pl.program_id(0) >= TP // tm, x_refs[1][...], x_refs[0][...])


def _tile_specs(x, tm):
    if not isinstance(x, tuple):
        return (x,), [pl.BlockSpec((tm, D), lambda i, *_: (i, 0))]
    return x, [pl.BlockSpec((tm, D), lambda i, *_: (jnp.minimum(i, TP // tm - 1), 0)),
               pl.BlockSpec((tm, D), lambda i, *_: (jnp.maximum(i - TP // tm, 0), 0))]


def _even_in_kernel(n_x, n_prev, *refs):
    x_refs, refs = refs[:n_x], refs[n_x:]
    (mod_ref, w_in_ref, qg_ref, kvg_ref, wq_ref, wkv_ref, ek_ref,
     cq_ref, sq_ref, ck_ref, sk_ref, cr_ref, sr_ref), refs = refs[:13], refs[13:]
    prev_refs, refs = refs[:2 * n_prev], refs[2 * n_prev:]
    q_ref, k_ref, v_ref, ckv_ref, kpe_ref, rq_ref, rk_ref, rv_ref, rg_ref = refs
    x = _tile_rows(x_refs, TM)
    h = (x * (1.0 + mod_ref[1:2, :]) + mod_ref[0:1, :]).astype(BF16)
    p = _bdot(h, w_in_ref[...])
    qn = _rms(p[:, 0:Q_LORA], qg_ref[...]).astype(BF16)
    qa = _bdot(qn, wq_ref[...])
    ckv = _rms(p[:, Q_LORA:Q_LORA + KV_LORA], kvg_ref[...])
    kv = _bdot(ckv.astype(BF16), wkv_ref[...])
    v_ref[...] = kv[:, HEADS * LANES:].astype(BF16)
    base = 2688
    ka = p[:, base + LANES:base + 2 * LANES]
    kb = p[:, base + 2 * LANES:base + 3 * LANES]
    rq = p[:, 640:1152]
    rk = p[:, 1152:1664]
    rv_ref[...] = p[:, 1664:2176].astype(BF16)
    rg_ref[...] = p[:, 2176:2688]
    lane = lax.broadcasted_iota(jnp.int32, qa.shape, 1) & (LANES - 1)
    qb = jnp.where(lane < NOPE + ROPE // 2,
                   -pltpu.roll(qa, qa.shape[1] - ROPE // 2, axis=1),
                   pltpu.roll(qa, ROPE // 2, axis=1))
    cq = jnp.concatenate([cq_ref[...]] * HEADS, axis=1)
    sq = jnp.concatenate([sq_ref[...]] * HEADS, axis=1)
    q_ref[...] = ((qa * cq + qb * sq) * Q_SCALE).astype(BF16)
    kpe_rot = ka * ck_ref[...] + kb * sk_ref[...]
    k_ref[...] = (kv[:, :HEADS * LANES] + _bdot(kpe_rot.astype(BF16), ek_ref[...])).astype(BF16)
    cr = jnp.concatenate([cr_ref[...]] * 4, axis=1)
    sr = jnp.concatenate([sr_ref[...]] * 4, axis=1)
    rq_ref[...] = (rq * cr + _swap_halves(rq, RDK // 2) * sr).astype(BF16)
    rk_ref[...] = (rk * cr + _swap_halves(rk, RDK // 2) * sr).astype(BF16)

    @pl.when(pl.program_id(0) < TP // TM)
    def _():
        for k in range(n_prev):
            ckv_ref[:, k] = prev_refs[2 * k][...]
            kpe_ref[:, k] = prev_refs[2 * k + 1][...]
        for s in range(TM // PROMPT_LEN):
            rows = slice(s * PROMPT_LEN, (s + 1) * PROMPT_LEN)
            ckv_ref[s, n_prev] = ckv[rows]
            kpe_ref[s, n_prev] = p[rows, base:base + ROPE]


def _even_in(x, mod_l, w_in, qg, kvg, wq, wkv, ek, tabs, prev_cache):
    x_args, x_specs = _tile_specs(x, TM)
    n_prev = 0 if prev_cache is None else prev_cache[0].shape[1]
    spt = TM // PROMPT_LEN
    tok = lambda w: pl.BlockSpec((TM, w), lambda i: (i, 0))
    full = lambda a: pl.BlockSpec(a.shape, lambda i: (0,) * a.ndim, pipeline_mode=pl.Buffered(1))
    lat_tiles = LAT_LEN // TM
    tab = pl.BlockSpec((TM, LANES), lambda i: (
        jnp.where(i < TP // TM, lat_tiles, jnp.maximum(i - TP // TM, 0) % lat_tiles), 0))
    seq = lambda i: (jnp.minimum(i, TP // TM - 1), 0, 0, 0)
    seq_k = lambda k, i: (jnp.minimum(i, TP // TM - 1), k, 0, 0)
    prev_args, prev_specs = [], []
    for k in range(n_prev):
        for a, w in zip(prev_cache, (KV_LORA, ROPE)):
            prev_args.append(a)
            prev_specs.append(pl.BlockSpec((spt, None, PROMPT_LEN, w), functools.partial(seq_k, k)))
    tok_outs = lambda dims: ([tok(w) for w, _ in dims], [jax.ShapeDtypeStruct((T, w), dt) for w, dt in dims])
    qkv_specs, qkv_shapes = tok_outs([(HEADS * LANES, BF16)] * 3)
    ret_specs, ret_shapes = tok_outs([(512, BF16), (512, BF16), (512, BF16), (512, F32)])
    cache_specs = [pl.BlockSpec((spt, n_prev + 1, PROMPT_LEN, w), seq) for w in (KV_LORA, ROPE)]
    cache_shapes = [jax.ShapeDtypeStruct((N_PROMPT_SEQ, n_prev + 1, PROMPT_LEN, w), F32) for w in (KV_LORA, ROPE)]
    q, k, v, ckv, kpe, rq, rk, rv, rg = pl.pallas_call(
        functools.partial(_even_in_kernel, len(x_args), n_prev),
        grid=(T // TM,),
        in_specs=x_specs + [pl.BlockSpec((None, 8, D), lambda i: (_group_of_tile(i, TM), 0, 0)),
                            full(w_in), full(qg), full(kvg), full(wq), full(wkv), full(ek)] + [tab] * 6 + prev_specs,
        out_specs=qkv_specs + cache_specs + ret_specs,
        out_shape=qkv_shapes + cache_shapes + ret_shapes,
        compiler_params=_cparams(("arbitrary",)),
        name="even_in",
    )(*x_args, mod_l, w_in, qg, kvg, wq, wkv, ek, *tabs, *prev_args)
    return q, k, v, (ckv, kpe), rq, rk, rv, rg


def _ctx_kv_kernel(ckv_ref, kpe_ref, wkv_ref, ek_ref, k_ref, v_ref):
    kv = _bdot(ckv_ref[...].astype(BF16), wkv_ref[...])
    k_ref[...] = (kv[:, :HEADS * LANES] + _bdot(kpe_ref[...].astype(BF16), ek_ref[...])).astype(BF16)
    v_ref[...] = kv[:, HEADS * LANES:].astype(BF16)


def _ctx_kv(ckv_c, kpe_c, wkv, ek):
    n = ckv_c.shape[0]
    full = lambda a: pl.BlockSpec(a.shape, lambda i: (0,) * a.ndim)
    return pl.pallas_call(
        _ctx_kv_kernel,
        grid=(n // PAST,),
        in_specs=[pl.BlockSpec((PAST, KV_LORA), lambda i: (i, 0)), pl.BlockSpec((PAST, LANES), lambda i: (i, 0)),
                  full(wkv), full(ek)],
        out_specs=[pl.BlockSpec((PAST, HEADS * LANES), lambda i: (i, 0))] * 2,
        out_shape=[jax.ShapeDtypeStruct((n, HEADS * LANES), BF16)] * 2,
        compiler_params=_cparams(("parallel",)),
        name="ctx_kv",
    )(ckv_c, kpe_c, wkv, ek)


def _attn_kernel(n_kv, q_ref, *refs):
    k_refs = refs[0:2 * n_kv:2]
    v_refs = refs[1:2 * n_kv:2]
    o_ref = refs[2 * n_kv]
    nt = (((1,), (1,)), ((), ()))
    for pair in range(HEADS // 2):
        acc = None
        for sub in range(2):
            sl = slice((2 * pair + sub) * LANES, (2 * pair + sub + 1) * LANES)
            qh = q_ref[:, sl]
            s = [lax.dot_general(qh, k[:, sl], nt, preferred_element_type=F32) for k in k_refs]
            m = functools.reduce(jnp.maximum, [jnp.max(a, axis=-1, keepdims=True) for a in s])
            e = [jnp.exp(a - m) for a in s]
            den = functools.reduce(jnp.add, [jnp.sum(a, axis=-1, keepdims=True) for a in e])
            o = functools.reduce(jnp.add, [_bdot(a.astype(BF16), v[:, sl]) for a, v in zip(e, v_refs)])
            o = o / den
            acc = o if acc is None else acc + o
        o_ref[:, pair * LANES:(pair + 1) * LANES] = acc.astype(BF16)


def _latent_seq(i):
    return jnp.maximum(i - TP // TQ, 0) // (LAT_LEN // TQ)


def _attn_tiles_kernel(q_ref, kp_ref, vp_ref, kc_ref, vc_ref, kl_ref, vl_ref, o_ref):
    is_latent = pl.program_id(0) >= TP // TQ

    @pl.when(jnp.logical_not(is_latent))
    def _():
        _attn_kernel(1, q_ref, kp_ref, vp_ref, o_ref)

    @pl.when(is_latent)
    def _():
        _attn_kernel(2, q_ref, kc_ref, vc_ref, kl_ref, vl_ref, o_ref)


def _attention(q, k, v, kc, vc):
    w = HEADS * LANES
    tile = lambda i: (i, 0)
    ctx_own = lambda i: (jnp.minimum(i, N_PROMPT_SEQ - 1), 0)
    cache = lambda i: (_latent_seq(i), 0)
    lat_own = lambda i: (TP // LAT_LEN + _latent_seq(i), 0)
    return pl.pallas_call(
        _attn_tiles_kernel,
        grid=(T // TQ,),
        in_specs=[pl.BlockSpec((TQ, w), tile),
                  pl.BlockSpec((PROMPT_LEN, w), ctx_own), pl.BlockSpec((PROMPT_LEN, w), ctx_own),
                  pl.BlockSpec((PAST, w), cache), pl.BlockSpec((PAST, w), cache),
                  pl.BlockSpec((LAT_LEN, w), lat_own), pl.BlockSpec((LAT_LEN, w), lat_own)],
        out_specs=pl.BlockSpec((TQ, HEADS * VDIM), tile),
        out_shape=jax.ShapeDtypeStruct((T, HEADS * VDIM), BF16),
        compiler_params=_cparams(("arbitrary",)),
        name="attention",
    )(q, k, v, kc, vc, k, v)


def _ret_prefix_kernel(lg_ref, rk_ref, rv_ref, s0f_ref, s0b_ref, pf_ref, qb_ref, sf_scr, sb_scr):
    s = pl.program_id(0)
    n_tiles, per_seq = TS // TQ, LAT_LEN // TQ
    row = lax.broadcasted_iota(jnp.int32, (LANES, 1), 0)
    lane = lax.broadcasted_iota(jnp.int32, (1, LANES), 1)
    top, lo = row < RDK, lane < RDK
    same_head = top == lo
    m_col = lax.broadcasted_iota(jnp.int32, (TQ, 1), 0).astype(F32)

    def scan_step(first, lg_off, s0_ref, scr, out_ref, pos):
        @pl.when(first)
        def _():
            scr[...] = s0_ref[...]

        for pair in range(HEADS // 2):
            sl = slice(pair * LANES, (pair + 1) * LANES)
            lg_even, lg_odd = lg_ref[lg_off + 2 * pair], lg_ref[lg_off + 2 * pair + 1]
            dec = jnp.exp(pos * jnp.where(lo, lg_even, lg_odd))
            local = _bdot((rk_ref[:, sl].astype(F32) * dec).T.astype(BF16), rv_ref[:, sl])
            out_ref[pair] = scr[pair].astype(BF16)
            scr[pair] = scr[pair] * jnp.exp(float(TQ) * jnp.where(top, lg_even, lg_odd)) + jnp.where(same_head, local, 0.0)

    @pl.when(s < n_tiles)
    def _():
        scan_step(s % per_seq == 0, 0, s0f_ref, sf_scr, pf_ref, TQ - 1.0 - m_col)

    @pl.when(s >= n_tiles)
    def _():
        scan_step((2 * n_tiles - 1 - s) % per_seq == per_seq - 1, HEADS, s0b_ref, sb_scr, qb_ref, m_col)


def _ret_prefix(lg, rk, rv, s0f, s0b):
    n_tiles, per_seq = TS // TQ, LAT_LEN // TQ
    tile_of = lambda s: jnp.where(s < n_tiles, s, 2 * n_tiles - 1 - s)
    st = pl.BlockSpec((None, HEADS // 2, LANES, LANES), lambda s, lg: (tile_of(s) // per_seq, 0, 0, 0))
    kv = pl.BlockSpec((TQ, HEADS * RDK), lambda s, lg: (TP // TQ + tile_of(s), 0))
    pf_blk = pl.BlockSpec((None, HEADS // 2, LANES, LANES), lambda s, lg: (jnp.minimum(s, n_tiles - 1), 0, 0, 0))
    qb_blk = pl.BlockSpec((None, HEADS // 2, LANES, LANES),
                          lambda s, lg: (jnp.minimum(2 * n_tiles - 1 - s, n_tiles - 1), 0, 0, 0))
    shape = jax.ShapeDtypeStruct((n_tiles, HEADS // 2, LANES, LANES), BF16)
    return pl.pallas_call(
        _ret_prefix_kernel,
        grid_spec=pltpu.PrefetchScalarGridSpec(
            num_scalar_prefetch=1, grid=(2 * n_tiles,),
            in_specs=[kv, kv, st, st], out_specs=[pf_blk, qb_blk],
            scratch_shapes=[pltpu.VMEM((HEADS // 2, LANES, LANES), F32)] * 2),
        out_shape=[shape, shape],
        compiler_params=_cparams(("arbitrary",)),
        name="ret_prefix",
    )(lg, rk, rv, s0f, s0b)


def _retention_kernel(latent, lg_ref, rq_ref, rk_ref, rv_ref, rg_ref, *refs):
    q0 = 0
    seq_len = rq_ref.shape[0]
    if latent:
        s0f_ref, s0b_ref, o_ref = refs
    else:
        unperm_ref, o_ref, sf_ref, sb_ref = refs
    tq, tk = rq_ref.shape[0], rk_ref.shape[0]
    nt = (((1,), (1,)), ((), ()))
    n_idx = (q0 + lax.broadcasted_iota(jnp.int32, (tq, tk), 0)).astype(F32)
    m_idx = lax.broadcasted_iota(jnp.int32, (tq, tk), 1).astype(F32)
    dist = n_idx - m_idx
    adist = jnp.abs(dist)
    fwd = dist > 0.0
    diag = jnp.where(dist == 0.0, 1.0, 0.0)
    lane = lax.broadcasted_iota(jnp.int32, (1, LANES), 1)
    lo = lane < RDK
    n_col = (q0 + lax.broadcasted_iota(jnp.int32, (tq, 1), 0)).astype(F32)
    m_col = lax.broadcasted_iota(jnp.int32, (tk, 1), 0).astype(F32)
    for pair in range(HEADS // 2):
        sl = slice(pair * LANES, (pair + 1) * LANES)
        qb, kb, vb = rq_ref[:, sl], rk_ref[:, sl], rv_ref[:, sl]
        acc = jnp.zeros((tq, LANES), F32)
        for sub in range(2):
            h = 2 * pair + sub
            lgf, lgb = lg_ref[h], lg_ref[HEADS + h]
            half = lo if sub == 0 else jnp.logical_not(lo)
            qm = jnp.where(half, qb, jnp.zeros_like(qb))
            vm = jnp.where(half, vb, jnp.zeros_like(vb))
            s = lax.dot_general(qm, kb, nt, preferred_element_type=F32)
            w = jnp.exp(adist * jnp.where(fwd, lgf, lgb)) + diag
            acc = acc + _bdot((s * w).astype(BF16), vm)
        lgf_l = jnp.where(lo, lg_ref[2 * pair], lg_ref[2 * pair + 1])
        lgb_l = jnp.where(lo, lg_ref[HEADS + 2 * pair], lg_ref[HEADS + 2 * pair + 1])
        if latent:
            acc = acc + _bdot(qb, s0f_ref[pair]) * jnp.exp((n_col + 1.0) * lgf_l)
            acc = acc + _bdot(qb, s0b_ref[pair]) * jnp.exp((seq_len - n_col) * lgb_l)
        else:
            v_swapped = pltpu.roll(vb.astype(F32), RDK, axis=1).astype(BF16)
            for st_ref, dec in ((sf_ref, jnp.exp((seq_len - 1.0 - m_col) * lgf_l)),
                                (sb_ref, jnp.exp(m_col * lgb_l))):
                kt = (kb.astype(F32) * dec).T.astype(BF16)
                kt = _bdot(unperm_ref[...], kt).astype(BF16)
                st_ref[2 * pair] = _bdot(kt, vb)[0:RDK, 0:RDK]
                st_ref[2 * pair + 1] = _bdot(kt, v_swapped)[RDK:, 0:RDK]
        inv = 1.0 / RDK
        mu = jnp.where(lo, jnp.sum(jnp.where(lo, acc, 0.0), axis=-1, keepdims=True),
                       jnp.sum(jnp.where(lo, 0.0, acc), axis=-1, keepdims=True)) * inv
        dlt = acc - mu
        d2 = dlt * dlt
        var = jnp.where(lo, jnp.sum(jnp.where(lo, d2, 0.0), axis=-1, keepdims=True),
                        jnp.sum(jnp.where(lo, 0.0, d2), axis=-1, keepdims=True)) * inv
        g = rg_ref[:, sl]
        o_ref[:, sl] = (dlt * lax.rsqrt(var + 1e-5) * (g * _sigmoid(g))).astype(BF16)


def _retention_tiles_kernel(n_prev, lg_ref, rq_ref, rk_ref, rv_ref, rg_ref, pf_ref, qb_ref, unperm_ref, *refs):
    prev_refs, (o_ref, sf_ref, sb_ref) = refs[:2 * n_prev], refs[2 * n_prev:]
    is_latent = pl.program_id(0) >= TP // TQ

    @pl.when(jnp.logical_not(is_latent))
    def _():
        for k in range(n_prev):
            sf_ref[k] = prev_refs[2 * k][...]
            sb_ref[k] = prev_refs[2 * k + 1][...]
        _retention_kernel(False, lg_ref, rq_ref, rk_ref, rv_ref, rg_ref, unperm_ref, o_ref,
                          sf_ref.at[n_prev], sb_ref.at[n_prev])

    @pl.when(is_latent)
    def _():
        _retention_kernel(True, lg_ref, rq_ref, rk_ref, rv_ref, rg_ref, pf_ref, qb_ref, o_ref)


def _retention(lg, rq, rk, rv, rg, s0f, s0b, unperm, prev_states):
    w = HEADS * RDK
    n_prev = 0 if prev_states is None else prev_states[0].shape[1]
    pf, qb = _ret_prefix(lg, rk, rv, s0f, s0b)
    tile = lambda i, lg: (i, 0)
    s0_blk = pl.BlockSpec((None, HEADS // 2, LANES, LANES), lambda i, lg: (jnp.maximum(i - TP // TQ, 0), 0, 0, 0))
    seq = lambda i, lg: (jnp.minimum(i, N_PROMPT_SEQ - 1), 0, 0, 0, 0)
    st_blk = pl.BlockSpec((None, n_prev + 1, HEADS, RDK, RDK), seq)
    st_shape = jax.ShapeDtypeStruct((N_PROMPT_SEQ, n_prev + 1, HEADS, RDK, RDK), F32)
    prev = () if prev_states is None else tuple(prev_states)
    prev_specs = [pl.BlockSpec((None, None, HEADS, RDK, RDK),
                               functools.partial(lambda k, i, lg: (jnp.minimum(i, N_PROMPT_SEQ - 1), k, 0, 0, 0), k))
                  for k in range(n_prev) for _ in range(2)]
    prev_args = [p for k in range(n_prev) for p in prev]
    return pl.pallas_call(
        functools.partial(_retention_tiles_kernel, n_prev),
        grid_spec=pltpu.PrefetchScalarGridSpec(
            num_scalar_prefetch=1, grid=(T // TQ,),
            in_specs=[pl.BlockSpec((TQ, w), tile)] * 4 + [s0_blk, s0_blk,
                      pl.BlockSpec((LANES, LANES), lambda i, lg: (0, 0))] + prev_specs,
            out_specs=[pl.BlockSpec((TQ, w), tile), st_blk, st_blk]),
        out_shape=[jax.ShapeDtypeStruct((T, w), BF16), st_shape, st_shape],
        compiler_params=_cparams(("arbitrary",)),
        name="retention",
    )(lg, rq, rk, rv, rg, pf, qb, unperm, *prev_args)


def _mix_ffn_kernel(n_x, *refs):
    x_refs, refs = refs[:n_x], refs[n_x:]
    a_ref, r_ref, mod_ref, wo_ref, wg_ref, wu_ref, wd_ref, ln_ref, o_ref = refs
    half = HEADS * VDIM
    y = _bdot(a_ref[...], wo_ref[0:half, :]) + _bdot(r_ref[...], wo_ref[half:, :])
    x1 = _layer_norm(ALPHA * _tile_rows(x_refs, TM_FF) + mod_ref[2:3, :] * y, ln_ref[0:1, :], ln_ref[1:2, :])
    h = (x1 * (1.0 + mod_ref[4:5, :]) + mod_ref[3:4, :]).astype(BF16)
    acc = None
    for f in range(D_FF // TF):
        cols = slice(f * TF, (f + 1) * TF)
        g = _bdot(h, wg_ref[:, cols])
        u = _bdot(h, wu_ref[:, cols])
        part = _bdot((g * _sigmoid(g) * u).astype(BF16), wd_ref[cols, :])
        acc = part if acc is None else acc + part
    o_ref[...] = _layer_norm(ALPHA * x1 + mod_ref[5:6, :] * acc, ln_ref[2:3, :], ln_ref[3:4, :])


def _mix_ffn(x, attn, ret, mod_l, w_out, wg, wu, wd, ln, j):
    x_args, x_specs = _tile_specs(x, TM_FF)
    tok = lambda w: pl.BlockSpec((TM_FF, w), lambda i: (i, 0))
    resident = lambda a: pl.BlockSpec((None,) + a.shape[1:], lambda i: (j, 0, 0), pipeline_mode=pl.Buffered(1))
    return pl.pallas_call(
        functools.partial(_mix_ffn_kernel, len(x_args)),
        grid=(T // TM_FF,),
        in_specs=x_specs + [tok(HEADS * VDIM), tok(HEADS * RDK),
                  pl.BlockSpec((None, 8, D), lambda i: (_group_of_tile(i, TM_FF), 0, 0)),
                  resident(w_out), resident(wg), resident(wu), resident(wd), pl.BlockSpec((8, D), lambda i: (0, 0))],
        out_specs=tok(D),
        out_shape=jax.ShapeDtypeStruct((T, D), F32),
        compiler_params=_cparams(("arbitrary",)),
        name="mix_ffn",
    )(*x_args, attn, ret, mod_l, w_out, wg, wu, wd, ln)


def _conv_in_kernel(x_ref, mod_ref, w_ref, b_ref, z_ref):
    h = (x_ref[...] * (1.0 + mod_ref[1:2, :]) + mod_ref[0:1, :]).astype(BF16)
    p = _bdot(h, w_ref[...])
    b_ref[...] = p[:, 0:D]
    z_ref[...] = p[:, D:2 * D] * p[:, 2 * D:3 * D]


def _conv_in(x, mod_l, w_in, j):
    tok = pl.BlockSpec((TM_FF, D), lambda i: (i, 0))
    return pl.pallas_call(
        _conv_in_kernel,
        grid=(T // TM_FF,),
        in_specs=[tok, pl.BlockSpec((None, 8, D), lambda i: (_group_of_tile(i, TM_FF), 0, 0)),
                  pl.BlockSpec((None,) + w_in.shape[1:], lambda i: (j, 0, 0))],
        out_specs=[tok, tok],
        out_shape=[jax.ShapeDtypeStruct((T, D), F32)] * 2,
        compiler_params=_cparams(("parallel",)),
        name="conv_in",
    )(x, mod_l, w_in)


def _conv_out_kernel(x_ref, b_ref, z_ref, zp_ref, zn_ref, mod_ref, cw_ref, w_ref, ln_ref, rw_ref, rb_ref,
                     o_ref, h_ref, route_ref):
    i = pl.program_id(0)
    z = z_ref[...]
    tm = z.shape[0]
    row = lax.broadcasted_iota(jnp.int32, (tm, 1), 0)
    seq_len = jnp.where(i < TP // tm, PROMPT_LEN, LAT_LEN)
    pos = (i * tm + row) & (seq_len - 1)
    prev = jnp.where(row == 0, zp_ref[7:8, :], pltpu.roll(z, 1, axis=0))
    prev = jnp.where(pos == 0, 0.0, prev)
    nxt = jnp.where(row == tm - 1, zn_ref[0:1, :], pltpu.roll(z, tm - 1, axis=0))
    nxt = jnp.where(pos == seq_len - 1, 0.0, nxt)
    y = prev * cw_ref[0:1, :] + z * cw_ref[1:2, :] + nxt * cw_ref[2:3, :]
    t = _bdot((b_ref[...] * y).astype(BF16), w_ref[...])
    x1 = _layer_norm(ALPHA * x_ref[...] + mod_ref[2:3, :] * t, ln_ref[0:1, :], ln_ref[1:2, :])
    o_ref[...] = x1
    h = x1 * (1.0 + mod_ref[4:5, :]) + mod_ref[3:4, :]
    h_hi = h.astype(BF16)
    h_ref[...] = h_hi
    h_lo = (h - h_hi.astype(F32)).astype(BF16)
    both = _bdot(h_hi, rw_ref[...])
    logits = both[:, :LANES] + both[:, LANES:] + _bdot(h_lo, rw_ref[:, :LANES]) + rb_ref[...]
    lane = lax.broadcasted_iota(jnp.int32, logits.shape, 1).astype(F32)
    t1 = jnp.max(logits, axis=-1, keepdims=True)
    i1 = jnp.min(jnp.where(logits == t1, lane, float(LANES)), axis=-1, keepdims=True)
    rest = jnp.where(lane == i1, -jnp.inf, logits)
    t2 = jnp.max(rest, axis=-1, keepdims=True)
    i2 = jnp.min(jnp.where(rest == t2, lane, float(LANES)), axis=-1, keepdims=True)
    e = jnp.exp(t2 - t1)
    den = 1.0 + e
    route_ref[...] = jnp.where(lane == 0.0, i1, jnp.where(lane == 1.0, i2,
                               jnp.where(lane == 2.0, 1.0 / den, jnp.where(lane == 3.0, e / den, 0.0))))


def _conv_out(x, b, z, mod_l, cw, w_out, ln, rw, rb, j):
    tok = pl.BlockSpec((TM_FF, D), lambda i: (i, 0))
    sub = TM_FF // 8
    return pl.pallas_call(
        _conv_out_kernel,
        grid=(T // TM_FF,),
        in_specs=[tok, tok, tok,
                  pl.BlockSpec((8, D), lambda i: (jnp.maximum(i * sub - 1, 0), 0)),
                  pl.BlockSpec((8, D), lambda i: (jnp.minimum((i + 1) * sub, T // 8 - 1), 0)),
                  pl.BlockSpec((None, 8, D), lambda i: (_group_of_tile(i, TM_FF), 0, 0)),
                  pl.BlockSpec((8, D), lambda i: (0, 0)), pl.BlockSpec((None, D, D), lambda i: (j, 0, 0)),
                  pl.BlockSpec((8, D), lambda i: (0, 0)),
                  pl.BlockSpec((D, 2 * LANES), lambda i: (0, 0)), pl.BlockSpec((1, LANES), lambda i: (0, 0))],
        out_specs=[tok, tok, pl.BlockSpec((TM_FF, LANES), lambda i: (i, 0))],
        out_shape=[jax.ShapeDtypeStruct((T, D), F32), jax.ShapeDtypeStruct((T, D), BF16),
                   jax.ShapeDtypeStruct((T, LANES), F32)],
        compiler_params=_cparams(("parallel",)),
        name="conv_out",
    )(x, b, z, z, z, mod_l, cw, w_out, ln, rw, rb)


def _moe_up_kernel(te_ref, nv_ref, x_ref, wg_ref, wu_ref, a_ref):
    @pl.when(pl.program_id(1) < nv_ref[0])
    def _():
        h = x_ref[...]
        g = _wdot(h, wg_ref[...])
        u = _wdot(h, wu_ref[...])
        a_ref[...] = (g * _sigmoid(g) * u).astype(BF16)

    @pl.when(pl.program_id(1) >= nv_ref[0])
    def _():
        a_ref[...] = jnp.zeros_like(a_ref)


def _moe_down_kernel(te_ref, nv_ref, a_ref, wd_ref, o_ref):
    @pl.when(pl.program_id(0) < nv_ref[0])
    def _():
        o_ref[...] = _wdot(a_ref[...], wd_ref[...])

    @pl.when(pl.program_id(0) >= nv_ref[0])
    def _():
        o_ref[...] = jnp.zeros_like(o_ref)


def _moe(tile_expert, n_valid, xs, wg, wu, wd, j):
    n_tiles = NP_ROWS // TM_FF
    act = pl.pallas_call(
        _moe_up_kernel,
        grid_spec=pltpu.PrefetchScalarGridSpec(
            num_scalar_prefetch=2, grid=(D_FF // TF, n_tiles),
            in_specs=[pl.BlockSpec((TM_FF, D), lambda f, i, te, nv: (i, 0)),
                      pl.BlockSpec((None, None, D, TF), lambda f, i, te, nv: (j, te[i], 0, f)),
                      pl.BlockSpec((None, None, D, TF), lambda f, i, te, nv: (j, te[i], 0, f))],
            out_specs=pl.BlockSpec((TM_FF, TF), lambda f, i, te, nv: (i, f))),
        out_shape=jax.ShapeDtypeStruct((NP_ROWS, D_FF), BF16),
        compiler_params=_cparams(("arbitrary", "arbitrary")),
        name="moe_up",
    )(tile_expert, n_valid, xs, wg, wu)
    return pl.pallas_call(
        _moe_down_kernel,
        grid_spec=pltpu.PrefetchScalarGridSpec(
            num_scalar_prefetch=2, grid=(n_tiles,),
            in_specs=[pl.BlockSpec((TM_FF, D_FF), lambda i, te, nv: (i, 0)),
                      pl.BlockSpec((None, None, D_FF, D), lambda i, te, nv: (j, te[i], 0, 0))],
            out_specs=pl.BlockSpec((TM_FF, D), lambda i, te, nv: (i, 0))),
        out_shape=jax.ShapeDtypeStruct((NP_ROWS, D), F32),
        compiler_params=_cparams(("arbitrary",)),
        name="moe_down",
    )(tile_expert, n_valid, act, wd)


def _combine_kernel(split, x_ref, o0_ref, o1_ref, route_ref, mod_ref, ln_ref, *o_refs):
    y = route_ref[:, 2:3] * o0_ref[...] + route_ref[:, 3:4] * o1_ref[...]
    out = _layer_norm(ALPHA * x_ref[...] + mod_ref[5:6, :] * y, ln_ref[2:3, :], ln_ref[3:4, :])
    if not split:
        o_refs[0][...] = out
        return
    is_latent = pl.program_id(0) >= TP // TM_FF

    @pl.when(jnp.logical_not(is_latent))
    def _():
        o_refs[0][...] = out

    @pl.when(is_latent)
    def _():
        o_refs[1][...] = out


def _combine(x, o0, o1, route, mod_l, ln, split):
    tok = pl.BlockSpec((TM_FF, D), lambda i: (i, 0))
    if split:
        out_specs = [pl.BlockSpec((TM_FF, D), lambda i: (jnp.minimum(i, TP // TM_FF - 1), 0)),
                     pl.BlockSpec((TM_FF, D), lambda i: (jnp.maximum(i - TP // TM_FF, 0), 0))]
        out_shape = [jax.ShapeDtypeStruct((TP, D), F32), jax.ShapeDtypeStruct((TS, D), F32)]
    else:
        out_specs, out_shape = tok, jax.ShapeDtypeStruct((T, D), F32)
    return pl.pallas_call(
        functools.partial(_combine_kernel, split),
        grid=(T // TM_FF,),
        in_specs=[tok, tok, tok, pl.BlockSpec((TM_FF, LANES), lambda i: (i, 0)),
                  pl.BlockSpec((None, 8, D), lambda i: (_group_of_tile(i, TM_FF), 0, 0)),
                  pl.BlockSpec((8, D), lambda i: (0, 0))],
        out_specs=out_specs,
        out_shape=out_shape,
        compiler_params=_cparams(("arbitrary",)),
        name="moe_combine",
    )(x, o0, o1, route, mod_l, ln)


def _routing_plan(route):
    e = jnp.concatenate([route[:, 0], route[:, 1]]).astype(jnp.int32)
    onehot = (e[:, None] == jnp.arange(N_EXP, dtype=jnp.int32)[None, :]).astype(jnp.int32)
    csum = jnp.cumsum(onehot, axis=0)
    counts = csum[-1]
    rank = jnp.sum((csum - onehot) * onehot, axis=1)
    padded = (counts + TM_FF - 1) // TM_FF * TM_FF
    pend = jnp.cumsum(padded)
    dest = jnp.sum(onehot * (pend - padded)[None, :], axis=1) + rank
    order = jnp.argsort(e, stable=True).astype(jnp.int32)
    rows = jnp.arange(NP_ROWS, dtype=jnp.int32)
    before = (rows[:, None] >= pend[None, :]).astype(jnp.int32)
    row_e = jnp.minimum(jnp.sum(before, axis=1), N_EXP - 1)
    row_cnt = jnp.sum((row_e[:, None] == jnp.arange(N_EXP, dtype=jnp.int32)[None, :]) * counts[None, :], axis=1)
    q = jnp.clip(rows - jnp.sum(before * padded[None, :], axis=1), 0, jnp.maximum(row_cnt - 1, 0))
    src = jnp.minimum(jnp.sum(before * counts[None, :], axis=1) + q, 2 * T - 1)
    row_token = order[src] % T
    n_valid = (pend[-1] // TM_FF).astype(jnp.int32)
    tile_start = jnp.minimum(jnp.arange(NP_ROWS // TM_FF, dtype=jnp.int32), n_valid - 1) * TM_FF
    tile_expert = jnp.minimum(jnp.sum((tile_start[:, None] >= pend[None, :]).astype(jnp.int32), axis=1), N_EXP - 1)
    return dest[:T], dest[T:], row_token, tile_expert.astype(jnp.int32), n_valid.reshape(1)


_INV_PERM_RDK = np.argsort(np.concatenate([np.arange(0, RDK, 2), np.arange(1, RDK, 2)]))


def _pad_cols(a, width):
    return jnp.pad(a, ((0, 0), (0, width - a.shape[1])))


def _deinterleave(a):
    n = a.shape[-1]
    return jnp.swapaxes(a.reshape(a.shape[:-1] + (n // 2, 2)), -1, -2).reshape(a.shape)


def _prep_even_weights(w_in, w_q_b, w_kv_b):
    o_kpe, o_rq, o_rk, o_rv = 640, 672, 1184, 1696
    heads = lambda a: _deinterleave(a.reshape(D, HEADS, RDK)).reshape(D, HEADS * RDK)
    kpe = w_in[:, o_kpe:o_kpe + ROPE]
    w_in_p = jnp.concatenate([
        w_in[:, :640], heads(w_in[:, o_rq:o_rk]), heads(w_in[:, o_rk:o_rv]) * (RDK ** -0.5), w_in[:, o_rv:],
        _pad_cols(kpe, LANES), _pad_cols(_deinterleave(kpe), LANES),
        _pad_cols(jnp.concatenate([-kpe[:, 1::2], kpe[:, 0::2]], axis=1), LANES)], axis=1).astype(BF16)
    wq = w_q_b.reshape(Q_LORA, HEADS, NOPE + ROPE)
    wq = jnp.concatenate([wq[:, :, :NOPE], _deinterleave(wq[:, :, NOPE:]),
                          jnp.zeros((Q_LORA, HEADS, LANES - NOPE - ROPE), F32)], axis=2)
    wq = wq.reshape(Q_LORA, HEADS * LANES).astype(BF16)
    wkv = w_kv_b.reshape(KV_LORA, HEADS, NOPE + VDIM)
    zero = jnp.zeros((KV_LORA, HEADS, LANES - NOPE), F32)
    wk = jnp.concatenate([wkv[:, :, :NOPE], zero], axis=2).reshape(KV_LORA, HEADS * LANES)
    wv = wkv[:, :, NOPE:].reshape(KV_LORA, HEADS // 2, 2, VDIM)
    zv = jnp.zeros((KV_LORA, HEADS // 2, VDIM), F32)
    wv = jnp.stack([jnp.concatenate([wv[:, :, 0], zv], axis=2), jnp.concatenate([zv, wv[:, :, 1]], axis=2)], axis=2)
    wkv_p = jnp.concatenate([wk, wv.reshape(KV_LORA, HEADS * LANES)], axis=1).astype(BF16)
    return w_in_p, wq, wkv_p


def _placement():
    ek = np.zeros((LANES, HEADS * LANES), np.float32)
    for h in range(HEADS):
        ek[np.arange(ROPE), h * LANES + NOPE + np.arange(ROPE)] = 1.0
    return jnp.asarray(ek, BF16)


def _rotary_tables():
    rows = LAT_LEN // GRID_W
    r, col = jnp.meshgrid(jnp.arange(rows, dtype=F32), jnp.arange(GRID_W, dtype=F32), indexing='ij')
    n_freq = ROPE // 4
    freqs = 1.0 / (10000.0 ** (jnp.arange(n_freq, dtype=F32) / n_freq))
    ang = jnp.concatenate([r.reshape(-1)[:, None] * freqs, col.reshape(-1)[:, None] * freqs], axis=-1)
    cos, sin = jnp.cos(ang), jnp.sin(ang)
    theta = 1.0 / (10000.0 ** jnp.linspace(0.0, 1.0, RDK // 2, dtype=F32))
    rang = jnp.arange(LAT_LEN, dtype=F32)[:, None] * theta
    rcos, rsin = jnp.cos(rang), jnp.sin(rang)
    one = lambda w: jnp.ones((LAT_LEN, w), F32)
    zero = lambda w: jnp.zeros((LAT_LEN, w), F32)
    lat = [jnp.concatenate([one(NOPE), cos, cos, one(LANES - NOPE - ROPE)], axis=1),
           jnp.concatenate([zero(NOPE), sin, sin, zero(LANES - NOPE - ROPE)], axis=1),
           jnp.concatenate([cos, cos, zero(LANES - ROPE)], axis=1),
           jnp.concatenate([sin, sin, zero(LANES - ROPE)], axis=1),
           jnp.concatenate([rcos] * 4, axis=1), jnp.concatenate([rsin] * 4, axis=1)]
    ident = [np.ones((TM, LANES), np.float32), np.zeros((TM, LANES), np.float32)]
    ident_k = np.concatenate([np.ones((TM, ROPE), np.float32), np.zeros((TM, LANES - ROPE), np.float32)], axis=1)
    ident = [ident[0], ident[1], ident_k, ident[1], ident[0], ident[1]]
    return [jnp.concatenate([l, jnp.asarray(c)], axis=0) for l, c in zip(lat, ident)]


def _block_diag_states(s0):
    s = jnp.swapaxes(_deinterleave(jnp.swapaxes(s0, -1, -2)), -1, -2)
    s = s.reshape(s0.shape[0], HEADS // 2, 2, RDK, RDK)
    z = jnp.zeros_like(s[:, :, 0])
    top = jnp.concatenate([s[:, :, 0], z], axis=-1)
    bot = jnp.concatenate([z, s[:, :, 1]], axis=-1)
    return jnp.concatenate([top, bot], axis=-2)


def _unpermute_matrix():
    m = np.zeros((LANES, LANES), np.float32)
    for blk in range(LANES // RDK):
        m[blk * RDK + np.arange(RDK), blk * RDK + _INV_PERM_RDK] = 1.0
    return jnp.asarray(m, BF16)


def kernel(x_prompt, x_sample, c, cache_ckv, cache_kpe, state_ret_fwd, state_ret_bwd, c_ctx, w_mod, b_mod, ln_g, ln_b, w_in_mix, q_a_gain, kv_a_gain, w_q_b, w_kv_b, ret_decay_fwd, ret_decay_bwd, w_out_mix, w_in_conv, conv_w, w_out_conv, ffn_gate, ffn_up, ffn_down, router_w, router_b, exp_gate, exp_up, exp_down):
    x = (x_prompt.reshape(TP, D), x_sample.reshape(TS, D))
    cond8 = jnp.concatenate([c_ctx[None], c, jnp.zeros((8 - 1 - N_LAT_SEQ, D), F32)], axis=0)
    mods = _modulation(cond8, w_mod, b_mod)
    mods = jnp.pad(mods.reshape(DEPTH, N_GROUPS, 6, D), ((0, 0), (0, 0), (0, 2), (0, 0)))
    ln = jnp.pad(jnp.concatenate([ln_g, ln_b], axis=1)[:, jnp.array([0, 2, 1, 3])], ((0, 0), (0, 4), (0, 0)))
    tabs = _rotary_tables()
    ek = _placement()
    unperm = _unpermute_matrix()
    bf = lambda a: a.astype(BF16)
    w_out_mix_b, w_in_conv_b, w_out_conv_b = bf(w_out_mix), bf(w_in_conv), bf(w_out_conv)
    ffn_b = (bf(ffn_gate), bf(ffn_up), bf(ffn_down))
    exp_b = (exp_gate, exp_up, exp_down)
    cache, states = None, None
    for layer in range(DEPTH):
        j = layer // 2
        mod_l, ln_l = mods[layer], ln[layer]
        if layer % 2 == 0:
            w_in_p, wq, wkv = _prep_even_weights(w_in_mix[j], w_q_b[j], w_kv_b[j])
            q, k, v, cache, rq, rk, rv, rg = _even_in(
                x, mod_l, w_in_p, q_a_gain[j][None], kv_a_gain[j][None], wq, wkv, ek, tabs, cache)
            kpe_c = _pad_cols(_deinterleave(cache_kpe[:, j]).reshape(N_LAT_SEQ * PAST, ROPE), LANES)
            kc, vc = _ctx_kv(cache_ckv[:, j].reshape(N_LAT_SEQ * PAST, KV_LORA), kpe_c, wkv, ek)
            attn = _attention(q, k, v, kc, vc)
            lg = jnp.concatenate([jax.nn.log_sigmoid(ret_decay_fwd[j].astype(F32)),
                                  jax.nn.log_sigmoid(ret_decay_bwd[j].astype(F32))])
            ret, sf, sb = _retention(lg, rq, rk, rv, rg, _block_diag_states(state_ret_fwd[:, j]),
                                     _block_diag_states(state_ret_bwd[:, j]), unperm, states)
            states = (sf, sb)
            x = _mix_ffn(x, attn, ret, mod_l, w_out_mix_b, *ffn_b, ln_l, j)
        else:
            b, z = _conv_in(x, mod_l, w_in_conv_b, j)
            cw = jnp.pad(conv_w[j], ((0, 5), (0, 0)))
            rw = _pad_cols(router_w[j], LANES)
            rw_hi = rw.astype(BF16)
            rw = jnp.concatenate([rw_hi, (rw - rw_hi.astype(F32)).astype(BF16)], axis=1)
            rb = jnp.concatenate([router_b[j].astype(F32), jnp.full((LANES - N_EXP,), -1e30, F32)])[None]
            x, h, route = _conv_out(x, b, z, mod_l, cw, w_out_conv_b, ln_l, rw, rb, j)
            dest0, dest1, row_token, tile_expert, n_valid = _routing_plan(route)
            out_sorted = _moe(tile_expert, n_valid, h[row_token], *exp_b, j)
            x = _combine(x, out_sorted[dest0], out_sorted[dest1], route, mod_l, ln_l, split=layer == DEPTH - 1)
    y_prompt = x[0].reshape(N_PROMPT_SEQ, PROMPT_LEN, D)
    y_sample = x[1].reshape(N_LAT_SEQ, LAT_LEN, D)
    return (y_prompt, y_sample, cache[0], cache[1], states[0], states[1])
```

```python
import functools

import numpy as np
import jax
import jax.numpy as jnp
from jax import lax
from jax.experimental import pallas as pl
from jax.experimental.pallas import tpu as pltpu

F32 = jnp.float32
BF16 = jnp.bfloat16

D = 1024
DEPTH = 4
N_PROMPT_SEQ, PROMPT_LEN = 32, 256
N_LAT_SEQ, LAT_LEN = 2, 2048
PAST = 512
GRID_W = 64
TP = N_PROMPT_SEQ * PROMPT_LEN
TS = N_LAT_SEQ * LAT_LEN
T = TP + TS
HEADS = 8
NOPE, ROPE, VDIM = 64, 32, 64
Q_LORA, KV_LORA = 384, 256
RDK = 64
D_FF = 2816
N_EXP = 8
ALPHA = (2.0 * DEPTH) ** 0.25
LOG2E = float(np.log2(np.e))
Q_SCALE = float((NOPE + ROPE) ** -0.5) * LOG2E
LANES = 128
N_GROUPS = 8

TM = 512
TM_FF = 512
TF = D_FF // 2
TQ = 256
NP_ROWS = 2 * T + N_EXP * TM_FF
VMEM_LIMIT = 56 * 1024 * 1024

IN_COLS = 3072


def _cparams(sem):
    return pltpu.CompilerParams(dimension_semantics=sem, vmem_limit_bytes=VMEM_LIMIT)


def _group_of_tile(i, tm):
    per_seq = LAT_LEN // tm
    return jnp.maximum(i - TP // tm + per_seq, 0) // per_seq


def _bdot(a, b):
    return jnp.dot(a, b, preferred_element_type=F32)


def _wdot(a, w):
    return lax.dot_general(a, w, (((1,), (0,)), ((), ())), preferred_element_type=F32)


def _sigmoid(v):
    return 1.0 / (1.0 + jnp.exp(-v))


def _layer_norm(v, g, b):
    mu = jnp.mean(v, axis=-1, keepdims=True)
    d = v - mu
    var = jnp.mean(d * d, axis=-1, keepdims=True)
    return d * lax.rsqrt(var + 1e-5) * g + b


def _rms(v, g):
    return v * lax.rsqrt(jnp.mean(v * v, axis=-1, keepdims=True) + 1e-6) * g


def _mod_kernel(c_ref, w_ref, b_ref, o_ref):
    c = c_ref[...]
    s = (c * _sigmoid(c)).astype(BF16)
    o_ref[...] = _bdot(s, w_ref[...].astype(BF16)) + b_ref[...]


def _modulation(cond8, w_mod, b_mod):
    tn = 1536
    return pl.pallas_call(
        _mod_kernel,
        grid=(DEPTH, 6 * D // tn),
        in_specs=[pl.BlockSpec((8, D), lambda l, n: (0, 0)),
                  pl.BlockSpec((None, D, tn), lambda l, n: (l, 0, n)),
                  pl.BlockSpec((None, 1, tn), lambda l, n: (l, 0, n))],
        out_specs=pl.BlockSpec((None, 8, tn), lambda l, n: (l, 0, n)),
        out_shape=jax.ShapeDtypeStruct((DEPTH, 8, 6 * D), F32),
        compiler_params=_cparams(("arbitrary", "arbitrary")),
        name="modulation",
    )(cond8, w_mod, b_mod.reshape(DEPTH, 1, 6 * D))


def _swap_halves(a, half):
    n = a.shape[-1]
    lane = lax.broadcasted_iota(jnp.int32, a.shape, 1)
    first = (lane & (2 * half - 1)) < half
    return jnp.where(first, -pltpu.roll(a, n - half, axis=1), pltpu.roll(a, half, axis=1))


def _tile_rows(x_refs, tm):
    if len(x_refs) == 1:
        return x_refs[0][...]
    return jnp.where(pl.program_id(0) >= TP // tm, x_refs[1][...], x_refs[0][...])


def _tile_specs(x, tm):
    if not isinstance(x, tuple):
        return (x,), [pl.BlockSpec((tm, D), lambda i, *_: (i, 0))]
    return x, [pl.BlockSpec((tm, D), lambda i, *_: (jnp.minimum(i, TP // tm - 1), 0)),
               pl.BlockSpec((tm, D), lambda i, *_: (jnp.maximum(i - TP // tm, 0), 0))]


def _even_in_kernel(n_x, n_prev, *refs):
    x_refs, refs = refs[:n_x], refs[n_x:]
    (mod_ref, w_in_ref, qg_ref, kvg_ref, wq_ref, wkv_ref, ek_ref,
     cq_ref, sq_ref, ck_ref, sk_ref, cr_ref, sr_ref), refs = refs[:13], refs[13:]
    prev_refs, refs = refs[:2 * n_prev], refs[2 * n_prev:]
    q_ref, k_ref, v_ref, ckv_ref, kpe_ref, rq_ref, rk_ref, rv_ref, rg_ref = refs
    x = _tile_rows(x_refs, TM)
    h = (x * (1.0 + mod_ref[1:2, :]) + mod_ref[0:1, :]).astype(BF16)
    p = _bdot(h, w_in_ref[...])
    qn = _rms(p[:, 0:Q_LORA], qg_ref[...]).astype(BF16)
    qa = _bdot(qn, wq_ref[...])
    ckv = _rms(p[:, Q_LORA:Q_LORA + KV_LORA], kvg_ref[...])
    kv = _bdot(ckv.astype(BF16), wkv_ref[...])
    v_ref[...] = kv[:, HEADS * LANES:].astype(BF16)
    base = 2688
    ka = p[:, base + LANES:base + 2 * LANES]
    kb = p[:, base + 2 * LANES:base + 3 * LANES]
    rq = p[:, 640:1152]
    rk = p[:, 1152:1664]
    rv_ref[...] = p[:, 1664:2176].astype(BF16)
    rg_ref[...] = p[:, 2176:2688]
    lane = lax.broadcasted_iota(jnp.int32, qa.shape, 1) & (LANES - 1)
    qb = jnp.where(lane < NOPE + ROPE // 2,
                   -pltpu.roll(qa, qa.shape[1] - ROPE // 2, axis=1),
                   pltpu.roll(qa, ROPE // 2, axis=1))
    cq = jnp.concatenate([cq_ref[...]] * HEADS, axis=1)
    sq = jnp.concatenate([sq_ref[...]] * HEADS, axis=1)
    q_ref[...] = ((qa * cq + qb * sq) * Q_SCALE).astype(BF16)
    kpe_rot = ka * ck_ref[...] + kb * sk_ref[...]
    k_ref[...] = (kv[:, :HEADS * LANES] + _bdot(kpe_rot.astype(BF16), ek_ref[...])).astype(BF16)
    cr = jnp.concatenate([cr_ref[...]] * 4, axis=1)
    sr = jnp.concatenate([sr_ref[...]] * 4, axis=1)
    rq_ref[...] = (rq * cr + _swap_halves(rq, RDK // 2) * sr).astype(BF16)
    rk_ref[...] = (rk * cr + _swap_halves(rk, RDK // 2) * sr).astype(BF16)

    @pl.when(pl.program_id(0) < TP // TM)
    def _():
        for k in range(n_prev):
            ckv_ref[:, k] = prev_refs[2 * k][...]
            kpe_ref[:, k] = prev_refs[2 * k + 1][...]
        for s in range(TM // PROMPT_LEN):
            rows = slice(s * PROMPT_LEN, (s + 1) * PROMPT_LEN)
            ckv_ref[s, n_prev] = ckv[rows]
            kpe_ref[s, n_prev] = p[rows, base:base + ROPE]


def _even_in(x, mod_l, w_in, qg, kvg, wq, wkv, ek, tabs, prev_cache):
    x_args, x_specs = _tile_specs(x, TM)
    n_prev = 0 if prev_cache is None else prev_cache[0].shape[1]
    spt = TM // PROMPT_LEN
    tok = lambda w: pl.BlockSpec((TM, w), lambda i: (i, 0))
    full = lambda a: pl.BlockSpec(a.shape, lambda i: (0,) * a.ndim, pipeline_mode=pl.Buffered(1))
    lat_tiles = LAT_LEN // TM
    tab = pl.BlockSpec((TM, LANES), lambda i: (
        jnp.where(i < TP // TM, lat_tiles, jnp.maximum(i - TP // TM, 0) % lat_tiles), 0))
    seq = lambda i: (jnp.minimum(i, TP // TM - 1), 0, 0, 0)
    seq_k = lambda k, i: (jnp.minimum(i, TP // TM - 1), k, 0, 0)
    prev_args, prev_specs = [], []
    for k in range(n_prev):
        for a, w in zip(prev_cache, (KV_LORA, ROPE)):
            prev_args.append(a)
            prev_specs.append(pl.BlockSpec((spt, None, PROMPT_LEN, w), functools.partial(seq_k, k)))
    tok_outs = lambda dims: ([tok(w) for w, _ in dims], [jax.ShapeDtypeStruct((T, w), dt) for w, dt in dims])
    qkv_specs, qkv_shapes = tok_outs([(HEADS * LANES, BF16)] * 3)
    ret_specs, ret_shapes = tok_outs([(512, BF16), (512, BF16), (512, BF16), (512, F32)])
    cache_specs = [pl.BlockSpec((spt, n_prev + 1, PROMPT_LEN, w), seq) for w in (KV_LORA, ROPE)]
    cache_shapes = [jax.ShapeDtypeStruct((N_PROMPT_SEQ, n_prev + 1, PROMPT_LEN, w), F32) for w in (KV_LORA, ROPE)]
    q, k, v, ckv, kpe, rq, rk, rv, rg = pl.pallas_call(
        functools.partial(_even_in_kernel, len(x_args), n_prev),
        grid=(T // TM,),
        in_specs=x_specs + [pl.BlockSpec((None, 8, D), lambda i: (_group_of_tile(i, TM), 0, 0)),
                            full(w_in), full(qg), full(kvg), full(wq), full(wkv), full(ek)] + [tab] * 6 + prev_specs,
        out_specs=qkv_specs + cache_specs + ret_specs,
        out_shape=qkv_shapes + cache_shapes + ret_shapes,
        compiler_params=_cparams(("arbitrary",)),
        name="even_in",
    )(*x_args, mod_l, w_in, qg, kvg, wq, wkv, ek, *tabs, *prev_args)
    return q, k, v, (ckv, kpe), rq, rk, rv, rg


def _ctx_kv_kernel(ckv_ref, kpe_ref, wkv_ref, ek_ref, k_ref, v_ref):
    kv = _bdot(ckv_ref[...].astype(BF16), wkv_ref[...])
    k_ref[...] = (kv[:, :HEADS * LANES] + _bdot(kpe_ref[...].astype(BF16), ek_ref[...])).astype(BF16)
    v_ref[...] = kv[:, HEADS * LANES:].astype(BF16)


def _ctx_kv(ckv_c, kpe_c, wkv, ek):
    n = ckv_c.shape[0]
    full = lambda a: pl.BlockSpec(a.shape, lambda i: (0,) * a.ndim)
    return pl.pallas_call(
        _ctx_kv_kernel,
        grid=(n // PAST,),
        in_specs=[pl.BlockSpec((PAST, KV_LORA), lambda i: (i, 0)), pl.BlockSpec((PAST, LANES), lambda i: (i, 0)),
                  full(wkv), full(ek)],
        out_specs=[pl.BlockSpec((PAST, HEADS * LANES), lambda i: (i, 0))] * 2,
        out_shape=[jax.ShapeDtypeStruct((n, HEADS * LANES), BF16)] * 2,
        compiler_params=_cparams(("parallel",)),
        name="ctx_kv",
    )(ckv_c, kpe_c, wkv, ek)


def _attn_kernel(n_kv, q_ref, *refs):
    k_refs = refs[0:2 * n_kv:2]
    v_refs = refs[1:2 * n_kv:2]
    o_ref = refs[2 * n_kv]
    nt = (((1,), (1,)), ((), ()))
    for pair in range(HEADS // 2):
        acc = None
        for sub in range(2):
            sl = slice((2 * pair + sub) * LANES, (2 * pair + sub + 1) * LANES)
            qh = q_ref[:, sl]
            s = [lax.dot_general(qh, k[:, sl], nt, preferred_element_type=F32) for k in k_refs]
            m = functools.reduce(jnp.maximum, [jnp.max(a, axis=-1, keepdims=True) for a in s])
            e = [jnp.exp2(a - m) for a in s]
            den = functools.reduce(jnp.add, [jnp.sum(a, axis=-1, keepdims=True) for a in e])
            o = functools.reduce(jnp.add, [_bdot(a.astype(BF16), v[:, sl]) for a, v in zip(e, v_refs)])
            o = o / den
            acc = o if acc is None else acc + o
        o_ref[:, pair * LANES:(pair + 1) * LANES] = acc.astype(BF16)


def _latent_seq(i):
    return jnp.maximum(i - TP // TQ, 0) // (LAT_LEN // TQ)


def _attn_tiles_kernel(q_ref, kp_ref, vp_ref, kc_ref, vc_ref, kl_ref, vl_ref, o_ref):
    is_latent = pl.program_id(0) >= TP // TQ

    @pl.when(jnp.logical_not(is_latent))
    def _():
        _attn_kernel(1, q_ref, kp_ref, vp_ref, o_ref)

    @pl.when(is_latent)
    def _():
        _attn_kernel(2, q_ref, kc_ref, vc_ref, kl_ref, vl_ref, o_ref)


def _attention(q, k, v, kc, vc):
    w = HEADS * LANES
    tile = lambda i: (i, 0)
    ctx_own = lambda i: (jnp.minimum(i, N_PROMPT_SEQ - 1), 0)
    cache = lambda i: (_latent_seq(i), 0)
    lat_own = lambda i: (TP // LAT_LEN + _latent_seq(i), 0)
    return pl.pallas_call(
        _attn_tiles_kernel,
        grid=(T // TQ,),
        in_specs=[pl.BlockSpec((TQ, w), tile),
                  pl.BlockSpec((PROMPT_LEN, w), ctx_own), pl.BlockSpec((PROMPT_LEN, w), ctx_own),
                  pl.BlockSpec((PAST, w), cache), pl.BlockSpec((PAST, w), cache),
                  pl.BlockSpec((LAT_LEN, w), lat_own), pl.BlockSpec((LAT_LEN, w), lat_own)],
        out_specs=pl.BlockSpec((TQ, HEADS * VDIM), tile),
        out_shape=jax.ShapeDtypeStruct((T, HEADS * VDIM), BF16),
        compiler_params=_cparams(("arbitrary",)),
        name="attention",
    )(q, k, v, kc, vc, k, v)


def _ret_prefix_kernel(lg_ref, rk_ref, rv_ref, s0f_ref, s0b_ref, pf_ref, qb_ref, sf_scr, sb_scr):
    s = pl.program_id(0)
    n_tiles, per_seq = TS // TQ, LAT_LEN // TQ
    row = lax.broadcasted_iota(jnp.int32, (LANES, 1), 0)
    lane = lax.broadcasted_iota(jnp.int32, (1, LANES), 1)
    top, lo = row < RDK, lane < RDK
    same_head = top == lo
    m_col = lax.broadcasted_iota(jnp.int32, (TQ, 1), 0).astype(F32)

    def scan_step(first, lg_off, s0_ref, scr, out_ref, pos):
        @pl.when(first)
        def _():
            scr[...] = s0_ref[...]

        for pair in range(HEADS // 2):
            sl = slice(pair * LANES, (pair + 1) * LANES)
            lg_even, lg_odd = lg_ref[lg_off + 2 * pair], lg_ref[lg_off + 2 * pair + 1]
            dec = jnp.exp2(pos * jnp.where(lo, lg_even, lg_odd))
            local = _bdot((rk_ref[:, sl].astype(F32) * dec).T.astype(BF16), rv_ref[:, sl])
            out_ref[pair] = scr[pair].astype(BF16)
            tile_decay = jnp.exp2(float(TQ) * jnp.where(top, lg_even, lg_odd))
            scr[pair] = scr[pair] * tile_decay + jnp.where(same_head, local, 0.0)

    @pl.when(s < n_tiles)
    def _():
        scan_step(s % per_seq == 0, 0, s0f_ref, sf_scr, pf_ref, TQ - 1.0 - m_col)

    @pl.when(s >= n_tiles)
    def _():
        scan_step((2 * n_tiles - 1 - s) % per_seq == per_seq - 1, HEADS, s0b_ref, sb_scr, qb_ref, m_col)


def _ret_prefix(lg, rk, rv, s0f, s0b):
    n_tiles, per_seq = TS // TQ, LAT_LEN // TQ
    tile_of = lambda s: jnp.where(s < n_tiles, s, 2 * n_tiles - 1 - s)
    st = pl.BlockSpec((None, HEADS // 2, LANES, LANES), lambda s, lg: (tile_of(s) // per_seq, 0, 0, 0))
    kv = pl.BlockSpec((TQ, HEADS * RDK), lambda s, lg: (TP // TQ + tile_of(s), 0))
    pf_blk = pl.BlockSpec((None, HEADS // 2, LANES, LANES), lambda s, lg: (jnp.minimum(s, n_tiles - 1), 0, 0, 0))
    qb_blk = pl.BlockSpec((None, HEADS // 2, LANES, LANES),
                          lambda s, lg: (jnp.minimum(2 * n_tiles - 1 - s, n_tiles - 1), 0, 0, 0))
    shape = jax.ShapeDtypeStruct((n_tiles, HEADS // 2, LANES, LANES), BF16)
    return pl.pallas_call(
        _ret_prefix_kernel,
        grid_spec=pltpu.PrefetchScalarGridSpec(
            num_scalar_prefetch=1, grid=(2 * n_tiles,),
            in_specs=[kv, kv, st, st], out_specs=[pf_blk, qb_blk],
            scratch_shapes=[pltpu.VMEM((HEADS // 2, LANES, LANES), F32)] * 2),
        out_shape=[shape, shape],
        compiler_params=_cparams(("arbitrary",)),
        name="ret_prefix",
    )(lg, rk, rv, s0f, s0b)


def _retention_kernel(latent, lg_ref, rq_ref, rk_ref, rv_ref, rg_ref, *refs):
    q0 = 0
    seq_len = rq_ref.shape[0]
    if latent:
        s0f_ref, s0b_ref, o_ref = refs
    else:
        unperm_ref, o_ref, sf_ref, sb_ref = refs
    tq, tk = rq_ref.shape[0], rk_ref.shape[0]
    nt = (((1,), (1,)), ((), ()))
    n_idx = (q0 + lax.broadcasted_iota(jnp.int32, (tq, tk), 0)).astype(F32)
    m_idx = lax.broadcasted_iota(jnp.int32, (tq, tk), 1).astype(F32)
    dist = n_idx - m_idx
    adist = jnp.abs(dist)
    fwd = dist > 0.0
    diag = jnp.where(dist == 0.0, 1.0, 0.0)
    lane = lax.broadcasted_iota(jnp.int32, (1, LANES), 1)
    lo = lane < RDK
    n_col = (q0 + lax.broadcasted_iota(jnp.int32, (tq, 1), 0)).astype(F32)
    m_col = lax.broadcasted_iota(jnp.int32, (tk, 1), 0).astype(F32)
    for pair in range(HEADS // 2):
        sl = slice(pair * LANES, (pair + 1) * LANES)
        qb, kb, vb = rq_ref[:, sl], rk_ref[:, sl], rv_ref[:, sl]
        acc = jnp.zeros((tq, LANES), F32)
        for sub in range(2):
            h = 2 * pair + sub
            lgf, lgb = lg_ref[h], lg_ref[HEADS + h]
            half = lo if sub == 0 else jnp.logical_not(lo)
            qm = jnp.where(half, qb, jnp.zeros_like(qb))
            vm = jnp.where(half, vb, jnp.zeros_like(vb))
            s = lax.dot_general(qm, kb, nt, preferred_element_type=F32)
            w = jnp.exp2(adist * jnp.where(fwd, lgf, lgb)) + diag
            acc = acc + _bdot((s * w).astype(BF16), vm)
        lgf_l = jnp.where(lo, lg_ref[2 * pair], lg_ref[2 * pair + 1])
        lgb_l = jnp.where(lo, lg_ref[HEADS + 2 * pair], lg_ref[HEADS + 2 * pair + 1])
        if latent:
            acc = acc + _bdot(qb, s0f_ref[pair]) * jnp.exp2((n_col + 1.0) * lgf_l)
            acc = acc + _bdot(qb, s0b_ref[pair]) * jnp.exp2((seq_len - n_col) * lgb_l)
        else:
            v_swapped = pltpu.roll(vb.astype(F32), RDK, axis=1).astype(BF16)
            for st_ref, dec in ((sf_ref, jnp.exp2((seq_len - 1.0 - m_col) * lgf_l)),
                                (sb_ref, jnp.exp2(m_col * lgb_l))):
                kt = (kb.astype(F32) * dec).T.astype(BF16)
                kt = _bdot(unperm_ref[...], kt).astype(BF16)
                st_ref[2 * pair] = _bdot(kt, vb)[0:RDK, 0:RDK]
                st_ref[2 * pair + 1] = _bdot(kt, v_swapped)[RDK:, 0:RDK]
        inv = 1.0 / RDK
        mu = jnp.where(lo, jnp.sum(jnp.where(lo, acc, 0.0), axis=-1, keepdims=True),
                       jnp.sum(jnp.where(lo, 0.0, acc), axis=-1, keepdims=True)) * inv
        dlt = acc - mu
        d2 = dlt * dlt
        var = jnp.where(lo, jnp.sum(jnp.where(lo, d2, 0.0), axis=-1, keepdims=True),
                        jnp.sum(jnp.where(lo, 0.0, d2), axis=-1, keepdims=True)) * inv
        g = rg_ref[:, sl]
        o_ref[:, sl] = (dlt * lax.rsqrt(var + 1e-5) * (g * _sigmoid(g))).astype(BF16)


def _retention_tiles_kernel(n_prev, lg_ref, rq_ref, rk_ref, rv_ref, rg_ref, pf_ref, qb_ref, unperm_ref, *refs):
    prev_refs, (o_ref, sf_ref, sb_ref) = refs[:2 * n_prev], refs[2 * n_prev:]
    is_latent = pl.program_id(0) >= TP // TQ

    @pl.when(jnp.logical_not(is_latent))
    def _():
        for k in range(n_prev):
            sf_ref[k] = prev_refs[2 * k][...]
            sb_ref[k] = prev_refs[2 * k + 1][...]
        _retention_kernel(False, lg_ref, rq_ref, rk_ref, rv_ref, rg_ref, unperm_ref, o_ref,
                          sf_ref.at[n_prev], sb_ref.at[n_prev])

    @pl.when(is_latent)
    def _():
        _retention_kernel(True, lg_ref, rq_ref, rk_ref, rv_ref, rg_ref, pf_ref, qb_ref, o_ref)


def _retention(lg, rq, rk, rv, rg, s0f, s0b, unperm, prev_states):
    w = HEADS * RDK
    n_prev = 0 if prev_states is None else prev_states[0].shape[1]
    pf, qb = _ret_prefix(lg, rk, rv, s0f, s0b)
    tile = lambda i, lg: (i, 0)
    s0_blk = pl.BlockSpec((None, HEADS // 2, LANES, LANES), lambda i, lg: (jnp.maximum(i - TP // TQ, 0), 0, 0, 0))
    seq = lambda i, lg: (jnp.minimum(i, N_PROMPT_SEQ - 1), 0, 0, 0, 0)
    st_blk = pl.BlockSpec((None, n_prev + 1, HEADS, RDK, RDK), seq)
    st_shape = jax.ShapeDtypeStruct((N_PROMPT_SEQ, n_prev + 1, HEADS, RDK, RDK), F32)
    prev = () if prev_states is None else tuple(prev_states)
    prev_specs = [pl.BlockSpec((None, None, HEADS, RDK, RDK),
                               functools.partial(lambda k, i, lg: (jnp.minimum(i, N_PROMPT_SEQ - 1), k, 0, 0, 0), k))
                  for k in range(n_prev) for _ in range(2)]
    prev_args = [p for k in range(n_prev) for p in prev]
    return pl.pallas_call(
        functools.partial(_retention_tiles_kernel, n_prev),
        grid_spec=pltpu.PrefetchScalarGridSpec(
            num_scalar_prefetch=1, grid=(T // TQ,),
            in_specs=[pl.BlockSpec((TQ, w), tile)] * 4 + [s0_blk, s0_blk,
                      pl.BlockSpec((LANES, LANES), lambda i, lg: (0, 0))] + prev_specs,
            out_specs=[pl.BlockSpec((TQ, w), tile), st_blk, st_blk]),
        out_shape=[jax.ShapeDtypeStruct((T, w), BF16), st_shape, st_shape],
        compiler_params=_cparams(("arbitrary",)),
        name="retention",
    )(lg, rq, rk, rv, rg, pf, qb, unperm, *prev_args)


def _mix_ffn_kernel(n_x, *refs):
    x_refs, refs = refs[:n_x], refs[n_x:]
    a_ref, r_ref, mod_ref, wo_ref, wg_ref, wu_ref, wd_ref, ln_ref, o_ref = refs
    half = HEADS * VDIM
    y = _bdot(a_ref[...], wo_ref[0:half, :]) + _bdot(r_ref[...], wo_ref[half:, :])
    x1 = _layer_norm(ALPHA * _tile_rows(x_refs, TM_FF) + mod_ref[2:3, :] * y, ln_ref[0:1, :], ln_ref[1:2, :])
    h = (x1 * (1.0 + mod_ref[4:5, :]) + mod_ref[3:4, :]).astype(BF16)
    acc = None
    for f in range(D_FF // TF):
        cols = slice(f * TF, (f + 1) * TF)
        g = _bdot(h, wg_ref[:, cols])
        u = _bdot(h, wu_ref[:, cols])
        part = _bdot((g * _sigmoid(g) * u).astype(BF16), wd_ref[cols, :])
        acc = part if acc is None else acc + part
    o_ref[...] = _layer_norm(ALPHA * x1 + mod_ref[5:6, :] * acc, ln_ref[2:3, :], ln_ref[3:4, :])


def _mix_ffn(x, attn, ret, mod_l, w_out, wg, wu, wd, ln, j):
    x_args, x_specs = _tile_specs(x, TM_FF)
    tok = lambda w: pl.BlockSpec((TM_FF, w), lambda i: (i, 0))
    resident = lambda a: pl.BlockSpec((None,) + a.shape[1:], lambda i: (j, 0, 0), pipeline_mode=pl.Buffered(1))
    return pl.pallas_call(
        functools.partial(_mix_ffn_kernel, len(x_args)),
        grid=(T // TM_FF,),
        in_specs=x_specs + [tok(HEADS * VDIM), tok(HEADS * RDK),
                  pl.BlockSpec((None, 8, D), lambda i: (_group_of_tile(i, TM_FF), 0, 0)),
                  resident(w_out), resident(wg), resident(wu), resident(wd), pl.BlockSpec((8, D), lambda i: (0, 0))],
        out_specs=tok(D),
        out_shape=jax.ShapeDtypeStruct((T, D), F32),
        compiler_params=_cparams(("arbitrary",)),
        name="mix_ffn",
    )(*x_args, attn, ret, mod_l, w_out, wg, wu, wd, ln)


def _conv_in_kernel(x_ref, mod_ref, w_ref, b_ref, z_ref):
    h = (x_ref[...] * (1.0 + mod_ref[1:2, :]) + mod_ref[0:1, :]).astype(BF16)
    p = _bdot(h, w_ref[...])
    b_ref[...] = p[:, 0:D]
    z_ref[...] = p[:, D:2 * D] * p[:, 2 * D:3 * D]


def _conv_in(x, mod_l, w_in, j):
    tok = pl.BlockSpec((TM_FF, D), lambda i: (i, 0))
    return pl.pallas_call(
        _conv_in_kernel,
        grid=(T // TM_FF,),
        in_specs=[tok, pl.BlockSpec((None, 8, D), lambda i: (_group_of_tile(i, TM_FF), 0, 0)),
                  pl.BlockSpec((None,) + w_in.shape[1:], lambda i: (j, 0, 0))],
        out_specs=[tok, tok],
        out_shape=[jax.ShapeDtypeStruct((T, D), F32)] * 2,
        compiler_params=_cparams(("parallel",)),
        name="conv_in",
    )(x, mod_l, w_in)


def _conv_out_kernel(x_ref, b_ref, z_ref, zp_ref, zn_ref, mod_ref, cw_ref, w_ref, ln_ref, rw_ref, rb_ref,
                     o_ref, h_ref, route_ref):
    i = pl.program_id(0)
    z = z_ref[...]
    tm = z.shape[0]
    row = lax.broadcasted_iota(jnp.int32, (tm, 1), 0)
    seq_len = jnp.where(i < TP // tm, PROMPT_LEN, LAT_LEN)
    pos = (i * tm + row) & (seq_len - 1)
    prev = jnp.where(row == 0, zp_ref[7:8, :], pltpu.roll(z, 1, axis=0))
    prev = jnp.where(pos == 0, 0.0, prev)
    nxt = jnp.where(row == tm - 1, zn_ref[0:1, :], pltpu.roll(z, tm - 1, axis=0))
    nxt = jnp.where(pos == seq_len - 1, 0.0, nxt)
    y = prev * cw_ref[0:1, :] + z * cw_ref[1:2, :] + nxt * cw_ref[2:3, :]
    t = _bdot((b_ref[...] * y).astype(BF16), w_ref[...])
    x1 = _layer_norm(ALPHA * x_ref[...] + mod_ref[2:3, :] * t, ln_ref[0:1, :], ln_ref[1:2, :])
    o_ref[...] = x1
    h = x1 * (1.0 + mod_ref[4:5, :]) + mod_ref[3:4, :]
    h_hi = h.astype(BF16)
    h_ref[...] = h_hi
    h_lo = (h - h_hi.astype(F32)).astype(BF16)
    both = _bdot(h_hi, rw_ref[...])
    logits = both[:, :LANES] + both[:, LANES:] + _bdot(h_lo, rw_ref[:, :LANES]) + rb_ref[...]
    lane = lax.broadcasted_iota(jnp.int32, logits.shape, 1).astype(F32)
    t1 = jnp.max(logits, axis=-1, keepdims=True)
    i1 = jnp.min(jnp.where(logits == t1, lane, float(LANES)), axis=-1, keepdims=True)
    rest = jnp.where(lane == i1, -jnp.inf, logits)
    t2 = jnp.max(rest, axis=-1, keepdims=True)
    i2 = jnp.min(jnp.where(rest == t2, lane, float(LANES)), axis=-1, keepdims=True)
    e = jnp.exp(t2 - t1)
    den = 1.0 + e
    route_ref[...] = jnp.where(lane == 0.0, i1, jnp.where(lane == 1.0, i2,
                               jnp.where(lane == 2.0, 1.0 / den, jnp.where(lane == 3.0, e / den, 0.0))))


def _conv_out(x, b, z, mod_l, cw, w_out, ln, rw, rb, j):
    tok = pl.BlockSpec((TM_FF, D), lambda i: (i, 0))
    sub = TM_FF // 8
    return pl.pallas_call(
        _conv_out_kernel,
        grid=(T // TM_FF,),
        in_specs=[tok, tok, tok,
                  pl.BlockSpec((8, D), lambda i: (jnp.maximum(i * sub - 1, 0), 0)),
                  pl.BlockSpec((8, D), lambda i: (jnp.minimum((i + 1) * sub, T // 8 - 1), 0)),
                  pl.BlockSpec((None, 8, D), lambda i: (_group_of_tile(i, TM_FF), 0, 0)),
                  pl.BlockSpec((8, D), lambda i: (0, 0)), pl.BlockSpec((None, D, D), lambda i: (j, 0, 0)),
                  pl.BlockSpec((8, D), lambda i: (0, 0)),
                  pl.BlockSpec((D, 2 * LANES), lambda i: (0, 0)), pl.BlockSpec((1, LANES), lambda i: (0, 0))],
        out_specs=[tok, tok, pl.BlockSpec((TM_FF, LANES), lambda i: (i, 0))],
        out_shape=[jax.ShapeDtypeStruct((T, D), F32), jax.ShapeDtypeStruct((T, D), BF16),
                   jax.ShapeDtypeStruct((T, LANES), F32)],
        compiler_params=_cparams(("parallel",)),
        name="conv_out",
    )(x, b, z, z, z, mod_l, cw, w_out, ln, rw, rb)


def _moe_up_kernel(te_ref, nv_ref, x_ref, wg_ref, wu_ref, a_ref):
    @pl.when(pl.program_id(1) < nv_ref[0])
    def _():
        h = x_ref[...]
        g = _wdot(h, wg_ref[...])
        u = _wdot(h, wu_ref[...])
        a_ref[...] = (g * _sigmoid(g) * u).astype(BF16)

    @pl.when(pl.program_id(1) >= nv_ref[0])
    def _():
        a_ref[...] = jnp.zeros_like(a_ref)


def _moe_down_kernel(te_ref, nv_ref, a_ref, wd_ref, o_ref):
    @pl.when(pl.program_id(0) < nv_ref[0])
    def _():
        o_ref[...] = _wdot(a_ref[...], wd_ref[...])

    @pl.when(pl.program_id(0) >= nv_ref[0])
    def _():
        o_ref[...] = jnp.zeros_like(o_ref)


def _moe(tile_expert, n_valid, xs, wg, wu, wd, j):
    n_tiles = NP_ROWS // TM_FF
    act = pl.pallas_call(
        _moe_up_kernel,
        grid_spec=pltpu.PrefetchScalarGridSpec(
            num_scalar_prefetch=2, grid=(D_FF // TF, n_tiles),
            in_specs=[pl.BlockSpec((TM_FF, D), lambda f, i, te, nv: (i, 0)),
                      pl.BlockSpec((None, None, D, TF), lambda f, i, te, nv: (j, te[i], 0, f)),
                      pl.BlockSpec((None, None, D, TF), lambda f, i, te, nv: (j, te[i], 0, f))],
            out_specs=pl.BlockSpec((TM_FF, TF), lambda f, i, te, nv: (i, f))),
        out_shape=jax.ShapeDtypeStruct((NP_ROWS, D_FF), BF16),
        compiler_params=_cparams(("arbitrary", "arbitrary")),
        name="moe_up",
    )(tile_expert, n_valid, xs, wg, wu)
    return pl.pallas_call(
        _moe_down_kernel,
        grid_spec=pltpu.PrefetchScalarGridSpec(
            num_scalar_prefetch=2, grid=(n_tiles,),
            in_specs=[pl.BlockSpec((TM_FF, D_FF), lambda i, te, nv: (i, 0)),
                      pl.BlockSpec((None, None, D_FF, D), lambda i, te, nv: (j, te[i], 0, 0))],
            out_specs=pl.BlockSpec((TM_FF, D), lambda i, te, nv: (i, 0))),
        out_shape=jax.ShapeDtypeStruct((NP_ROWS, D), F32),
        compiler_params=_cparams(("arbitrary",)),
        name="moe_down",
    )(tile_expert, n_valid, act, wd)


def _combine_kernel(split, x_ref, o0_ref, o1_ref, route_ref, mod_ref, ln_ref, *o_refs):
    y = route_ref[:, 2:3] * o0_ref[...] + route_ref[:, 3:4] * o1_ref[...]
    out = _layer_norm(ALPHA * x_ref[...] + mod_ref[5:6, :] * y, ln_ref[2:3, :], ln_ref[3:4, :])
    if not split:
        o_refs[0][...] = out
        return
    is_latent = pl.program_id(0) >= TP // TM_FF

    @pl.when(jnp.logical_not(is_latent))
    def _():
        o_refs[0][...] = out

    @pl.when(is_latent)
    def _():
        o_refs[1][...] = out


def _combine(x, o0, o1, route, mod_l, ln, split):
    tok = pl.BlockSpec((TM_FF, D), lambda i: (i, 0))
    if split:
        out_specs = [pl.BlockSpec((TM_FF, D), lambda i: (jnp.minimum(i, TP // TM_FF - 1), 0)),
                     pl.BlockSpec((TM_FF, D), lambda i: (jnp.maximum(i - TP // TM_FF, 0), 0))]
        out_shape = [jax.ShapeDtypeStruct((TP, D), F32), jax.ShapeDtypeStruct((TS, D), F32)]
    else:
        out_specs, out_shape = tok, jax.ShapeDtypeStruct((T, D), F32)
    return pl.pallas_call(
        functools.partial(_combine_kernel, split),
        grid=(T // TM_FF,),
        in_specs=[tok, tok, tok, pl.BlockSpec((TM_FF, LANES), lambda i: (i, 0)),
                  pl.BlockSpec((None, 8, D), lambda i: (_group_of_tile(i, TM_FF), 0, 0)),
                  pl.BlockSpec((8, D), lambda i: (0, 0))],
        out_specs=out_specs,
        out_shape=out_shape,
        compiler_params=_cparams(("arbitrary",)),
        name="moe_combine",
    )(x, o0, o1, route, mod_l, ln)


def _routing_plan(route):
    e = jnp.concatenate([route[:, 0], route[:, 1]]).astype(jnp.int32)
    onehot = (e[:, None] == jnp.arange(N_EXP, dtype=jnp.int32)[None, :]).astype(jnp.int32)
    csum = jnp.cumsum(onehot, axis=0)
    counts = csum[-1]
    rank = jnp.sum((csum - onehot) * onehot, axis=1)
    padded = (counts + TM_FF - 1) // TM_FF * TM_FF
    pend = jnp.cumsum(padded)
    dest = jnp.sum(onehot * (pend - padded)[None, :], axis=1) + rank
    order = jnp.argsort(e, stable=True).astype(jnp.int32)
    rows = jnp.arange(NP_ROWS, dtype=jnp.int32)
    before = (rows[:, None] >= pend[None, :]).astype(jnp.int32)
    row_e = jnp.minimum(jnp.sum(before, axis=1), N_EXP - 1)
    row_cnt = jnp.sum((row_e[:, None] == jnp.arange(N_EXP, dtype=jnp.int32)[None, :]) * counts[None, :], axis=1)
    q = jnp.clip(rows - jnp.sum(before * padded[None, :], axis=1), 0, jnp.maximum(row_cnt - 1, 0))
    src = jnp.minimum(jnp.sum(before * counts[None, :], axis=1) + q, 2 * T - 1)
    row_token = order[src] % T
    n_valid = (pend[-1] // TM_FF).astype(jnp.int32)
    tile_start = jnp.minimum(jnp.arange(NP_ROWS // TM_FF, dtype=jnp.int32), n_valid - 1) * TM_FF
    tile_expert = jnp.minimum(jnp.sum((tile_start[:, None] >= pend[None, :]).astype(jnp.int32), axis=1), N_EXP - 1)
    return dest[:T], dest[T:], row_token, tile_expert.astype(jnp.int32), n_valid.reshape(1)


_INV_PERM_RDK = np.argsort(np.concatenate([np.arange(0, RDK, 2), np.arange(1, RDK, 2)]))


def _pad_cols(a, width):
    return jnp.pad(a, ((0, 0), (0, width - a.shape[1])))


def _deinterleave(a):
    n = a.shape[-1]
    return jnp.swapaxes(a.reshape(a.shape[:-1] + (n // 2, 2)), -1, -2).reshape(a.shape)


def _prep_even_weights(w_in, w_q_b, w_kv_b):
    o_kpe, o_rq, o_rk, o_rv = 640, 672, 1184, 1696
    heads = lambda a: _deinterleave(a.reshape(D, HEADS, RDK)).reshape(D, HEADS * RDK)
    kpe = w_in[:, o_kpe:o_kpe + ROPE]
    w_in_p = jnp.concatenate([
        w_in[:, :640], heads(w_in[:, o_rq:o_rk]), heads(w_in[:, o_rk:o_rv]) * (RDK ** -0.5), w_in[:, o_rv:],
        _pad_cols(kpe, LANES), _pad_cols(_deinterleave(kpe), LANES),
        _pad_cols(jnp.concatenate([-kpe[:, 1::2], kpe[:, 0::2]], axis=1), LANES)], axis=1).astype(BF16)
    wq = w_q_b.reshape(Q_LORA, HEADS, NOPE + ROPE)
    wq = jnp.concatenate([wq[:, :, :NOPE], _deinterleave(wq[:, :, NOPE:]),
                          jnp.zeros((Q_LORA, HEADS, LANES - NOPE - ROPE), F32)], axis=2)
    wq = wq.reshape(Q_LORA, HEADS * LANES).astype(BF16)
    wkv = w_kv_b.reshape(KV_LORA, HEADS, NOPE + VDIM)
    zero = jnp.zeros((KV_LORA, HEADS, LANES - NOPE), F32)
    wk = jnp.concatenate([wkv[:, :, :NOPE], zero], axis=2).reshape(KV_LORA, HEADS * LANES)
    wv = wkv[:, :, NOPE:].reshape(KV_LORA, HEADS // 2, 2, VDIM)
    zv = jnp.zeros((KV_LORA, HEADS // 2, VDIM), F32)
    wv = jnp.stack([jnp.concatenate([wv[:, :, 0], zv], axis=2), jnp.concatenate([zv, wv[:, :, 1]], axis=2)], axis=2)
    wkv_p = jnp.concatenate([wk, wv.reshape(KV_LORA, HEADS * LANES)], axis=1).astype(BF16)
    return w_in_p, wq, wkv_p


def _placement():
    ek = np.zeros((LANES, HEADS * LANES), np.float32)
    for h in range(HEADS):
        ek[np.arange(ROPE), h * LANES + NOPE + np.arange(ROPE)] = 1.0
    return jnp.asarray(ek, BF16)


def _rotary_tables():
    rows = LAT_LEN // GRID_W
    r, col = jnp.meshgrid(jnp.arange(rows, dtype=F32), jnp.arange(GRID_W, dtype=F32), indexing='ij')
    n_freq = ROPE // 4
    freqs = 1.0 / (10000.0 ** (jnp.arange(n_freq, dtype=F32) / n_freq))
    ang = jnp.concatenate([r.reshape(-1)[:, None] * freqs, col.reshape(-1)[:, None] * freqs], axis=-1)
    cos, sin = jnp.cos(ang), jnp.sin(ang)
    theta = 1.0 / (10000.0 ** jnp.linspace(0.0, 1.0, RDK // 2, dtype=F32))
    rang = jnp.arange(LAT_LEN, dtype=F32)[:, None] * theta
    rcos, rsin = jnp.cos(rang), jnp.sin(rang)
    one = lambda w: jnp.ones((LAT_LEN, w), F32)
    zero = lambda w: jnp.zeros((LAT_LEN, w), F32)
    lat = [jnp.concatenate([one(NOPE), cos, cos, one(LANES - NOPE - ROPE)], axis=1),
           jnp.concatenate([zero(NOPE), sin, sin, zero(LANES - NOPE - ROPE)], axis=1),
           jnp.concatenate([cos, cos, zero(LANES - ROPE)], axis=1),
           jnp.concatenate([sin, sin, zero(LANES - ROPE)], axis=1),
           jnp.concatenate([rcos] * 4, axis=1), jnp.concatenate([rsin] * 4, axis=1)]
    ident = [np.ones((TM, LANES), np.float32), np.zeros((TM, LANES), np.float32)]
    ident_k = np.concatenate([np.ones((TM, ROPE), np.float32), np.zeros((TM, LANES - ROPE), np.float32)], axis=1)
    ident = [ident[0], ident[1], ident_k, ident[1], ident[0], ident[1]]
    return [jnp.concatenate([l, jnp.asarray(c)], axis=0) for l, c in zip(lat, ident)]


def _block_diag_states(s0):
    s = jnp.swapaxes(_deinterleave(jnp.swapaxes(s0, -1, -2)), -1, -2)
    s = s.reshape(s0.shape[0], HEADS // 2, 2, RDK, RDK)
    z = jnp.zeros_like(s[:, :, 0])
    top = jnp.concatenate([s[:, :, 0], z], axis=-1)
    bot = jnp.concatenate([z, s[:, :, 1]], axis=-1)
    return jnp.concatenate([top, bot], axis=-2)


def _unpermute_matrix():
    m = np.zeros((LANES, LANES), np.float32)
    for blk in range(LANES // RDK):
        m[blk * RDK + np.arange(RDK), blk * RDK + _INV_PERM_RDK] = 1.0
    return jnp.asarray(m, BF16)


def kernel(x_prompt, x_sample, c, cache_ckv, cache_kpe, state_ret_fwd, state_ret_bwd, c_ctx, w_mod, b_mod, ln_g, ln_b, w_in_mix, q_a_gain, kv_a_gain, w_q_b, w_kv_b, ret_decay_fwd, ret_decay_bwd, w_out_mix, w_in_conv, conv_w, w_out_conv, ffn_gate, ffn_up, ffn_down, router_w, router_b, exp_gate, exp_up, exp_down):
    x = (x_prompt.reshape(TP, D), x_sample.reshape(TS, D))
    cond8 = jnp.concatenate([c_ctx[None], c, jnp.zeros((8 - 1 - N_LAT_SEQ, D), F32)], axis=0)
    mods = _modulation(cond8, w_mod, b_mod)
    mods = jnp.pad(mods.reshape(DEPTH, N_GROUPS, 6, D), ((0, 0), (0, 0), (0, 2), (0, 0)))
    ln = jnp.pad(jnp.concatenate([ln_g, ln_b], axis=1)[:, jnp.array([0, 2, 1, 3])], ((0, 0), (0, 4), (0, 0)))
    tabs = _rotary_tables()
    ek = _placement()
    unperm = _unpermute_matrix()
    bf = lambda a: a.astype(BF16)
    w_out_mix_b, w_in_conv_b, w_out_conv_b = bf(w_out_mix), bf(w_in_conv), bf(w_out_conv)
    ffn_b = (bf(ffn_gate), bf(ffn_up), bf(ffn_down))
    exp_b = (exp_gate, exp_up, exp_down)
    cache, states = None, None
    for layer in range(DEPTH):
        j = layer // 2
        mod_l, ln_l = mods[layer], ln[layer]
        if layer % 2 == 0:
            w_in_p, wq, wkv = _prep_even_weights(w_in_mix[j], w_q_b[j], w_kv_b[j])
            q, k, v, cache, rq, rk, rv, rg = _even_in(
                x, mod_l, w_in_p, q_a_gain[j][None], kv_a_gain[j][None], wq, wkv, ek, tabs, cache)
            kpe_c = _pad_cols(_deinterleave(cache_kpe[:, j]).reshape(N_LAT_SEQ * PAST, ROPE), LANES)
            kc, vc = _ctx_kv(cache_ckv[:, j].reshape(N_LAT_SEQ * PAST, KV_LORA), kpe_c, wkv, ek)
            attn = _attention(q, k, v, kc, vc)
            lg = LOG2E * jnp.concatenate([jax.nn.log_sigmoid(ret_decay_fwd[j].astype(F32)),
                                          jax.nn.log_sigmoid(ret_decay_bwd[j].astype(F32))])
            ret, sf, sb = _retention(lg, rq, rk, rv, rg, _block_diag_states(state_ret_fwd[:, j]),
                                     _block_diag_states(state_ret_bwd[:, j]), unperm, states)
            states = (sf, sb)
            x = _mix_ffn(x, attn, ret, mod_l, w_out_mix_b, *ffn_b, ln_l, j)
        else:
            b, z = _conv_in(x, mod_l, w_in_conv_b, j)
            cw = jnp.pad(conv_w[j], ((0, 5), (0, 0)))
            rw = _pad_cols(router_w[j], LANES)
            rw_hi = rw.astype(BF16)
            rw = jnp.concatenate([rw_hi, (rw - rw_hi.astype(F32)).astype(BF16)], axis=1)
            rb = jnp.concatenate([router_b[j].astype(F32), jnp.full((LANES - N_EXP,), -1e30, F32)])[None]
            x, h, route = _conv_out(x, b, z, mod_l, cw, w_out_conv_b, ln_l, rw, rb, j)
            dest0, dest1, row_token, tile_expert, n_valid = _routing_plan(route)
            out_sorted = _moe(tile_expert, n_valid, h[row_token], *exp_b, j)
            x = _combine(x, out_sorted[dest0], out_sorted[dest1], route, mod_l, ln_l, split=layer == DEPTH - 1)
    y_prompt = x[0].reshape(N_PROMPT_SEQ, PROMPT_LEN, D)
    y_sample = x[1].reshape(N_LAT_SEQ, LAT_LEN, D)
    return (y_prompt, y_sample, cache[0], cache[1], states[0], states[1])
```

```python
import functools

import numpy as np
import jax
import jax.numpy as jnp
from jax import lax
from jax.experimental import pallas as pl
from jax.experimental.pallas import tpu as pltpu

F32 = jnp.float32
BF16 = jnp.bfloat16

D = 1024
DEPTH = 4
N_PROMPT_SEQ, PROMPT_LEN = 32, 256
N_LAT_SEQ, LAT_LEN = 2, 2048
PAST = 512
GRID_W = 64
TP = N_PROMPT_SEQ * PROMPT_LEN
TS = N_LAT_SEQ * LAT_LEN
T = TP + TS
HEADS = 8
NOPE, ROPE, VDIM = 64, 32, 64
Q_LORA, KV_LORA = 384, 256
RDK = 64
D_FF = 2816
N_EXP = 8
ALPHA = (2.0 * DEPTH) ** 0.25
LOG2E = float(np.log2(np.e))
Q_SCALE = float((NOPE + ROPE) ** -0.5) * LOG2E
LANES = 128
N_GROUPS = 8

TM = 512
TM_FF = 512
TF = D_FF // 2
TQ = 256
NP_ROWS = 2 * T + N_EXP * TM_FF
VMEM_LIMIT = 56 * 1024 * 1024

RET_W = HEADS * RDK
COL_KV = Q_LORA
COL_RQ = COL_KV + KV_LORA
COL_RK, COL_RV, COL_RG = COL_RQ + RET_W, COL_RQ + 2 * RET_W, COL_RQ + 3 * RET_W
COL_KPE = COL_RQ + 4 * RET_W
IN_COLS = COL_KPE + 3 * LANES


def _cparams(sem):
    return pltpu.CompilerParams(dimension_semantics=sem, vmem_limit_bytes=VMEM_LIMIT)


def _group_of_tile(i, tm):
    per_seq = LAT_LEN // tm
    return jnp.maximum(i - TP // tm + per_seq, 0) // per_seq


def _bdot(a, b):
    return jnp.dot(a, b, preferred_element_type=F32)


def _wdot(a, w):
    return lax.dot_general(a, w, (((1,), (0,)), ((), ())), preferred_element_type=F32)


def _sigmoid(v):
    return 1.0 / (1.0 + jnp.exp(-v))


def _layer_norm(v, g, b):
    mu = jnp.mean(v, axis=-1, keepdims=True)
    d = v - mu
    var = jnp.mean(d * d, axis=-1, keepdims=True)
    return d * lax.rsqrt(var + 1e-5) * g + b


def _rms(v, g):
    return v * lax.rsqrt(jnp.mean(v * v, axis=-1, keepdims=True) + 1e-6) * g


def _mod_kernel(c_ref, w_ref, b_ref, o_ref):
    c = c_ref[...]
    s = (c * _sigmoid(c)).astype(BF16)
    o_ref[...] = _bdot(s, w_ref[...].astype(BF16)) + b_ref[...]


def _modulation(cond8, w_mod, b_mod):
    tn = 1536
    return pl.pallas_call(
        _mod_kernel,
        grid=(DEPTH, 6 * D // tn),
        in_specs=[pl.BlockSpec((8, D), lambda l, n: (0, 0)),
                  pl.BlockSpec((None, D, tn), lambda l, n: (l, 0, n)),
                  pl.BlockSpec((None, 1, tn), lambda l, n: (l, 0, n))],
        out_specs=pl.BlockSpec((None, 8, tn), lambda l, n: (l, 0, n)),
        out_shape=jax.ShapeDtypeStruct((DEPTH, 8, 6 * D), F32),
        compiler_params=_cparams(("arbitrary", "arbitrary")),
        name="modulation",
    )(cond8, w_mod, b_mod.reshape(DEPTH, 1, 6 * D))


def _swap_halves(a, half):
    n = a.shape[-1]
    lane = lax.broadcasted_iota(jnp.int32, a.shape, 1)
    first = (lane & (2 * half - 1)) < half
    return jnp.where(first, -pltpu.roll(a, n - half, axis=1), pltpu.roll(a, half, axis=1))


def _tile_rows(x_refs, tm):
    if len(x_refs) == 1:
        return x_refs[0][...]
    return jnp.where(pl.program_id(0) >= TP // tm, x_refs[1][...], x_refs[0][...])


def _tile_specs(x, tm):
    if not isinstance(x, tuple):
        return (x,), [pl.BlockSpec((tm, D), lambda i, *_: (i, 0))]
    return x, [pl.BlockSpec((tm, D), lambda i, *_: (jnp.minimum(i, TP // tm - 1), 0)),
               pl.BlockSpec((tm, D), lambda i, *_: (jnp.maximum(i - TP // tm, 0), 0))]


def _even_in_kernel(n_x, n_prev, *refs):
    x_refs, refs = refs[:n_x], refs[n_x:]
    (mod_ref, w_in_ref, qg_ref, kvg_ref, wq_ref, wkv_ref, ek_ref,
     cq_ref, sq_ref, ck_ref, sk_ref, cr_ref, sr_ref), refs = refs[:13], refs[13:]
    prev_refs, refs = refs[:2 * n_prev], refs[2 * n_prev:]
    q_ref, k_ref, v_ref, ckv_ref, kpe_ref, rq_ref, rk_ref, rv_ref, rg_ref = refs
    x = _tile_rows(x_refs, TM)
    h = (x * (1.0 + mod_ref[1:2, :]) + mod_ref[0:1, :]).astype(BF16)
    p = _bdot(h, w_in_ref[...])
    qn = _rms(p[:, 0:Q_LORA], qg_ref[...]).astype(BF16)
    qa = _bdot(qn, wq_ref[...])
    ckv = _rms(p[:, COL_KV:COL_RQ], kvg_ref[...])
    kv = _bdot(ckv.astype(BF16), wkv_ref[...])
    v_ref[...] = kv[:, HEADS * LANES:].astype(BF16)
    base = COL_KPE
    ka = p[:, base + LANES:base + 2 * LANES]
    kb = p[:, base + 2 * LANES:base + 3 * LANES]
    rq = p[:, COL_RQ:COL_RK]
    rk = p[:, COL_RK:COL_RV]
    rv_ref[...] = p[:, COL_RV:COL_RG].astype(BF16)
    rg_ref[...] = p[:, COL_RG:COL_KPE]
    lane = lax.broadcasted_iota(jnp.int32, qa.shape, 1) & (LANES - 1)
    qb = jnp.where(lane < NOPE + ROPE // 2,
                   -pltpu.roll(qa, qa.shape[1] - ROPE // 2, axis=1),
                   pltpu.roll(qa, ROPE // 2, axis=1))
    cq = jnp.concatenate([cq_ref[...]] * HEADS, axis=1)
    sq = jnp.concatenate([sq_ref[...]] * HEADS, axis=1)
    q_ref[...] = ((qa * cq + qb * sq) * Q_SCALE).astype(BF16)
    kpe_rot = ka * ck_ref[...] + kb * sk_ref[...]
    k_ref[...] = (kv[:, :HEADS * LANES] + _bdot(kpe_rot.astype(BF16), ek_ref[...])).astype(BF16)
    cr = jnp.concatenate([cr_ref[...]] * (RET_W // LANES), axis=1)
    sr = jnp.concatenate([sr_ref[...]] * (RET_W // LANES), axis=1)
    rq_ref[...] = (rq * cr + _swap_halves(rq, RDK // 2) * sr).astype(BF16)
    rk_ref[...] = (rk * cr + _swap_halves(rk, RDK // 2) * sr).astype(BF16)

    @pl.when(pl.program_id(0) < TP // TM)
    def _():
        for k in range(n_prev):
            ckv_ref[:, k] = prev_refs[2 * k][...]
            kpe_ref[:, k] = prev_refs[2 * k + 1][...]
        for s in range(TM // PROMPT_LEN):
            rows = slice(s * PROMPT_LEN, (s + 1) * PROMPT_LEN)
            ckv_ref[s, n_prev] = ckv[rows]
            kpe_ref[s, n_prev] = p[rows, base:base + ROPE]


def _even_in(x, mod_l, w_in, qg, kvg, wq, wkv, ek, tabs, prev_cache):
    x_args, x_specs = _tile_specs(x, TM)
    n_prev = 0 if prev_cache is None else prev_cache[0].shape[1]
    spt = TM // PROMPT_LEN
    tok = lambda w: pl.BlockSpec((TM, w), lambda i: (i, 0))
    full = lambda a: pl.BlockSpec(a.shape, lambda i: (0,) * a.ndim, pipeline_mode=pl.Buffered(1))
    lat_tiles = LAT_LEN // TM
    tab = pl.BlockSpec((TM, LANES), lambda i: (
        jnp.where(i < TP // TM, lat_tiles, jnp.maximum(i - TP // TM, 0) % lat_tiles), 0))
    seq = lambda i: (jnp.minimum(i, TP // TM - 1), 0, 0, 0)
    seq_k = lambda k, i: (jnp.minimum(i, TP // TM - 1), k, 0, 0)
    prev_args, prev_specs = [], []
    for k in range(n_prev):
        for a, w in zip(prev_cache, (KV_LORA, ROPE)):
            prev_args.append(a)
            prev_specs.append(pl.BlockSpec((spt, None, PROMPT_LEN, w), functools.partial(seq_k, k)))
    tok_outs = lambda dims: ([tok(w) for w, _ in dims], [jax.ShapeDtypeStruct((T, w), dt) for w, dt in dims])
    qkv_specs, qkv_shapes = tok_outs([(HEADS * LANES, BF16)] * 3)
    ret_specs, ret_shapes = tok_outs([(RET_W, BF16), (RET_W, BF16), (RET_W, BF16), (RET_W, F32)])
    cache_specs = [pl.BlockSpec((spt, n_prev + 1, PROMPT_LEN, w), seq) for w in (KV_LORA, ROPE)]
    cache_shapes = [jax.ShapeDtypeStruct((N_PROMPT_SEQ, n_prev + 1, PROMPT_LEN, w), F32) for w in (KV_LORA, ROPE)]
    q, k, v, ckv, kpe, rq, rk, rv, rg = pl.pallas_call(
        functools.partial(_even_in_kernel, len(x_args), n_prev),
        grid=(T // TM,),
        in_specs=x_specs + [pl.BlockSpec((None, 8, D), lambda i: (_group_of_tile(i, TM), 0, 0)),
                            full(w_in), full(qg), full(kvg), full(wq), full(wkv), full(ek)] + [tab] * 6 + prev_specs,
        out_specs=qkv_specs + cache_specs + ret_specs,
        out_shape=qkv_shapes + cache_shapes + ret_shapes,
        compiler_params=_cparams(("arbitrary",)),
        name="even_in",
    )(*x_args, mod_l, w_in, qg, kvg, wq, wkv, ek, *tabs, *prev_args)
    return q, k, v, (ckv, kpe), rq, rk, rv, rg


def _ctx_kv_kernel(ckv_ref, kpe_ref, wkv_ref, ek_ref, k_ref, v_ref):
    kv = _bdot(ckv_ref[...].astype(BF16), wkv_ref[...])
    k_ref[...] = (kv[:, :HEADS * LANES] + _bdot(kpe_ref[...].astype(BF16), ek_ref[...])).astype(BF16)
    v_ref[...] = kv[:, HEADS * LANES:].astype(BF16)


def _ctx_kv(ckv_c, kpe_c, wkv, ek):
    n = ckv_c.shape[0]
    full = lambda a: pl.BlockSpec(a.shape, lambda i: (0,) * a.ndim)
    return pl.pallas_call(
        _ctx_kv_kernel,
        grid=(n // PAST,),
        in_specs=[pl.BlockSpec((PAST, KV_LORA), lambda i: (i, 0)), pl.BlockSpec((PAST, LANES), lambda i: (i, 0)),
                  full(wkv), full(ek)],
        out_specs=[pl.BlockSpec((PAST, HEADS * LANES), lambda i: (i, 0))] * 2,
        out_shape=[jax.ShapeDtypeStruct((n, HEADS * LANES), BF16)] * 2,
        compiler_params=_cparams(("parallel",)),
        name="ctx_kv",
    )(ckv_c, kpe_c, wkv, ek)


def _attn_kernel(n_kv, q_ref, *refs):
    k_refs = refs[0:2 * n_kv:2]
    v_refs = refs[1:2 * n_kv:2]
    o_ref = refs[2 * n_kv]
    nt = (((1,), (1,)), ((), ()))
    for pair in range(HEADS // 2):
        acc = None
        for sub in range(2):
            sl = slice((2 * pair + sub) * LANES, (2 * pair + sub + 1) * LANES)
            qh = q_ref[:, sl]
            s = [lax.dot_general(qh, k[:, sl], nt, preferred_element_type=F32) for k in k_refs]
            m = functools.reduce(jnp.maximum, [jnp.max(a, axis=-1, keepdims=True) for a in s])
            e = [jnp.exp2(a - m) for a in s]
            den = functools.reduce(jnp.add, [jnp.sum(a, axis=-1, keepdims=True) for a in e])
            o = functools.reduce(jnp.add, [_bdot(a.astype(BF16), v[:, sl]) for a, v in zip(e, v_refs)])
            o = o / den
            acc = o if acc is None else acc + o
        o_ref[:, pair * LANES:(pair + 1) * LANES] = acc.astype(BF16)


def _latent_seq(i):
    return jnp.maximum(i - TP // TQ, 0) // (LAT_LEN // TQ)


def _attn_tiles_kernel(q_ref, kp_ref, vp_ref, kc_ref, vc_ref, kl_ref, vl_ref, o_ref):
    is_latent = pl.program_id(0) >= TP // TQ

    @pl.when(jnp.logical_not(is_latent))
    def _():
        _attn_kernel(1, q_ref, kp_ref, vp_ref, o_ref)

    @pl.when(is_latent)
    def _():
        _attn_kernel(2, q_ref, kc_ref, vc_ref, kl_ref, vl_ref, o_ref)


def _attention(q, k, v, kc, vc):
    w = HEADS * LANES
    tile = lambda i: (i, 0)
    ctx_own = lambda i: (jnp.minimum(i, N_PROMPT_SEQ - 1), 0)
    cache = lambda i: (_latent_seq(i), 0)
    lat_own = lambda i: (TP // LAT_LEN + _latent_seq(i), 0)
    return pl.pallas_call(
        _attn_tiles_kernel,
        grid=(T // TQ,),
        in_specs=[pl.BlockSpec((TQ, w), tile),
                  pl.BlockSpec((PROMPT_LEN, w), ctx_own), pl.BlockSpec((PROMPT_LEN, w), ctx_own),
                  pl.BlockSpec((PAST, w), cache), pl.BlockSpec((PAST, w), cache),
                  pl.BlockSpec((LAT_LEN, w), lat_own), pl.BlockSpec((LAT_LEN, w), lat_own)],
        out_specs=pl.BlockSpec((TQ, HEADS * VDIM), tile),
        out_shape=jax.ShapeDtypeStruct((T, HEADS * VDIM), BF16),
        compiler_params=_cparams(("arbitrary",)),
        name="attention",
    )(q, k, v, kc, vc, k, v)


def _ret_prefix_kernel(lg_ref, rk_ref, rv_ref, s0f_ref, s0b_ref, pf_ref, qb_ref, sf_scr, sb_scr):
    s = pl.program_id(0)
    n_tiles, per_seq = TS // TQ, LAT_LEN // TQ
    row = lax.broadcasted_iota(jnp.int32, (LANES, 1), 0)
    lane = lax.broadcasted_iota(jnp.int32, (1, LANES), 1)
    top, lo = row < RDK, lane < RDK
    same_head = top == lo
    m_col = lax.broadcasted_iota(jnp.int32, (TQ, 1), 0).astype(F32)

    def scan_step(first, lg_off, s0_ref, scr, out_ref, pos):
        @pl.when(first)
        def _():
            scr[...] = s0_ref[...]

        for pair in range(HEADS // 2):
            sl = slice(pair * LANES, (pair + 1) * LANES)
            lg_even, lg_odd = lg_ref[lg_off + 2 * pair], lg_ref[lg_off + 2 * pair + 1]
            dec = jnp.exp2(pos * jnp.where(lo, lg_even, lg_odd))
            local = _bdot((rk_ref[:, sl].astype(F32) * dec).T.astype(BF16), rv_ref[:, sl])
            out_ref[pair] = scr[pair].astype(BF16)
            tile_decay = jnp.exp2(float(TQ) * jnp.where(top, lg_even, lg_odd))
            scr[pair] = scr[pair] * tile_decay + jnp.where(same_head, local, 0.0)

    @pl.when(s < n_tiles)
    def _():
        scan_step(s % per_seq == 0, 0, s0f_ref, sf_scr, pf_ref, TQ - 1.0 - m_col)

    @pl.when(s >= n_tiles)
    def _():
        scan_step((2 * n_tiles - 1 - s) % per_seq == per_seq - 1, HEADS, s0b_ref, sb_scr, qb_ref, m_col)


def _ret_prefix(lg, rk, rv, s0f, s0b):
    n_tiles, per_seq = TS // TQ, LAT_LEN // TQ
    tile_of = lambda s: jnp.where(s < n_tiles, s, 2 * n_tiles - 1 - s)
    st = pl.BlockSpec((None, HEADS // 2, LANES, LANES), lambda s, lg: (tile_of(s) // per_seq, 0, 0, 0))
    kv = pl.BlockSpec((TQ, HEADS * RDK), lambda s, lg: (TP // TQ + tile_of(s), 0))
    pf_blk = pl.BlockSpec((None, HEADS // 2, LANES, LANES), lambda s, lg: (jnp.minimum(s, n_tiles - 1), 0, 0, 0))
    qb_blk = pl.BlockSpec((None, HEADS // 2, LANES, LANES),
                          lambda s, lg: (jnp.minimum(2 * n_tiles - 1 - s, n_tiles - 1), 0, 0, 0))
    shape = jax.ShapeDtypeStruct((n_tiles, HEADS // 2, LANES, LANES), BF16)
    return pl.pallas_call(
        _ret_prefix_kernel,
        grid_spec=pltpu.PrefetchScalarGridSpec(
            num_scalar_prefetch=1, grid=(2 * n_tiles,),
            in_specs=[kv, kv, st, st], out_specs=[pf_blk, qb_blk],
            scratch_shapes=[pltpu.VMEM((HEADS // 2, LANES, LANES), F32)] * 2),
        out_shape=[shape, shape],
        compiler_params=_cparams(("arbitrary",)),
        name="ret_prefix",
    )(lg, rk, rv, s0f, s0b)


def _retention_kernel(latent, lg_ref, rq_ref, rk_ref, rv_ref, rg_ref, *refs):
    q0 = 0
    seq_len = rq_ref.shape[0]
    if latent:
        s0f_ref, s0b_ref, o_ref = refs
    else:
        unperm_ref, o_ref, sf_ref, sb_ref = refs
    tq, tk = rq_ref.shape[0], rk_ref.shape[0]
    nt = (((1,), (1,)), ((), ()))
    n_idx = (q0 + lax.broadcasted_iota(jnp.int32, (tq, tk), 0)).astype(F32)
    m_idx = lax.broadcasted_iota(jnp.int32, (tq, tk), 1).astype(F32)
    dist = n_idx - m_idx
    adist = jnp.abs(dist)
    fwd = dist > 0.0
    diag = jnp.where(dist == 0.0, 1.0, 0.0)
    lane = lax.broadcasted_iota(jnp.int32, (1, LANES), 1)
    lo = lane < RDK
    n_col = (q0 + lax.broadcasted_iota(jnp.int32, (tq, 1), 0)).astype(F32)
    m_col = lax.broadcasted_iota(jnp.int32, (tk, 1), 0).astype(F32)
    for pair in range(HEADS // 2):
        sl = slice(pair * LANES, (pair + 1) * LANES)
        qb, kb, vb = rq_ref[:, sl], rk_ref[:, sl], rv_ref[:, sl]
        acc = jnp.zeros((tq, LANES), F32)
        for sub in range(2):
            h = 2 * pair + sub
            lgf, lgb = lg_ref[h], lg_ref[HEADS + h]
            half = lo if sub == 0 else jnp.logical_not(lo)
            qm = jnp.where(half, qb, jnp.zeros_like(qb))
            vm = jnp.where(half, vb, jnp.zeros_like(vb))
            s = lax.dot_general(qm, kb, nt, preferred_element_type=F32)
            w = jnp.exp2(adist * jnp.where(fwd, lgf, lgb)) + diag
            acc = acc + _bdot((s * w).astype(BF16), vm)
        lgf_l = jnp.where(lo, lg_ref[2 * pair], lg_ref[2 * pair + 1])
        lgb_l = jnp.where(lo, lg_ref[HEADS + 2 * pair], lg_ref[HEADS + 2 * pair + 1])
        if latent:
            acc = acc + _bdot(qb, s0f_ref[pair]) * jnp.exp2((n_col + 1.0) * lgf_l)
            acc = acc + _bdot(qb, s0b_ref[pair]) * jnp.exp2((seq_len - n_col) * lgb_l)
        else:
            v_swapped = pltpu.roll(vb.astype(F32), RDK, axis=1).astype(BF16)
            for st_ref, dec in ((sf_ref, jnp.exp2((seq_len - 1.0 - m_col) * lgf_l)),
                                (sb_ref, jnp.exp2(m_col * lgb_l))):
                kt = (kb.astype(F32) * dec).T.astype(BF16)
                kt = _bdot(unperm_ref[...], kt).astype(BF16)
                st_ref[2 * pair] = _bdot(kt, vb)[0:RDK, 0:RDK]
                st_ref[2 * pair + 1] = _bdot(kt, v_swapped)[RDK:, 0:RDK]
        inv = 1.0 / RDK
        mu = jnp.where(lo, jnp.sum(jnp.where(lo, acc, 0.0), axis=-1, keepdims=True),
                       jnp.sum(jnp.where(lo, 0.0, acc), axis=-1, keepdims=True)) * inv
        dlt = acc - mu
        d2 = dlt * dlt
        var = jnp.where(lo, jnp.sum(jnp.where(lo, d2, 0.0), axis=-1, keepdims=True),
                        jnp.sum(jnp.where(lo, 0.0, d2), axis=-1, keepdims=True)) * inv
        g = rg_ref[:, sl]
        o_ref[:, sl] = (dlt * lax.rsqrt(var + 1e-5) * (g * _sigmoid(g))).astype(BF16)


def _retention_tiles_kernel(n_prev, lg_ref, rq_ref, rk_ref, rv_ref, rg_ref, pf_ref, qb_ref, unperm_ref, *refs):
    prev_refs, (o_ref, sf_ref, sb_ref) = refs[:2 * n_prev], refs[2 * n_prev:]
    is_latent = pl.program_id(0) >= TP // TQ

    @pl.when(jnp.logical_not(is_latent))
    def _():
        for k in range(n_prev):
            sf_ref[k] = prev_refs[2 * k][...]
            sb_ref[k] = prev_refs[2 * k + 1][...]
        _retention_kernel(False, lg_ref, rq_ref, rk_ref, rv_ref, rg_ref, unperm_ref, o_ref,
                          sf_ref.at[n_prev], sb_ref.at[n_prev])

    @pl.when(is_latent)
    def _():
        _retention_kernel(True, lg_ref, rq_ref, rk_ref, rv_ref, rg_ref, pf_ref, qb_ref, o_ref)


def _retention(lg, rq, rk, rv, rg, s0f, s0b, unperm, prev_states):
    w = HEADS * RDK
    n_prev = 0 if prev_states is None else prev_states[0].shape[1]
    pf, qb = _ret_prefix(lg, rk, rv, s0f, s0b)
    tile = lambda i, lg: (i, 0)
    s0_blk = pl.BlockSpec((None, HEADS // 2, LANES, LANES), lambda i, lg: (jnp.maximum(i - TP // TQ, 0), 0, 0, 0))
    seq = lambda i, lg: (jnp.minimum(i, N_PROMPT_SEQ - 1), 0, 0, 0, 0)
    st_blk = pl.BlockSpec((None, n_prev + 1, HEADS, RDK, RDK), seq)
    st_shape = jax.ShapeDtypeStruct((N_PROMPT_SEQ, n_prev + 1, HEADS, RDK, RDK), F32)
    prev = () if prev_states is None else tuple(prev_states)
    prev_specs = [pl.BlockSpec((None, None, HEADS, RDK, RDK),
                               functools.partial(lambda k, i, lg: (jnp.minimum(i, N_PROMPT_SEQ - 1), k, 0, 0, 0), k))
                  for k in range(n_prev) for _ in range(2)]
    prev_args = [p for k in range(n_prev) for p in prev]
    return pl.pallas_call(
        functools.partial(_retention_tiles_kernel, n_prev),
        grid_spec=pltpu.PrefetchScalarGridSpec(
            num_scalar_prefetch=1, grid=(T // TQ,),
            in_specs=[pl.BlockSpec((TQ, w), tile)] * 4 + [s0_blk, s0_blk,
                      pl.BlockSpec((LANES, LANES), lambda i, lg: (0, 0))] + prev_specs,
            out_specs=[pl.BlockSpec((TQ, w), tile), st_blk, st_blk]),
        out_shape=[jax.ShapeDtypeStruct((T, w), BF16), st_shape, st_shape],
        compiler_params=_cparams(("arbitrary",)),
        name="retention",
    )(lg, rq, rk, rv, rg, pf, qb, unperm, *prev_args)


def _mix_ffn_kernel(n_x, *refs):
    x_refs, refs = refs[:n_x], refs[n_x:]
    a_ref, r_ref, mod_ref, wo_ref, wg_ref, wu_ref, wd_ref, ln_ref, o_ref = refs
    half = HEADS * VDIM
    y = _bdot(a_ref[...], wo_ref[0:half, :]) + _bdot(r_ref[...], wo_ref[half:, :])
    x1 = _layer_norm(ALPHA * _tile_rows(x_refs, TM_FF) + mod_ref[2:3, :] * y, ln_ref[0:1, :], ln_ref[1:2, :])
    h = (x1 * (1.0 + mod_ref[4:5, :]) + mod_ref[3:4, :]).astype(BF16)
    acc = None
    for f in range(D_FF // TF):
        cols = slice(f * TF, (f + 1) * TF)
        g = _bdot(h, wg_ref[:, cols])
        u = _bdot(h, wu_ref[:, cols])
        part = _bdot((g * _sigmoid(g) * u).astype(BF16), wd_ref[cols, :])
        acc = part if acc is None else acc + part
    o_ref[...] = _layer_norm(ALPHA * x1 + mod_ref[5:6, :] * acc, ln_ref[2:3, :], ln_ref[3:4, :])


def _mix_ffn(x, attn, ret, mod_l, w_out, wg, wu, wd, ln, j):
    x_args, x_specs = _tile_specs(x, TM_FF)
    tok = lambda w: pl.BlockSpec((TM_FF, w), lambda i: (i, 0))
    resident = lambda a: pl.BlockSpec((None,) + a.shape[1:], lambda i: (j, 0, 0), pipeline_mode=pl.Buffered(1))
    return pl.pallas_call(
        functools.partial(_mix_ffn_kernel, len(x_args)),
        grid=(T // TM_FF,),
        in_specs=x_specs + [tok(HEADS * VDIM), tok(HEADS * RDK),
                  pl.BlockSpec((None, 8, D), lambda i: (_group_of_tile(i, TM_FF), 0, 0)),
                  resident(w_out), resident(wg), resident(wu), resident(wd), pl.BlockSpec((8, D), lambda i: (0, 0))],
        out_specs=tok(D),
        out_shape=jax.ShapeDtypeStruct((T, D), F32),
        compiler_params=_cparams(("arbitrary",)),
        name="mix_ffn",
    )(*x_args, attn, ret, mod_l, w_out, wg, wu, wd, ln)


def _conv_in_kernel(x_ref, mod_ref, w_ref, b_ref, z_ref):
    h = (x_ref[...] * (1.0 + mod_ref[1:2, :]) + mod_ref[0:1, :]).astype(BF16)
    p = _bdot(h, w_ref[...])
    b_ref[...] = p[:, 0:D]
    z_ref[...] = p[:, D:2 * D] * p[:, 2 * D:3 * D]


def _conv_in(x, mod_l, w_in, j):
    tok = pl.BlockSpec((TM_FF, D), lambda i: (i, 0))
    return pl.pallas_call(
        _conv_in_kernel,
        grid=(T // TM_FF,),
        in_specs=[tok, pl.BlockSpec((None, 8, D), lambda i: (_group_of_tile(i, TM_FF), 0, 0)),
                  pl.BlockSpec((None,) + w_in.shape[1:], lambda i: (j, 0, 0))],
        out_specs=[tok, tok],
        out_shape=[jax.ShapeDtypeStruct((T, D), F32)] * 2,
        compiler_params=_cparams(("parallel",)),
        name="conv_in",
    )(x, mod_l, w_in)


def _conv_out_kernel(x_ref, b_ref, z_ref, zp_ref, zn_ref, mod_ref, cw_ref, w_ref, ln_ref, rw_ref, rb_ref,
                     o_ref, h_ref, route_ref):
    i = pl.program_id(0)
    z = z_ref[...]
    tm = z.shape[0]
    row = lax.broadcasted_iota(jnp.int32, (tm, 1), 0)
    seq_len = jnp.where(i < TP // tm, PROMPT_LEN, LAT_LEN)
    pos = (i * tm + row) & (seq_len - 1)
    prev = jnp.where(row == 0, zp_ref[7:8, :], pltpu.roll(z, 1, axis=0))
    prev = jnp.where(pos == 0, 0.0, prev)
    nxt = jnp.where(row == tm - 1, zn_ref[0:1, :], pltpu.roll(z, tm - 1, axis=0))
    nxt = jnp.where(pos == seq_len - 1, 0.0, nxt)
    y = prev * cw_ref[0:1, :] + z * cw_ref[1:2, :] + nxt * cw_ref[2:3, :]
    t = _bdot((b_ref[...] * y).astype(BF16), w_ref[...])
    x1 = _layer_norm(ALPHA * x_ref[...] + mod_ref[2:3, :] * t, ln_ref[0:1, :], ln_ref[1:2, :])
    o_ref[...] = x1
    h = x1 * (1.0 + mod_ref[4:5, :]) + mod_ref[3:4, :]
    h_hi = h.astype(BF16)
    h_ref[...] = h_hi
    h_lo = (h - h_hi.astype(F32)).astype(BF16)
    both = _bdot(h_hi, rw_ref[...])
    logits = both[:, :LANES] + both[:, LANES:] + _bdot(h_lo, rw_ref[:, :LANES]) + rb_ref[...]
    lane = lax.broadcasted_iota(jnp.int32, logits.shape, 1).astype(F32)
    t1 = jnp.max(logits, axis=-1, keepdims=True)
    i1 = jnp.min(jnp.where(logits == t1, lane, float(LANES)), axis=-1, keepdims=True)
    rest = jnp.where(lane == i1, -jnp.inf, logits)
    t2 = jnp.max(rest, axis=-1, keepdims=True)
    i2 = jnp.min(jnp.where(rest == t2, lane, float(LANES)), axis=-1, keepdims=True)
    e = jnp.exp(t2 - t1)
    den = 1.0 + e
    route_ref[...] = jnp.where(lane == 0.0, i1, jnp.where(lane == 1.0, i2,
                               jnp.where(lane == 2.0, 1.0 / den, jnp.where(lane == 3.0, e / den, 0.0))))


def _conv_out(x, b, z, mod_l, cw, w_out, ln, rw, rb, j):
    tok = pl.BlockSpec((TM_FF, D), lambda i: (i, 0))
    sub = TM_FF // 8
    return pl.pallas_call(
        _conv_out_kernel,
        grid=(T // TM_FF,),
        in_specs=[tok, tok, tok,
                  pl.BlockSpec((8, D), lambda i: (jnp.maximum(i * sub - 1, 0), 0)),
                  pl.BlockSpec((8, D), lambda i: (jnp.minimum((i + 1) * sub, T // 8 - 1), 0)),
                  pl.BlockSpec((None, 8, D), lambda i: (_group_of_tile(i, TM_FF), 0, 0)),
                  pl.BlockSpec((8, D), lambda i: (0, 0)), pl.BlockSpec((None, D, D), lambda i: (j, 0, 0)),
                  pl.BlockSpec((8, D), lambda i: (0, 0)),
                  pl.BlockSpec((D, 2 * LANES), lambda i: (0, 0)), pl.BlockSpec((1, LANES), lambda i: (0, 0))],
        out_specs=[tok, tok, pl.BlockSpec((TM_FF, LANES), lambda i: (i, 0))],
        out_shape=[jax.ShapeDtypeStruct((T, D), F32), jax.ShapeDtypeStruct((T, D), BF16),
                   jax.ShapeDtypeStruct((T, LANES), F32)],
        compiler_params=_cparams(("parallel",)),
        name="conv_out",
    )(x, b, z, z, z, mod_l, cw, w_out, ln, rw, rb)


def _by_fill(rows, o_ref, compute):
    half = TM_FF // 2

    @pl.when(rows > half)
    def _():
        o_ref[...] = compute(slice(None))

    @pl.when(jnp.logical_and(rows > 0, rows <= half))
    def _():
        o_ref[0:half, :] = compute(slice(0, half))
        o_ref[half:, :] = jnp.zeros((TM_FF - half, o_ref.shape[1]), o_ref.dtype)

    @pl.when(rows == 0)
    def _():
        o_ref[...] = jnp.zeros_like(o_ref)


def _moe_up_kernel(te_ref, tr_ref, x_ref, wg_ref, wu_ref, a_ref):
    def swiglu(rows):
        h = x_ref[rows, :]
        g = _wdot(h, wg_ref[...])
        u = _wdot(h, wu_ref[...])
        return (g * _sigmoid(g) * u).astype(BF16)

    _by_fill(tr_ref[pl.program_id(1)], a_ref, swiglu)


def _moe_down_kernel(te_ref, tr_ref, a_ref, wd_ref, o_ref):
    _by_fill(tr_ref[pl.program_id(0)], o_ref, lambda rows: _wdot(a_ref[rows, :], wd_ref[...]))


def _moe(tile_expert, tile_rows, xs, wg, wu, wd, j):
    n_tiles = NP_ROWS // TM_FF
    act = pl.pallas_call(
        _moe_up_kernel,
        grid_spec=pltpu.PrefetchScalarGridSpec(
            num_scalar_prefetch=2, grid=(D_FF // TF, n_tiles),
            in_specs=[pl.BlockSpec((TM_FF, D), lambda f, i, te, nv: (i, 0)),
                      pl.BlockSpec((None, None, D, TF), lambda f, i, te, nv: (j, te[i], 0, f)),
                      pl.BlockSpec((None, None, D, TF), lambda f, i, te, nv: (j, te[i], 0, f))],
            out_specs=pl.BlockSpec((TM_FF, TF), lambda f, i, te, nv: (i, f))),
        out_shape=jax.ShapeDtypeStruct((NP_ROWS, D_FF), BF16),
        compiler_params=_cparams(("arbitrary", "arbitrary")),
        name="moe_up",
    )(tile_expert, tile_rows, xs, wg, wu)
    return pl.pallas_call(
        _moe_down_kernel,
        grid_spec=pltpu.PrefetchScalarGridSpec(
            num_scalar_prefetch=2, grid=(n_tiles,),
            in_specs=[pl.BlockSpec((TM_FF, D_FF), lambda i, te, nv: (i, 0)),
                      pl.BlockSpec((None, None, D_FF, D), lambda i, te, nv: (j, te[i], 0, 0))],
            out_specs=pl.BlockSpec((TM_FF, D), lambda i, te, nv: (i, 0))),
        out_shape=jax.ShapeDtypeStruct((NP_ROWS, D), F32),
        compiler_params=_cparams(("arbitrary",)),
        name="moe_down",
    )(tile_expert, tile_rows, act, wd)


def _combine_kernel(split, x_ref, o0_ref, o1_ref, route_ref, mod_ref, ln_ref, *o_refs):
    y = route_ref[:, 2:3] * o0_ref[...] + route_ref[:, 3:4] * o1_ref[...]
    out = _layer_norm(ALPHA * x_ref[...] + mod_ref[5:6, :] * y, ln_ref[2:3, :], ln_ref[3:4, :])
    if not split:
        o_refs[0][...] = out
        return
    is_latent = pl.program_id(0) >= TP // TM_FF

    @pl.when(jnp.logical_not(is_latent))
    def _():
        o_refs[0][...] = out

    @pl.when(is_latent)
    def _():
        o_refs[1][...] = out


def _combine(x, o0, o1, route, mod_l, ln, split):
    tok = pl.BlockSpec((TM_FF, D), lambda i: (i, 0))
    if split:
        out_specs = [pl.BlockSpec((TM_FF, D), lambda i: (jnp.minimum(i, TP // TM_FF - 1), 0)),
                     pl.BlockSpec((TM_FF, D), lambda i: (jnp.maximum(i - TP // TM_FF, 0), 0))]
        out_shape = [jax.ShapeDtypeStruct((TP, D), F32), jax.ShapeDtypeStruct((TS, D), F32)]
    else:
        out_specs, out_shape = tok, jax.ShapeDtypeStruct((T, D), F32)
    return pl.pallas_call(
        functools.partial(_combine_kernel, split),
        grid=(T // TM_FF,),
        in_specs=[tok, tok, tok, pl.BlockSpec((TM_FF, LANES), lambda i: (i, 0)),
                  pl.BlockSpec((None, 8, D), lambda i: (_group_of_tile(i, TM_FF), 0, 0)),
                  pl.BlockSpec((8, D), lambda i: (0, 0))],
        out_specs=out_specs,
        out_shape=out_shape,
        compiler_params=_cparams(("arbitrary",)),
        name="moe_combine",
    )(x, o0, o1, route, mod_l, ln)


def _routing_plan(route):
    e = jnp.concatenate([route[:, 0], route[:, 1]]).astype(jnp.int32)
    onehot = (e[:, None] == jnp.arange(N_EXP, dtype=jnp.int32)[None, :]).astype(jnp.int32)
    csum = jnp.cumsum(onehot, axis=0)
    counts = csum[-1]
    rank = jnp.sum((csum - onehot) * onehot, axis=1)
    padded = (counts + TM_FF - 1) // TM_FF * TM_FF
    pend = jnp.cumsum(padded)
    dest = jnp.sum(onehot * (pend - padded)[None, :], axis=1) + rank
    order = jnp.argsort(e, stable=True).astype(jnp.int32)
    rows = jnp.arange(NP_ROWS, dtype=jnp.int32)
    before = (rows[:, None] >= pend[None, :]).astype(jnp.int32)
    row_e = jnp.minimum(jnp.sum(before, axis=1), N_EXP - 1)
    row_cnt = jnp.sum((row_e[:, None] == jnp.arange(N_EXP, dtype=jnp.int32)[None, :]) * counts[None, :], axis=1)
    in_region = rows - jnp.sum(before * padded[None, :], axis=1)
    q = jnp.clip(in_region, 0, jnp.maximum(row_cnt - 1, 0))
    src = jnp.minimum(jnp.sum(before * counts[None, :], axis=1) + q, 2 * T - 1)
    row_token = order[src] % T
    holds_data = jnp.logical_and(in_region < row_cnt, rows < pend[-1])
    tile_rows = jnp.sum(holds_data.reshape(NP_ROWS // TM_FF, TM_FF).astype(jnp.int32), axis=1)
    n_valid = (pend[-1] // TM_FF).astype(jnp.int32)
    tile_start = jnp.minimum(jnp.arange(NP_ROWS // TM_FF, dtype=jnp.int32), n_valid - 1) * TM_FF
    tile_expert = jnp.minimum(jnp.sum((tile_start[:, None] >= pend[None, :]).astype(jnp.int32), axis=1), N_EXP - 1)
    return dest[:T], dest[T:], row_token, tile_expert.astype(jnp.int32), tile_rows


_INV_PERM_RDK = np.argsort(np.concatenate([np.arange(0, RDK, 2), np.arange(1, RDK, 2)]))


def _pad_cols(a, width):
    return jnp.pad(a, ((0, 0), (0, width - a.shape[1])))


def _deinterleave(a):
    n = a.shape[-1]
    return jnp.swapaxes(a.reshape(a.shape[:-1] + (n // 2, 2)), -1, -2).reshape(a.shape)


def _prep_even_weights(w_in, w_q_b, w_kv_b):
    o_kpe = Q_LORA + KV_LORA
    o_rq, o_rk, o_rv = o_kpe + ROPE, o_kpe + ROPE + RET_W, o_kpe + ROPE + 2 * RET_W
    heads = lambda a: _deinterleave(a.reshape(D, HEADS, RDK)).reshape(D, HEADS * RDK)
    kpe = w_in[:, o_kpe:o_kpe + ROPE]
    w_in_p = jnp.concatenate([
        w_in[:, :o_kpe], heads(w_in[:, o_rq:o_rk]), heads(w_in[:, o_rk:o_rv]) * (RDK ** -0.5), w_in[:, o_rv:],
        _pad_cols(kpe, LANES), _pad_cols(_deinterleave(kpe), LANES),
        _pad_cols(jnp.concatenate([-kpe[:, 1::2], kpe[:, 0::2]], axis=1), LANES)], axis=1).astype(BF16)
    wq = w_q_b.reshape(Q_LORA, HEADS, NOPE + ROPE)
    wq = jnp.concatenate([wq[:, :, :NOPE], _deinterleave(wq[:, :, NOPE:]),
                          jnp.zeros((Q_LORA, HEADS, LANES - NOPE - ROPE), F32)], axis=2)
    wq = wq.reshape(Q_LORA, HEADS * LANES).astype(BF16)
    wkv = w_kv_b.reshape(KV_LORA, HEADS, NOPE + VDIM)
    zero = jnp.zeros((KV_LORA, HEADS, LANES - NOPE), F32)
    wk = jnp.concatenate([wkv[:, :, :NOPE], zero], axis=2).reshape(KV_LORA, HEADS * LANES)
    wv = wkv[:, :, NOPE:].reshape(KV_LORA, HEADS // 2, 2, VDIM)
    zv = jnp.zeros((KV_LORA, HEADS // 2, VDIM), F32)
    wv = jnp.stack([jnp.concatenate([wv[:, :, 0], zv], axis=2), jnp.concatenate([zv, wv[:, :, 1]], axis=2)], axis=2)
    wkv_p = jnp.concatenate([wk, wv.reshape(KV_LORA, HEADS * LANES)], axis=1).astype(BF16)
    return w_in_p, wq, wkv_p


def _placement():
    ek = np.zeros((LANES, HEADS * LANES), np.float32)
    for h in range(HEADS):
        ek[np.arange(ROPE), h * LANES + NOPE + np.arange(ROPE)] = 1.0
    return jnp.asarray(ek, BF16)


def _rotary_tables():
    rows = LAT_LEN // GRID_W
    r, col = jnp.meshgrid(jnp.arange(rows, dtype=F32), jnp.arange(GRID_W, dtype=F32), indexing='ij')
    n_freq = ROPE // 4
    freqs = 1.0 / (10000.0 ** (jnp.arange(n_freq, dtype=F32) / n_freq))
    ang = jnp.concatenate([r.reshape(-1)[:, None] * freqs, col.reshape(-1)[:, None] * freqs], axis=-1)
    cos, sin = jnp.cos(ang), jnp.sin(ang)
    theta = 1.0 / (10000.0 ** jnp.linspace(0.0, 1.0, RDK // 2, dtype=F32))
    rang = jnp.arange(LAT_LEN, dtype=F32)[:, None] * theta
    rcos, rsin = jnp.cos(rang), jnp.sin(rang)
    one = lambda w: jnp.ones((LAT_LEN, w), F32)
    zero = lambda w: jnp.zeros((LAT_LEN, w), F32)
    lat = [jnp.concatenate([one(NOPE), cos, cos, one(LANES - NOPE - ROPE)], axis=1),
           jnp.concatenate([zero(NOPE), sin, sin, zero(LANES - NOPE - ROPE)], axis=1),
           jnp.concatenate([cos, cos, zero(LANES - ROPE)], axis=1),
           jnp.concatenate([sin, sin, zero(LANES - ROPE)], axis=1),
           jnp.concatenate([rcos] * 4, axis=1), jnp.concatenate([rsin] * 4, axis=1)]
    ident = [np.ones((TM, LANES), np.float32), np.zeros((TM, LANES), np.float32)]
    ident_k = np.concatenate([np.ones((TM, ROPE), np.float32), np.zeros((TM, LANES - ROPE), np.float32)], axis=1)
    ident = [ident[0], ident[1], ident_k, ident[1], ident[0], ident[1]]
    return [jnp.concatenate([l, jnp.asarray(c)], axis=0) for l, c in zip(lat, ident)]


def _block_diag_states(s0):
    s = jnp.swapaxes(_deinterleave(jnp.swapaxes(s0, -1, -2)), -1, -2)
    s = s.reshape(s0.shape[0], HEADS // 2, 2, RDK, RDK)
    z = jnp.zeros_like(s[:, :, 0])
    top = jnp.concatenate([s[:, :, 0], z], axis=-1)
    bot = jnp.concatenate([z, s[:, :, 1]], axis=-1)
    return jnp.concatenate([top, bot], axis=-2)


def _unpermute_matrix():
    m = np.zeros((LANES, LANES), np.float32)
    for blk in range(LANES // RDK):
        m[blk * RDK + np.arange(RDK), blk * RDK + _INV_PERM_RDK] = 1.0
    return jnp.asarray(m, BF16)


def kernel(x_prompt, x_sample, c, cache_ckv, cache_kpe, state_ret_fwd, state_ret_bwd, c_ctx, w_mod, b_mod, ln_g, ln_b, w_in_mix, q_a_gain, kv_a_gain, w_q_b, w_kv_b, ret_decay_fwd, ret_decay_bwd, w_out_mix, w_in_conv, conv_w, w_out_conv, ffn_gate, ffn_up, ffn_down, router_w, router_b, exp_gate, exp_up, exp_down):
    x = (x_prompt.reshape(TP, D), x_sample.reshape(TS, D))
    cond8 = jnp.concatenate([c_ctx[None], c, jnp.zeros((8 - 1 - N_LAT_SEQ, D), F32)], axis=0)
    mods = _modulation(cond8, w_mod, b_mod)
    mods = jnp.pad(mods.reshape(DEPTH, N_GROUPS, 6, D), ((0, 0), (0, 0), (0, 2), (0, 0)))
    ln = jnp.pad(jnp.concatenate([ln_g, ln_b], axis=1)[:, jnp.array([0, 2, 1, 3])], ((0, 0), (0, 4), (0, 0)))
    tabs = _rotary_tables()
    ek = _placement()
    unperm = _unpermute_matrix()
    bf = lambda a: a.astype(BF16)
    w_out_mix_b, w_in_conv_b, w_out_conv_b = bf(w_out_mix), bf(w_in_conv), bf(w_out_conv)
    ffn_b = (bf(ffn_gate), bf(ffn_up), bf(ffn_down))
    exp_b = (exp_gate, exp_up, exp_down)
    cache, states = None, None
    for layer in range(DEPTH):
        j = layer // 2
        mod_l, ln_l = mods[layer], ln[layer]
        if layer % 2 == 0:
            w_in_p, wq, wkv = _prep_even_weights(w_in_mix[j], w_q_b[j], w_kv_b[j])
            q, k, v, cache, rq, rk, rv, rg = _even_in(
                x, mod_l, w_in_p, q_a_gain[j][None], kv_a_gain[j][None], wq, wkv, ek, tabs, cache)
            kpe_c = _pad_cols(_deinterleave(cache_kpe[:, j]).reshape(N_LAT_SEQ * PAST, ROPE), LANES)
            kc, vc = _ctx_kv(cache_ckv[:, j].reshape(N_LAT_SEQ * PAST, KV_LORA), kpe_c, wkv, ek)
            attn = _attention(q, k, v, kc, vc)
            lg = LOG2E * jnp.concatenate([jax.nn.log_sigmoid(ret_decay_fwd[j].astype(F32)),
                                          jax.nn.log_sigmoid(ret_decay_bwd[j].astype(F32))])
            ret, sf, sb = _retention(lg, rq, rk, rv, rg, _block_diag_states(state_ret_fwd[:, j]),
                                     _block_diag_states(state_ret_bwd[:, j]), unperm, states)
            states = (sf, sb)
            x = _mix_ffn(x, attn, ret, mod_l, w_out_mix_b, *ffn_b, ln_l, j)
        else:
            b, z = _conv_in(x, mod_l, w_in_conv_b, j)
            cw = jnp.pad(conv_w[j], ((0, 5), (0, 0)))
            rw = _pad_cols(router_w[j], LANES)
            rw_hi = rw.astype(BF16)
            rw = jnp.concatenate([rw_hi, (rw - rw_hi.astype(F32)).astype(BF16)], axis=1)
            rb = jnp.concatenate([router_b[j].astype(F32), jnp.full((LANES - N_EXP,), -1e30, F32)])[None]
            x, h, route = _conv_out(x, b, z, mod_l, cw, w_out_conv_b, ln_l, rw, rb, j)
            dest0, dest1, row_token, tile_expert, tile_rows = _routing_plan(route)
            out_sorted = _moe(tile_expert, tile_rows, h[row_token], *exp_b, j)
            x = _combine(x, out_sorted[dest0], out_sorted[dest1], route, mod_l, ln_l, split=layer == DEPTH - 1)
    y_prompt = x[0].reshape(N_PROMPT_SEQ, PROMPT_LEN, D)
    y_sample = x[1].reshape(N_LAT_SEQ, LAT_LEN, D)
    return (y_prompt, y_sample, cache[0], cache[1], states[0], states[1])
```

```python
import functools

import numpy as np
import jax
import jax.numpy as jnp
from jax import lax
from jax.experimental import pallas as pl
from jax.experimental.pallas import tpu as pltpu

F32 = jnp.float32
BF16 = jnp.bfloat16

D = 1024
DEPTH = 4
N_PROMPT_SEQ, PROMPT_LEN = 32, 256
N_LAT_SEQ, LAT_LEN = 2, 2048
PAST = 512
GRID_W = 64
TP = N_PROMPT_SEQ * PROMPT_LEN
TS = N_LAT_SEQ * LAT_LEN
T = TP + TS
HEADS = 8
NOPE, ROPE, VDIM = 64, 32, 64
Q_LORA, KV_LORA = 384, 256
RDK = 64
D_FF = 2816
N_EXP = 8
ALPHA = (2.0 * DEPTH) ** 0.25
LOG2E = float(np.log2(np.e))
Q_SCALE = float((NOPE + ROPE) ** -0.5) * LOG2E
LANES = 128
N_GROUPS = 8

TM = 512
TM_FF = 512
TF = D_FF // 2
TQ = 256
NP_ROWS = 2 * T + N_EXP * TM_FF
VMEM_LIMIT = 56 * 1024 * 1024

RET_W = HEADS * RDK
COL_KV = Q_LORA
COL_RQ = COL_KV + KV_LORA
COL_RK, COL_RV, COL_RG = COL_RQ + RET_W, COL_RQ + 2 * RET_W, COL_RQ + 3 * RET_W
COL_KPE = COL_RQ + 4 * RET_W
IN_COLS = COL_KPE + 3 * LANES


def _cparams(sem):
    return pltpu.CompilerParams(dimension_semantics=sem, vmem_limit_bytes=VMEM_LIMIT)


def _group_of_tile(i, tm):
    per_seq = LAT_LEN // tm
    return jnp.maximum(i - TP // tm + per_seq, 0) // per_seq


def _bdot(a, b):
    return jnp.dot(a, b, preferred_element_type=F32)


def _wdot(a, w):
    return lax.dot_general(a, w, (((1,), (0,)), ((), ())), preferred_element_type=F32)


def _sigmoid(v):
    return 1.0 / (1.0 + jnp.exp(-v))


def _layer_norm(v, g, b):
    mu = jnp.mean(v, axis=-1, keepdims=True)
    d = v - mu
    var = jnp.mean(d * d, axis=-1, keepdims=True)
    return d * lax.rsqrt(var + 1e-5) * g + b


def _rms(v, g):
    return v * lax.rsqrt(jnp.mean(v * v, axis=-1, keepdims=True) + 1e-6) * g


def _mod_kernel(c_ref, w_ref, b_ref, o_ref):
    c = c_ref[...]
    s = (c * _sigmoid(c)).astype(BF16)
    o_ref[...] = _bdot(s, w_ref[...].astype(BF16)) + b_ref[...]


def _modulation(cond8, w_mod, b_mod):
    tn = 1536
    return pl.pallas_call(
        _mod_kernel,
        grid=(DEPTH, 6 * D // tn),
        in_specs=[pl.BlockSpec((8, D), lambda l, n: (0, 0)),
                  pl.BlockSpec((None, D, tn), lambda l, n: (l, 0, n)),
                  pl.BlockSpec((None, 1, tn), lambda l, n: (l, 0, n))],
        out_specs=pl.BlockSpec((None, 8, tn), lambda l, n: (l, 0, n)),
        out_shape=jax.ShapeDtypeStruct((DEPTH, 8, 6 * D), F32),
        compiler_params=_cparams(("arbitrary", "arbitrary")),
        name="modulation",
    )(cond8, w_mod, b_mod.reshape(DEPTH, 1, 6 * D))


def _swap_halves(a, half):
    n = a.shape[-1]
    lane = lax.broadcasted_iota(jnp.int32, a.shape, 1)
    first = (lane & (2 * half - 1)) < half
    return jnp.where(first, -pltpu.roll(a, n - half, axis=1), pltpu.roll(a, half, axis=1))


def _tile_rows(x_refs, tm):
    if len(x_refs) == 1:
        return x_refs[0][...]
    return jnp.where(pl.program_id(0) >= TP // tm, x_refs[1][...], x_refs[0][...])


def _tile_specs(x, tm):
    if not isinstance(x, tuple):
        return (x,), [pl.BlockSpec((tm, D), lambda i, *_: (i, 0))]
    return x, [pl.BlockSpec((tm, D), lambda i, *_: (jnp.minimum(i, TP // tm - 1), 0)),
               pl.BlockSpec((tm, D), lambda i, *_: (jnp.maximum(i - TP // tm, 0), 0))]


def _even_in_kernel(n_x, n_prev, *refs):
    x_refs, refs = refs[:n_x], refs[n_x:]
    (mod_ref, w_in_ref, qg_ref, kvg_ref, wq_ref, wkv_ref, ek_ref,
     cq_ref, sq_ref, ck_ref, sk_ref, cr_ref, sr_ref), refs = refs[:13], refs[13:]
    prev_refs, refs = refs[:2 * n_prev], refs[2 * n_prev:]
    q_ref, k_ref, v_ref, ckv_ref, kpe_ref, rq_ref, rk_ref, rv_ref, rg_ref = refs
    x = _tile_rows(x_refs, TM)
    h = (x * (1.0 + mod_ref[1:2, :]) + mod_ref[0:1, :]).astype(BF16)
    p = _bdot(h, w_in_ref[...])
    qn = _rms(p[:, 0:Q_LORA], qg_ref[...]).astype(BF16)
    qa = _bdot(qn, wq_ref[...])
    ckv = _rms(p[:, COL_KV:COL_RQ], kvg_ref[...])
    kv = _bdot(ckv.astype(BF16), wkv_ref[...])
    v_ref[...] = kv[:, HEADS * LANES:].astype(BF16)
    base = COL_KPE
    ka = p[:, base + LANES:base + 2 * LANES]
    kb = p[:, base + 2 * LANES:base + 3 * LANES]
    rq = p[:, COL_RQ:COL_RK]
    rk = p[:, COL_RK:COL_RV]
    rv_ref[...] = p[:, COL_RV:COL_RG].astype(BF16)
    rg_ref[...] = p[:, COL_RG:COL_KPE]
    lane = lax.broadcasted_iota(jnp.int32, qa.shape, 1) & (LANES - 1)
    qb = jnp.where(lane < NOPE + ROPE // 2,
                   -pltpu.roll(qa, qa.shape[1] - ROPE // 2, axis=1),
                   pltpu.roll(qa, ROPE // 2, axis=1))
    cq = jnp.concatenate([cq_ref[...]] * HEADS, axis=1)
    sq = jnp.concatenate([sq_ref[...]] * HEADS, axis=1)
    q_ref[...] = ((qa * cq + qb * sq) * Q_SCALE).astype(BF16)
    kpe_rot = ka * ck_ref[...] + kb * sk_ref[...]
    k_ref[...] = (kv[:, :HEADS * LANES] + _bdot(kpe_rot.astype(BF16), ek_ref[...])).astype(BF16)
    cr = jnp.concatenate([cr_ref[...]] * (RET_W // LANES), axis=1)
    sr = jnp.concatenate([sr_ref[...]] * (RET_W // LANES), axis=1)
    rq_ref[...] = (rq * cr + _swap_halves(rq, RDK // 2) * sr).astype(BF16)
    rk_ref[...] = (rk * cr + _swap_halves(rk, RDK // 2) * sr).astype(BF16)

    @pl.when(pl.program_id(0) < TP // TM)
    def _():
        for k in range(n_prev):
            ckv_ref[:, k] = prev_refs[2 * k][...]
            kpe_ref[:, k] = prev_refs[2 * k + 1][...]
        for s in range(TM // PROMPT_LEN):
            rows = slice(s * PROMPT_LEN, (s + 1) * PROMPT_LEN)
            ckv_ref[s, n_prev] = ckv[rows]
            kpe_ref[s, n_prev] = p[rows, base:base + ROPE]


def _even_in(x, mod_l, w_in, qg, kvg, wq, wkv, ek, tabs, prev_cache):
    x_args, x_specs = _tile_specs(x, TM)
    n_prev = 0 if prev_cache is None else prev_cache[0].shape[1]
    spt = TM // PROMPT_LEN
    tok = lambda w: pl.BlockSpec((TM, w), lambda i: (i, 0))
    full = lambda a: pl.BlockSpec(a.shape, lambda i: (0,) * a.ndim, pipeline_mode=pl.Buffered(1))
    lat_tiles = LAT_LEN // TM
    tab = pl.BlockSpec((TM, LANES), lambda i: (
        jnp.where(i < TP // TM, lat_tiles, jnp.maximum(i - TP // TM, 0) % lat_tiles), 0))
    seq = lambda i: (jnp.minimum(i, TP // TM - 1), 0, 0, 0)
    seq_k = lambda k, i: (jnp.minimum(i, TP // TM - 1), k, 0, 0)
    prev_args, prev_specs = [], []
    for k in range(n_prev):
        for a, w in zip(prev_cache, (KV_LORA, ROPE)):
            prev_args.append(a)
            prev_specs.append(pl.BlockSpec((spt, None, PROMPT_LEN, w), functools.partial(seq_k, k)))
    tok_outs = lambda dims: ([tok(w) for w, _ in dims], [jax.ShapeDtypeStruct((T, w), dt) for w, dt in dims])
    qkv_specs, qkv_shapes = tok_outs([(HEADS * LANES, BF16)] * 3)
    ret_specs, ret_shapes = tok_outs([(RET_W, BF16), (RET_W, BF16), (RET_W, BF16), (RET_W, F32)])
    cache_specs = [pl.BlockSpec((spt, n_prev + 1, PROMPT_LEN, w), seq) for w in (KV_LORA, ROPE)]
    cache_shapes = [jax.ShapeDtypeStruct((N_PROMPT_SEQ, n_prev + 1, PROMPT_LEN, w), F32) for w in (KV_LORA, ROPE)]
    q, k, v, ckv, kpe, rq, rk, rv, rg = pl.pallas_call(
        functools.partial(_even_in_kernel, len(x_args), n_prev),
        grid=(T // TM,),
        in_specs=x_specs + [pl.BlockSpec((None, 8, D), lambda i: (_group_of_tile(i, TM), 0, 0)),
                            full(w_in), full(qg), full(kvg), full(wq), full(wkv), full(ek)] + [tab] * 6 + prev_specs,
        out_specs=qkv_specs + cache_specs + ret_specs,
        out_shape=qkv_shapes + cache_shapes + ret_shapes,
        compiler_params=_cparams(("arbitrary",)),
        name="even_in",
    )(*x_args, mod_l, w_in, qg, kvg, wq, wkv, ek, *tabs, *prev_args)
    return q, k, v, (ckv, kpe), rq, rk, rv, rg


def _ctx_kv_kernel(ckv_ref, kpe_ref, wkv_ref, ek_ref, k_ref, v_ref):
    kv = _bdot(ckv_ref[...].astype(BF16), wkv_ref[...])
    k_ref[...] = (kv[:, :HEADS * LANES] + _bdot(kpe_ref[...].astype(BF16), ek_ref[...])).astype(BF16)
    v_ref[...] = kv[:, HEADS * LANES:].astype(BF16)


def _ctx_kv(ckv_c, kpe_c, wkv, ek):
    n = ckv_c.shape[0]
    full = lambda a: pl.BlockSpec(a.shape, lambda i: (0,) * a.ndim)
    return pl.pallas_call(
        _ctx_kv_kernel,
        grid=(n // PAST,),
        in_specs=[pl.BlockSpec((PAST, KV_LORA), lambda i: (i, 0)), pl.BlockSpec((PAST, LANES), lambda i: (i, 0)),
                  full(wkv), full(ek)],
        out_specs=[pl.BlockSpec((PAST, HEADS * LANES), lambda i: (i, 0))] * 2,
        out_shape=[jax.ShapeDtypeStruct((n, HEADS * LANES), BF16)] * 2,
        compiler_params=_cparams(("parallel",)),
        name="ctx_kv",
    )(ckv_c, kpe_c, wkv, ek)


def _attn_kernel(n_kv, q_ref, *refs):
    k_refs = refs[0:2 * n_kv:2]
    v_refs = refs[1:2 * n_kv:2]
    o_ref = refs[2 * n_kv]
    nt = (((1,), (1,)), ((), ()))
    for pair in range(HEADS // 2):
        acc = None
        for sub in range(2):
            sl = slice((2 * pair + sub) * LANES, (2 * pair + sub + 1) * LANES)
            qh = q_ref[:, sl]
            s = [lax.dot_general(qh, k[:, sl], nt, preferred_element_type=F32) for k in k_refs]
            m = functools.reduce(jnp.maximum, [jnp.max(a, axis=-1, keepdims=True) for a in s])
            e = [jnp.exp2(a - m) for a in s]
            den = functools.reduce(jnp.add, [jnp.sum(a, axis=-1, keepdims=True) for a in e])
            o = functools.reduce(jnp.add, [_bdot(a.astype(BF16), v[:, sl]) for a, v in zip(e, v_refs)])
            o = o / den
            acc = o if acc is None else acc + o
        o_ref[:, pair * LANES:(pair + 1) * LANES] = acc.astype(BF16)


def _latent_seq(i):
    return jnp.maximum(i - TP // TQ, 0) // (LAT_LEN // TQ)


def _attn_tiles_kernel(q_ref, kp_ref, vp_ref, kc_ref, vc_ref, kl_ref, vl_ref, o_ref):
    is_latent = pl.program_id(0) >= TP // TQ

    @pl.when(jnp.logical_not(is_latent))
    def _():
        _attn_kernel(1, q_ref, kp_ref, vp_ref, o_ref)

    @pl.when(is_latent)
    def _():
        _attn_kernel(2, q_ref, kc_ref, vc_ref, kl_ref, vl_ref, o_ref)


def _attention(q, k, v, kc, vc):
    w = HEADS * LANES
    tile = lambda i: (i, 0)
    ctx_own = lambda i: (jnp.minimum(i, N_PROMPT_SEQ - 1), 0)
    cache = lambda i: (_latent_seq(i), 0)
    lat_own = lambda i: (TP // LAT_LEN + _latent_seq(i), 0)
    return pl.pallas_call(
        _attn_tiles_kernel,
        grid=(T // TQ,),
        in_specs=[pl.BlockSpec((TQ, w), tile),
                  pl.BlockSpec((PROMPT_LEN, w), ctx_own), pl.BlockSpec((PROMPT_LEN, w), ctx_own),
                  pl.BlockSpec((PAST, w), cache), pl.BlockSpec((PAST, w), cache),
                  pl.BlockSpec((LAT_LEN, w), lat_own), pl.BlockSpec((LAT_LEN, w), lat_own)],
        out_specs=pl.BlockSpec((TQ, HEADS * VDIM), tile),
        out_shape=jax.ShapeDtypeStruct((T, HEADS * VDIM), BF16),
        compiler_params=_cparams(("arbitrary",)),
        name="attention",
    )(q, k, v, kc, vc, k, v)


def _ret_prefix_kernel(lg_ref, rk_ref, rv_ref, s0f_ref, s0b_ref, pf_ref, qb_ref, sf_scr, sb_scr):
    s = pl.program_id(0)
    n_tiles, per_seq = TS // TQ, LAT_LEN // TQ
    row = lax.broadcasted_iota(jnp.int32, (LANES, 1), 0)
    lane = lax.broadcasted_iota(jnp.int32, (1, LANES), 1)
    top, lo = row < RDK, lane < RDK
    same_head = top == lo
    m_col = lax.broadcasted_iota(jnp.int32, (TQ, 1), 0).astype(F32)

    def scan_step(first, lg_off, s0_ref, scr, out_ref, pos):
        @pl.when(first)
        def _():
            scr[...] = s0_ref[...]

        for pair in range(HEADS // 2):
            sl = slice(pair * LANES, (pair + 1) * LANES)
            lg_even, lg_odd = lg_ref[lg_off + 2 * pair], lg_ref[lg_off + 2 * pair + 1]
            dec = jnp.exp2(pos * jnp.where(lo, lg_even, lg_odd))
            local = _bdot((rk_ref[:, sl].astype(F32) * dec).T.astype(BF16), rv_ref[:, sl])
            out_ref[pair] = scr[pair].astype(BF16)
            tile_decay = jnp.exp2(float(TQ) * jnp.where(top, lg_even, lg_odd))
            scr[pair] = scr[pair] * tile_decay + jnp.where(same_head, local, 0.0)

    @pl.when(s < n_tiles)
    def _():
        scan_step(s % per_seq == 0, 0, s0f_ref, sf_scr, pf_ref, TQ - 1.0 - m_col)

    @pl.when(s >= n_tiles)
    def _():
        scan_step((2 * n_tiles - 1 - s) % per_seq == per_seq - 1, HEADS, s0b_ref, sb_scr, qb_ref, m_col)


def _ret_prefix(lg, rk, rv, s0f, s0b):
    n_tiles, per_seq = TS // TQ, LAT_LEN // TQ
    tile_of = lambda s: jnp.where(s < n_tiles, s, 2 * n_tiles - 1 - s)
    st = pl.BlockSpec((None, HEADS // 2, LANES, LANES), lambda s, lg: (tile_of(s) // per_seq, 0, 0, 0))
    kv = pl.BlockSpec((TQ, HEADS * RDK), lambda s, lg: (TP // TQ + tile_of(s), 0))
    pf_blk = pl.BlockSpec((None, HEADS // 2, LANES, LANES), lambda s, lg: (jnp.minimum(s, n_tiles - 1), 0, 0, 0))
    qb_blk = pl.BlockSpec((None, HEADS // 2, LANES, LANES),
                          lambda s, lg: (jnp.minimum(2 * n_tiles - 1 - s, n_tiles - 1), 0, 0, 0))
    shape = jax.ShapeDtypeStruct((n_tiles, HEADS // 2, LANES, LANES), BF16)
    return pl.pallas_call(
        _ret_prefix_kernel,
        grid_spec=pltpu.PrefetchScalarGridSpec(
            num_scalar_prefetch=1, grid=(2 * n_tiles,),
            in_specs=[kv, kv, st, st], out_specs=[pf_blk, qb_blk],
            scratch_shapes=[pltpu.VMEM((HEADS // 2, LANES, LANES), F32)] * 2),
        out_shape=[shape, shape],
        compiler_params=_cparams(("arbitrary",)),
        name="ret_prefix",
    )(lg, rk, rv, s0f, s0b)


def _retention_kernel(latent, lg_ref, rq_ref, rk_ref, rv_ref, rg_ref, *refs):
    q0 = 0
    seq_len = rq_ref.shape[0]
    if latent:
        s0f_ref, s0b_ref, o_ref = refs
    else:
        unperm_ref, o_ref, sf_ref, sb_ref = refs
    tq, tk = rq_ref.shape[0], rk_ref.shape[0]
    nt = (((1,), (1,)), ((), ()))
    n_idx = (q0 + lax.broadcasted_iota(jnp.int32, (tq, tk), 0)).astype(F32)
    m_idx = lax.broadcasted_iota(jnp.int32, (tq, tk), 1).astype(F32)
    dist = n_idx - m_idx
    adist = jnp.abs(dist)
    fwd = dist > 0.0
    diag = jnp.where(dist == 0.0, 1.0, 0.0)
    lane = lax.broadcasted_iota(jnp.int32, (1, LANES), 1)
    lo = lane < RDK
    n_col = (q0 + lax.broadcasted_iota(jnp.int32, (tq, 1), 0)).astype(F32)
    m_col = lax.broadcasted_iota(jnp.int32, (tk, 1), 0).astype(F32)
    for pair in range(HEADS // 2):
        sl = slice(pair * LANES, (pair + 1) * LANES)
        qb, kb, vb = rq_ref[:, sl], rk_ref[:, sl], rv_ref[:, sl]
        acc = jnp.zeros((tq, LANES), F32)
        for sub in range(2):
            h = 2 * pair + sub
            lgf, lgb = lg_ref[h], lg_ref[HEADS + h]
            half = lo if sub == 0 else jnp.logical_not(lo)
            qm = jnp.where(half, qb, jnp.zeros_like(qb))
            vm = jnp.where(half, vb, jnp.zeros_like(vb))
            s = lax.dot_general(qm, kb, nt, preferred_element_type=F32)
            w = jnp.exp2(adist * jnp.where(fwd, lgf, lgb)) + diag
            acc = acc + _bdot((s * w).astype(BF16), vm)
        lgf_l = jnp.where(lo, lg_ref[2 * pair], lg_ref[2 * pair + 1])
        lgb_l = jnp.where(lo, lg_ref[HEADS + 2 * pair], lg_ref[HEADS + 2 * pair + 1])
        if latent:
            acc = acc + _bdot(qb, s0f_ref[pair]) * jnp.exp2((n_col + 1.0) * lgf_l)
            acc = acc + _bdot(qb, s0b_ref[pair]) * jnp.exp2((seq_len - n_col) * lgb_l)
        else:
            v_swapped = pltpu.roll(vb.astype(F32), RDK, axis=1).astype(BF16)
            for st_ref, dec in ((sf_ref, jnp.exp2((seq_len - 1.0 - m_col) * lgf_l)),
                                (sb_ref, jnp.exp2(m_col * lgb_l))):
                kt = (kb.astype(F32) * dec).T.astype(BF16)
                kt = _bdot(unperm_ref[...], kt).astype(BF16)
                st_ref[2 * pair] = _bdot(kt, vb)[0:RDK, 0:RDK]
                st_ref[2 * pair + 1] = _bdot(kt, v_swapped)[RDK:, 0:RDK]
        inv = 1.0 / RDK
        mu = jnp.where(lo, jnp.sum(jnp.where(lo, acc, 0.0), axis=-1, keepdims=True),
                       jnp.sum(jnp.where(lo, 0.0, acc), axis=-1, keepdims=True)) * inv
        dlt = acc - mu
        d2 = dlt * dlt
        var = jnp.where(lo, jnp.sum(jnp.where(lo, d2, 0.0), axis=-1, keepdims=True),
                        jnp.sum(jnp.where(lo, 0.0, d2), axis=-1, keepdims=True)) * inv
        g = rg_ref[:, sl]
        o_ref[:, sl] = (dlt * lax.rsqrt(var + 1e-5) * (g * _sigmoid(g))).astype(BF16)


def _retention_tiles_kernel(n_prev, lg_ref, rq_ref, rk_ref, rv_ref, rg_ref, pf_ref, qb_ref, unperm_ref, *refs):
    prev_refs, (o_ref, sf_ref, sb_ref) = refs[:2 * n_prev], refs[2 * n_prev:]
    is_latent = pl.program_id(0) >= TP // TQ

    @pl.when(jnp.logical_not(is_latent))
    def _():
        for k in range(n_prev):
            sf_ref[k] = prev_refs[2 * k][...]
            sb_ref[k] = prev_refs[2 * k + 1][...]
        _retention_kernel(False, lg_ref, rq_ref, rk_ref, rv_ref, rg_ref, unperm_ref, o_ref,
                          sf_ref.at[n_prev], sb_ref.at[n_prev])

    @pl.when(is_latent)
    def _():
        _retention_kernel(True, lg_ref, rq_ref, rk_ref, rv_ref, rg_ref, pf_ref, qb_ref, o_ref)


def _retention(lg, rq, rk, rv, rg, s0f, s0b, unperm, prev_states):
    w = HEADS * RDK
    n_prev = 0 if prev_states is None else prev_states[0].shape[1]
    pf, qb = _ret_prefix(lg, rk, rv, s0f, s0b)
    tile = lambda i, lg: (i, 0)
    s0_blk = pl.BlockSpec((None, HEADS // 2, LANES, LANES), lambda i, lg: (jnp.maximum(i - TP // TQ, 0), 0, 0, 0))
    seq = lambda i, lg: (jnp.minimum(i, N_PROMPT_SEQ - 1), 0, 0, 0, 0)
    st_blk = pl.BlockSpec((None, n_prev + 1, HEADS, RDK, RDK), seq)
    st_shape = jax.ShapeDtypeStruct((N_PROMPT_SEQ, n_prev + 1, HEADS, RDK, RDK), F32)
    prev = () if prev_states is None else tuple(prev_states)
    prev_specs = [pl.BlockSpec((None, None, HEADS, RDK, RDK),
                               functools.partial(lambda k, i, lg: (jnp.minimum(i, N_PROMPT_SEQ - 1), k, 0, 0, 0), k))
                  for k in range(n_prev) for _ in range(2)]
    prev_args = [p for k in range(n_prev) for p in prev]
    return pl.pallas_call(
        functools.partial(_retention_tiles_kernel, n_prev),
        grid_spec=pltpu.PrefetchScalarGridSpec(
            num_scalar_prefetch=1, grid=(T // TQ,),
            in_specs=[pl.BlockSpec((TQ, w), tile)] * 4 + [s0_blk, s0_blk,
                      pl.BlockSpec((LANES, LANES), lambda i, lg: (0, 0))] + prev_specs,
            out_specs=[pl.BlockSpec((TQ, w), tile), st_blk, st_blk]),
        out_shape=[jax.ShapeDtypeStruct((T, w), BF16), st_shape, st_shape],
        compiler_params=_cparams(("arbitrary",)),
        name="retention",
    )(lg, rq, rk, rv, rg, pf, qb, unperm, *prev_args)


def _mix_ffn_kernel(n_x, *refs):
    x_refs, refs = refs[:n_x], refs[n_x:]
    a_ref, r_ref, mod_ref, wo_ref, wg_ref, wu_ref, wd_ref, ln_ref, o_ref = refs
    half = HEADS * VDIM
    y = _bdot(a_ref[...], wo_ref[0:half, :]) + _bdot(r_ref[...], wo_ref[half:, :])
    x1 = _layer_norm(ALPHA * _tile_rows(x_refs, TM_FF) + mod_ref[2:3, :] * y, ln_ref[0:1, :], ln_ref[1:2, :])
    h = (x1 * (1.0 + mod_ref[4:5, :]) + mod_ref[3:4, :]).astype(BF16)
    acc = None
    for f in range(D_FF // TF):
        cols = slice(f * TF, (f + 1) * TF)
        g = _bdot(h, wg_ref[:, cols])
        u = _bdot(h, wu_ref[:, cols])
        part = _bdot((g * _sigmoid(g) * u).astype(BF16), wd_ref[cols, :])
        acc = part if acc is None else acc + part
    o_ref[...] = _layer_norm(ALPHA * x1 + mod_ref[5:6, :] * acc, ln_ref[2:3, :], ln_ref[3:4, :])


def _mix_ffn(x, attn, ret, mod_l, w_out, wg, wu, wd, ln, j):
    x_args, x_specs = _tile_specs(x, TM_FF)
    tok = lambda w: pl.BlockSpec((TM_FF, w), lambda i: (i, 0))
    resident = lambda a: pl.BlockSpec((None,) + a.shape[1:], lambda i: (j, 0, 0), pipeline_mode=pl.Buffered(1))
    return pl.pallas_call(
        functools.partial(_mix_ffn_kernel, len(x_args)),
        grid=(T // TM_FF,),
        in_specs=x_specs + [tok(HEADS * VDIM), tok(HEADS * RDK),
                  pl.BlockSpec((None, 8, D), lambda i: (_group_of_tile(i, TM_FF), 0, 0)),
                  resident(w_out), resident(wg), resident(wu), resident(wd), pl.BlockSpec((8, D), lambda i: (0, 0))],
        out_specs=tok(D),
        out_shape=jax.ShapeDtypeStruct((T, D), F32),
        compiler_params=_cparams(("arbitrary",)),
        name="mix_ffn",
    )(*x_args, attn, ret, mod_l, w_out, wg, wu, wd, ln)


def _conv_in_kernel(x_ref, mod_ref, w_ref, b_ref, z_ref):
    h = (x_ref[...] * (1.0 + mod_ref[1:2, :]) + mod_ref[0:1, :]).astype(BF16)
    p = _bdot(h, w_ref[...])
    b_ref[...] = p[:, 0:D]
    z_ref[...] = p[:, D:2 * D] * p[:, 2 * D:3 * D]


def _conv_in(x, mod_l, w_in, j):
    tok = pl.BlockSpec((TM_FF, D), lambda i: (i, 0))
    return pl.pallas_call(
        _conv_in_kernel,
        grid=(T // TM_FF,),
        in_specs=[tok, pl.BlockSpec((None, 8, D), lambda i: (_group_of_tile(i, TM_FF), 0, 0)),
                  pl.BlockSpec((None,) + w_in.shape[1:], lambda i: (j, 0, 0))],
        out_specs=[tok, tok],
        out_shape=[jax.ShapeDtypeStruct((T, D), F32)] * 2,
        compiler_params=_cparams(("parallel",)),
        name="conv_in",
    )(x, mod_l, w_in)


def _conv_out_kernel(x_ref, b_ref, z_ref, zp_ref, zn_ref, mod_ref, cw_ref, w_ref, ln_ref, rw_ref, rb_ref,
                     o_ref, h_ref, route_ref):
    i = pl.program_id(0)
    z = z_ref[...]
    tm = z.shape[0]
    row = lax.broadcasted_iota(jnp.int32, (tm, 1), 0)
    seq_len = jnp.where(i < TP // tm, PROMPT_LEN, LAT_LEN)
    pos = (i * tm + row) & (seq_len - 1)
    prev = jnp.where(row == 0, zp_ref[7:8, :], pltpu.roll(z, 1, axis=0))
    prev = jnp.where(pos == 0, 0.0, prev)
    nxt = jnp.where(row == tm - 1, zn_ref[0:1, :], pltpu.roll(z, tm - 1, axis=0))
    nxt = jnp.where(pos == seq_len - 1, 0.0, nxt)
    y = prev * cw_ref[0:1, :] + z * cw_ref[1:2, :] + nxt * cw_ref[2:3, :]
    t = _bdot((b_ref[...] * y).astype(BF16), w_ref[...])
    x1 = _layer_norm(ALPHA * x_ref[...] + mod_ref[2:3, :] * t, ln_ref[0:1, :], ln_ref[1:2, :])
    o_ref[...] = x1
    h = x1 * (1.0 + mod_ref[4:5, :]) + mod_ref[3:4, :]
    h_hi = h.astype(BF16)
    h_ref[...] = h_hi
    h_lo = (h - h_hi.astype(F32)).astype(BF16)
    both = _bdot(h_hi, rw_ref[...])
    logits = both[:, :LANES] + both[:, LANES:] + _bdot(h_lo, rw_ref[:, :LANES]) + rb_ref[...]
    lane = lax.broadcasted_iota(jnp.int32, logits.shape, 1).astype(F32)
    t1 = jnp.max(logits, axis=-1, keepdims=True)
    i1 = jnp.min(jnp.where(logits == t1, lane, float(LANES)), axis=-1, keepdims=True)
    rest = jnp.where(lane == i1, -jnp.inf, logits)
    t2 = jnp.max(rest, axis=-1, keepdims=True)
    i2 = jnp.min(jnp.where(rest == t2, lane, float(LANES)), axis=-1, keepdims=True)
    e = jnp.exp(t2 - t1)
    den = 1.0 + e
    route_ref[...] = jnp.where(lane == 0.0, i1, jnp.where(lane == 1.0, i2,
                               jnp.where(lane == 2.0, 1.0 / den, jnp.where(lane == 3.0, e / den, 0.0))))


def _conv_out(x, b, z, mod_l, cw, w_out, ln, rw, rb, j):
    tok = pl.BlockSpec((TM_FF, D), lambda i: (i, 0))
    sub = TM_FF // 8
    return pl.pallas_call(
        _conv_out_kernel,
        grid=(T // TM_FF,),
        in_specs=[tok, tok, tok,
                  pl.BlockSpec((8, D), lambda i: (jnp.maximum(i * sub - 1, 0), 0)),
                  pl.BlockSpec((8, D), lambda i: (jnp.minimum((i + 1) * sub, T // 8 - 1), 0)),
                  pl.BlockSpec((None, 8, D), lambda i: (_group_of_tile(i, TM_FF), 0, 0)),
                  pl.BlockSpec((8, D), lambda i: (0, 0)), pl.BlockSpec((None, D, D), lambda i: (j, 0, 0)),
                  pl.BlockSpec((8, D), lambda i: (0, 0)),
                  pl.BlockSpec((D, 2 * LANES), lambda i: (0, 0)), pl.BlockSpec((1, LANES), lambda i: (0, 0))],
        out_specs=[tok, tok, pl.BlockSpec((TM_FF, LANES), lambda i: (i, 0))],
        out_shape=[jax.ShapeDtypeStruct((T, D), F32), jax.ShapeDtypeStruct((T, D), BF16),
                   jax.ShapeDtypeStruct((T, LANES), F32)],
        compiler_params=_cparams(("parallel",)),
        name="conv_out",
    )(x, b, z, z, z, mod_l, cw, w_out, ln, rw, rb)


def _moe_up_kernel(te_ref, nv_ref, x_ref, wg_ref, wu_ref, a_ref):
    @pl.when(pl.program_id(1) < nv_ref[0])
    def _():
        h = x_ref[...]
        g = _wdot(h, wg_ref[...])
        u = _wdot(h, wu_ref[...])
        a_ref[...] = (g * _sigmoid(g) * u).astype(BF16)

    @pl.when(pl.program_id(1) >= nv_ref[0])
    def _():
        a_ref[...] = jnp.zeros_like(a_ref)


def _moe_down_kernel(te_ref, nv_ref, a_ref, wd_ref, o_ref):
    @pl.when(pl.program_id(0) < nv_ref[0])
    def _():
        o_ref[...] = _wdot(a_ref[...], wd_ref[...])

    @pl.when(pl.program_id(0) >= nv_ref[0])
    def _():
        o_ref[...] = jnp.zeros_like(o_ref)


def _moe(tile_expert, n_valid, xs, wg, wu, wd, j):
    n_tiles = NP_ROWS // TM_FF
    act = pl.pallas_call(
        _moe_up_kernel,
        grid_spec=pltpu.PrefetchScalarGridSpec(
            num_scalar_prefetch=2, grid=(D_FF // TF, n_tiles),
            in_specs=[pl.BlockSpec((TM_FF, D), lambda f, i, te, nv: (i, 0)),
                      pl.BlockSpec((None, None, D, TF), lambda f, i, te, nv: (j, te[i], 0, f)),
                      pl.BlockSpec((None, None, D, TF), lambda f, i, te, nv: (j, te[i], 0, f))],
            out_specs=pl.BlockSpec((TM_FF, TF), lambda f, i, te, nv: (i, f))),
        out_shape=jax.ShapeDtypeStruct((NP_ROWS, D_FF), BF16),
        compiler_params=_cparams(("arbitrary", "arbitrary")),
        name="moe_up",
    )(tile_expert, n_valid, xs, wg, wu)
    return pl.pallas_call(
        _moe_down_kernel,
        grid_spec=pltpu.PrefetchScalarGridSpec(
            num_scalar_prefetch=2, grid=(n_tiles,),
            in_specs=[pl.BlockSpec((TM_FF, D_FF), lambda i, te, nv: (i, 0)),
                      pl.BlockSpec((None, None, D_FF, D), lambda i, te, nv: (j, te[i], 0, 0))],
            out_specs=pl.BlockSpec((TM_FF, D), lambda i, te, nv: (i, 0))),
        out_shape=jax.ShapeDtypeStruct((NP_ROWS, D), F32),
        compiler_params=_cparams(("arbitrary",)),
        name="moe_down",
    )(tile_expert, n_valid, act, wd)


def _combine_kernel(split, x_ref, o0_ref, o1_ref, route_ref, mod_ref, ln_ref, *o_refs):
    y = route_ref[:, 2:3] * o0_ref[...] + route_ref[:, 3:4] * o1_ref[...]
    out = _layer_norm(ALPHA * x_ref[...] + mod_ref[5:6, :] * y, ln_ref[2:3, :], ln_ref[3:4, :])
    if not split:
        o_refs[0][...] = out
        return
    is_latent = pl.program_id(0) >= TP // TM_FF

    @pl.when(jnp.logical_not(is_latent))
    def _():
        o_refs[0][...] = out

    @pl.when(is_latent)
    def _():
        o_refs[1][...] = out


def _combine(x, o0, o1, route, mod_l, ln, split):
    tok = pl.BlockSpec((TM_FF, D), lambda i: (i, 0))
    if split:
        out_specs = [pl.BlockSpec((TM_FF, D), lambda i: (jnp.minimum(i, TP // TM_FF - 1), 0)),
                     pl.BlockSpec((TM_FF, D), lambda i: (jnp.maximum(i - TP // TM_FF, 0), 0))]
        out_shape = [jax.ShapeDtypeStruct((TP, D), F32), jax.ShapeDtypeStruct((TS, D), F32)]
    else:
        out_specs, out_shape = tok, jax.ShapeDtypeStruct((T, D), F32)
    return pl.pallas_call(
        functools.partial(_combine_kernel, split),
        grid=(T // TM_FF,),
        in_specs=[tok, tok, tok, pl.BlockSpec((TM_FF, LANES), lambda i: (i, 0)),
                  pl.BlockSpec((None, 8, D), lambda i: (_group_of_tile(i, TM_FF), 0, 0)),
                  pl.BlockSpec((8, D), lambda i: (0, 0))],
        out_specs=out_specs,
        out_shape=out_shape,
        compiler_params=_cparams(("arbitrary",)),
        name="moe_combine",
    )(x, o0, o1, route, mod_l, ln)


def _routing_plan(route):
    e = jnp.concatenate([route[:, 0], route[:, 1]]).astype(jnp.int32)
    onehot = (e[:, None] == jnp.arange(N_EXP, dtype=jnp.int32)[None, :]).astype(jnp.int32)
    csum = jnp.cumsum(onehot, axis=0)
    counts = csum[-1]
    rank = jnp.sum((csum - onehot) * onehot, axis=1)
    padded = (counts + TM_FF - 1) // TM_FF * TM_FF
    pend = jnp.cumsum(padded)
    dest = jnp.sum(onehot * (pend - padded)[None, :], axis=1) + rank
    order = jnp.argsort(e, stable=True).astype(jnp.int32)
    rows = jnp.arange(NP_ROWS, dtype=jnp.int32)
    before = (rows[:, None] >= pend[None, :]).astype(jnp.int32)
    row_e = jnp.minimum(jnp.sum(before, axis=1), N_EXP - 1)
    row_cnt = jnp.sum((row_e[:, None] == jnp.arange(N_EXP, dtype=jnp.int32)[None, :]) * counts[None, :], axis=1)
    q = jnp.clip(rows - jnp.sum(before * padded[None, :], axis=1), 0, jnp.maximum(row_cnt - 1, 0))
    src = jnp.minimum(jnp.sum(before * counts[None, :], axis=1) + q, 2 * T - 1)
    row_token = order[src] % T
    n_valid = (pend[-1] // TM_FF).astype(jnp.int32)
    tile_start = jnp.minimum(jnp.arange(NP_ROWS // TM_FF, dtype=jnp.int32), n_valid - 1) * TM_FF
    tile_expert = jnp.minimum(jnp.sum((tile_start[:, None] >= pend[None, :]).astype(jnp.int32), axis=1), N_EXP - 1)
    return dest[:T], dest[T:], row_token, tile_expert.astype(jnp.int32), n_valid.reshape(1)


_INV_PERM_RDK = np.argsort(np.concatenate([np.arange(0, RDK, 2), np.arange(1, RDK, 2)]))


def _pad_cols(a, width):
    return jnp.pad(a, ((0, 0), (0, width - a.shape[1])))


def _deinterleave(a):
    n = a.shape[-1]
    return jnp.swapaxes(a.reshape(a.shape[:-1] + (n // 2, 2)), -1, -2).reshape(a.shape)


def _prep_even_weights(w_in, w_q_b, w_kv_b):
    o_kpe = Q_LORA + KV_LORA
    o_rq, o_rk, o_rv = o_kpe + ROPE, o_kpe + ROPE + RET_W, o_kpe + ROPE + 2 * RET_W
    heads = lambda a: _deinterleave(a.reshape(D, HEADS, RDK)).reshape(D, HEADS * RDK)
    kpe = w_in[:, o_kpe:o_kpe + ROPE]
    w_in_p = jnp.concatenate([
        w_in[:, :o_kpe], heads(w_in[:, o_rq:o_rk]), heads(w_in[:, o_rk:o_rv]) * (RDK ** -0.5), w_in[:, o_rv:],
        _pad_cols(kpe, LANES), _pad_cols(_deinterleave(kpe), LANES),
        _pad_cols(jnp.concatenate([-kpe[:, 1::2], kpe[:, 0::2]], axis=1), LANES)], axis=1).astype(BF16)
    wq = w_q_b.reshape(Q_LORA, HEADS, NOPE + ROPE)
    wq = jnp.concatenate([wq[:, :, :NOPE], _deinterleave(wq[:, :, NOPE:]),
                          jnp.zeros((Q_LORA, HEADS, LANES - NOPE - ROPE), F32)], axis=2)
    wq = wq.reshape(Q_LORA, HEADS * LANES).astype(BF16)
    wkv = w_kv_b.reshape(KV_LORA, HEADS, NOPE + VDIM)
    zero = jnp.zeros((KV_LORA, HEADS, LANES - NOPE), F32)
    wk = jnp.concatenate([wkv[:, :, :NOPE], zero], axis=2).reshape(KV_LORA, HEADS * LANES)
    wv = wkv[:, :, NOPE:].reshape(KV_LORA, HEADS // 2, 2, VDIM)
    zv = jnp.zeros((KV_LORA, HEADS // 2, VDIM), F32)
    wv = jnp.stack([jnp.concatenate([wv[:, :, 0], zv], axis=2), jnp.concatenate([zv, wv[:, :, 1]], axis=2)], axis=2)
    wkv_p = jnp.concatenate([wk, wv.reshape(KV_LORA, HEADS * LANES)], axis=1).astype(BF16)
    return w_in_p, wq, wkv_p


def _placement():
    ek = np.zeros((LANES, HEADS * LANES), np.float32)
    for h in range(HEADS):
        ek[np.arange(ROPE), h * LANES + NOPE + np.arange(ROPE)] = 1.0
    return jnp.asarray(ek, BF16)


def _rotary_tables():
    rows = LAT_LEN // GRID_W
    r, col = jnp.meshgrid(jnp.arange(rows, dtype=F32), jnp.arange(GRID_W, dtype=F32), indexing='ij')
    n_freq = ROPE // 4
    freqs = 1.0 / (10000.0 ** (jnp.arange(n_freq, dtype=F32) / n_freq))
    ang = jnp.concatenate([r.reshape(-1)[:, None] * freqs, col.reshape(-1)[:, None] * freqs], axis=-1)
    cos, sin = jnp.cos(ang), jnp.sin(ang)
    theta = 1.0 / (10000.0 ** jnp.linspace(0.0, 1.0, RDK // 2, dtype=F32))
    rang = jnp.arange(LAT_LEN, dtype=F32)[:, None] * theta
    rcos, rsin = jnp.cos(rang), jnp.sin(rang)
    one = lambda w: jnp.ones((LAT_LEN, w), F32)
    zero = lambda w: jnp.zeros((LAT_LEN, w), F32)
    lat = [jnp.concatenate([one(NOPE), cos, cos, one(LANES - NOPE - ROPE)], axis=1),
           jnp.concatenate([zero(NOPE), sin, sin, zero(LANES - NOPE - ROPE)], axis=1),
           jnp.concatenate([cos, cos, zero(LANES - ROPE)], axis=1),
           jnp.concatenate([sin, sin, zero(LANES - ROPE)], axis=1),
           jnp.concatenate([rcos] * 4, axis=1), jnp.concatenate([rsin] * 4, axis=1)]
    ident = [np.ones((TM, LANES), np.float32), np.zeros((TM, LANES), np.float32)]
    ident_k = np.concatenate([np.ones((TM, ROPE), np.float32), np.zeros((TM, LANES - ROPE), np.float32)], axis=1)
    ident = [ident[0], ident[1], ident_k, ident[1], ident[0], ident[1]]
    return [jnp.concatenate([l, jnp.asarray(c)], axis=0) for l, c in zip(lat, ident)]


def _block_diag_states(s0):
    s = jnp.swapaxes(_deinterleave(jnp.swapaxes(s0, -1, -2)), -1, -2)
    s = s.reshape(s0.shape[0], HEADS // 2, 2, RDK, RDK)
    z = jnp.zeros_like(s[:, :, 0])
    top = jnp.concatenate([s[:, :, 0], z], axis=-1)
    bot = jnp.concatenate([z, s[:, :, 1]], axis=-1)
    return jnp.concatenate([top, bot], axis=-2)


def _unpermute_matrix():
    m = np.zeros((LANES, LANES), np.float32)
    for blk in range(LANES // RDK):
        m[blk * RDK + np.arange(RDK), blk * RDK + _INV_PERM_RDK] = 1.0
    return jnp.asarray(m, BF16)


def kernel(x_prompt, x_sample, c, cache_ckv, cache_kpe, state_ret_fwd, state_ret_bwd, c_ctx, w_mod, b_mod, ln_g, ln_b, w_in_mix, q_a_gain, kv_a_gain, w_q_b, w_kv_b, ret_decay_fwd, ret_decay_bwd, w_out_mix, w_in_conv, conv_w, w_out_conv, ffn_gate, ffn_up, ffn_down, router_w, router_b, exp_gate, exp_up, exp_down):
    x = (x_prompt.reshape(TP, D), x_sample.reshape(TS, D))
    cond8 = jnp.concatenate([c_ctx[None], c, jnp.zeros((8 - 1 - N_LAT_SEQ, D), F32)], axis=0)
    mods = _modulation(cond8, w_mod, b_mod)
    mods = jnp.pad(mods.reshape(DEPTH, N_GROUPS, 6, D), ((0, 0), (0, 0), (0, 2), (0, 0)))
    ln = jnp.pad(jnp.concatenate([ln_g, ln_b], axis=1)[:, jnp.array([0, 2, 1, 3])], ((0, 0), (0, 4), (0, 0)))
    tabs = _rotary_tables()
    ek = _placement()
    unperm = _unpermute_matrix()
    bf = lambda a: a.astype(BF16)
    w_out_mix_b, w_in_conv_b, w_out_conv_b = bf(w_out_mix), bf(w_in_conv), bf(w_out_conv)
    ffn_b = (bf(ffn_gate), bf(ffn_up), bf(ffn_down))
    exp_b = (exp_gate, exp_up, exp_down)
    cache, states = None, None
    for layer in range(DEPTH):
        j = layer // 2
        mod_l, ln_l = mods[layer], ln[layer]
        if layer % 2 == 0:
            w_in_p, wq, wkv = _prep_even_weights(w_in_mix[j], w_q_b[j], w_kv_b[j])
            q, k, v, cache, rq, rk, rv, rg = _even_in(
                x, mod_l, w_in_p, q_a_gain[j][None], kv_a_gain[j][None], wq, wkv, ek, tabs, cache)
            kpe_c = _pad_cols(_deinterleave(cache_kpe[:, j]).reshape(N_LAT_SEQ * PAST, ROPE), LANES)
            kc, vc = _ctx_kv(cache_ckv[:, j].reshape(N_LAT_SEQ * PAST, KV_LORA), kpe_c, wkv, ek)
            attn = _attention(q, k, v, kc, vc)
            lg = LOG2E * jnp.concatenate([jax.nn.log_sigmoid(ret_decay_fwd[j].astype(F32)),
                                          jax.nn.log_sigmoid(ret_decay_bwd[j].astype(F32))])
            ret, sf, sb = _retention(lg, rq, rk, rv, rg, _block_diag_states(state_ret_fwd[:, j]),
                                     _block_diag_states(state_ret_bwd[:, j]), unperm, states)
            states = (sf, sb)
            x = _mix_ffn(x, attn, ret, mod_l, w_out_mix_b, *ffn_b, ln_l, j)
        else:
            b, z = _conv_in(x, mod_l, w_in_conv_b, j)
            cw = jnp.pad(conv_w[j], ((0, 5), (0, 0)))
            rw = _pad_cols(router_w[j], LANES)
            rw_hi = rw.astype(BF16)
            rw = jnp.concatenate([rw_hi, (rw - rw_hi.astype(F32)).astype(BF16)], axis=1)
            rb = jnp.concatenate([router_b[j].astype(F32), jnp.full((LANES - N_EXP,), -1e30, F32)])[None]
            x, h, route = _conv_out(x, b, z, mod_l, cw, w_out_conv_b, ln_l, rw, rb, j)
            dest0, dest1, row_token, tile_expert, n_valid = _routing_plan(route)
            out_sorted = _moe(tile_expert, n_valid, h[row_token], *exp_b, j)
            x = _combine(x, out_sorted[dest0], out_sorted[dest1], route, mod_l, ln_l, split=layer == DEPTH - 1)
    y_prompt = x[0].reshape(N_PROMPT_SEQ, PROMPT_LEN, D)
    y_sample = x[1].reshape(N_LAT_SEQ, LAT_LEN, D)
    return (y_prompt, y_sample, cache[0], cache[1], states[0], states[1])
```

```python
import functools

import numpy as np
import jax
import jax.numpy as jnp
from jax import lax
from jax.experimental import pallas as pl
from jax.experimental.pallas import tpu as pltpu

F32 = jnp.float32
BF16 = jnp.bfloat16

D = 1024
DEPTH = 4
N_PROMPT_SEQ, PROMPT_LEN = 32, 256
N_LAT_SEQ, LAT_LEN = 2, 2048
PAST = 512
GRID_W = 64
TP = N_PROMPT_SEQ * PROMPT_LEN
TS = N_LAT_SEQ * LAT_LEN
T = TP + TS
HEADS = 8
NOPE, ROPE, VDIM = 64, 32, 64
Q_LORA, KV_LORA = 384, 256
RDK = 64
D_FF = 2816
N_EXP = 8
ALPHA = (2.0 * DEPTH) ** 0.25
LOG2E = float(np.log2(np.e))
Q_SCALE = float((NOPE + ROPE) ** -0.5) * LOG2E
LANES = 128
N_GROUPS = 8

TM = 512
TM_FF = 512
TF = D_FF // 2
TQ = 256
TQA = 512
NP_ROWS = 2 * T + N_EXP * TM_FF
VMEM_LIMIT = 56 * 1024 * 1024

RET_W = HEADS * RDK
COL_KV = Q_LORA
COL_RQ = COL_KV + KV_LORA
COL_RK, COL_RV, COL_RG = COL_RQ + RET_W, COL_RQ + 2 * RET_W, COL_RQ + 3 * RET_W
COL_KPE = COL_RQ + 4 * RET_W
IN_COLS = COL_KPE + 3 * LANES


def _cparams(sem):
    return pltpu.CompilerParams(dimension_semantics=sem, vmem_limit_bytes=VMEM_LIMIT)


def _group_of_tile(i, tm):
    per_seq = LAT_LEN // tm
    return jnp.maximum(i - TP // tm + per_seq, 0) // per_seq


def _bdot(a, b):
    return jnp.dot(a, b, preferred_element_type=F32)


def _wdot(a, w):
    return lax.dot_general(a, w, (((1,), (0,)), ((), ())), preferred_element_type=F32)


def _sigmoid(v):
    return 1.0 / (1.0 + jnp.exp(-v))


def _layer_norm(v, g, b):
    mu = jnp.mean(v, axis=-1, keepdims=True)
    d = v - mu
    var = jnp.mean(d * d, axis=-1, keepdims=True)
    return d * lax.rsqrt(var + 1e-5) * g + b


def _rms(v, g):
    return v * lax.rsqrt(jnp.mean(v * v, axis=-1, keepdims=True) + 1e-6) * g


def _mod_kernel(c_ref, w_ref, b_ref, o_ref):
    c = c_ref[...]
    s = (c * _sigmoid(c)).astype(BF16)
    o_ref[...] = _bdot(s, w_ref[...].astype(BF16)) + b_ref[...]


def _modulation(cond8, w_mod, b_mod):
    tn = 1536
    return pl.pallas_call(
        _mod_kernel,
        grid=(DEPTH, 6 * D // tn),
        in_specs=[pl.BlockSpec((8, D), lambda l, n: (0, 0)),
                  pl.BlockSpec((None, D, tn), lambda l, n: (l, 0, n)),
                  pl.BlockSpec((None, 1, tn), lambda l, n: (l, 0, n))],
        out_specs=pl.BlockSpec((None, 8, tn), lambda l, n: (l, 0, n)),
        out_shape=jax.ShapeDtypeStruct((DEPTH, 8, 6 * D), F32),
        compiler_params=_cparams(("arbitrary", "arbitrary")),
        name="modulation",
    )(cond8, w_mod, b_mod.reshape(DEPTH, 1, 6 * D))


def _swap_halves(a, half):
    n = a.shape[-1]
    lane = lax.broadcasted_iota(jnp.int32, a.shape, 1)
    first = (lane & (2 * half - 1)) < half
    return jnp.where(first, -pltpu.roll(a, n - half, axis=1), pltpu.roll(a, half, axis=1))


def _tile_rows(x_refs, tm):
    if len(x_refs) == 1:
        return x_refs[0][...]
    return jnp.where(pl.program_id(0) >= TP // tm, x_refs[1][...], x_refs[0][...])


def _tile_specs(x, tm):
    if not isinstance(x, tuple):
        return (x,), [pl.BlockSpec((tm, D), lambda i, *_: (i, 0))]
    return x, [pl.BlockSpec((tm, D), lambda i, *_: (jnp.minimum(i, TP // tm - 1), 0)),
               pl.BlockSpec((tm, D), lambda i, *_: (jnp.maximum(i - TP // tm, 0), 0))]


def _even_in_kernel(n_x, n_prev, *refs):
    x_refs, refs = refs[:n_x], refs[n_x:]
    (mod_ref, w_in_ref, qg_ref, kvg_ref, wq_ref, wkv_ref, ek_ref,
     cq_ref, sq_ref, ck_ref, sk_ref, cr_ref, sr_ref), refs = refs[:13], refs[13:]
    prev_refs, refs = refs[:2 * n_prev], refs[2 * n_prev:]
    q_ref, k_ref, v_ref, ckv_ref, kpe_ref, rq_ref, rk_ref, rv_ref, rg_ref = refs
    x = _tile_rows(x_refs, TM)
    h = (x * (1.0 + mod_ref[1:2, :]) + mod_ref[0:1, :]).astype(BF16)
    p = _bdot(h, w_in_ref[...])
    qn = _rms(p[:, 0:Q_LORA], qg_ref[...]).astype(BF16)
    qa = _bdot(qn, wq_ref[...])
    ckv = _rms(p[:, COL_KV:COL_RQ], kvg_ref[...])
    kv = _bdot(ckv.astype(BF16), wkv_ref[...])
    v_ref[...] = kv[:, HEADS * LANES:].astype(BF16)
    base = COL_KPE
    ka = p[:, base + LANES:base + 2 * LANES]
    kb = p[:, base + 2 * LANES:base + 3 * LANES]
    rq = p[:, COL_RQ:COL_RK]
    rk = p[:, COL_RK:COL_RV]
    rv_ref[...] = p[:, COL_RV:COL_RG].astype(BF16)
    rg_ref[...] = p[:, COL_RG:COL_KPE]
    lane = lax.broadcasted_iota(jnp.int32, qa.shape, 1) & (LANES - 1)
    qb = jnp.where(lane < NOPE + ROPE // 2,
                   -pltpu.roll(qa, qa.shape[1] - ROPE // 2, axis=1),
                   pltpu.roll(qa, ROPE // 2, axis=1))
    cq = jnp.concatenate([cq_ref[...]] * HEADS, axis=1)
    sq = jnp.concatenate([sq_ref[...]] * HEADS, axis=1)
    q_ref[...] = ((qa * cq + qb * sq) * Q_SCALE).astype(BF16)
    kpe_rot = ka * ck_ref[...] + kb * sk_ref[...]
    k_ref[...] = (kv[:, :HEADS * LANES] + _bdot(kpe_rot.astype(BF16), ek_ref[...])).astype(BF16)
    cr = jnp.concatenate([cr_ref[...]] * (RET_W // LANES), axis=1)
    sr = jnp.concatenate([sr_ref[...]] * (RET_W // LANES), axis=1)
    rq_ref[...] = (rq * cr + _swap_halves(rq, RDK // 2) * sr).astype(BF16)
    rk_ref[...] = (rk * cr + _swap_halves(rk, RDK // 2) * sr).astype(BF16)

    @pl.when(pl.program_id(0) < TP // TM)
    def _():
        for k in range(n_prev):
            ckv_ref[:, k] = prev_refs[2 * k][...]
            kpe_ref[:, k] = prev_refs[2 * k + 1][...]
        for s in range(TM // PROMPT_LEN):
            rows = slice(s * PROMPT_LEN, (s + 1) * PROMPT_LEN)
            ckv_ref[s, n_prev] = ckv[rows]
            kpe_ref[s, n_prev] = p[rows, base:base + ROPE]


def _even_in(x, mod_l, w_in, qg, kvg, wq, wkv, ek, tabs, prev_cache):
    x_args, x_specs = _tile_specs(x, TM)
    n_prev = 0 if prev_cache is None else prev_cache[0].shape[1]
    spt = TM // PROMPT_LEN
    tok = lambda w: pl.BlockSpec((TM, w), lambda i: (i, 0))
    full = lambda a: pl.BlockSpec(a.shape, lambda i: (0,) * a.ndim, pipeline_mode=pl.Buffered(1))
    lat_tiles = LAT_LEN // TM
    tab = pl.BlockSpec((TM, LANES), lambda i: (
        jnp.where(i < TP // TM, lat_tiles, jnp.maximum(i - TP // TM, 0) % lat_tiles), 0))
    seq = lambda i: (jnp.minimum(i, TP // TM - 1), 0, 0, 0)
    seq_k = lambda k, i: (jnp.minimum(i, TP // TM - 1), k, 0, 0)
    prev_args, prev_specs = [], []
    for k in range(n_prev):
        for a, w in zip(prev_cache, (KV_LORA, ROPE)):
            prev_args.append(a)
            prev_specs.append(pl.BlockSpec((spt, None, PROMPT_LEN, w), functools.partial(seq_k, k)))
    tok_outs = lambda dims: ([tok(w) for w, _ in dims], [jax.ShapeDtypeStruct((T, w), dt) for w, dt in dims])
    qkv_specs, qkv_shapes = tok_outs([(HEADS * LANES, BF16)] * 3)
    ret_specs, ret_shapes = tok_outs([(RET_W, BF16), (RET_W, BF16), (RET_W, BF16), (RET_W, F32)])
    cache_specs = [pl.BlockSpec((spt, n_prev + 1, PROMPT_LEN, w), seq) for w in (KV_LORA, ROPE)]
    cache_shapes = [jax.ShapeDtypeStruct((N_PROMPT_SEQ, n_prev + 1, PROMPT_LEN, w), F32) for w in (KV_LORA, ROPE)]
    q, k, v, ckv, kpe, rq, rk, rv, rg = pl.pallas_call(
        functools.partial(_even_in_kernel, len(x_args), n_prev),
        grid=(T // TM,),
        in_specs=x_specs + [pl.BlockSpec((None, 8, D), lambda i: (_group_of_tile(i, TM), 0, 0)),
                            full(w_in), full(qg), full(kvg), full(wq), full(wkv), full(ek)] + [tab] * 6 + prev_specs,
        out_specs=qkv_specs + cache_specs + ret_specs,
        out_shape=qkv_shapes + cache_shapes + ret_shapes,
        compiler_params=_cparams(("arbitrary",)),
        name="even_in",
    )(*x_args, mod_l, w_in, qg, kvg, wq, wkv, ek, *tabs, *prev_args)
    return q, k, v, (ckv, kpe), rq, rk, rv, rg


def _ctx_kv_kernel(ckv_ref, kpe_ref, wkv_ref, ek_ref, k_ref, v_ref):
    kv = _bdot(ckv_ref[...].astype(BF16), wkv_ref[...])
    k_ref[...] = (kv[:, :HEADS * LANES] + _bdot(kpe_ref[...].astype(BF16), ek_ref[...])).astype(BF16)
    v_ref[...] = kv[:, HEADS * LANES:].astype(BF16)


def _ctx_kv(ckv_c, kpe_c, wkv, ek):
    n = ckv_c.shape[0]
    full = lambda a: pl.BlockSpec(a.shape, lambda i: (0,) * a.ndim)
    return pl.pallas_call(
        _ctx_kv_kernel,
        grid=(n // PAST,),
        in_specs=[pl.BlockSpec((PAST, KV_LORA), lambda i: (i, 0)), pl.BlockSpec((PAST, LANES), lambda i: (i, 0)),
                  full(wkv), full(ek)],
        out_specs=[pl.BlockSpec((PAST, HEADS * LANES), lambda i: (i, 0))] * 2,
        out_shape=[jax.ShapeDtypeStruct((n, HEADS * LANES), BF16)] * 2,
        compiler_params=_cparams(("parallel",)),
        name="ctx_kv",
    )(ckv_c, kpe_c, wkv, ek)


def _attn_kernel(n_kv, q_ref, *refs):
    k_refs = refs[0:2 * n_kv:2]
    v_refs = refs[1:2 * n_kv:2]
    o_ref = refs[2 * n_kv]
    nt = (((1,), (1,)), ((), ()))
    for pair in range(HEADS // 2):
        acc = None
        for sub in range(2):
            sl = slice((2 * pair + sub) * LANES, (2 * pair + sub + 1) * LANES)
            qh = q_ref[:, sl]
            s = [lax.dot_general(qh, k[:, sl], nt, preferred_element_type=F32) for k in k_refs]
            m = functools.reduce(jnp.maximum, [jnp.max(a, axis=-1, keepdims=True) for a in s])
            e = [jnp.exp2(a - m) for a in s]
            den = functools.reduce(jnp.add, [jnp.sum(a, axis=-1, keepdims=True) for a in e])
            o = functools.reduce(jnp.add, [_bdot(a.astype(BF16), v[:, sl]) for a, v in zip(e, v_refs)])
            o = o / den
            acc = o if acc is None else acc + o
        o_ref[:, pair * LANES:(pair + 1) * LANES] = acc.astype(BF16)


def _latent_seq(i):
    return jnp.maximum(i - TP // TQA, 0) // (LAT_LEN // TQA)


def _attn_tiles_kernel(q_ref, kp_ref, vp_ref, kc_ref, vc_ref, kl_ref, vl_ref, o_ref):
    is_latent = pl.program_id(0) >= TP // TQA

    @pl.when(jnp.logical_not(is_latent))
    def _():
        for s in range(TQA // PROMPT_LEN):
            rows = pl.ds(s * PROMPT_LEN, PROMPT_LEN)
            _attn_kernel(1, q_ref.at[rows], kp_ref.at[rows], vp_ref.at[rows], o_ref.at[rows])

    @pl.when(is_latent)
    def _():
        _attn_kernel(2, q_ref, kc_ref, vc_ref, kl_ref, vl_ref, o_ref)


def _attention(q, k, v, kc, vc):
    w = HEADS * LANES
    tile = lambda i: (i, 0)
    ctx_own = lambda i: (jnp.minimum(i, TP // TQA - 1), 0)
    cache = lambda i: (_latent_seq(i), 0)
    lat_own = lambda i: (TP // LAT_LEN + _latent_seq(i), 0)
    once = lambda rows, index: pl.BlockSpec((rows, w), index, pipeline_mode=pl.Buffered(1))
    return pl.pallas_call(
        _attn_tiles_kernel,
        grid=(T // TQA,),
        in_specs=[pl.BlockSpec((TQA, w), tile),
                  pl.BlockSpec((TQA, w), ctx_own), pl.BlockSpec((TQA, w), ctx_own),
                  once(PAST, cache), once(PAST, cache), once(LAT_LEN, lat_own), once(LAT_LEN, lat_own)],
        out_specs=pl.BlockSpec((TQA, HEADS * VDIM), tile),
        out_shape=jax.ShapeDtypeStruct((T, HEADS * VDIM), BF16),
        compiler_params=_cparams(("arbitrary",)),
        name="attention",
    )(q, k, v, kc, vc, k, v)


def _ret_prefix_kernel(lg_ref, rk_ref, rv_ref, s0f_ref, s0b_ref, pf_ref, qb_ref, sf_scr, sb_scr):
    s = pl.program_id(0)
    n_tiles, per_seq = TS // TQ, LAT_LEN // TQ
    row = lax.broadcasted_iota(jnp.int32, (LANES, 1), 0)
    lane = lax.broadcasted_iota(jnp.int32, (1, LANES), 1)
    top, lo = row < RDK, lane < RDK
    same_head = top == lo
    m_col = lax.broadcasted_iota(jnp.int32, (TQ, 1), 0).astype(F32)

    def scan_step(first, lg_off, s0_ref, scr, out_ref, pos):
        @pl.when(first)
        def _():
            scr[...] = s0_ref[...]

        for pair in range(HEADS // 2):
            sl = slice(pair * LANES, (pair + 1) * LANES)
            lg_even, lg_odd = lg_ref[lg_off + 2 * pair], lg_ref[lg_off + 2 * pair + 1]
            dec = jnp.exp2(pos * jnp.where(lo, lg_even, lg_odd))
            local = _bdot((rk_ref[:, sl].astype(F32) * dec).T.astype(BF16), rv_ref[:, sl])
            out_ref[pair] = scr[pair].astype(BF16)
            tile_decay = jnp.exp2(float(TQ) * jnp.where(top, lg_even, lg_odd))
            scr[pair] = scr[pair] * tile_decay + jnp.where(same_head, local, 0.0)

    @pl.when(s < n_tiles)
    def _():
        scan_step(s % per_seq == 0, 0, s0f_ref, sf_scr, pf_ref, TQ - 1.0 - m_col)

    @pl.when(s >= n_tiles)
    def _():
        scan_step((2 * n_tiles - 1 - s) % per_seq == per_seq - 1, HEADS, s0b_ref, sb_scr, qb_ref, m_col)


def _ret_prefix(lg, rk, rv, s0f, s0b):
    n_tiles, per_seq = TS // TQ, LAT_LEN // TQ
    tile_of = lambda s: jnp.where(s < n_tiles, s, 2 * n_tiles - 1 - s)
    st = pl.BlockSpec((None, HEADS // 2, LANES, LANES), lambda s, lg: (tile_of(s) // per_seq, 0, 0, 0))
    kv = pl.BlockSpec((TQ, HEADS * RDK), lambda s, lg: (TP // TQ + tile_of(s), 0))
    pf_blk = pl.BlockSpec((None, HEADS // 2, LANES, LANES), lambda s, lg: (jnp.minimum(s, n_tiles - 1), 0, 0, 0))
    qb_blk = pl.BlockSpec((None, HEADS // 2, LANES, LANES),
                          lambda s, lg: (jnp.minimum(2 * n_tiles - 1 - s, n_tiles - 1), 0, 0, 0))
    shape = jax.ShapeDtypeStruct((n_tiles, HEADS // 2, LANES, LANES), BF16)
    return pl.pallas_call(
        _ret_prefix_kernel,
        grid_spec=pltpu.PrefetchScalarGridSpec(
            num_scalar_prefetch=1, grid=(2 * n_tiles,),
            in_specs=[kv, kv, st, st], out_specs=[pf_blk, qb_blk],
            scratch_shapes=[pltpu.VMEM((HEADS // 2, LANES, LANES), F32)] * 2),
        out_shape=[shape, shape],
        compiler_params=_cparams(("arbitrary",)),
        name="ret_prefix",
    )(lg, rk, rv, s0f, s0b)


def _retention_kernel(latent, lg_ref, rq_ref, rk_ref, rv_ref, rg_ref, *refs):
    q0 = 0
    seq_len = rq_ref.shape[0]
    if latent:
        s0f_ref, s0b_ref, o_ref = refs
    else:
        unperm_ref, o_ref, sf_ref, sb_ref = refs
    tq, tk = rq_ref.shape[0], rk_ref.shape[0]
    nt = (((1,), (1,)), ((), ()))
    n_idx = (q0 + lax.broadcasted_iota(jnp.int32, (tq, tk), 0)).astype(F32)
    m_idx = lax.broadcasted_iota(jnp.int32, (tq, tk), 1).astype(F32)
    dist = n_idx - m_idx
    adist = jnp.abs(dist)
    fwd = dist > 0.0
    diag = jnp.where(dist == 0.0, 1.0, 0.0)
    lane = lax.broadcasted_iota(jnp.int32, (1, LANES), 1)
    lo = lane < RDK
    n_col = (q0 + lax.broadcasted_iota(jnp.int32, (tq, 1), 0)).astype(F32)
    m_col = lax.broadcasted_iota(jnp.int32, (tk, 1), 0).astype(F32)
    for pair in range(HEADS // 2):
        sl = slice(pair * LANES, (pair + 1) * LANES)
        qb, kb, vb = rq_ref[:, sl], rk_ref[:, sl], rv_ref[:, sl]
        acc = jnp.zeros((tq, LANES), F32)
        for sub in range(2):
            h = 2 * pair + sub
            lgf, lgb = lg_ref[h], lg_ref[HEADS + h]
            half = lo if sub == 0 else jnp.logical_not(lo)
            qm = jnp.where(half, qb, jnp.zeros_like(qb))
            vm = jnp.where(half, vb, jnp.zeros_like(vb))
            s = lax.dot_general(qm, kb, nt, preferred_element_type=F32)
            w = jnp.exp2(adist * jnp.where(fwd, lgf, lgb)) + diag
            acc = acc + _bdot((s * w).astype(BF16), vm)
        lgf_l = jnp.where(lo, lg_ref[2 * pair], lg_ref[2 * pair + 1])
        lgb_l = jnp.where(lo, lg_ref[HEADS + 2 * pair], lg_ref[HEADS + 2 * pair + 1])
        if latent:
            acc = acc + _bdot(qb, s0f_ref[pair]) * jnp.exp2((n_col + 1.0) * lgf_l)
            acc = acc + _bdot(qb, s0b_ref[pair]) * jnp.exp2((seq_len - n_col) * lgb_l)
        else:
            v_swapped = pltpu.roll(vb.astype(F32), RDK, axis=1).astype(BF16)
            for st_ref, dec in ((sf_ref, jnp.exp2((seq_len - 1.0 - m_col) * lgf_l)),
                                (sb_ref, jnp.exp2(m_col * lgb_l))):
                kt = (kb.astype(F32) * dec).T.astype(BF16)
                kt = _bdot(unperm_ref[...], kt).astype(BF16)
                st_ref[2 * pair] = _bdot(kt, vb)[0:RDK, 0:RDK]
                st_ref[2 * pair + 1] = _bdot(kt, v_swapped)[RDK:, 0:RDK]
        inv = 1.0 / RDK
        mu = jnp.where(lo, jnp.sum(jnp.where(lo, acc, 0.0), axis=-1, keepdims=True),
                       jnp.sum(jnp.where(lo, 0.0, acc), axis=-1, keepdims=True)) * inv
        dlt = acc - mu
        d2 = dlt * dlt
        var = jnp.where(lo, jnp.sum(jnp.where(lo, d2, 0.0), axis=-1, keepdims=True),
                        jnp.sum(jnp.where(lo, 0.0, d2), axis=-1, keepdims=True)) * inv
        g = rg_ref[:, sl]
        o_ref[:, sl] = (dlt * lax.rsqrt(var + 1e-5) * (g * _sigmoid(g))).astype(BF16)


def _retention_tiles_kernel(n_prev, lg_ref, rq_ref, rk_ref, rv_ref, rg_ref, pf_ref, qb_ref, unperm_ref, *refs):
    prev_refs, (o_ref, sf_ref, sb_ref) = refs[:2 * n_prev], refs[2 * n_prev:]
    is_latent = pl.program_id(0) >= TP // TQA
    views = lambda s: [r.at[pl.ds(s * TQ, TQ)] for r in (rq_ref, rk_ref, rv_ref, rg_ref, o_ref)]

    @pl.when(jnp.logical_not(is_latent))
    def _():
        for k in range(n_prev):
            sf_ref[:, k] = prev_refs[2 * k][...]
            sb_ref[:, k] = prev_refs[2 * k + 1][...]
        for s in range(TQA // TQ):
            rq, rk, rv, rg, o = views(s)
            _retention_kernel(False, lg_ref, rq, rk, rv, rg, unperm_ref, o, sf_ref.at[s, n_prev], sb_ref.at[s, n_prev])

    @pl.when(is_latent)
    def _():
        for s in range(TQA // TQ):
            rq, rk, rv, rg, o = views(s)
            _retention_kernel(True, lg_ref, rq, rk, rv, rg, pf_ref.at[s], qb_ref.at[s], o)


def _retention(lg, rq, rk, rv, rg, s0f, s0b, unperm, prev_states):
    w = HEADS * RDK
    n_prev = 0 if prev_states is None else prev_states[0].shape[1]
    pf, qb = _ret_prefix(lg, rk, rv, s0f, s0b)
    assert TQ == PROMPT_LEN
    per_tile = TQA // TQ
    tile = lambda i, lg: (i, 0)
    s0_blk = pl.BlockSpec((per_tile, HEADS // 2, LANES, LANES),
                          lambda i, lg: (jnp.maximum(i - TP // TQA, 0), 0, 0, 0))
    seq = lambda i, lg: (jnp.minimum(i, TP // TQA - 1), 0, 0, 0, 0)
    st_blk = pl.BlockSpec((per_tile, n_prev + 1, HEADS, RDK, RDK), seq)
    st_shape = jax.ShapeDtypeStruct((N_PROMPT_SEQ, n_prev + 1, HEADS, RDK, RDK), F32)
    prev = () if prev_states is None else tuple(prev_states)
    prev_specs = [pl.BlockSpec((per_tile, None, HEADS, RDK, RDK),
                               functools.partial(lambda k, i, lg: (jnp.minimum(i, TP // TQA - 1), k, 0, 0, 0), k))
                  for k in range(n_prev) for _ in range(2)]
    prev_args = [p for k in range(n_prev) for p in prev]
    return pl.pallas_call(
        functools.partial(_retention_tiles_kernel, n_prev),
        grid_spec=pltpu.PrefetchScalarGridSpec(
            num_scalar_prefetch=1, grid=(T // TQA,),
            in_specs=[pl.BlockSpec((TQA, w), tile)] * 4 + [s0_blk, s0_blk,
                      pl.BlockSpec((LANES, LANES), lambda i, lg: (0, 0))] + prev_specs,
            out_specs=[pl.BlockSpec((TQA, w), tile), st_blk, st_blk]),
        out_shape=[jax.ShapeDtypeStruct((T, w), BF16), st_shape, st_shape],
        compiler_params=_cparams(("arbitrary",)),
        name="retention",
    )(lg, rq, rk, rv, rg, pf, qb, unperm, *prev_args)


def _mix_ffn_kernel(n_x, *refs):
    x_refs, refs = refs[:n_x], refs[n_x:]
    a_ref, r_ref, mod_ref, wo_ref, wg_ref, wu_ref, wd_ref, ln_ref, o_ref = refs
    half = HEADS * VDIM
    y = _bdot(a_ref[...], wo_ref[0:half, :]) + _bdot(r_ref[...], wo_ref[half:, :])
    x1 = _layer_norm(ALPHA * _tile_rows(x_refs, TM_FF) + mod_ref[2:3, :] * y, ln_ref[0:1, :], ln_ref[1:2, :])
    h = (x1 * (1.0 + mod_ref[4:5, :]) + mod_ref[3:4, :]).astype(BF16)
    acc = None
    for f in range(D_FF // TF):
        cols = slice(f * TF, (f + 1) * TF)
        g = _bdot(h, wg_ref[:, cols])
        u = _bdot(h, wu_ref[:, cols])
        part = _bdot((g * _sigmoid(g) * u).astype(BF16), wd_ref[cols, :])
        acc = part if acc is None else acc + part
    o_ref[...] = _layer_norm(ALPHA * x1 + mod_ref[5:6, :] * acc, ln_ref[2:3, :], ln_ref[3:4, :])


def _mix_ffn(x, attn, ret, mod_l, w_out, wg, wu, wd, ln, j):
    x_args, x_specs = _tile_specs(x, TM_FF)
    tok = lambda w: pl.BlockSpec((TM_FF, w), lambda i: (i, 0))
    resident = lambda a: pl.BlockSpec((None,) + a.shape[1:], lambda i: (j, 0, 0), pipeline_mode=pl.Buffered(1))
    return pl.pallas_call(
        functools.partial(_mix_ffn_kernel, len(x_args)),
        grid=(T // TM_FF,),
        in_specs=x_specs + [tok(HEADS * VDIM), tok(HEADS * RDK),
                  pl.BlockSpec((None, 8, D), lambda i: (_group_of_tile(i, TM_FF), 0, 0)),
                  resident(w_out), resident(wg), resident(wu), resident(wd), pl.BlockSpec((8, D), lambda i: (0, 0))],
        out_specs=tok(D),
        out_shape=jax.ShapeDtypeStruct((T, D), F32),
        compiler_params=_cparams(("arbitrary",)),
        name="mix_ffn",
    )(*x_args, attn, ret, mod_l, w_out, wg, wu, wd, ln)


def _conv_in_kernel(x_ref, mod_ref, w_ref, b_ref, z_ref):
    h = (x_ref[...] * (1.0 + mod_ref[1:2, :]) + mod_ref[0:1, :]).astype(BF16)
    p = _bdot(h, w_ref[...])
    b_ref[...] = p[:, 0:D]
    z_ref[...] = p[:, D:2 * D] * p[:, 2 * D:3 * D]


def _conv_in(x, mod_l, w_in, j):
    tok = pl.BlockSpec((TM_FF, D), lambda i: (i, 0))
    return pl.pallas_call(
        _conv_in_kernel,
        grid=(T // TM_FF,),
        in_specs=[tok, pl.BlockSpec((None, 8, D), lambda i: (_group_of_tile(i, TM_FF), 0, 0)),
                  pl.BlockSpec((None,) + w_in.shape[1:], lambda i: (j, 0, 0))],
        out_specs=[tok, tok],
        out_shape=[jax.ShapeDtypeStruct((T, D), F32)] * 2,
        compiler_params=_cparams(("parallel",)),
        name="conv_in",
    )(x, mod_l, w_in)


def _conv_out_kernel(x_ref, b_ref, z_ref, zp_ref, zn_ref, mod_ref, cw_ref, w_ref, ln_ref, rw_ref, rb_ref,
                     o_ref, h_ref, route_ref):
    i = pl.program_id(0)
    z = z_ref[...]
    tm = z.shape[0]
    row = lax.broadcasted_iota(jnp.int32, (tm, 1), 0)
    seq_len = jnp.where(i < TP // tm, PROMPT_LEN, LAT_LEN)
    pos = (i * tm + row) & (seq_len - 1)
    prev = jnp.where(row == 0, zp_ref[7:8, :], pltpu.roll(z, 1, axis=0))
    prev = jnp.where(pos == 0, 0.0, prev)
    nxt = jnp.where(row == tm - 1, zn_ref[0:1, :], pltpu.roll(z, tm - 1, axis=0))
    nxt = jnp.where(pos == seq_len - 1, 0.0, nxt)
    y = prev * cw_ref[0:1, :] + z * cw_ref[1:2, :] + nxt * cw_ref[2:3, :]
    t = _bdot((b_ref[...] * y).astype(BF16), w_ref[...])
    x1 = _layer_norm(ALPHA * x_ref[...] + mod_ref[2:3, :] * t, ln_ref[0:1, :], ln_ref[1:2, :])
    o_ref[...] = x1
    h = x1 * (1.0 + mod_ref[4:5, :]) + mod_ref[3:4, :]
    h_hi = h.astype(BF16)
    h_ref[...] = h_hi
    h_lo = (h - h_hi.astype(F32)).astype(BF16)
    both = _bdot(h_hi, rw_ref[...])
    logits = both[:, :LANES] + both[:, LANES:] + _bdot(h_lo, rw_ref[:, :LANES]) + rb_ref[...]
    lane = lax.broadcasted_iota(jnp.int32, logits.shape, 1).astype(F32)
    t1 = jnp.max(logits, axis=-1, keepdims=True)
    i1 = jnp.min(jnp.where(logits == t1, lane, float(LANES)), axis=-1, keepdims=True)
    rest = jnp.where(lane == i1, -jnp.inf, logits)
    t2 = jnp.max(rest, axis=-1, keepdims=True)
    i2 = jnp.min(jnp.where(rest == t2, lane, float(LANES)), axis=-1, keepdims=True)
    e = jnp.exp(t2 - t1)
    den = 1.0 + e
    route_ref[...] = jnp.where(lane == 0.0, i1, jnp.where(lane == 1.0, i2,
                               jnp.where(lane == 2.0, 1.0 / den, jnp.where(lane == 3.0, e / den, 0.0))))


def _conv_out(x, b, z, mod_l, cw, w_out, ln, rw, rb, j):
    tok = pl.BlockSpec((TM_FF, D), lambda i: (i, 0))
    sub = TM_FF // 8
    return pl.pallas_call(
        _conv_out_kernel,
        grid=(T // TM_FF,),
        in_specs=[tok, tok, tok,
                  pl.BlockSpec((8, D), lambda i: (jnp.maximum(i * sub - 1, 0), 0)),
                  pl.BlockSpec((8, D), lambda i: (jnp.minimum((i + 1) * sub, T // 8 - 1), 0)),
                  pl.BlockSpec((None, 8, D), lambda i: (_group_of_tile(i, TM_FF), 0, 0)),
                  pl.BlockSpec((8, D), lambda i: (0, 0)), pl.BlockSpec((None, D, D), lambda i: (j, 0, 0)),
                  pl.BlockSpec((8, D), lambda i: (0, 0)),
                  pl.BlockSpec((D, 2 * LANES), lambda i: (0, 0)), pl.BlockSpec((1, LANES), lambda i: (0, 0))],
        out_specs=[tok, tok, pl.BlockSpec((TM_FF, LANES), lambda i: (i, 0))],
        out_shape=[jax.ShapeDtypeStruct((T, D), F32), jax.ShapeDtypeStruct((T, D), BF16),
                   jax.ShapeDtypeStruct((T, LANES), F32)],
        compiler_params=_cparams(("parallel",)),
        name="conv_out",
    )(x, b, z, z, z, mod_l, cw, w_out, ln, rw, rb)


def _moe_up_kernel(te_ref, nv_ref, x_ref, wg_ref, wu_ref, a_ref):
    @pl.when(pl.program_id(1) < nv_ref[0])
    def _():
        h = x_ref[...]
        g = _wdot(h, wg_ref[...])
        u = _wdot(h, wu_ref[...])
        a_ref[...] = (g * _sigmoid(g) * u).astype(BF16)

    @pl.when(pl.program_id(1) >= nv_ref[0])
    def _():
        a_ref[...] = jnp.zeros_like(a_ref)


def _moe_down_kernel(te_ref, nv_ref, a_ref, wd_ref, o_ref):
    @pl.when(pl.program_id(0) < nv_ref[0])
    def _():
        o_ref[...] = _wdot(a_ref[...], wd_ref[...])

    @pl.when(pl.program_id(0) >= nv_ref[0])
    def _():
        o_ref[...] = jnp.zeros_like(o_ref)


def _moe(tile_expert, n_valid, xs, wg, wu, wd, j):
    n_tiles = NP_ROWS // TM_FF
    act = pl.pallas_call(
        _moe_up_kernel,
        grid_spec=pltpu.PrefetchScalarGridSpec(
            num_scalar_prefetch=2, grid=(D_FF // TF, n_tiles),
            in_specs=[pl.BlockSpec((TM_FF, D), lambda f, i, te, nv: (i, 0)),
                      pl.BlockSpec((None, None, D, TF), lambda f, i, te, nv: (j, te[i], 0, f)),
                      pl.BlockSpec((None, None, D, TF), lambda f, i, te, nv: (j, te[i], 0, f))],
            out_specs=pl.BlockSpec((TM_FF, TF), lambda f, i, te, nv: (i, f))),
        out_shape=jax.ShapeDtypeStruct((NP_ROWS, D_FF), BF16),
        compiler_params=_cparams(("arbitrary", "arbitrary")),
        name="moe_up",
    )(tile_expert, n_valid, xs, wg, wu)
    return pl.pallas_call(
        _moe_down_kernel,
        grid_spec=pltpu.PrefetchScalarGridSpec(
            num_scalar_prefetch=2, grid=(n_tiles,),
            in_specs=[pl.BlockSpec((TM_FF, D_FF), lambda i, te, nv: (i, 0)),
                      pl.BlockSpec((None, None, D_FF, D), lambda i, te, nv: (j, te[i], 0, 0))],
            out_specs=pl.BlockSpec((TM_FF, D), lambda i, te, nv: (i, 0))),
        out_shape=jax.ShapeDtypeStruct((NP_ROWS, D), F32),
        compiler_params=_cparams(("arbitrary",)),
        name="moe_down",
    )(tile_expert, n_valid, act, wd)


def _combine_kernel(split, x_ref, o0_ref, o1_ref, route_ref, mod_ref, ln_ref, *o_refs):
    y = route_ref[:, 2:3] * o0_ref[...] + route_ref[:, 3:4] * o1_ref[...]
    out = _layer_norm(ALPHA * x_ref[...] + mod_ref[5:6, :] * y, ln_ref[2:3, :], ln_ref[3:4, :])
    if not split:
        o_refs[0][...] = out
        return
    is_latent = pl.program_id(0) >= TP // TM_FF

    @pl.when(jnp.logical_not(is_latent))
    def _():
        o_refs[0][...] = out

    @pl.when(is_latent)
    def _():
        o_refs[1][...] = out


def _combine(x, o0, o1, route, mod_l, ln, split):
    tok = pl.BlockSpec((TM_FF, D), lambda i: (i, 0))
    if split:
        out_specs = [pl.BlockSpec((TM_FF, D), lambda i: (jnp.minimum(i, TP // TM_FF - 1), 0)),
                     pl.BlockSpec((TM_FF, D), lambda i: (jnp.maximum(i - TP // TM_FF, 0), 0))]
        out_shape = [jax.ShapeDtypeStruct((TP, D), F32), jax.ShapeDtypeStruct((TS, D), F32)]
    else:
        out_specs, out_shape = tok, jax.ShapeDtypeStruct((T, D), F32)
    return pl.pallas_call(
        functools.partial(_combine_kernel, split),
        grid=(T // TM_FF,),
        in_specs=[tok, tok, tok, pl.BlockSpec((TM_FF, LANES), lambda i: (i, 0)),
                  pl.BlockSpec((None, 8, D), lambda i: (_group_of_tile(i, TM_FF), 0, 0)),
                  pl.BlockSpec((8, D), lambda i: (0, 0))],
        out_specs=out_specs,
        out_shape=out_shape,
        compiler_params=_cparams(("arbitrary",)),
        name="moe_combine",
    )(x, o0, o1, route, mod_l, ln)


def _routing_plan(route):
    e = jnp.concatenate([route[:, 0], route[:, 1]]).astype(jnp.int32)
    onehot = (e[:, None] == jnp.arange(N_EXP, dtype=jnp.int32)[None, :]).astype(jnp.int32)
    csum = jnp.cumsum(onehot, axis=0)
    counts = csum[-1]
    rank = jnp.sum((csum - onehot) * onehot, axis=1)
    padded = (counts + TM_FF - 1) // TM_FF * TM_FF
    pend = jnp.cumsum(padded)
    dest = jnp.sum(onehot * (pend - padded)[None, :], axis=1) + rank
    order = jnp.argsort(e, stable=True).astype(jnp.int32)
    rows = jnp.arange(NP_ROWS, dtype=jnp.int32)
    before = (rows[:, None] >= pend[None, :]).astype(jnp.int32)
    row_e = jnp.minimum(jnp.sum(before, axis=1), N_EXP - 1)
    row_cnt = jnp.sum((row_e[:, None] == jnp.arange(N_EXP, dtype=jnp.int32)[None, :]) * counts[None, :], axis=1)
    q = jnp.clip(rows - jnp.sum(before * padded[None, :], axis=1), 0, jnp.maximum(row_cnt - 1, 0))
    src = jnp.minimum(jnp.sum(before * counts[None, :], axis=1) + q, 2 * T - 1)
    row_token = order[src] % T
    n_valid = (pend[-1] // TM_FF).astype(jnp.int32)
    tile_start = jnp.minimum(jnp.arange(NP_ROWS // TM_FF, dtype=jnp.int32), n_valid - 1) * TM_FF
    tile_expert = jnp.minimum(jnp.sum((tile_start[:, None] >= pend[None, :]).astype(jnp.int32), axis=1), N_EXP - 1)
    return dest[:T], dest[T:], row_token, tile_expert.astype(jnp.int32), n_valid.reshape(1)


_INV_PERM_RDK = np.argsort(np.concatenate([np.arange(0, RDK, 2), np.arange(1, RDK, 2)]))


def _pad_cols(a, width):
    return jnp.pad(a, ((0, 0), (0, width - a.shape[1])))


def _deinterleave(a):
    n = a.shape[-1]
    return jnp.swapaxes(a.reshape(a.shape[:-1] + (n // 2, 2)), -1, -2).reshape(a.shape)


def _prep_even_weights(w_in, w_q_b, w_kv_b):
    o_kpe = Q_LORA + KV_LORA
    o_rq, o_rk, o_rv = o_kpe + ROPE, o_kpe + ROPE + RET_W, o_kpe + ROPE + 2 * RET_W
    heads = lambda a: _deinterleave(a.reshape(D, HEADS, RDK)).reshape(D, HEADS * RDK)
    kpe = w_in[:, o_kpe:o_kpe + ROPE]
    w_in_p = jnp.concatenate([
        w_in[:, :o_kpe], heads(w_in[:, o_rq:o_rk]), heads(w_in[:, o_rk:o_rv]) * (RDK ** -0.5), w_in[:, o_rv:],
        _pad_cols(kpe, LANES), _pad_cols(_deinterleave(kpe), LANES),
        _pad_cols(jnp.concatenate([-kpe[:, 1::2], kpe[:, 0::2]], axis=1), LANES)], axis=1).astype(BF16)
    wq = w_q_b.reshape(Q_LORA, HEADS, NOPE + ROPE)
    wq = jnp.concatenate([wq[:, :, :NOPE], _deinterleave(wq[:, :, NOPE:]),
                          jnp.zeros((Q_LORA, HEADS, LANES - NOPE - ROPE), F32)], axis=2)
    wq = wq.reshape(Q_LORA, HEADS * LANES).astype(BF16)
    wkv = w_kv_b.reshape(KV_LORA, HEADS, NOPE + VDIM)
    zero = jnp.zeros((KV_LORA, HEADS, LANES - NOPE), F32)
    wk = jnp.concatenate([wkv[:, :, :NOPE], zero], axis=2).reshape(KV_LORA, HEADS * LANES)
    wv = wkv[:, :, NOPE:].reshape(KV_LORA, HEADS // 2, 2, VDIM)
    zv = jnp.zeros((KV_LORA, HEADS // 2, VDIM), F32)
    wv = jnp.stack([jnp.concatenate([wv[:, :, 0], zv], axis=2), jnp.concatenate([zv, wv[:, :, 1]], axis=2)], axis=2)
    wkv_p = jnp.concatenate([wk, wv.reshape(KV_LORA, HEADS * LANES)], axis=1).astype(BF16)
    return w_in_p, wq, wkv_p


def _placement():
    ek = np.zeros((LANES, HEADS * LANES), np.float32)
    for h in range(HEADS):
        ek[np.arange(ROPE), h * LANES + NOPE + np.arange(ROPE)] = 1.0
    return jnp.asarray(ek, BF16)


def _rotary_tables():
    rows = LAT_LEN // GRID_W
    r, col = jnp.meshgrid(jnp.arange(rows, dtype=F32), jnp.arange(GRID_W, dtype=F32), indexing='ij')
    n_freq = ROPE // 4
    freqs = 1.0 / (10000.0 ** (jnp.arange(n_freq, dtype=F32) / n_freq))
    ang = jnp.concatenate([r.reshape(-1)[:, None] * freqs, col.reshape(-1)[:, None] * freqs], axis=-1)
    cos, sin = jnp.cos(ang), jnp.sin(ang)
    theta = 1.0 / (10000.0 ** jnp.linspace(0.0, 1.0, RDK // 2, dtype=F32))
    rang = jnp.arange(LAT_LEN, dtype=F32)[:, None] * theta
    rcos, rsin = jnp.cos(rang), jnp.sin(rang)
    one = lambda w: jnp.ones((LAT_LEN, w), F32)
    zero = lambda w: jnp.zeros((LAT_LEN, w), F32)
    lat = [jnp.concatenate([one(NOPE), cos, cos, one(LANES - NOPE - ROPE)], axis=1),
           jnp.concatenate([zero(NOPE), sin, sin, zero(LANES - NOPE - ROPE)], axis=1),
           jnp.concatenate([cos, cos, zero(LANES - ROPE)], axis=1),
           jnp.concatenate([sin, sin, zero(LANES - ROPE)], axis=1),
           jnp.concatenate([rcos] * 4, axis=1), jnp.concatenate([rsin] * 4, axis=1)]
    ident = [np.ones((TM, LANES), np.float32), np.zeros((TM, LANES), np.float32)]
    ident_k = np.concatenate([np.ones((TM, ROPE), np.float32), np.zeros((TM, LANES - ROPE), np.float32)], axis=1)
    ident = [ident[0], ident[1], ident_k, ident[1], ident[0], ident[1]]
    return [jnp.concatenate([l, jnp.asarray(c)], axis=0) for l, c in zip(lat, ident)]


def _block_diag_states(s0):
    s = jnp.swapaxes(_deinterleave(jnp.swapaxes(s0, -1, -2)), -1, -2)
    s = s.reshape(s0.shape[0], HEADS // 2, 2, RDK, RDK)
    z = jnp.zeros_like(s[:, :, 0])
    top = jnp.concatenate([s[:, :, 0], z], axis=-1)
    bot = jnp.concatenate([z, s[:, :, 1]], axis=-1)
    return jnp.concatenate([top, bot], axis=-2)


def _unpermute_matrix():
    m = np.zeros((LANES, LANES), np.float32)
    for blk in range(LANES // RDK):
        m[blk * RDK + np.arange(RDK), blk * RDK + _INV_PERM_RDK] = 1.0
    return jnp.asarray(m, BF16)


def kernel(x_prompt, x_sample, c, cache_ckv, cache_kpe, state_ret_fwd, state_ret_bwd, c_ctx, w_mod, b_mod, ln_g, ln_b, w_in_mix, q_a_gain, kv_a_gain, w_q_b, w_kv_b, ret_decay_fwd, ret_decay_bwd, w_out_mix, w_in_conv, conv_w, w_out_conv, ffn_gate, ffn_up, ffn_down, router_w, router_b, exp_gate, exp_up, exp_down):
    x = (x_prompt.reshape(TP, D), x_sample.reshape(TS, D))
    cond8 = jnp.concatenate([c_ctx[None], c, jnp.zeros((8 - 1 - N_LAT_SEQ, D), F32)], axis=0)
    mods = _modulation(cond8, w_mod, b_mod)
    mods = jnp.pad(mods.reshape(DEPTH, N_GROUPS, 6, D), ((0, 0), (0, 0), (0, 2), (0, 0)))
    ln = jnp.pad(jnp.concatenate([ln_g, ln_b], axis=1)[:, jnp.array([0, 2, 1, 3])], ((0, 0), (0, 4), (0, 0)))
    tabs = _rotary_tables()
    ek = _placement()
    unperm = _unpermute_matrix()
    bf = lambda a: a.astype(BF16)
    w_out_mix_b, w_in_conv_b, w_out_conv_b = bf(w_out_mix), bf(w_in_conv), bf(w_out_conv)
    ffn_b = (bf(ffn_gate), bf(ffn_up), bf(ffn_down))
    exp_b = (exp_gate, exp_up, exp_down)
    cache, states = None, None
    for layer in range(DEPTH):
        j = layer // 2
        mod_l, ln_l = mods[layer], ln[layer]
        if layer % 2 == 0:
            w_in_p, wq, wkv = _prep_even_weights(w_in_mix[j], w_q_b[j], w_kv_b[j])
            q, k, v, cache, rq, rk, rv, rg = _even_in(
                x, mod_l, w_in_p, q_a_gain[j][None], kv_a_gain[j][None], wq, wkv, ek, tabs, cache)
            kpe_c = _pad_cols(_deinterleave(cache_kpe[:, j]).reshape(N_LAT_SEQ * PAST, ROPE), LANES)
            kc, vc = _ctx_kv(cache_ckv[:, j].reshape(N_LAT_SEQ * PAST, KV_LORA), kpe_c, wkv, ek)
            attn = _attention(q, k, v, kc, vc)
            lg = LOG2E * jnp.concatenate([jax.nn.log_sigmoid(ret_decay_fwd[j].astype(F32)),
                                          jax.nn.log_sigmoid(ret_decay_bwd[j].astype(F32))])
            ret, sf, sb = _retention(lg, rq, rk, rv, rg, _block_diag_states(state_ret_fwd[:, j]),
                                     _block_diag_states(state_ret_bwd[:, j]), unperm, states)
            states = (sf, sb)
            x = _mix_ffn(x, attn, ret, mod_l, w_out_mix_b, *ffn_b, ln_l, j)
        else:
            b, z = _conv_in(x, mod_l, w_in_conv_b, j)
            cw = jnp.pad(conv_w[j], ((0, 5), (0, 0)))
            rw = _pad_cols(router_w[j], LANES)
            rw_hi = rw.astype(BF16)
            rw = jnp.concatenate([rw_hi, (rw - rw_hi.astype(F32)).astype(BF16)], axis=1)
            rb = jnp.concatenate([router_b[j].astype(F32), jnp.full((LANES - N_EXP,), -1e30, F32)])[None]
            x, h, route = _conv_out(x, b, z, mod_l, cw, w_out_conv_b, ln_l, rw, rb, j)
            dest0, dest1, row_token, tile_expert, n_valid = _routing_plan(route)
            out_sorted = _moe(tile_expert, n_valid, h[row_token], *exp_b, j)
            x = _combine(x, out_sorted[dest0], out_sorted[dest1], route, mod_l, ln_l, split=layer == DEPTH - 1)
    y_prompt = x[0].reshape(N_PROMPT_SEQ, PROMPT_LEN, D)
    y_sample = x[1].reshape(N_LAT_SEQ, LAT_LEN, D)
    return (y_prompt, y_sample, cache[0], cache[1], states[0], states[1])
```

```python
import functools

import numpy as np
import jax
import jax.numpy as jnp
from jax import lax
from jax.experimental import pallas as pl
from jax.experimental.pallas import tpu as pltpu

F32 = jnp.float32
BF16 = jnp.bfloat16

D = 1024
DEPTH = 4
N_PROMPT_SEQ, PROMPT_LEN = 32, 256
N_LAT_SEQ, LAT_LEN = 2, 2048
PAST = 512
GRID_W = 64
TP = N_PROMPT_SEQ * PROMPT_LEN
TS = N_LAT_SEQ * LAT_LEN
T = TP + TS
HEADS = 8
NOPE, ROPE, VDIM = 64, 32, 64
Q_LORA, KV_LORA = 384, 256
RDK = 64
D_FF = 2816
N_EXP = 8
ALPHA = (2.0 * DEPTH) ** 0.25
LOG2E = float(np.log2(np.e))
Q_SCALE = float((NOPE + ROPE) ** -0.5) * LOG2E
LANES = 128
N_GROUPS = 8

TM = 512
TM_FF = 512
TF = D_FF // 2
TQ = 256
TQA = 512
NP_ROWS = 2 * T + N_EXP * TM_FF
VMEM_LIMIT = 56 * 1024 * 1024

RET_W = HEADS * RDK
COL_KV = Q_LORA
COL_RQ = COL_KV + KV_LORA
COL_RK, COL_RV, COL_RG = COL_RQ + RET_W, COL_RQ + 2 * RET_W, COL_RQ + 3 * RET_W
COL_KPE = COL_RQ + 4 * RET_W
IN_COLS = COL_KPE + 3 * LANES


def _cparams(sem):
    return pltpu.CompilerParams(dimension_semantics=sem, vmem_limit_bytes=VMEM_LIMIT)


def _group_of_tile(i, tm):
    per_seq = LAT_LEN // tm
    return jnp.maximum(i - TP // tm + per_seq, 0) // per_seq


def _bdot(a, b):
    return jnp.dot(a, b, preferred_element_type=F32)


def _wdot(a, w):
    return lax.dot_general(a, w, (((1,), (0,)), ((), ())), preferred_element_type=F32)


def _sigmoid(v):
    return 1.0 / (1.0 + jnp.exp(-v))


def _layer_norm(v, g, b):
    mu = jnp.mean(v, axis=-1, keepdims=True)
    d = v - mu
    var = jnp.mean(d * d, axis=-1, keepdims=True)
    return d * lax.rsqrt(var + 1e-5) * g + b


def _rms(v, g):
    return v * lax.rsqrt(jnp.mean(v * v, axis=-1, keepdims=True) + 1e-6) * g


def _mod_kernel(c_ref, w_ref, b_ref, o_ref):
    c = c_ref[...]
    s = (c * _sigmoid(c)).astype(BF16)
    o_ref[...] = _bdot(s, w_ref[...].astype(BF16)) + b_ref[...]


def _modulation(cond8, w_mod, b_mod):
    tn = 1536
    return pl.pallas_call(
        _mod_kernel,
        grid=(DEPTH, 6 * D // tn),
        in_specs=[pl.BlockSpec((8, D), lambda l, n: (0, 0)),
                  pl.BlockSpec((None, D, tn), lambda l, n: (l, 0, n)),
                  pl.BlockSpec((None, 1, tn), lambda l, n: (l, 0, n))],
        out_specs=pl.BlockSpec((None, 8, tn), lambda l, n: (l, 0, n)),
        out_shape=jax.ShapeDtypeStruct((DEPTH, 8, 6 * D), F32),
        compiler_params=_cparams(("arbitrary", "arbitrary")),
        name="modulation",
    )(cond8, w_mod, b_mod.reshape(DEPTH, 1, 6 * D))


def _swap_halves(a, half):
    n = a.shape[-1]
    lane = lax.broadcasted_iota(jnp.int32, a.shape, 1)
    first = (lane & (2 * half - 1)) < half
    return jnp.where(first, -pltpu.roll(a, n - half, axis=1), pltpu.roll(a, half, axis=1))


def _tile_rows(x_refs, tm):
    if len(x_refs) == 1:
        return x_refs[0][...]
    return jnp.where(pl.program_id(0) >= TP // tm, x_refs[1][...], x_refs[0][...])


def _tile_specs(x, tm):
    if not isinstance(x, tuple):
        return (x,), [pl.BlockSpec((tm, D), lambda i, *_: (i, 0))]
    return x, [pl.BlockSpec((tm, D), lambda i, *_: (jnp.minimum(i, TP // tm - 1), 0)),
               pl.BlockSpec((tm, D), lambda i, *_: (jnp.maximum(i - TP // tm, 0), 0))]


def _even_in_kernel(n_x, n_prev, *refs):
    x_refs, refs = refs[:n_x], refs[n_x:]
    (mod_ref, w_in_ref, qg_ref, kvg_ref, wq_ref, wkv_ref, ek_ref,
     cq_ref, sq_ref, ck_ref, sk_ref, cr_ref, sr_ref), refs = refs[:13], refs[13:]
    prev_refs, refs = refs[:2 * n_prev], refs[2 * n_prev:]
    q_ref, k_ref, v_ref, ckv_ref, kpe_ref, rq_ref, rk_ref, rv_ref, rg_ref = refs
    x = _tile_rows(x_refs, TM)
    h = (x * (1.0 + mod_ref[1:2, :]) + mod_ref[0:1, :]).astype(BF16)
    p = _bdot(h, w_in_ref[...])
    qn = _rms(p[:, 0:Q_LORA], qg_ref[...]).astype(BF16)
    qa = _bdot(qn, wq_ref[...])
    ckv = _rms(p[:, COL_KV:COL_RQ], kvg_ref[...])
    kv = _bdot(ckv.astype(BF16), wkv_ref[...])
    v_ref[...] = kv[:, HEADS * LANES:].astype(BF16)
    base = COL_KPE
    ka = p[:, base + LANES:base + 2 * LANES]
    kb = p[:, base + 2 * LANES:base + 3 * LANES]
    rq = p[:, COL_RQ:COL_RK]
    rk = p[:, COL_RK:COL_RV]
    rv_ref[...] = p[:, COL_RV:COL_RG].astype(BF16)
    rg_ref[...] = p[:, COL_RG:COL_KPE]
    lane = lax.broadcasted_iota(jnp.int32, qa.shape, 1) & (LANES - 1)
    qb = jnp.where(lane < NOPE + ROPE // 2,
                   -pltpu.roll(qa, qa.shape[1] - ROPE // 2, axis=1),
                   pltpu.roll(qa, ROPE // 2, axis=1))
    cq = jnp.concatenate([cq_ref[...]] * HEADS, axis=1)
    sq = jnp.concatenate([sq_ref[...]] * HEADS, axis=1)
    q_ref[...] = ((qa * cq + qb * sq) * Q_SCALE).astype(BF16)
    kpe_rot = ka * ck_ref[...] + kb * sk_ref[...]
    k_ref[...] = (kv[:, :HEADS * LANES] + _bdot(kpe_rot.astype(BF16), ek_ref[...])).astype(BF16)
    cr = jnp.concatenate([cr_ref[...]] * (RET_W // LANES), axis=1)
    sr = jnp.concatenate([sr_ref[...]] * (RET_W // LANES), axis=1)
    rq_ref[...] = (rq * cr + _swap_halves(rq, RDK // 2) * sr).astype(BF16)
    rk_ref[...] = (rk * cr + _swap_halves(rk, RDK // 2) * sr).astype(BF16)

    @pl.when(pl.program_id(0) < TP // TM)
    def _():
        for k in range(n_prev):
            ckv_ref[:, k] = prev_refs[2 * k][...]
            kpe_ref[:, k] = prev_refs[2 * k + 1][...]
        for s in range(TM // PROMPT_LEN):
            rows = slice(s * PROMPT_LEN, (s + 1) * PROMPT_LEN)
            ckv_ref[s, n_prev] = ckv[rows]
            kpe_ref[s, n_prev] = p[rows, base:base + ROPE]


def _even_in(x, mod_l, w_in, qg, kvg, wq, wkv, ek, tabs, prev_cache):
    x_args, x_specs = _tile_specs(x, TM)
    n_prev = 0 if prev_cache is None else prev_cache[0].shape[1]
    spt = TM // PROMPT_LEN
    tok = lambda w: pl.BlockSpec((TM, w), lambda i: (i, 0))
    full = lambda a: pl.BlockSpec(a.shape, lambda i: (0,) * a.ndim, pipeline_mode=pl.Buffered(1))
    lat_tiles = LAT_LEN // TM
    tab = pl.BlockSpec((TM, LANES), lambda i: (
        jnp.where(i < TP // TM, lat_tiles, jnp.maximum(i - TP // TM, 0) % lat_tiles), 0))
    seq = lambda i: (jnp.minimum(i, TP // TM - 1), 0, 0, 0)
    seq_k = lambda k, i: (jnp.minimum(i, TP // TM - 1), k, 0, 0)
    prev_args, prev_specs = [], []
    for k in range(n_prev):
        for a, w in zip(prev_cache, (KV_LORA, ROPE)):
            prev_args.append(a)
            prev_specs.append(pl.BlockSpec((spt, None, PROMPT_LEN, w), functools.partial(seq_k, k)))
    tok_outs = lambda dims: ([tok(w) for w, _ in dims], [jax.ShapeDtypeStruct((T, w), dt) for w, dt in dims])
    qkv_specs, qkv_shapes = tok_outs([(HEADS * LANES, BF16)] * 3)
    ret_specs, ret_shapes = tok_outs([(RET_W, BF16), (RET_W, BF16), (RET_W, BF16), (RET_W, F32)])
    cache_specs = [pl.BlockSpec((spt, n_prev + 1, PROMPT_LEN, w), seq) for w in (KV_LORA, ROPE)]
    cache_shapes = [jax.ShapeDtypeStruct((N_PROMPT_SEQ, n_prev + 1, PROMPT_LEN, w), F32) for w in (KV_LORA, ROPE)]
    q, k, v, ckv, kpe, rq, rk, rv, rg = pl.pallas_call(
        functools.partial(_even_in_kernel, len(x_args), n_prev),
        grid=(T // TM,),
        in_specs=x_specs + [pl.BlockSpec((None, 8, D), lambda i: (_group_of_tile(i, TM), 0, 0)),
                            full(w_in), full(qg), full(kvg), full(wq), full(wkv), full(ek)] + [tab] * 6 + prev_specs,
        out_specs=qkv_specs + cache_specs + ret_specs,
        out_shape=qkv_shapes + cache_shapes + ret_shapes,
        compiler_params=_cparams(("arbitrary",)),
        name="even_in",
    )(*x_args, mod_l, w_in, qg, kvg, wq, wkv, ek, *tabs, *prev_args)
    return q, k, v, (ckv, kpe), rq, rk, rv, rg


def _ctx_kv_kernel(ckv_ref, kpe_ref, wkv_ref, ek_ref, k_ref, v_ref):
    kv = _bdot(ckv_ref[...].astype(BF16), wkv_ref[...])
    k_ref[...] = (kv[:, :HEADS * LANES] + _bdot(kpe_ref[...].astype(BF16), ek_ref[...])).astype(BF16)
    v_ref[...] = kv[:, HEADS * LANES:].astype(BF16)


def _ctx_kv(ckv_c, kpe_c, wkv, ek):
    n = ckv_c.shape[0]
    full = lambda a: pl.BlockSpec(a.shape, lambda i: (0,) * a.ndim)
    return pl.pallas_call(
        _ctx_kv_kernel,
        grid=(n // PAST,),
        in_specs=[pl.BlockSpec((PAST, KV_LORA), lambda i: (i, 0)), pl.BlockSpec((PAST, LANES), lambda i: (i, 0)),
                  full(wkv), full(ek)],
        out_specs=[pl.BlockSpec((PAST, HEADS * LANES), lambda i: (i, 0))] * 2,
        out_shape=[jax.ShapeDtypeStruct((n, HEADS * LANES), BF16)] * 2,
        compiler_params=_cparams(("parallel",)),
        name="ctx_kv",
    )(ckv_c, kpe_c, wkv, ek)


def _attn_kernel(n_kv, q_ref, *refs):
    k_refs = refs[0:2 * n_kv:2]
    v_refs = refs[1:2 * n_kv:2]
    o_ref = refs[2 * n_kv]
    nt = (((1,), (1,)), ((), ()))
    for pair in range(HEADS // 2):
        acc = None
        for sub in range(2):
            sl = slice((2 * pair + sub) * LANES, (2 * pair + sub + 1) * LANES)
            qh = q_ref[:, sl]
            s = [lax.dot_general(qh, k[:, sl], nt, preferred_element_type=F32) for k in k_refs]
            m = functools.reduce(jnp.maximum, [jnp.max(a, axis=-1, keepdims=True) for a in s])
            e = [jnp.exp2(a - m) for a in s]
            den = functools.reduce(jnp.add, [jnp.sum(a, axis=-1, keepdims=True) for a in e])
            o = functools.reduce(jnp.add, [_bdot(a.astype(BF16), v[:, sl]) for a, v in zip(e, v_refs)])
            o = o / den
            acc = o if acc is None else acc + o
        o_ref[:, pair * LANES:(pair + 1) * LANES] = acc.astype(BF16)


def _latent_seq(i):
    return jnp.maximum(i - TP // TQA, 0) // (LAT_LEN // TQA)


def _attn_tiles_kernel(q_ref, kp_ref, vp_ref, kc_ref, vc_ref, kl_ref, vl_ref, o_ref):
    is_latent = pl.program_id(0) >= TP // TQA

    @pl.when(jnp.logical_not(is_latent))
    def _():
        for s in range(TQA // PROMPT_LEN):
            rows = pl.ds(s * PROMPT_LEN, PROMPT_LEN)
            _attn_kernel(1, q_ref.at[rows], kp_ref.at[rows], vp_ref.at[rows], o_ref.at[rows])

    @pl.when(is_latent)
    def _():
        _attn_kernel(2, q_ref, kc_ref, vc_ref, kl_ref, vl_ref, o_ref)


def _attention(q, k, v, kc, vc):
    w = HEADS * LANES
    tile = lambda i: (i, 0)
    ctx_own = lambda i: (jnp.minimum(i, TP // TQA - 1), 0)
    cache = lambda i: (_latent_seq(i), 0)
    lat_own = lambda i: (TP // LAT_LEN + _latent_seq(i), 0)
    once = lambda rows, index: pl.BlockSpec((rows, w), index, pipeline_mode=pl.Buffered(1))
    return pl.pallas_call(
        _attn_tiles_kernel,
        grid=(T // TQA,),
        in_specs=[pl.BlockSpec((TQA, w), tile),
                  pl.BlockSpec((TQA, w), ctx_own), pl.BlockSpec((TQA, w), ctx_own),
                  once(PAST, cache), once(PAST, cache), once(LAT_LEN, lat_own), once(LAT_LEN, lat_own)],
        out_specs=pl.BlockSpec((TQA, HEADS * VDIM), tile),
        out_shape=jax.ShapeDtypeStruct((T, HEADS * VDIM), BF16),
        compiler_params=_cparams(("arbitrary",)),
        name="attention",
    )(q, k, v, kc, vc, k, v)


def _ret_prefix_kernel(lg_ref, rk_ref, rv_ref, s0f_ref, s0b_ref, pf_ref, qb_ref):
    s = pl.program_id(0)
    per_seq = LAT_LEN // TQ
    row = lax.broadcasted_iota(jnp.int32, (LANES, 1), 0)
    lane = lax.broadcasted_iota(jnp.int32, (1, LANES), 1)
    top, lo = row < RDK, lane < RDK
    same_head = top == lo
    m_col = lax.broadcasted_iota(jnp.int32, (TQ, 1), 0).astype(F32)

    def scan(lg_off, s0_ref, out_ref, pos, tiles):
        for pair in range(HEADS // 2):
            sl = slice(pair * LANES, (pair + 1) * LANES)
            lg_even, lg_odd = lg_ref[lg_off + 2 * pair], lg_ref[lg_off + 2 * pair + 1]
            dec = jnp.exp2(pos * jnp.where(lo, lg_even, lg_odd))
            tile_decay = jnp.exp2(float(TQ) * jnp.where(top, lg_even, lg_odd))
            state = s0_ref[pair]
            for c in tiles:
                rows = slice(c * TQ, (c + 1) * TQ)
                out_ref[c, pair] = state.astype(BF16)
                local = _bdot((rk_ref[rows, sl].astype(F32) * dec).T.astype(BF16), rv_ref[rows, sl])
                state = state * tile_decay + jnp.where(same_head, local, 0.0)

    @pl.when(s < N_LAT_SEQ)
    def _():
        scan(0, s0f_ref, pf_ref, TQ - 1.0 - m_col, range(per_seq))

    @pl.when(s >= N_LAT_SEQ)
    def _():
        scan(HEADS, s0b_ref, qb_ref, m_col, range(per_seq - 1, -1, -1))


def _ret_prefix(lg, rk, rv, s0f, s0b):
    n_tiles, per_seq = TS // TQ, LAT_LEN // TQ
    seq_of = lambda s: jnp.where(s < N_LAT_SEQ, s, 2 * N_LAT_SEQ - 1 - s)
    st = pl.BlockSpec((None, HEADS // 2, LANES, LANES), lambda s, lg: (seq_of(s), 0, 0, 0))
    kv = pl.BlockSpec((LAT_LEN, HEADS * RDK), lambda s, lg: (TP // LAT_LEN + seq_of(s), 0))
    pf_blk = pl.BlockSpec((per_seq, HEADS // 2, LANES, LANES), lambda s, lg: (jnp.minimum(s, N_LAT_SEQ - 1), 0, 0, 0))
    qb_blk = pl.BlockSpec((per_seq, HEADS // 2, LANES, LANES),
                          lambda s, lg: (jnp.minimum(2 * N_LAT_SEQ - 1 - s, N_LAT_SEQ - 1), 0, 0, 0))
    shape = jax.ShapeDtypeStruct((n_tiles, HEADS // 2, LANES, LANES), BF16)
    return pl.pallas_call(
        _ret_prefix_kernel,
        grid_spec=pltpu.PrefetchScalarGridSpec(
            num_scalar_prefetch=1, grid=(2 * N_LAT_SEQ,),
            in_specs=[kv, kv, st, st], out_specs=[pf_blk, qb_blk]),
        out_shape=[shape, shape],
        compiler_params=_cparams(("arbitrary",)),
        name="ret_prefix",
    )(lg, rk, rv, s0f, s0b)


def _retention_kernel(latent, lg_ref, rq_ref, rk_ref, rv_ref, rg_ref, *refs):
    q0 = 0
    seq_len = rq_ref.shape[0]
    if latent:
        s0f_ref, s0b_ref, o_ref = refs
    else:
        unperm_ref, o_ref, sf_ref, sb_ref = refs
    tq, tk = rq_ref.shape[0], rk_ref.shape[0]
    nt = (((1,), (1,)), ((), ()))
    n_idx = (q0 + lax.broadcasted_iota(jnp.int32, (tq, tk), 0)).astype(F32)
    m_idx = lax.broadcasted_iota(jnp.int32, (tq, tk), 1).astype(F32)
    dist = n_idx - m_idx
    adist = jnp.abs(dist)
    fwd = dist > 0.0
    diag = jnp.where(dist == 0.0, 1.0, 0.0)
    lane = lax.broadcasted_iota(jnp.int32, (1, LANES), 1)
    lo = lane < RDK
    n_col = (q0 + lax.broadcasted_iota(jnp.int32, (tq, 1), 0)).astype(F32)
    m_col = lax.broadcasted_iota(jnp.int32, (tk, 1), 0).astype(F32)
    for pair in range(HEADS // 2):
        sl = slice(pair * LANES, (pair + 1) * LANES)
        qb, kb, vb = rq_ref[:, sl], rk_ref[:, sl], rv_ref[:, sl]
        acc = jnp.zeros((tq, LANES), F32)
        for sub in range(2):
            h = 2 * pair + sub
            lgf, lgb = lg_ref[h], lg_ref[HEADS + h]
            half = lo if sub == 0 else jnp.logical_not(lo)
            qm = jnp.where(half, qb, jnp.zeros_like(qb))
            vm = jnp.where(half, vb, jnp.zeros_like(vb))
            s = lax.dot_general(qm, kb, nt, preferred_element_type=F32)
            w = jnp.exp2(adist * jnp.where(fwd, lgf, lgb)) + diag
            acc = acc + _bdot((s * w).astype(BF16), vm)
        lgf_l = jnp.where(lo, lg_ref[2 * pair], lg_ref[2 * pair + 1])
        lgb_l = jnp.where(lo, lg_ref[HEADS + 2 * pair], lg_ref[HEADS + 2 * pair + 1])
        if latent:
            acc = acc + _bdot(qb, s0f_ref[pair]) * jnp.exp2((n_col + 1.0) * lgf_l)
            acc = acc + _bdot(qb, s0b_ref[pair]) * jnp.exp2((seq_len - n_col) * lgb_l)
        else:
            v_swapped = pltpu.roll(vb.astype(F32), RDK, axis=1).astype(BF16)
            for st_ref, dec in ((sf_ref, jnp.exp2((seq_len - 1.0 - m_col) * lgf_l)),
                                (sb_ref, jnp.exp2(m_col * lgb_l))):
                kt = (kb.astype(F32) * dec).T.astype(BF16)
                kt = _bdot(unperm_ref[...], kt).astype(BF16)
                st_ref[2 * pair] = _bdot(kt, vb)[0:RDK, 0:RDK]
                st_ref[2 * pair + 1] = _bdot(kt, v_swapped)[RDK:, 0:RDK]
        inv = 1.0 / RDK
        mu = jnp.where(lo, jnp.sum(jnp.where(lo, acc, 0.0), axis=-1, keepdims=True),
                       jnp.sum(jnp.where(lo, 0.0, acc), axis=-1, keepdims=True)) * inv
        dlt = acc - mu
        d2 = dlt * dlt
        var = jnp.where(lo, jnp.sum(jnp.where(lo, d2, 0.0), axis=-1, keepdims=True),
                        jnp.sum(jnp.where(lo, 0.0, d2), axis=-1, keepdims=True)) * inv
        g = rg_ref[:, sl]
        o_ref[:, sl] = (dlt * lax.rsqrt(var + 1e-5) * (g * _sigmoid(g))).astype(BF16)


def _retention_tiles_kernel(n_prev, lg_ref, rq_ref, rk_ref, rv_ref, rg_ref, pf_ref, qb_ref, unperm_ref, *refs):
    prev_refs, (o_ref, sf_ref, sb_ref) = refs[:2 * n_prev], refs[2 * n_prev:]
    is_latent = pl.program_id(0) >= TP // TQA
    views = lambda s: [r.at[pl.ds(s * TQ, TQ)] for r in (rq_ref, rk_ref, rv_ref, rg_ref, o_ref)]

    @pl.when(jnp.logical_not(is_latent))
    def _():
        for k in range(n_prev):
            sf_ref[:, k] = prev_refs[2 * k][...]
            sb_ref[:, k] = prev_refs[2 * k + 1][...]
        for s in range(TQA // TQ):
            rq, rk, rv, rg, o = views(s)
            _retention_kernel(False, lg_ref, rq, rk, rv, rg, unperm_ref, o, sf_ref.at[s, n_prev], sb_ref.at[s, n_prev])

    @pl.when(is_latent)
    def _():
        for s in range(TQA // TQ):
            rq, rk, rv, rg, o = views(s)
            _retention_kernel(True, lg_ref, rq, rk, rv, rg, pf_ref.at[s], qb_ref.at[s], o)


def _retention(lg, rq, rk, rv, rg, s0f, s0b, unperm, prev_states):
    w = HEADS * RDK
    n_prev = 0 if prev_states is None else prev_states[0].shape[1]
    pf, qb = _ret_prefix(lg, rk, rv, s0f, s0b)
    assert TQ == PROMPT_LEN
    per_tile = TQA // TQ
    tile = lambda i, lg: (i, 0)
    s0_blk = pl.BlockSpec((per_tile, HEADS // 2, LANES, LANES),
                          lambda i, lg: (jnp.maximum(i - TP // TQA, 0), 0, 0, 0))
    seq = lambda i, lg: (jnp.minimum(i, TP // TQA - 1), 0, 0, 0, 0)
    st_blk = pl.BlockSpec((per_tile, n_prev + 1, HEADS, RDK, RDK), seq)
    st_shape = jax.ShapeDtypeStruct((N_PROMPT_SEQ, n_prev + 1, HEADS, RDK, RDK), F32)
    prev = () if prev_states is None else tuple(prev_states)
    prev_specs = [pl.BlockSpec((per_tile, None, HEADS, RDK, RDK),
                               functools.partial(lambda k, i, lg: (jnp.minimum(i, TP // TQA - 1), k, 0, 0, 0), k))
                  for k in range(n_prev) for _ in range(2)]
    prev_args = [p for k in range(n_prev) for p in prev]
    return pl.pallas_call(
        functools.partial(_retention_tiles_kernel, n_prev),
        grid_spec=pltpu.PrefetchScalarGridSpec(
            num_scalar_prefetch=1, grid=(T // TQA,),
            in_specs=[pl.BlockSpec((TQA, w), tile)] * 4 + [s0_blk, s0_blk,
                      pl.BlockSpec((LANES, LANES), lambda i, lg: (0, 0))] + prev_specs,
            out_specs=[pl.BlockSpec((TQA, w), tile), st_blk, st_blk]),
        out_shape=[jax.ShapeDtypeStruct((T, w), BF16), st_shape, st_shape],
        compiler_params=_cparams(("arbitrary",)),
        name="retention",
    )(lg, rq, rk, rv, rg, pf, qb, unperm, *prev_args)


def _mix_ffn_kernel(n_x, *refs):
    x_refs, refs = refs[:n_x], refs[n_x:]
    a_ref, r_ref, mod_ref, wo_ref, wg_ref, wu_ref, wd_ref, ln_ref, o_ref = refs
    half = HEADS * VDIM
    y = _bdot(a_ref[...], wo_ref[0:half, :]) + _bdot(r_ref[...], wo_ref[half:, :])
    x1 = _layer_norm(ALPHA * _tile_rows(x_refs, TM_FF) + mod_ref[2:3, :] * y, ln_ref[0:1, :], ln_ref[1:2, :])
    h = (x1 * (1.0 + mod_ref[4:5, :]) + mod_ref[3:4, :]).astype(BF16)
    acc = None
    for f in range(D_FF // TF):
        cols = slice(f * TF, (f + 1) * TF)
        g = _bdot(h, wg_ref[:, cols])
        u = _bdot(h, wu_ref[:, cols])
        part = _bdot((g * _sigmoid(g) * u).astype(BF16), wd_ref[cols, :])
        acc = part if acc is None else acc + part
    o_ref[...] = _layer_norm(ALPHA * x1 + mod_ref[5:6, :] * acc, ln_ref[2:3, :], ln_ref[3:4, :])


def _mix_ffn(x, attn, ret, mod_l, w_out, wg, wu, wd, ln, j):
    x_args, x_specs = _tile_specs(x, TM_FF)
    tok = lambda w: pl.BlockSpec((TM_FF, w), lambda i: (i, 0))
    resident = lambda a: pl.BlockSpec((None,) + a.shape[1:], lambda i: (j, 0, 0), pipeline_mode=pl.Buffered(1))
    return pl.pallas_call(
        functools.partial(_mix_ffn_kernel, len(x_args)),
        grid=(T // TM_FF,),
        in_specs=x_specs + [tok(HEADS * VDIM), tok(HEADS * RDK),
                  pl.BlockSpec((None, 8, D), lambda i: (_group_of_tile(i, TM_FF), 0, 0)),
                  resident(w_out), resident(wg), resident(wu), resident(wd), pl.BlockSpec((8, D), lambda i: (0, 0))],
        out_specs=tok(D),
        out_shape=jax.ShapeDtypeStruct((T, D), F32),
        compiler_params=_cparams(("arbitrary",)),
        name="mix_ffn",
    )(*x_args, attn, ret, mod_l, w_out, wg, wu, wd, ln)


def _conv_in_kernel(x_ref, mod_ref, w_ref, b_ref, z_ref):
    h = (x_ref[...] * (1.0 + mod_ref[1:2, :]) + mod_ref[0:1, :]).astype(BF16)
    p = _bdot(h, w_ref[...])
    b_ref[...] = p[:, 0:D]
    z_ref[...] = p[:, D:2 * D] * p[:, 2 * D:3 * D]


def _conv_in(x, mod_l, w_in, j):
    tm = 2 * TM_FF
    tok = pl.BlockSpec((tm, D), lambda i: (i, 0))
    return pl.pallas_call(
        _conv_in_kernel,
        grid=(T // tm,),
        in_specs=[tok, pl.BlockSpec((None, 8, D), lambda i: (_group_of_tile(i, tm), 0, 0)),
                  pl.BlockSpec((None,) + w_in.shape[1:], lambda i: (j, 0, 0), pipeline_mode=pl.Buffered(1))],
        out_specs=[tok, tok],
        out_shape=[jax.ShapeDtypeStruct((T, D), F32)] * 2,
        compiler_params=_cparams(("parallel",)),
        name="conv_in",
    )(x, mod_l, w_in)


def _conv_out_kernel(x_ref, b_ref, z_ref, zp_ref, zn_ref, mod_ref, cw_ref, w_ref, ln_ref, rw_ref, rb_ref,
                     o_ref, h_ref, route_ref):
    i = pl.program_id(0)
    z = z_ref[...]
    tm = z.shape[0]
    row = lax.broadcasted_iota(jnp.int32, (tm, 1), 0)
    seq_len = jnp.where(i < TP // tm, PROMPT_LEN, LAT_LEN)
    pos = (i * tm + row) & (seq_len - 1)
    prev = jnp.where(row == 0, zp_ref[7:8, :], pltpu.roll(z, 1, axis=0))
    prev = jnp.where(pos == 0, 0.0, prev)
    nxt = jnp.where(row == tm - 1, zn_ref[0:1, :], pltpu.roll(z, tm - 1, axis=0))
    nxt = jnp.where(pos == seq_len - 1, 0.0, nxt)
    y = prev * cw_ref[0:1, :] + z * cw_ref[1:2, :] + nxt * cw_ref[2:3, :]
    t = _bdot((b_ref[...] * y).astype(BF16), w_ref[...])
    x1 = _layer_norm(ALPHA * x_ref[...] + mod_ref[2:3, :] * t, ln_ref[0:1, :], ln_ref[1:2, :])
    o_ref[...] = x1
    h = x1 * (1.0 + mod_ref[4:5, :]) + mod_ref[3:4, :]
    h_hi = h.astype(BF16)
    h_ref[...] = h_hi
    h_lo = (h - h_hi.astype(F32)).astype(BF16)
    both = _bdot(h_hi, rw_ref[...])
    logits = both[:, :LANES] + both[:, LANES:] + _bdot(h_lo, rw_ref[:, :LANES]) + rb_ref[...]
    lane = lax.broadcasted_iota(jnp.int32, logits.shape, 1).astype(F32)
    t1 = jnp.max(logits, axis=-1, keepdims=True)
    i1 = jnp.min(jnp.where(logits == t1, lane, float(LANES)), axis=-1, keepdims=True)
    rest = jnp.where(lane == i1, -jnp.inf, logits)
    t2 = jnp.max(rest, axis=-1, keepdims=True)
    i2 = jnp.min(jnp.where(rest == t2, lane, float(LANES)), axis=-1, keepdims=True)
    e = jnp.exp(t2 - t1)
    den = 1.0 + e
    route_ref[...] = jnp.where(lane == 0.0, i1, jnp.where(lane == 1.0, i2,
                               jnp.where(lane == 2.0, 1.0 / den, jnp.where(lane == 3.0, e / den, 0.0))))


def _conv_out(x, b, z, mod_l, cw, w_out, ln, rw, rb, j):
    tok = pl.BlockSpec((TM_FF, D), lambda i: (i, 0))
    sub = TM_FF // 8
    return pl.pallas_call(
        _conv_out_kernel,
        grid=(T // TM_FF,),
        in_specs=[tok, tok, tok,
                  pl.BlockSpec((8, D), lambda i: (jnp.maximum(i * sub - 1, 0), 0)),
                  pl.BlockSpec((8, D), lambda i: (jnp.minimum((i + 1) * sub, T // 8 - 1), 0)),
                  pl.BlockSpec((None, 8, D), lambda i: (_group_of_tile(i, TM_FF), 0, 0)),
                  pl.BlockSpec((8, D), lambda i: (0, 0)), pl.BlockSpec((None, D, D), lambda i: (j, 0, 0)),
                  pl.BlockSpec((8, D), lambda i: (0, 0)),
                  pl.BlockSpec((D, 2 * LANES), lambda i: (0, 0)), pl.BlockSpec((1, LANES), lambda i: (0, 0))],
        out_specs=[tok, tok, pl.BlockSpec((TM_FF, LANES), lambda i: (i, 0))],
        out_shape=[jax.ShapeDtypeStruct((T, D), F32), jax.ShapeDtypeStruct((T, D), BF16),
                   jax.ShapeDtypeStruct((T, LANES), F32)],
        compiler_params=_cparams(("parallel",)),
        name="conv_out",
    )(x, b, z, z, z, mod_l, cw, w_out, ln, rw, rb)


def _moe_up_kernel(te_ref, nv_ref, x_ref, wg_ref, wu_ref, a_ref):
    @pl.when(pl.program_id(1) < nv_ref[0])
    def _():
        h = x_ref[...]
        g = _wdot(h, wg_ref[...])
        u = _wdot(h, wu_ref[...])
        a_ref[...] = (g * _sigmoid(g) * u).astype(BF16)

    @pl.when(pl.program_id(1) >= nv_ref[0])
    def _():
        a_ref[...] = jnp.zeros_like(a_ref)


def _moe_down_kernel(te_ref, nv_ref, a_ref, wd_ref, o_ref):
    @pl.when(pl.program_id(0) < nv_ref[0])
    def _():
        o_ref[...] = _wdot(a_ref[...], wd_ref[...])

    @pl.when(pl.program_id(0) >= nv_ref[0])
    def _():
        o_ref[...] = jnp.zeros_like(o_ref)


def _moe(tile_expert, n_valid, xs, wg, wu, wd, j):
    n_tiles = NP_ROWS // TM_FF
    act = pl.pallas_call(
        _moe_up_kernel,
        grid_spec=pltpu.PrefetchScalarGridSpec(
            num_scalar_prefetch=2, grid=(D_FF // TF, n_tiles),
            in_specs=[pl.BlockSpec((TM_FF, D), lambda f, i, te, nv: (i, 0)),
                      pl.BlockSpec((None, None, D, TF), lambda f, i, te, nv: (j, te[i], 0, f)),
                      pl.BlockSpec((None, None, D, TF), lambda f, i, te, nv: (j, te[i], 0, f))],
            out_specs=pl.BlockSpec((TM_FF, TF), lambda f, i, te, nv: (i, f))),
        out_shape=jax.ShapeDtypeStruct((NP_ROWS, D_FF), BF16),
        compiler_params=_cparams(("arbitrary", "arbitrary")),
        name="moe_up",
    )(tile_expert, n_valid, xs, wg, wu)
    return pl.pallas_call(
        _moe_down_kernel,
        grid_spec=pltpu.PrefetchScalarGridSpec(
            num_scalar_prefetch=2, grid=(n_tiles,),
            in_specs=[pl.BlockSpec((TM_FF, D_FF), lambda i, te, nv: (i, 0)),
                      pl.BlockSpec((None, None, D_FF, D), lambda i, te, nv: (j, te[i], 0, 0))],
            out_specs=pl.BlockSpec((TM_FF, D), lambda i, te, nv: (i, 0))),
        out_shape=jax.ShapeDtypeStruct((NP_ROWS, D), F32),
        compiler_params=_cparams(("arbitrary",)),
        name="moe_down",
    )(tile_expert, n_valid, act, wd)


def _combine_kernel(split, x_ref, o0_ref, o1_ref, route_ref, mod_ref, ln_ref, *o_refs):
    y = route_ref[:, 2:3] * o0_ref[...] + route_ref[:, 3:4] * o1_ref[...]
    out = _layer_norm(ALPHA * x_ref[...] + mod_ref[5:6, :] * y, ln_ref[2:3, :], ln_ref[3:4, :])
    if not split:
        o_refs[0][...] = out
        return
    is_latent = pl.program_id(0) >= TP // TM_FF

    @pl.when(jnp.logical_not(is_latent))
    def _():
        o_refs[0][...] = out

    @pl.when(is_latent)
    def _():
        o_refs[1][...] = out


def _combine(x, o0, o1, route, mod_l, ln, split):
    tok = pl.BlockSpec((TM_FF, D), lambda i: (i, 0))
    if split:
        out_specs = [pl.BlockSpec((TM_FF, D), lambda i: (jnp.minimum(i, TP // TM_FF - 1), 0)),
                     pl.BlockSpec((TM_FF, D), lambda i: (jnp.maximum(i - TP // TM_FF, 0), 0))]
        out_shape = [jax.ShapeDtypeStruct((TP, D), F32), jax.ShapeDtypeStruct((TS, D), F32)]
    else:
        out_specs, out_shape = tok, jax.ShapeDtypeStruct((T, D), F32)
    return pl.pallas_call(
        functools.partial(_combine_kernel, split),
        grid=(T // TM_FF,),
        in_specs=[tok, tok, tok, pl.BlockSpec((TM_FF, LANES), lambda i: (i, 0)),
                  pl.BlockSpec((None, 8, D), lambda i: (_group_of_tile(i, TM_FF), 0, 0)),
                  pl.BlockSpec((8, D), lambda i: (0, 0))],
        out_specs=out_specs,
        out_shape=out_shape,
        compiler_params=_cparams(("arbitrary",)),
        name="moe_combine",
    )(x, o0, o1, route, mod_l, ln)


def _routing_plan(route):
    e = jnp.concatenate([route[:, 0], route[:, 1]]).astype(jnp.int32)
    onehot = (e[:, None] == jnp.arange(N_EXP, dtype=jnp.int32)[None, :]).astype(jnp.int32)
    csum = jnp.cumsum(onehot, axis=0)
    counts = csum[-1]
    rank = jnp.sum((csum - onehot) * onehot, axis=1)
    padded = (counts + TM_FF - 1) // TM_FF * TM_FF
    pend = jnp.cumsum(padded)
    dest = jnp.sum(onehot * (pend - padded)[None, :], axis=1) + rank
    order = jnp.argsort(e, stable=True).astype(jnp.int32)
    rows = jnp.arange(NP_ROWS, dtype=jnp.int32)
    before = (rows[:, None] >= pend[None, :]).astype(jnp.int32)
    row_e = jnp.minimum(jnp.sum(before, axis=1), N_EXP - 1)
    row_cnt = jnp.sum((row_e[:, None] == jnp.arange(N_EXP, dtype=jnp.int32)[None, :]) * counts[None, :], axis=1)
    q = jnp.clip(rows - jnp.sum(before * padded[None, :], axis=1), 0, jnp.maximum(row_cnt - 1, 0))
    src = jnp.minimum(jnp.sum(before * counts[None, :], axis=1) + q, 2 * T - 1)
    row_token = order[src] % T
    n_valid = (pend[-1] // TM_FF).astype(jnp.int32)
    tile_start = jnp.minimum(jnp.arange(NP_ROWS // TM_FF, dtype=jnp.int32), n_valid - 1) * TM_FF
    tile_expert = jnp.minimum(jnp.sum((tile_start[:, None] >= pend[None, :]).astype(jnp.int32), axis=1), N_EXP - 1)
    return dest[:T], dest[T:], row_token, tile_expert.astype(jnp.int32), n_valid.reshape(1)


_INV_PERM_RDK = np.argsort(np.concatenate([np.arange(0, RDK, 2), np.arange(1, RDK, 2)]))


def _pad_cols(a, width):
    return jnp.pad(a, ((0, 0), (0, width - a.shape[1])))


def _deinterleave(a):
    n = a.shape[-1]
    return jnp.swapaxes(a.reshape(a.shape[:-1] + (n // 2, 2)), -1, -2).reshape(a.shape)


def _prep_even_weights(w_in, w_q_b, w_kv_b):
    o_kpe = Q_LORA + KV_LORA
    o_rq, o_rk, o_rv = o_kpe + ROPE, o_kpe + ROPE + RET_W, o_kpe + ROPE + 2 * RET_W
    heads = lambda a: _deinterleave(a.reshape(D, HEADS, RDK)).reshape(D, HEADS * RDK)
    kpe = w_in[:, o_kpe:o_kpe + ROPE]
    w_in_p = jnp.concatenate([
        w_in[:, :o_kpe], heads(w_in[:, o_rq:o_rk]), heads(w_in[:, o_rk:o_rv]) * (RDK ** -0.5), w_in[:, o_rv:],
        _pad_cols(kpe, LANES), _pad_cols(_deinterleave(kpe), LANES),
        _pad_cols(jnp.concatenate([-kpe[:, 1::2], kpe[:, 0::2]], axis=1), LANES)], axis=1).astype(BF16)
    wq = w_q_b.reshape(Q_LORA, HEADS, NOPE + ROPE)
    wq = jnp.concatenate([wq[:, :, :NOPE], _deinterleave(wq[:, :, NOPE:]),
                          jnp.zeros((Q_LORA, HEADS, LANES - NOPE - ROPE), F32)], axis=2)
    wq = wq.reshape(Q_LORA, HEADS * LANES).astype(BF16)
    wkv = w_kv_b.reshape(KV_LORA, HEADS, NOPE + VDIM)
    zero = jnp.zeros((KV_LORA, HEADS, LANES - NOPE), F32)
    wk = jnp.concatenate([wkv[:, :, :NOPE], zero], axis=2).reshape(KV_LORA, HEADS * LANES)
    wv = wkv[:, :, NOPE:].reshape(KV_LORA, HEADS // 2, 2, VDIM)
    zv = jnp.zeros((KV_LORA, HEADS // 2, VDIM), F32)
    wv = jnp.stack([jnp.concatenate([wv[:, :, 0], zv], axis=2), jnp.concatenate([zv, wv[:, :, 1]], axis=2)], axis=2)
    wkv_p = jnp.concatenate([wk, wv.reshape(KV_LORA, HEADS * LANES)], axis=1).astype(BF16)
    return w_in_p, wq, wkv_p


def _placement():
    ek = np.zeros((LANES, HEADS * LANES), np.float32)
    for h in range(HEADS):
        ek[np.arange(ROPE), h * LANES + NOPE + np.arange(ROPE)] = 1.0
    return jnp.asarray(ek, BF16)


def _rotary_tables():
    rows = LAT_LEN // GRID_W
    r, col = jnp.meshgrid(jnp.arange(rows, dtype=F32), jnp.arange(GRID_W, dtype=F32), indexing='ij')
    n_freq = ROPE // 4
    freqs = 1.0 / (10000.0 ** (jnp.arange(n_freq, dtype=F32) / n_freq))
    ang = jnp.concatenate([r.reshape(-1)[:, None] * freqs, col.reshape(-1)[:, None] * freqs], axis=-1)
    cos, sin = jnp.cos(ang), jnp.sin(ang)
    theta = 1.0 / (10000.0 ** jnp.linspace(0.0, 1.0, RDK // 2, dtype=F32))
    rang = jnp.arange(LAT_LEN, dtype=F32)[:, None] * theta
    rcos, rsin = jnp.cos(rang), jnp.sin(rang)
    one = lambda w: jnp.ones((LAT_LEN, w), F32)
    zero = lambda w: jnp.zeros((LAT_LEN, w), F32)
    lat = [jnp.concatenate([one(NOPE), cos, cos, one(LANES - NOPE - ROPE)], axis=1),
           jnp.concatenate([zero(NOPE), sin, sin, zero(LANES - NOPE - ROPE)], axis=1),
           jnp.concatenate([cos, cos, zero(LANES - ROPE)], axis=1),
           jnp.concatenate([sin, sin, zero(LANES - ROPE)], axis=1),
           jnp.concatenate([rcos] * 4, axis=1), jnp.concatenate([rsin] * 4, axis=1)]
    ident = [np.ones((TM, LANES), np.float32), np.zeros((TM, LANES), np.float32)]
    ident_k = np.concatenate([np.ones((TM, ROPE), np.float32), np.zeros((TM, LANES - ROPE), np.float32)], axis=1)
    ident = [ident[0], ident[1], ident_k, ident[1], ident[0], ident[1]]
    return [jnp.concatenate([l, jnp.asarray(c)], axis=0) for l, c in zip(lat, ident)]


def _block_diag_states(s0):
    s = jnp.swapaxes(_deinterleave(jnp.swapaxes(s0, -1, -2)), -1, -2)
    s = s.reshape(s0.shape[0], HEADS // 2, 2, RDK, RDK)
    z = jnp.zeros_like(s[:, :, 0])
    top = jnp.concatenate([s[:, :, 0], z], axis=-1)
    bot = jnp.concatenate([z, s[:, :, 1]], axis=-1)
    return jnp.concatenate([top, bot], axis=-2)


def _unpermute_matrix():
    m = np.zeros((LANES, LANES), np.float32)
    for blk in range(LANES // RDK):
        m[blk * RDK + np.arange(RDK), blk * RDK + _INV_PERM_RDK] = 1.0
    return jnp.asarray(m, BF16)


def kernel(x_prompt, x_sample, c, cache_ckv, cache_kpe, state_ret_fwd, state_ret_bwd, c_ctx, w_mod, b_mod, ln_g, ln_b, w_in_mix, q_a_gain, kv_a_gain, w_q_b, w_kv_b, ret_decay_fwd, ret_decay_bwd, w_out_mix, w_in_conv, conv_w, w_out_conv, ffn_gate, ffn_up, ffn_down, router_w, router_b, exp_gate, exp_up, exp_down):
    x = (x_prompt.reshape(TP, D), x_sample.reshape(TS, D))
    cond8 = jnp.concatenate([c_ctx[None], c, jnp.zeros((8 - 1 - N_LAT_SEQ, D), F32)], axis=0)
    mods = _modulation(cond8, w_mod, b_mod)
    mods = jnp.pad(mods.reshape(DEPTH, N_GROUPS, 6, D), ((0, 0), (0, 0), (0, 2), (0, 0)))
    ln = jnp.pad(jnp.concatenate([ln_g, ln_b], axis=1)[:, jnp.array([0, 2, 1, 3])], ((0, 0), (0, 4), (0, 0)))
    tabs = _rotary_tables()
    ek = _placement()
    unperm = _unpermute_matrix()
    bf = lambda a: a.astype(BF16)
    w_out_mix_b, w_in_conv_b, w_out_conv_b = bf(w_out_mix), bf(w_in_conv), bf(w_out_conv)
    ffn_b = (bf(ffn_gate), bf(ffn_up), bf(ffn_down))
    exp_b = (exp_gate, exp_up, exp_down)
    cache, states = None, None
    for layer in range(DEPTH):
        j = layer // 2
        mod_l, ln_l = mods[layer], ln[layer]
        if layer % 2 == 0:
            w_in_p, wq, wkv = _prep_even_weights(w_in_mix[j], w_q_b[j], w_kv_b[j])
            q, k, v, cache, rq, rk, rv, rg = _even_in(
                x, mod_l, w_in_p, q_a_gain[j][None], kv_a_gain[j][None], wq, wkv, ek, tabs, cache)
            kpe_c = _pad_cols(_deinterleave(cache_kpe[:, j]).reshape(N_LAT_SEQ * PAST, ROPE), LANES)
            kc, vc = _ctx_kv(cache_ckv[:, j].reshape(N_LAT_SEQ * PAST, KV_LORA), kpe_c, wkv, ek)
            attn = _attention(q, k, v, kc, vc)
            lg = LOG2E * jnp.concatenate([jax.nn.log_sigmoid(ret_decay_fwd[j].astype(F32)),
                                          jax.nn.log_sigmoid(ret_decay_bwd[j].astype(F32))])
            ret, sf, sb = _retention(lg, rq, rk, rv, rg, _block_diag_states(state_ret_fwd[:, j]),
                                     _block_diag_states(state_ret_bwd[:, j]), unperm, states)
            states = (sf, sb)
            x = _mix_ffn(x, attn, ret, mod_l, w_out_mix_b, *ffn_b, ln_l, j)
        else:
            b, z = _conv_in(x, mod_l, w_in_conv_b, j)
            cw = jnp.pad(conv_w[j], ((0, 5), (0, 0)))
            rw = _pad_cols(router_w[j], LANES)
            rw_hi = rw.astype(BF16)
            rw = jnp.concatenate([rw_hi, (rw - rw_hi.astype(F32)).astype(BF16)], axis=1)
            rb = jnp.concatenate([router_b[j].astype(F32), jnp.full((LANES - N_EXP,), -1e30, F32)])[None]
            x, h, route = _conv_out(x, b, z, mod_l, cw, w_out_conv_b, ln_l, rw, rb, j)
            dest0, dest1, row_token, tile_expert, n_valid = _routing_plan(route)
            out_sorted = _moe(tile_expert, n_valid, h[row_token], *exp_b, j)
            x = _combine(x, out_sorted[dest0], out_sorted[dest1], route, mod_l, ln_l, split=layer == DEPTH - 1)
    y_prompt = x[0].reshape(N_PROMPT_SEQ, PROMPT_LEN, D)
    y_sample = x[1].reshape(N_LAT_SEQ, LAT_LEN, D)
    return (y_prompt, y_sample, cache[0], cache[1], states[0], states[1])
```

```python
import functools

import numpy as np
import jax
import jax.numpy as jnp
from jax import lax
from jax.experimental import pallas as pl
from jax.experimental.pallas import tpu as pltpu

F32 = jnp.float32
BF16 = jnp.bfloat16

D = 1024
DEPTH = 4
N_PROMPT_SEQ, PROMPT_LEN = 32, 256
N_LAT_SEQ, LAT_LEN = 2, 2048
PAST = 512
GRID_W = 64
TP = N_PROMPT_SEQ * PROMPT_LEN
TS = N_LAT_SEQ * LAT_LEN
T = TP + TS
HEADS = 8
NOPE, ROPE, VDIM = 64, 32, 64
Q_LORA, KV_LORA = 384, 256
RDK = 64
D_FF = 2816
N_EXP = 8
ALPHA = (2.0 * DEPTH) ** 0.25
LOG2E = float(np.log2(np.e))
Q_SCALE = float((NOPE + ROPE) ** -0.5) * LOG2E
LANES = 128
N_GROUPS = 8

TM = 512
TM_FF = 512
TF = D_FF // 2
TQ = 256
TQA = 512
TQR = 1024
NP_ROWS = 2 * T + N_EXP * TM_FF
VMEM_LIMIT = 56 * 1024 * 1024

RET_W = HEADS * RDK
COL_KV = Q_LORA
COL_RQ = COL_KV + KV_LORA
COL_RK, COL_RV, COL_RG = COL_RQ + RET_W, COL_RQ + 2 * RET_W, COL_RQ + 3 * RET_W
COL_KPE = COL_RQ + 4 * RET_W
IN_COLS = COL_KPE + 3 * LANES


def _cparams(sem):
    return pltpu.CompilerParams(dimension_semantics=sem, vmem_limit_bytes=VMEM_LIMIT)


def _group_of_tile(i, tm):
    per_seq = LAT_LEN // tm
    return jnp.maximum(i - TP // tm + per_seq, 0) // per_seq


def _bdot(a, b):
    return jnp.dot(a, b, preferred_element_type=F32)


def _wdot(a, w):
    return lax.dot_general(a, w, (((1,), (0,)), ((), ())), preferred_element_type=F32)


def _sigmoid(v):
    return 1.0 / (1.0 + jnp.exp(-v))


def _layer_norm(v, g, b):
    mu = jnp.mean(v, axis=-1, keepdims=True)
    d = v - mu
    var = jnp.mean(d * d, axis=-1, keepdims=True)
    return d * lax.rsqrt(var + 1e-5) * g + b


def _rms(v, g):
    return v * lax.rsqrt(jnp.mean(v * v, axis=-1, keepdims=True) + 1e-6) * g


def _mod_kernel(c_ref, w_ref, b_ref, o_ref):
    c = c_ref[...]
    s = (c * _sigmoid(c)).astype(BF16)
    o_ref[...] = _bdot(s, w_ref[...].astype(BF16)) + b_ref[...]


def _modulation(cond8, w_mod, b_mod):
    tn = 1536
    return pl.pallas_call(
        _mod_kernel,
        grid=(DEPTH, 6 * D // tn),
        in_specs=[pl.BlockSpec((8, D), lambda l, n: (0, 0)),
                  pl.BlockSpec((None, D, tn), lambda l, n: (l, 0, n)),
                  pl.BlockSpec((None, 1, tn), lambda l, n: (l, 0, n))],
        out_specs=pl.BlockSpec((None, 8, tn), lambda l, n: (l, 0, n)),
        out_shape=jax.ShapeDtypeStruct((DEPTH, 8, 6 * D), F32),
        compiler_params=_cparams(("arbitrary", "arbitrary")),
        name="modulation",
    )(cond8, w_mod, b_mod.reshape(DEPTH, 1, 6 * D))


def _swap_halves(a, half):
    n = a.shape[-1]
    lane = lax.broadcasted_iota(jnp.int32, a.shape, 1)
    first = (lane & (2 * half - 1)) < half
    return jnp.where(first, -pltpu.roll(a, n - half, axis=1), pltpu.roll(a, half, axis=1))


def _tile_rows(x_refs, tm):
    if len(x_refs) == 1:
        return x_refs[0][...]
    return jnp.where(pl.program_id(0) >= TP // tm, x_refs[1][...], x_refs[0][...])


def _tile_specs(x, tm):
    if not isinstance(x, tuple):
        return (x,), [pl.BlockSpec((tm, D), lambda i, *_: (i, 0))]
    return x, [pl.BlockSpec((tm, D), lambda i, *_: (jnp.minimum(i, TP // tm - 1), 0)),
               pl.BlockSpec((tm, D), lambda i, *_: (jnp.maximum(i - TP // tm, 0), 0))]


def _even_in_kernel(n_x, n_prev, *refs):
    x_refs, refs = refs[:n_x], refs[n_x:]
    (mod_ref, w_in_ref, qg_ref, kvg_ref, wq_ref, wkv_ref, ek_ref,
     cq_ref, sq_ref, ck_ref, sk_ref, cr_ref, sr_ref), refs = refs[:13], refs[13:]
    prev_refs, refs = refs[:2 * n_prev], refs[2 * n_prev:]
    q_ref, k_ref, v_ref, ckv_ref, kpe_ref, rq_ref, rk_ref, rv_ref, rg_ref = refs
    x = _tile_rows(x_refs, TM)
    h = (x * (1.0 + mod_ref[1:2, :]) + mod_ref[0:1, :]).astype(BF16)
    p = _bdot(h, w_in_ref[...])
    qn = _rms(p[:, 0:Q_LORA], qg_ref[...]).astype(BF16)
    qa = _bdot(qn, wq_ref[...])
    ckv = _rms(p[:, COL_KV:COL_RQ], kvg_ref[...])
    kv = _bdot(ckv.astype(BF16), wkv_ref[...])
    v_ref[...] = kv[:, HEADS * LANES:].astype(BF16)
    base = COL_KPE
    ka = p[:, base + LANES:base + 2 * LANES]
    kb = p[:, base + 2 * LANES:base + 3 * LANES]
    rq = p[:, COL_RQ:COL_RK]
    rk = p[:, COL_RK:COL_RV]
    rv_ref[...] = p[:, COL_RV:COL_RG].astype(BF16)
    rg_ref[...] = p[:, COL_RG:COL_KPE]
    lane = lax.broadcasted_iota(jnp.int32, qa.shape, 1) & (LANES - 1)
    qb = jnp.where(lane < NOPE + ROPE // 2,
                   -pltpu.roll(qa, qa.shape[1] - ROPE // 2, axis=1),
                   pltpu.roll(qa, ROPE // 2, axis=1))
    cq = jnp.concatenate([cq_ref[...]] * HEADS, axis=1)
    sq = jnp.concatenate([sq_ref[...]] * HEADS, axis=1)
    q_ref[...] = ((qa * cq + qb * sq) * Q_SCALE).astype(BF16)
    kpe_rot = ka * ck_ref[...] + kb * sk_ref[...]
    k_ref[...] = (kv[:, :HEADS * LANES] + _bdot(kpe_rot.astype(BF16), ek_ref[...])).astype(BF16)
    cr = jnp.concatenate([cr_ref[...]] * (RET_W // LANES), axis=1)
    sr = jnp.concatenate([sr_ref[...]] * (RET_W // LANES), axis=1)
    rq_ref[...] = (rq * cr + _swap_halves(rq, RDK // 2) * sr).astype(BF16)
    rk_ref[...] = (rk * cr + _swap_halves(rk, RDK // 2) * sr).astype(BF16)

    @pl.when(pl.program_id(0) < TP // TM)
    def _():
        for k in range(n_prev):
            ckv_ref[:, k] = prev_refs[2 * k][...]
            kpe_ref[:, k] = prev_refs[2 * k + 1][...]
        for s in range(TM // PROMPT_LEN):
            rows = slice(s * PROMPT_LEN, (s + 1) * PROMPT_LEN)
            ckv_ref[s, n_prev] = ckv[rows]
            kpe_ref[s, n_prev] = p[rows, base:base + ROPE]


def _even_in(x, mod_l, w_in, qg, kvg, wq, wkv, ek, tabs, prev_cache):
    x_args, x_specs = _tile_specs(x, TM)
    n_prev = 0 if prev_cache is None else prev_cache[0].shape[1]
    spt = TM // PROMPT_LEN
    tok = lambda w: pl.BlockSpec((TM, w), lambda i: (i, 0))
    full = lambda a: pl.BlockSpec(a.shape, lambda i: (0,) * a.ndim, pipeline_mode=pl.Buffered(1))
    lat_tiles = LAT_LEN // TM
    tab = pl.BlockSpec((TM, LANES), lambda i: (
        jnp.where(i < TP // TM, lat_tiles, jnp.maximum(i - TP // TM, 0) % lat_tiles), 0))
    seq = lambda i: (jnp.minimum(i, TP // TM - 1), 0, 0, 0)
    seq_k = lambda k, i: (jnp.minimum(i, TP // TM - 1), k, 0, 0)
    prev_args, prev_specs = [], []
    for k in range(n_prev):
        for a, w in zip(prev_cache, (KV_LORA, ROPE)):
            prev_args.append(a)
            prev_specs.append(pl.BlockSpec((spt, None, PROMPT_LEN, w), functools.partial(seq_k, k)))
    tok_outs = lambda dims: ([tok(w) for w, _ in dims], [jax.ShapeDtypeStruct((T, w), dt) for w, dt in dims])
    qkv_specs, qkv_shapes = tok_outs([(HEADS * LANES, BF16)] * 3)
    ret_specs, ret_shapes = tok_outs([(RET_W, BF16), (RET_W, BF16), (RET_W, BF16), (RET_W, F32)])
    cache_specs = [pl.BlockSpec((spt, n_prev + 1, PROMPT_LEN, w), seq) for w in (KV_LORA, ROPE)]
    cache_shapes = [jax.ShapeDtypeStruct((N_PROMPT_SEQ, n_prev + 1, PROMPT_LEN, w), F32) for w in (KV_LORA, ROPE)]
    q, k, v, ckv, kpe, rq, rk, rv, rg = pl.pallas_call(
        functools.partial(_even_in_kernel, len(x_args), n_prev),
        grid=(T // TM,),
        in_specs=x_specs + [pl.BlockSpec((None, 8, D), lambda i: (_group_of_tile(i, TM), 0, 0)),
                            full(w_in), full(qg), full(kvg), full(wq), full(wkv), full(ek)] + [tab] * 6 + prev_specs,
        out_specs=qkv_specs + cache_specs + ret_specs,
        out_shape=qkv_shapes + cache_shapes + ret_shapes,
        compiler_params=_cparams(("arbitrary",)),
        name="even_in",
    )(*x_args, mod_l, w_in, qg, kvg, wq, wkv, ek, *tabs, *prev_args)
    return q, k, v, (ckv, kpe), rq, rk, rv, rg


def _ctx_kv_kernel(ckv_ref, kpe_ref, wkv_ref, ek_ref, k_ref, v_ref):
    kv = _bdot(ckv_ref[...].astype(BF16), wkv_ref[...])
    k_ref[...] = (kv[:, :HEADS * LANES] + _bdot(kpe_ref[...].astype(BF16), ek_ref[...])).astype(BF16)
    v_ref[...] = kv[:, HEADS * LANES:].astype(BF16)


def _ctx_kv(ckv_c, kpe_c, wkv, ek):
    n = ckv_c.shape[0]
    full = lambda a: pl.BlockSpec(a.shape, lambda i: (0,) * a.ndim)
    return pl.pallas_call(
        _ctx_kv_kernel,
        grid=(n // PAST,),
        in_specs=[pl.BlockSpec((PAST, KV_LORA), lambda i: (i, 0)), pl.BlockSpec((PAST, LANES), lambda i: (i, 0)),
                  full(wkv), full(ek)],
        out_specs=[pl.BlockSpec((PAST, HEADS * LANES), lambda i: (i, 0))] * 2,
        out_shape=[jax.ShapeDtypeStruct((n, HEADS * LANES), BF16)] * 2,
        compiler_params=_cparams(("parallel",)),
        name="ctx_kv",
    )(ckv_c, kpe_c, wkv, ek)


def _attn_kernel(n_kv, q_ref, *refs):
    k_refs = refs[0:2 * n_kv:2]
    v_refs = refs[1:2 * n_kv:2]
    o_ref = refs[2 * n_kv]
    nt = (((1,), (1,)), ((), ()))
    for pair in range(HEADS // 2):
        acc = None
        for sub in range(2):
            sl = slice((2 * pair + sub) * LANES, (2 * pair + sub + 1) * LANES)
            qh = q_ref[:, sl]
            s = [lax.dot_general(qh, k[:, sl], nt, preferred_element_type=F32) for k in k_refs]
            m = functools.reduce(jnp.maximum, [jnp.max(a, axis=-1, keepdims=True) for a in s])
            e = [jnp.exp2(a - m) for a in s]
            den = functools.reduce(jnp.add, [jnp.sum(a, axis=-1, keepdims=True) for a in e])
            o = functools.reduce(jnp.add, [_bdot(a.astype(BF16), v[:, sl]) for a, v in zip(e, v_refs)])
            o = o / den
            acc = o if acc is None else acc + o
        o_ref[:, pair * LANES:(pair + 1) * LANES] = acc.astype(BF16)


def _latent_seq(i):
    return jnp.maximum(i - TP // TQA, 0) // (LAT_LEN // TQA)


def _attn_tiles_kernel(q_ref, kp_ref, vp_ref, kc_ref, vc_ref, kl_ref, vl_ref, o_ref):
    is_latent = pl.program_id(0) >= TP // TQA

    @pl.when(jnp.logical_not(is_latent))
    def _():
        for s in range(TQA // PROMPT_LEN):
            rows = pl.ds(s * PROMPT_LEN, PROMPT_LEN)
            _attn_kernel(1, q_ref.at[rows], kp_ref.at[rows], vp_ref.at[rows], o_ref.at[rows])

    @pl.when(is_latent)
    def _():
        _attn_kernel(2, q_ref, kc_ref, vc_ref, kl_ref, vl_ref, o_ref)


def _attention(q, k, v, kc, vc):
    w = HEADS * LANES
    tile = lambda i: (i, 0)
    ctx_own = lambda i: (jnp.minimum(i, TP // TQA - 1), 0)
    cache = lambda i: (_latent_seq(i), 0)
    lat_own = lambda i: (TP // LAT_LEN + _latent_seq(i), 0)
    once = lambda rows, index: pl.BlockSpec((rows, w), index, pipeline_mode=pl.Buffered(1))
    return pl.pallas_call(
        _attn_tiles_kernel,
        grid=(T // TQA,),
        in_specs=[pl.BlockSpec((TQA, w), tile),
                  pl.BlockSpec((TQA, w), ctx_own), pl.BlockSpec((TQA, w), ctx_own),
                  once(PAST, cache), once(PAST, cache), once(LAT_LEN, lat_own), once(LAT_LEN, lat_own)],
        out_specs=pl.BlockSpec((TQA, HEADS * VDIM), tile),
        out_shape=jax.ShapeDtypeStruct((T, HEADS * VDIM), BF16),
        compiler_params=_cparams(("arbitrary",)),
        name="attention",
    )(q, k, v, kc, vc, k, v)


def _ret_prefix_kernel(lg_ref, rk_ref, rv_ref, s0f_ref, s0b_ref, pf_ref, qb_ref):
    s = pl.program_id(0)
    per_seq = LAT_LEN // TQ
    row = lax.broadcasted_iota(jnp.int32, (LANES, 1), 0)
    lane = lax.broadcasted_iota(jnp.int32, (1, LANES), 1)
    top, lo = row < RDK, lane < RDK
    same_head = top == lo
    m_col = lax.broadcasted_iota(jnp.int32, (TQ, 1), 0).astype(F32)

    def scan(lg_off, s0_ref, out_ref, pos, tiles):
        for pair in range(HEADS // 2):
            sl = slice(pair * LANES, (pair + 1) * LANES)
            lg_even, lg_odd = lg_ref[lg_off + 2 * pair], lg_ref[lg_off + 2 * pair + 1]
            dec = jnp.exp2(pos * jnp.where(lo, lg_even, lg_odd))
            tile_decay = jnp.exp2(float(TQ) * jnp.where(top, lg_even, lg_odd))
            state = s0_ref[pair]
            for c in tiles:
                rows = slice(c * TQ, (c + 1) * TQ)
                out_ref[c, pair] = state.astype(BF16)
                local = _bdot((rk_ref[rows, sl].astype(F32) * dec).T.astype(BF16), rv_ref[rows, sl])
                state = state * tile_decay + jnp.where(same_head, local, 0.0)

    @pl.when(s < N_LAT_SEQ)
    def _():
        scan(0, s0f_ref, pf_ref, TQ - 1.0 - m_col, range(per_seq))

    @pl.when(s >= N_LAT_SEQ)
    def _():
        scan(HEADS, s0b_ref, qb_ref, m_col, range(per_seq - 1, -1, -1))


def _ret_prefix(lg, rk, rv, s0f, s0b):
    n_tiles, per_seq = TS // TQ, LAT_LEN // TQ
    seq_of = lambda s: jnp.where(s < N_LAT_SEQ, s, 2 * N_LAT_SEQ - 1 - s)
    st = pl.BlockSpec((None, HEADS // 2, LANES, LANES), lambda s, lg: (seq_of(s), 0, 0, 0))
    kv = pl.BlockSpec((LAT_LEN, HEADS * RDK), lambda s, lg: (TP // LAT_LEN + seq_of(s), 0))
    pf_blk = pl.BlockSpec((per_seq, HEADS // 2, LANES, LANES), lambda s, lg: (jnp.minimum(s, N_LAT_SEQ - 1), 0, 0, 0))
    qb_blk = pl.BlockSpec((per_seq, HEADS // 2, LANES, LANES),
                          lambda s, lg: (jnp.minimum(2 * N_LAT_SEQ - 1 - s, N_LAT_SEQ - 1), 0, 0, 0))
    shape = jax.ShapeDtypeStruct((n_tiles, HEADS // 2, LANES, LANES), BF16)
    return pl.pallas_call(
        _ret_prefix_kernel,
        grid_spec=pltpu.PrefetchScalarGridSpec(
            num_scalar_prefetch=1, grid=(2 * N_LAT_SEQ,),
            in_specs=[kv, kv, st, st], out_specs=[pf_blk, qb_blk]),
        out_shape=[shape, shape],
        compiler_params=_cparams(("arbitrary",)),
        name="ret_prefix",
    )(lg, rk, rv, s0f, s0b)


def _retention_kernel(latent, lg_ref, rq_ref, rk_ref, rv_ref, rg_ref, *refs):
    q0 = 0
    seq_len = rq_ref.shape[0]
    if latent:
        s0f_ref, s0b_ref, o_ref = refs
    else:
        unperm_ref, o_ref, sf_ref, sb_ref = refs
    tq, tk = rq_ref.shape[0], rk_ref.shape[0]
    nt = (((1,), (1,)), ((), ()))
    n_idx = (q0 + lax.broadcasted_iota(jnp.int32, (tq, tk), 0)).astype(F32)
    m_idx = lax.broadcasted_iota(jnp.int32, (tq, tk), 1).astype(F32)
    dist = n_idx - m_idx
    adist = jnp.abs(dist)
    fwd = dist > 0.0
    diag = jnp.where(dist == 0.0, 1.0, 0.0)
    lane = lax.broadcasted_iota(jnp.int32, (1, LANES), 1)
    lo = lane < RDK
    n_col = (q0 + lax.broadcasted_iota(jnp.int32, (tq, 1), 0)).astype(F32)
    m_col = lax.broadcasted_iota(jnp.int32, (tk, 1), 0).astype(F32)
    for pair in range(HEADS // 2):
        sl = slice(pair * LANES, (pair + 1) * LANES)
        qb, kb, vb = rq_ref[:, sl], rk_ref[:, sl], rv_ref[:, sl]
        acc = jnp.zeros((tq, LANES), F32)
        for sub in range(2):
            h = 2 * pair + sub
            lgf, lgb = lg_ref[h], lg_ref[HEADS + h]
            half = lo if sub == 0 else jnp.logical_not(lo)
            qm = jnp.where(half, qb, jnp.zeros_like(qb))
            vm = jnp.where(half, vb, jnp.zeros_like(vb))
            s = lax.dot_general(qm, kb, nt, preferred_element_type=F32)
            w = jnp.exp2(adist * jnp.where(fwd, lgf, lgb)) + diag
            acc = acc + _bdot((s * w).astype(BF16), vm)
        lgf_l = jnp.where(lo, lg_ref[2 * pair], lg_ref[2 * pair + 1])
        lgb_l = jnp.where(lo, lg_ref[HEADS + 2 * pair], lg_ref[HEADS + 2 * pair + 1])
        if latent:
            acc = acc + _bdot(qb, s0f_ref[pair]) * jnp.exp2((n_col + 1.0) * lgf_l)
            acc = acc + _bdot(qb, s0b_ref[pair]) * jnp.exp2((seq_len - n_col) * lgb_l)
        else:
            v_swapped = pltpu.roll(vb.astype(F32), RDK, axis=1).astype(BF16)
            for st_ref, dec in ((sf_ref, jnp.exp2((seq_len - 1.0 - m_col) * lgf_l)),
                                (sb_ref, jnp.exp2(m_col * lgb_l))):
                kt = (kb.astype(F32) * dec).T.astype(BF16)
                kt = _bdot(unperm_ref[...], kt).astype(BF16)
                st_ref[2 * pair] = _bdot(kt, vb)[0:RDK, 0:RDK]
                st_ref[2 * pair + 1] = _bdot(kt, v_swapped)[RDK:, 0:RDK]
        inv = 1.0 / RDK
        mu = jnp.where(lo, jnp.sum(jnp.where(lo, acc, 0.0), axis=-1, keepdims=True),
                       jnp.sum(jnp.where(lo, 0.0, acc), axis=-1, keepdims=True)) * inv
        dlt = acc - mu
        d2 = dlt * dlt
        var = jnp.where(lo, jnp.sum(jnp.where(lo, d2, 0.0), axis=-1, keepdims=True),
                        jnp.sum(jnp.where(lo, 0.0, d2), axis=-1, keepdims=True)) * inv
        g = rg_ref[:, sl]
        o_ref[:, sl] = (dlt * lax.rsqrt(var + 1e-5) * (g * _sigmoid(g))).astype(BF16)


def _retention_tiles_kernel(n_prev, lg_ref, rq_ref, rk_ref, rv_ref, rg_ref, pf_ref, qb_ref, unperm_ref, *refs):
    prev_refs, (o_ref, sf_ref, sb_ref) = refs[:2 * n_prev], refs[2 * n_prev:]
    is_latent = pl.program_id(0) >= TP // TQR
    views = lambda s: [r.at[pl.ds(s * TQ, TQ)] for r in (rq_ref, rk_ref, rv_ref, rg_ref, o_ref)]

    @pl.when(jnp.logical_not(is_latent))
    def _():
        for k in range(n_prev):
            sf_ref[:, k] = prev_refs[2 * k][...]
            sb_ref[:, k] = prev_refs[2 * k + 1][...]
        for s in range(TQR // TQ):
            rq, rk, rv, rg, o = views(s)
            _retention_kernel(False, lg_ref, rq, rk, rv, rg, unperm_ref, o, sf_ref.at[s, n_prev], sb_ref.at[s, n_prev])

    @pl.when(is_latent)
    def _():
        for s in range(TQR // TQ):
            rq, rk, rv, rg, o = views(s)
            _retention_kernel(True, lg_ref, rq, rk, rv, rg, pf_ref.at[s], qb_ref.at[s], o)


def _retention(lg, rq, rk, rv, rg, s0f, s0b, unperm, prev_states):
    w = HEADS * RDK
    n_prev = 0 if prev_states is None else prev_states[0].shape[1]
    pf, qb = _ret_prefix(lg, rk, rv, s0f, s0b)
    assert TQ == PROMPT_LEN
    per_tile = TQR // TQ
    tile = lambda i, lg: (i, 0)
    s0_blk = pl.BlockSpec((per_tile, HEADS // 2, LANES, LANES),
                          lambda i, lg: (jnp.maximum(i - TP // TQR, 0), 0, 0, 0))
    seq = lambda i, lg: (jnp.minimum(i, TP // TQR - 1), 0, 0, 0, 0)
    st_blk = pl.BlockSpec((per_tile, n_prev + 1, HEADS, RDK, RDK), seq)
    st_shape = jax.ShapeDtypeStruct((N_PROMPT_SEQ, n_prev + 1, HEADS, RDK, RDK), F32)
    prev = () if prev_states is None else tuple(prev_states)
    prev_specs = [pl.BlockSpec((per_tile, None, HEADS, RDK, RDK),
                               functools.partial(lambda k, i, lg: (jnp.minimum(i, TP // TQR - 1), k, 0, 0, 0), k))
                  for k in range(n_prev) for _ in range(2)]
    prev_args = [p for k in range(n_prev) for p in prev]
    return pl.pallas_call(
        functools.partial(_retention_tiles_kernel, n_prev),
        grid_spec=pltpu.PrefetchScalarGridSpec(
            num_scalar_prefetch=1, grid=(T // TQR,),
            in_specs=[pl.BlockSpec((TQR, w), tile)] * 4 + [s0_blk, s0_blk,
                      pl.BlockSpec((LANES, LANES), lambda i, lg: (0, 0))] + prev_specs,
            out_specs=[pl.BlockSpec((TQR, w), tile), st_blk, st_blk]),
        out_shape=[jax.ShapeDtypeStruct((T, w), BF16), st_shape, st_shape],
        compiler_params=_cparams(("arbitrary",)),
        name="retention",
    )(lg, rq, rk, rv, rg, pf, qb, unperm, *prev_args)


def _mix_ffn_kernel(n_x, *refs):
    x_refs, refs = refs[:n_x], refs[n_x:]
    a_ref, r_ref, mod_ref, wo_ref, wg_ref, wu_ref, wd_ref, ln_ref, o_ref = refs
    half = HEADS * VDIM
    y = _bdot(a_ref[...], wo_ref[0:half, :]) + _bdot(r_ref[...], wo_ref[half:, :])
    x1 = _layer_norm(ALPHA * _tile_rows(x_refs, TM_FF) + mod_ref[2:3, :] * y, ln_ref[0:1, :], ln_ref[1:2, :])
    h = (x1 * (1.0 + mod_ref[4:5, :]) + mod_ref[3:4, :]).astype(BF16)
    acc = None
    for f in range(D_FF // TF):
        cols = slice(f * TF, (f + 1) * TF)
        g = _bdot(h, wg_ref[:, cols])
        u = _bdot(h, wu_ref[:, cols])
        part = _bdot((g * _sigmoid(g) * u).astype(BF16), wd_ref[cols, :])
        acc = part if acc is None else acc + part
    o_ref[...] = _layer_norm(ALPHA * x1 + mod_ref[5:6, :] * acc, ln_ref[2:3, :], ln_ref[3:4, :])


def _mix_ffn(x, attn, ret, mod_l, w_out, wg, wu, wd, ln, j):
    x_args, x_specs = _tile_specs(x, TM_FF)
    tok = lambda w: pl.BlockSpec((TM_FF, w), lambda i: (i, 0))
    resident = lambda a: pl.BlockSpec((None,) + a.shape[1:], lambda i: (j, 0, 0), pipeline_mode=pl.Buffered(1))
    return pl.pallas_call(
        functools.partial(_mix_ffn_kernel, len(x_args)),
        grid=(T // TM_FF,),
        in_specs=x_specs + [tok(HEADS * VDIM), tok(HEADS * RDK),
                  pl.BlockSpec((None, 8, D), lambda i: (_group_of_tile(i, TM_FF), 0, 0)),
                  resident(w_out), resident(wg), resident(wu), resident(wd), pl.BlockSpec((8, D), lambda i: (0, 0))],
        out_specs=tok(D),
        out_shape=jax.ShapeDtypeStruct((T, D), F32),
        compiler_params=_cparams(("arbitrary",)),
        name="mix_ffn",
    )(*x_args, attn, ret, mod_l, w_out, wg, wu, wd, ln)


def _conv_in_kernel(x_ref, mod_ref, w_ref, b_ref, z_ref):
    h = (x_ref[...] * (1.0 + mod_ref[1:2, :]) + mod_ref[0:1, :]).astype(BF16)
    p = _bdot(h, w_ref[...])
    b_ref[...] = p[:, 0:D]
    z_ref[...] = p[:, D:2 * D] * p[:, 2 * D:3 * D]


def _conv_in(x, mod_l, w_in, j):
    tm = 2 * TM_FF
    tok = pl.BlockSpec((tm, D), lambda i: (i, 0))
    return pl.pallas_call(
        _conv_in_kernel,
        grid=(T // tm,),
        in_specs=[tok, pl.BlockSpec((None, 8, D), lambda i: (_group_of_tile(i, tm), 0, 0)),
                  pl.BlockSpec((None,) + w_in.shape[1:], lambda i: (j, 0, 0), pipeline_mode=pl.Buffered(1))],
        out_specs=[tok, tok],
        out_shape=[jax.ShapeDtypeStruct((T, D), F32)] * 2,
        compiler_params=_cparams(("parallel",)),
        name="conv_in",
    )(x, mod_l, w_in)


def _conv_out_kernel(x_ref, b_ref, z_ref, zp_ref, zn_ref, mod_ref, cw_ref, w_ref, ln_ref, rw_ref, rb_ref,
                     o_ref, h_ref, route_ref):
    i = pl.program_id(0)
    z = z_ref[...]
    tm = z.shape[0]
    row = lax.broadcasted_iota(jnp.int32, (tm, 1), 0)
    seq_len = jnp.where(i < TP // tm, PROMPT_LEN, LAT_LEN)
    pos = (i * tm + row) & (seq_len - 1)
    prev = jnp.where(row == 0, zp_ref[7:8, :], pltpu.roll(z, 1, axis=0))
    prev = jnp.where(pos == 0, 0.0, prev)
    nxt = jnp.where(row == tm - 1, zn_ref[0:1, :], pltpu.roll(z, tm - 1, axis=0))
    nxt = jnp.where(pos == seq_len - 1, 0.0, nxt)
    y = prev * cw_ref[0:1, :] + z * cw_ref[1:2, :] + nxt * cw_ref[2:3, :]
    t = _bdot((b_ref[...] * y).astype(BF16), w_ref[...])
    x1 = _layer_norm(ALPHA * x_ref[...] + mod_ref[2:3, :] * t, ln_ref[0:1, :], ln_ref[1:2, :])
    o_ref[...] = x1
    h = x1 * (1.0 + mod_ref[4:5, :]) + mod_ref[3:4, :]
    h_hi = h.astype(BF16)
    h_ref[...] = h_hi
    h_lo = (h - h_hi.astype(F32)).astype(BF16)
    both = _bdot(h_hi, rw_ref[...])
    logits = both[:, :LANES] + both[:, LANES:] + _bdot(h_lo, rw_ref[:, :LANES]) + rb_ref[...]
    lane = lax.broadcasted_iota(jnp.int32, logits.shape, 1).astype(F32)
    t1 = jnp.max(logits, axis=-1, keepdims=True)
    i1 = jnp.min(jnp.where(logits == t1, lane, float(LANES)), axis=-1, keepdims=True)
    rest = jnp.where(lane == i1, -jnp.inf, logits)
    t2 = jnp.max(rest, axis=-1, keepdims=True)
    i2 = jnp.min(jnp.where(rest == t2, lane, float(LANES)), axis=-1, keepdims=True)
    e = jnp.exp(t2 - t1)
    den = 1.0 + e
    route_ref[...] = jnp.where(lane == 0.0, i1, jnp.where(lane == 1.0, i2,
                               jnp.where(lane == 2.0, 1.0 / den, jnp.where(lane == 3.0, e / den, 0.0))))


def _conv_out(x, b, z, mod_l, cw, w_out, ln, rw, rb, j):
    tok = pl.BlockSpec((TM_FF, D), lambda i: (i, 0))
    sub = TM_FF // 8
    return pl.pallas_call(
        _conv_out_kernel,
        grid=(T // TM_FF,),
        in_specs=[tok, tok, tok,
                  pl.BlockSpec((8, D), lambda i: (jnp.maximum(i * sub - 1, 0), 0)),
                  pl.BlockSpec((8, D), lambda i: (jnp.minimum((i + 1) * sub, T // 8 - 1), 0)),
                  pl.BlockSpec((None, 8, D), lambda i: (_group_of_tile(i, TM_FF), 0, 0)),
                  pl.BlockSpec((8, D), lambda i: (0, 0)), pl.BlockSpec((None, D, D), lambda i: (j, 0, 0)),
                  pl.BlockSpec((8, D), lambda i: (0, 0)),
                  pl.BlockSpec((D, 2 * LANES), lambda i: (0, 0)), pl.BlockSpec((1, LANES), lambda i: (0, 0))],
        out_specs=[tok, tok, pl.BlockSpec((TM_FF, LANES), lambda i: (i, 0))],
        out_shape=[jax.ShapeDtypeStruct((T, D), F32), jax.ShapeDtypeStruct((T, D), BF16),
                   jax.ShapeDtypeStruct((T, LANES), F32)],
        compiler_params=_cparams(("parallel",)),
        name="conv_out",
    )(x, b, z, z, z, mod_l, cw, w_out, ln, rw, rb)


def _moe_up_kernel(te_ref, nv_ref, x_ref, wg_ref, wu_ref, a_ref):
    @pl.when(pl.program_id(1) < nv_ref[0])
    def _():
        h = x_ref[...]
        g = _wdot(h, wg_ref[...])
        u = _wdot(h, wu_ref[...])
        a_ref[...] = (g * _sigmoid(g) * u).astype(BF16)

    @pl.when(pl.program_id(1) >= nv_ref[0])
    def _():
        a_ref[...] = jnp.zeros_like(a_ref)


def _moe_down_kernel(te_ref, nv_ref, a_ref, wd_ref, o_ref):
    @pl.when(pl.program_id(0) < nv_ref[0])
    def _():
        o_ref[...] = _wdot(a_ref[...], wd_ref[...])

    @pl.when(pl.program_id(0) >= nv_ref[0])
    def _():
        o_ref[...] = jnp.zeros_like(o_ref)


def _moe(tile_expert, n_valid, xs, wg, wu, wd, j):
    n_tiles = NP_ROWS // TM_FF
    act = pl.pallas_call(
        _moe_up_kernel,
        grid_spec=pltpu.PrefetchScalarGridSpec(
            num_scalar_prefetch=2, grid=(D_FF // TF, n_tiles),
            in_specs=[pl.BlockSpec((TM_FF, D), lambda f, i, te, nv: (i, 0)),
                      pl.BlockSpec((None, None, D, TF), lambda f, i, te, nv: (j, te[i], 0, f)),
                      pl.BlockSpec((None, None, D, TF), lambda f, i, te, nv: (j, te[i], 0, f))],
            out_specs=pl.BlockSpec((TM_FF, TF), lambda f, i, te, nv: (i, f))),
        out_shape=jax.ShapeDtypeStruct((NP_ROWS, D_FF), BF16),
        compiler_params=_cparams(("arbitrary", "arbitrary")),
        name="moe_up",
    )(tile_expert, n_valid, xs, wg, wu)
    return pl.pallas_call(
        _moe_down_kernel,
        grid_spec=pltpu.PrefetchScalarGridSpec(
            num_scalar_prefetch=2, grid=(n_tiles,),
            in_specs=[pl.BlockSpec((TM_FF, D_FF), lambda i, te, nv: (i, 0)),
                      pl.BlockSpec((None, None, D_FF, D), lambda i, te, nv: (j, te[i], 0, 0))],
            out_specs=pl.BlockSpec((TM_FF, D), lambda i, te, nv: (i, 0))),
        out_shape=jax.ShapeDtypeStruct((NP_ROWS, D), F32),
        compiler_params=_cparams(("arbitrary",)),
        name="moe_down",
    )(tile_expert, n_valid, act, wd)


def _combine_kernel(split, x_ref, o0_ref, o1_ref, route_ref, mod_ref, ln_ref, *o_refs):
    y = route_ref[:, 2:3] * o0_ref[...] + route_ref[:, 3:4] * o1_ref[...]
    out = _layer_norm(ALPHA * x_ref[...] + mod_ref[5:6, :] * y, ln_ref[2:3, :], ln_ref[3:4, :])
    if not split:
        o_refs[0][...] = out
        return
    is_latent = pl.program_id(0) >= TP // TM_FF

    @pl.when(jnp.logical_not(is_latent))
    def _():
        o_refs[0][...] = out

    @pl.when(is_latent)
    def _():
        o_refs[1][...] = out


def _combine(x, o0, o1, route, mod_l, ln, split):
    tok = pl.BlockSpec((TM_FF, D), lambda i: (i, 0))
    if split:
        out_specs = [pl.BlockSpec((TM_FF, D), lambda i: (jnp.minimum(i, TP // TM_FF - 1), 0)),
                     pl.BlockSpec((TM_FF, D), lambda i: (jnp.maximum(i - TP // TM_FF, 0), 0))]
        out_shape = [jax.ShapeDtypeStruct((TP, D), F32), jax.ShapeDtypeStruct((TS, D), F32)]
    else:
        out_specs, out_shape = tok, jax.ShapeDtypeStruct((T, D), F32)
    return pl.pallas_call(
        functools.partial(_combine_kernel, split),
        grid=(T // TM_FF,),
        in_specs=[tok, tok, tok, pl.BlockSpec((TM_FF, LANES), lambda i: (i, 0)),
                  pl.BlockSpec((None, 8, D), lambda i: (_group_of_tile(i, TM_FF), 0, 0)),
                  pl.BlockSpec((8, D), lambda i: (0, 0))],
        out_specs=out_specs,
        out_shape=out_shape,
        compiler_params=_cparams(("arbitrary",)),
        name="moe_combine",
    )(x, o0, o1, route, mod_l, ln)


def _routing_plan(route):
    e = jnp.concatenate([route[:, 0], route[:, 1]]).astype(jnp.int32)
    onehot = (e[:, None] == jnp.arange(N_EXP, dtype=jnp.int32)[None, :]).astype(jnp.int32)
    csum = jnp.cumsum(onehot, axis=0)
    counts = csum[-1]
    rank = jnp.sum((csum - onehot) * onehot, axis=1)
    padded = (counts + TM_FF - 1) // TM_FF * TM_FF
    pend = jnp.cumsum(padded)
    dest = jnp.sum(onehot * (pend - padded)[None, :], axis=1) + rank
    order = jnp.argsort(e, stable=True).astype(jnp.int32)
    rows = jnp.arange(NP_ROWS, dtype=jnp.int32)
    before = (rows[:, None] >= pend[None, :]).astype(jnp.int32)
    row_e = jnp.minimum(jnp.sum(before, axis=1), N_EXP - 1)
    row_cnt = jnp.sum((row_e[:, None] == jnp.arange(N_EXP, dtype=jnp.int32)[None, :]) * counts[None, :], axis=1)
    q = jnp.clip(rows - jnp.sum(before * padded[None, :], axis=1), 0, jnp.maximum(row_cnt - 1, 0))
    src = jnp.minimum(jnp.sum(before * counts[None, :], axis=1) + q, 2 * T - 1)
    row_token = order[src] % T
    n_valid = (pend[-1] // TM_FF).astype(jnp.int32)
    tile_start = jnp.minimum(jnp.arange(NP_ROWS // TM_FF, dtype=jnp.int32), n_valid - 1) * TM_FF
    tile_expert = jnp.minimum(jnp.sum((tile_start[:, None] >= pend[None, :]).astype(jnp.int32), axis=1), N_EXP - 1)
    return dest[:T], dest[T:], row_token, tile_expert.astype(jnp.int32), n_valid.reshape(1)


_INV_PERM_RDK = np.argsort(np.concatenate([np.arange(0, RDK, 2), np.arange(1, RDK, 2)]))


def _pad_cols(a, width):
    return jnp.pad(a, ((0, 0), (0, width - a.shape[1])))


def _deinterleave(a):
    n = a.shape[-1]
    return jnp.swapaxes(a.reshape(a.shape[:-1] + (n // 2, 2)), -1, -2).reshape(a.shape)


def _prep_even_weights(w_in, w_q_b, w_kv_b):
    o_kpe = Q_LORA + KV_LORA
    o_rq, o_rk, o_rv = o_kpe + ROPE, o_kpe + ROPE + RET_W, o_kpe + ROPE + 2 * RET_W
    heads = lambda a: _deinterleave(a.reshape(D, HEADS, RDK)).reshape(D, HEADS * RDK)
    kpe = w_in[:, o_kpe:o_kpe + ROPE]
    w_in_p = jnp.concatenate([
        w_in[:, :o_kpe], heads(w_in[:, o_rq:o_rk]), heads(w_in[:, o_rk:o_rv]) * (RDK ** -0.5), w_in[:, o_rv:],
        _pad_cols(kpe, LANES), _pad_cols(_deinterleave(kpe), LANES),
        _pad_cols(jnp.concatenate([-kpe[:, 1::2], kpe[:, 0::2]], axis=1), LANES)], axis=1).astype(BF16)
    wq = w_q_b.reshape(Q_LORA, HEADS, NOPE + ROPE)
    wq = jnp.concatenate([wq[:, :, :NOPE], _deinterleave(wq[:, :, NOPE:]),
                          jnp.zeros((Q_LORA, HEADS, LANES - NOPE - ROPE), F32)], axis=2)
    wq = wq.reshape(Q_LORA, HEADS * LANES).astype(BF16)
    wkv = w_kv_b.reshape(KV_LORA, HEADS, NOPE + VDIM)
    zero = jnp.zeros((KV_LORA, HEADS, LANES - NOPE), F32)
    wk = jnp.concatenate([wkv[:, :, :NOPE], zero], axis=2).reshape(KV_LORA, HEADS * LANES)
    wv = wkv[:, :, NOPE:].reshape(KV_LORA, HEADS // 2, 2, VDIM)
    zv = jnp.zeros((KV_LORA, HEADS // 2, VDIM), F32)
    wv = jnp.stack([jnp.concatenate([wv[:, :, 0], zv], axis=2), jnp.concatenate([zv, wv[:, :, 1]], axis=2)], axis=2)
    wkv_p = jnp.concatenate([wk, wv.reshape(KV_LORA, HEADS * LANES)], axis=1).astype(BF16)
    return w_in_p, wq, wkv_p


def _placement():
    ek = np.zeros((LANES, HEADS * LANES), np.float32)
    for h in range(HEADS):
        ek[np.arange(ROPE), h * LANES + NOPE + np.arange(ROPE)] = 1.0
    return jnp.asarray(ek, BF16)


def _rotary_tables():
    rows = LAT_LEN // GRID_W
    r, col = jnp.meshgrid(jnp.arange(rows, dtype=F32), jnp.arange(GRID_W, dtype=F32), indexing='ij')
    n_freq = ROPE // 4
    freqs = 1.0 / (10000.0 ** (jnp.arange(n_freq, dtype=F32) / n_freq))
    ang = jnp.concatenate([r.reshape(-1)[:, None] * freqs, col.reshape(-1)[:, None] * freqs], axis=-1)
    cos, sin = jnp.cos(ang), jnp.sin(ang)
    theta = 1.0 / (10000.0 ** jnp.linspace(0.0, 1.0, RDK // 2, dtype=F32))
    rang = jnp.arange(LAT_LEN, dtype=F32)[:, None] * theta
    rcos, rsin = jnp.cos(rang), jnp.sin(rang)
    one = lambda w: jnp.ones((LAT_LEN, w), F32)
    zero = lambda w: jnp.zeros((LAT_LEN, w), F32)
    lat = [jnp.concatenate([one(NOPE), cos, cos, one(LANES - NOPE - ROPE)], axis=1),
           jnp.concatenate([zero(NOPE), sin, sin, zero(LANES - NOPE - ROPE)], axis=1),
           jnp.concatenate([cos, cos, zero(LANES - ROPE)], axis=1),
           jnp.concatenate([sin, sin, zero(LANES - ROPE)], axis=1),
           jnp.concatenate([rcos] * 4, axis=1), jnp.concatenate([rsin] * 4, axis=1)]
    ident = [np.ones((TM, LANES), np.float32), np.zeros((TM, LANES), np.float32)]
    ident_k = np.concatenate([np.ones((TM, ROPE), np.float32), np.zeros((TM, LANES - ROPE), np.float32)], axis=1)
    ident = [ident[0], ident[1], ident_k, ident[1], ident[0], ident[1]]
    return [jnp.concatenate([l, jnp.asarray(c)], axis=0) for l, c in zip(lat, ident)]


def _block_diag_states(s0):
    s = jnp.swapaxes(_deinterleave(jnp.swapaxes(s0, -1, -2)), -1, -2)
    s = s.reshape(s0.shape[0], HEADS // 2, 2, RDK, RDK)
    z = jnp.zeros_like(s[:, :, 0])
    top = jnp.concatenate([s[:, :, 0], z], axis=-1)
    bot = jnp.concatenate([z, s[:, :, 1]], axis=-1)
    return jnp.concatenate([top, bot], axis=-2)


def _unpermute_matrix():
    m = np.zeros((LANES, LANES), np.float32)
    for blk in range(LANES // RDK):
        m[blk * RDK + np.arange(RDK), blk * RDK + _INV_PERM_RDK] = 1.0
    return jnp.asarray(m, BF16)


def kernel(x_prompt, x_sample, c, cache_ckv, cache_kpe, state_ret_fwd, state_ret_bwd, c_ctx, w_mod, b_mod, ln_g, ln_b, w_in_mix, q_a_gain, kv_a_gain, w_q_b, w_kv_b, ret_decay_fwd, ret_decay_bwd, w_out_mix, w_in_conv, conv_w, w_out_conv, ffn_gate, ffn_up, ffn_down, router_w, router_b, exp_gate, exp_up, exp_down):
    x = (x_prompt.reshape(TP, D), x_sample.reshape(TS, D))
    cond8 = jnp.concatenate([c_ctx[None], c, jnp.zeros((8 - 1 - N_LAT_SEQ, D), F32)], axis=0)
    mods = _modulation(cond8, w_mod, b_mod)
    mods = jnp.pad(mods.reshape(DEPTH, N_GROUPS, 6, D), ((0, 0), (0, 0), (0, 2), (0, 0)))
    ln = jnp.pad(jnp.concatenate([ln_g, ln_b], axis=1)[:, jnp.array([0, 2, 1, 3])], ((0, 0), (0, 4), (0, 0)))
    tabs = _rotary_tables()
    ek = _placement()
    unperm = _unpermute_matrix()
    bf = lambda a: a.astype(BF16)
    w_out_mix_b, w_in_conv_b, w_out_conv_b = bf(w_out_mix), bf(w_in_conv), bf(w_out_conv)
    ffn_b = (bf(ffn_gate), bf(ffn_up), bf(ffn_down))
    exp_b = (exp_gate, exp_up, exp_down)
    cache, states = None, None
    for layer in range(DEPTH):
        j = layer // 2
        mod_l, ln_l = mods[layer], ln[layer]
        if layer % 2 == 0:
            w_in_p, wq, wkv = _prep_even_weights(w_in_mix[j], w_q_b[j], w_kv_b[j])
            q, k, v, cache, rq, rk, rv, rg = _even_in(
                x, mod_l, w_in_p, q_a_gain[j][None], kv_a_gain[j][None], wq, wkv, ek, tabs, cache)
            kpe_c = _pad_cols(_deinterleave(cache_kpe[:, j]).reshape(N_LAT_SEQ * PAST, ROPE), LANES)
            kc, vc = _ctx_kv(cache_ckv[:, j].reshape(N_LAT_SEQ * PAST, KV_LORA), kpe_c, wkv, ek)
            attn = _attention(q, k, v, kc, vc)
            lg = LOG2E * jnp.concatenate([jax.nn.log_sigmoid(ret_decay_fwd[j].astype(F32)),
                                          jax.nn.log_sigmoid(ret_decay_bwd[j].astype(F32))])
            ret, sf, sb = _retention(lg, rq, rk, rv, rg, _block_diag_states(state_ret_fwd[:, j]),
                                     _block_diag_states(state_ret_bwd[:, j]), unperm, states)
            states = (sf, sb)
            x = _mix_ffn(x, attn, ret, mod_l, w_out_mix_b, *ffn_b, ln_l, j)
        else:
            b, z = _conv_in(x, mod_l, w_in_conv_b, j)
            cw = jnp.pad(conv_w[j], ((0, 5), (0, 0)))
            rw = _pad_cols(router_w[j], LANES)
            rw_hi = rw.astype(BF16)
            rw = jnp.concatenate([rw_hi, (rw - rw_hi.astype(F32)).astype(BF16)], axis=1)
            rb = jnp.concatenate([router_b[j].astype(F32), jnp.full((LANES - N_EXP,), -1e30, F32)])[None]
            x, h, route = _conv_out(x, b, z, mod_l, cw, w_out_conv_b, ln_l, rw, rb, j)
            dest0, dest1, row_token, tile_expert, n_valid = _routing_plan(route)
            out_sorted = _moe(tile_expert, n_valid, h[row_token], *exp_b, j)
            x = _combine(x, out_sorted[dest0], out_sorted[dest1], route, mod_l, ln_l, split=layer == DEPTH - 1)
    y_prompt = x[0].reshape(N_PROMPT_SEQ, PROMPT_LEN, D)
    y_sample = x[1].reshape(N_LAT_SEQ, LAT_LEN, D)
    return (y_prompt, y_sample, cache[0], cache[1], states[0], states[1])
```

```python
import functools

import numpy as np
import jax
import jax.numpy as jnp
from jax import lax
from jax.experimental import pallas as pl
from jax.experimental.pallas import tpu as pltpu

F32 = jnp.float32
BF16 = jnp.bfloat16

D = 1024
DEPTH = 4
N_PROMPT_SEQ, PROMPT_LEN = 32, 256
N_LAT_SEQ, LAT_LEN = 2, 2048
PAST = 512
GRID_W = 64
TP = N_PROMPT_SEQ * PROMPT_LEN
TS = N_LAT_SEQ * LAT_LEN
T = TP + TS
HEADS = 8
NOPE, ROPE, VDIM = 64, 32, 64
Q_LORA, KV_LORA = 384, 256
RDK = 64
D_FF = 2816
N_EXP = 8
ALPHA = (2.0 * DEPTH) ** 0.25
LOG2E = float(np.log2(np.e))
Q_SCALE = float((NOPE + ROPE) ** -0.5) * LOG2E
LANES = 128
N_GROUPS = 8

TM = 512
TM_FF = 512
TF = D_FF // 2
TQ = 256
TQA = 512
TQR = 1024
NP_ROWS = 2 * T + N_EXP * TM_FF
VMEM_LIMIT = 56 * 1024 * 1024

RET_W = HEADS * RDK
COL_KV = Q_LORA
COL_RQ = COL_KV + KV_LORA
COL_RK, COL_RV, COL_RG = COL_RQ + RET_W, COL_RQ + 2 * RET_W, COL_RQ + 3 * RET_W
COL_KPE = COL_RQ + 4 * RET_W
IN_COLS = COL_KPE + 3 * LANES


def _cparams(sem):
    return pltpu.CompilerParams(dimension_semantics=sem, vmem_limit_bytes=VMEM_LIMIT)


def _group_of_tile(i, tm):
    per_seq = LAT_LEN // tm
    return jnp.maximum(i - TP // tm + per_seq, 0) // per_seq


def _bdot(a, b):
    return jnp.dot(a, b, preferred_element_type=F32)


def _wdot(a, w):
    return lax.dot_general(a, w, (((1,), (0,)), ((), ())), preferred_element_type=F32)


def _sigmoid(v):
    return 1.0 / (1.0 + jnp.exp(-v))


def _layer_norm(v, g, b):
    mu = jnp.mean(v, axis=-1, keepdims=True)
    d = v - mu
    var = jnp.mean(d * d, axis=-1, keepdims=True)
    return d * lax.rsqrt(var + 1e-5) * g + b


def _rms(v, g):
    return v * lax.rsqrt(jnp.mean(v * v, axis=-1, keepdims=True) + 1e-6) * g


def _mod_kernel(c_ref, w_ref, b_ref, o_ref):
    c = c_ref[...]
    s = (c * _sigmoid(c)).astype(BF16)
    o_ref[...] = _bdot(s, w_ref[...].astype(BF16)) + b_ref[...]


def _modulation(cond8, w_mod, b_mod):
    tn = 1536
    return pl.pallas_call(
        _mod_kernel,
        grid=(DEPTH, 6 * D // tn),
        in_specs=[pl.BlockSpec((8, D), lambda l, n: (0, 0)),
                  pl.BlockSpec((None, D, tn), lambda l, n: (l, 0, n)),
                  pl.BlockSpec((None, 1, tn), lambda l, n: (l, 0, n))],
        out_specs=pl.BlockSpec((None, 8, tn), lambda l, n: (l, 0, n)),
        out_shape=jax.ShapeDtypeStruct((DEPTH, 8, 6 * D), F32),
        compiler_params=_cparams(("arbitrary", "arbitrary")),
        name="modulation",
    )(cond8, w_mod, b_mod.reshape(DEPTH, 1, 6 * D))


def _swap_halves(a, half):
    n = a.shape[-1]
    lane = lax.broadcasted_iota(jnp.int32, a.shape, 1)
    first = (lane & (2 * half - 1)) < half
    return jnp.where(first, -pltpu.roll(a, n - half, axis=1), pltpu.roll(a, half, axis=1))


def _tile_rows(x_refs, tm):
    if len(x_refs) == 1:
        return x_refs[0][...]
    return jnp.where(pl.program_id(0) >= TP // tm, x_refs[1][...], x_refs[0][...])


def _tile_specs(x, tm):
    if not isinstance(x, tuple):
        return (x,), [pl.BlockSpec((tm, D), lambda i, *_: (i, 0))]
    return x, [pl.BlockSpec((tm, D), lambda i, *_: (jnp.minimum(i, TP // tm - 1), 0)),
               pl.BlockSpec((tm, D), lambda i, *_: (jnp.maximum(i - TP // tm, 0), 0))]


def _even_in_kernel(n_x, n_prev, *refs):
    x_refs, refs = refs[:n_x], refs[n_x:]
    (mod_ref, w_in_ref, qg_ref, kvg_ref, wq_ref, wkv_ref, ek_ref,
     cq_ref, sq_ref, ck_ref, sk_ref, cr_ref, sr_ref), refs = refs[:13], refs[13:]
    prev_refs, refs = refs[:2 * n_prev], refs[2 * n_prev:]
    q_ref, k_ref, v_ref, ckv_ref, kpe_ref, rq_ref, rk_ref, rv_ref, rg_ref = refs
    x = _tile_rows(x_refs, TM)
    h = (x * (1.0 + mod_ref[1:2, :]) + mod_ref[0:1, :]).astype(BF16)
    p = _bdot(h, w_in_ref[...])
    qn = _rms(p[:, 0:Q_LORA], qg_ref[...]).astype(BF16)
    qa = _bdot(qn, wq_ref[...])
    ckv = _rms(p[:, COL_KV:COL_RQ], kvg_ref[...])
    kv = _bdot(ckv.astype(BF16), wkv_ref[...])
    v_ref[...] = kv[:, HEADS * LANES:].astype(BF16)
    base = COL_KPE
    ka = p[:, base + LANES:base + 2 * LANES]
    kb = p[:, base + 2 * LANES:base + 3 * LANES]
    rq = p[:, COL_RQ:COL_RK]
    rk = p[:, COL_RK:COL_RV]
    rv_ref[...] = p[:, COL_RV:COL_RG].astype(BF16)
    rg_ref[...] = p[:, COL_RG:COL_KPE]
    lane = lax.broadcasted_iota(jnp.int32, qa.shape, 1) & (LANES - 1)
    qb = jnp.where(lane < NOPE + ROPE // 2,
                   -pltpu.roll(qa, qa.shape[1] - ROPE // 2, axis=1),
                   pltpu.roll(qa, ROPE // 2, axis=1))
    cq = jnp.concatenate([cq_ref[...]] * HEADS, axis=1)
    sq = jnp.concatenate([sq_ref[...]] * HEADS, axis=1)
    q_ref[...] = ((qa * cq + qb * sq) * Q_SCALE).astype(BF16)
    kpe_rot = ka * ck_ref[...] + kb * sk_ref[...]
    k_ref[...] = (kv[:, :HEADS * LANES] + _bdot(kpe_rot.astype(BF16), ek_ref[...])).astype(BF16)
    cr = jnp.concatenate([cr_ref[...]] * (RET_W // LANES), axis=1)
    sr = jnp.concatenate([sr_ref[...]] * (RET_W // LANES), axis=1)
    rq_ref[...] = (rq * cr + _swap_halves(rq, RDK // 2) * sr).astype(BF16)
    rk_ref[...] = (rk * cr + _swap_halves(rk, RDK // 2) * sr).astype(BF16)

    @pl.when(pl.program_id(0) < TP // TM)
    def _():
        for k in range(n_prev):
            ckv_ref[:, k] = prev_refs[2 * k][...]
            kpe_ref[:, k] = prev_refs[2 * k + 1][...]
        for s in range(TM // PROMPT_LEN):
            rows = slice(s * PROMPT_LEN, (s + 1) * PROMPT_LEN)
            ckv_ref[s, n_prev] = ckv[rows]
            kpe_ref[s, n_prev] = p[rows, base:base + ROPE]


def _even_in(x, mod_l, w_in, qg, kvg, wq, wkv, ek, tabs, prev_cache):
    x_args, x_specs = _tile_specs(x, TM)
    n_prev = 0 if prev_cache is None else prev_cache[0].shape[1]
    spt = TM // PROMPT_LEN
    tok = lambda w: pl.BlockSpec((TM, w), lambda i: (i, 0))
    full = lambda a: pl.BlockSpec(a.shape, lambda i: (0,) * a.ndim, pipeline_mode=pl.Buffered(1))
    lat_tiles = LAT_LEN // TM
    tab = pl.BlockSpec((TM, LANES), lambda i: (
        jnp.where(i < TP // TM, lat_tiles, jnp.maximum(i - TP // TM, 0) % lat_tiles), 0))
    seq = lambda i: (jnp.minimum(i, TP // TM - 1), 0, 0, 0)
    seq_k = lambda k, i: (jnp.minimum(i, TP // TM - 1), k, 0, 0)
    prev_args, prev_specs = [], []
    for k in range(n_prev):
        for a, w in zip(prev_cache, (KV_LORA, ROPE)):
            prev_args.append(a)
            prev_specs.append(pl.BlockSpec((spt, None, PROMPT_LEN, w), functools.partial(seq_k, k)))
    tok_outs = lambda dims: ([tok(w) for w, _ in dims], [jax.ShapeDtypeStruct((T, w), dt) for w, dt in dims])
    qkv_specs, qkv_shapes = tok_outs([(HEADS * LANES, BF16)] * 3)
    ret_specs, ret_shapes = tok_outs([(RET_W, BF16), (RET_W, BF16), (RET_W, BF16), (RET_W, F32)])
    cache_specs = [pl.BlockSpec((spt, n_prev + 1, PROMPT_LEN, w), seq) for w in (KV_LORA, ROPE)]
    cache_shapes = [jax.ShapeDtypeStruct((N_PROMPT_SEQ, n_prev + 1, PROMPT_LEN, w), F32) for w in (KV_LORA, ROPE)]
    q, k, v, ckv, kpe, rq, rk, rv, rg = pl.pallas_call(
        functools.partial(_even_in_kernel, len(x_args), n_prev),
        grid=(T // TM,),
        in_specs=x_specs + [pl.BlockSpec((None, 8, D), lambda i: (_group_of_tile(i, TM), 0, 0)),
                            full(w_in), full(qg), full(kvg), full(wq), full(wkv), full(ek)] + [tab] * 6 + prev_specs,
        out_specs=qkv_specs + cache_specs + ret_specs,
        out_shape=qkv_shapes + cache_shapes + ret_shapes,
        compiler_params=_cparams(("arbitrary",)),
        name="even_in",
    )(*x_args, mod_l, w_in, qg, kvg, wq, wkv, ek, *tabs, *prev_args)
    return q, k, v, (ckv, kpe), rq, rk, rv, rg


def _ctx_kv_kernel(ckv_ref, kpe_ref, wkv_ref, ek_ref, k_ref, v_ref):
    kv = _bdot(ckv_ref[...].astype(BF16), wkv_ref[...])
    k_ref[...] = (kv[:, :HEADS * LANES] + _bdot(kpe_ref[...].astype(BF16), ek_ref[...])).astype(BF16)
    v_ref[...] = kv[:, HEADS * LANES:].astype(BF16)


def _ctx_kv(ckv_c, kpe_c, wkv, ek):
    n = ckv_c.shape[0]
    full = lambda a: pl.BlockSpec(a.shape, lambda i: (0,) * a.ndim)
    return pl.pallas_call(
        _ctx_kv_kernel,
        grid=(n // PAST,),
        in_specs=[pl.BlockSpec((PAST, KV_LORA), lambda i: (i, 0)), pl.BlockSpec((PAST, LANES), lambda i: (i, 0)),
                  full(wkv), full(ek)],
        out_specs=[pl.BlockSpec((PAST, HEADS * LANES), lambda i: (i, 0))] * 2,
        out_shape=[jax.ShapeDtypeStruct((n, HEADS * LANES), BF16)] * 2,
        compiler_params=_cparams(("parallel",)),
        name="ctx_kv",
    )(ckv_c, kpe_c, wkv, ek)


def _attn_kernel(n_kv, q_ref, *refs):
    k_refs = refs[0:2 * n_kv:2]
    v_refs = refs[1:2 * n_kv:2]
    o_ref = refs[2 * n_kv]
    nt = (((1,), (1,)), ((), ()))
    for pair in range(HEADS // 2):
        acc = None
        for sub in range(2):
            sl = slice((2 * pair + sub) * LANES, (2 * pair + sub + 1) * LANES)
            qh = q_ref[:, sl]
            s = [lax.dot_general(qh, k[:, sl], nt, preferred_element_type=F32) for k in k_refs]
            m = functools.reduce(jnp.maximum, [jnp.max(a, axis=-1, keepdims=True) for a in s])
            e = [jnp.exp2(a - m) for a in s]
            den = functools.reduce(jnp.add, [jnp.sum(a, axis=-1, keepdims=True) for a in e])
            o = functools.reduce(jnp.add, [_bdot(a.astype(BF16), v[:, sl]) for a, v in zip(e, v_refs)])
            o = o / den
            acc = o if acc is None else acc + o
        o_ref[:, pair * LANES:(pair + 1) * LANES] = acc.astype(BF16)


def _latent_seq(i):
    return jnp.maximum(i - TP // TQA, 0) // (LAT_LEN // TQA)


def _attn_tiles_kernel(q_ref, kp_ref, vp_ref, kc_ref, vc_ref, kl_ref, vl_ref, o_ref):
    is_latent = pl.program_id(0) >= TP // TQA

    @pl.when(jnp.logical_not(is_latent))
    def _():
        for s in range(TQA // PROMPT_LEN):
            rows = pl.ds(s * PROMPT_LEN, PROMPT_LEN)
            _attn_kernel(1, q_ref.at[rows], kp_ref.at[rows], vp_ref.at[rows], o_ref.at[rows])

    @pl.when(is_latent)
    def _():
        _attn_kernel(2, q_ref, kc_ref, vc_ref, kl_ref, vl_ref, o_ref)


def _attention(q, k, v, kc, vc):
    w = HEADS * LANES
    tile = lambda i: (i, 0)
    ctx_own = lambda i: (jnp.minimum(i, TP // TQA - 1), 0)
    cache = lambda i: (_latent_seq(i), 0)
    lat_own = lambda i: (TP // LAT_LEN + _latent_seq(i), 0)
    once = lambda rows, index: pl.BlockSpec((rows, w), index, pipeline_mode=pl.Buffered(1))
    return pl.pallas_call(
        _attn_tiles_kernel,
        grid=(T // TQA,),
        in_specs=[pl.BlockSpec((TQA, w), tile),
                  pl.BlockSpec((TQA, w), ctx_own), pl.BlockSpec((TQA, w), ctx_own),
                  once(PAST, cache), once(PAST, cache), once(LAT_LEN, lat_own), once(LAT_LEN, lat_own)],
        out_specs=pl.BlockSpec((TQA, HEADS * VDIM), tile),
        out_shape=jax.ShapeDtypeStruct((T, HEADS * VDIM), BF16),
        compiler_params=_cparams(("arbitrary",)),
        name="attention",
    )(q, k, v, kc, vc, k, v)


def _ret_prefix_kernel(lg_ref, rk_ref, rv_ref, s0f_ref, s0b_ref, pf_ref, qb_ref):
    s = pl.program_id(0)
    per_seq = LAT_LEN // TQ
    row = lax.broadcasted_iota(jnp.int32, (LANES, 1), 0)
    lane = lax.broadcasted_iota(jnp.int32, (1, LANES), 1)
    top, lo = row < RDK, lane < RDK
    same_head = top == lo
    m_col = lax.broadcasted_iota(jnp.int32, (TQ, 1), 0).astype(F32)

    def scan(lg_off, s0_ref, out_ref, pos, tiles):
        for pair in range(HEADS // 2):
            sl = slice(pair * LANES, (pair + 1) * LANES)
            lg_even, lg_odd = lg_ref[lg_off + 2 * pair], lg_ref[lg_off + 2 * pair + 1]
            dec = jnp.exp2(pos * jnp.where(lo, lg_even, lg_odd))
            tile_decay = jnp.exp2(float(TQ) * jnp.where(top, lg_even, lg_odd))
            state = s0_ref[pair]
            for c in tiles:
                rows = slice(c * TQ, (c + 1) * TQ)
                out_ref[c, pair] = state.astype(BF16)
                local = _bdot((rk_ref[rows, sl].astype(F32) * dec).T.astype(BF16), rv_ref[rows, sl])
                state = state * tile_decay + jnp.where(same_head, local, 0.0)

    @pl.when(s < N_LAT_SEQ)
    def _():
        scan(0, s0f_ref, pf_ref, TQ - 1.0 - m_col, range(per_seq))

    @pl.when(s >= N_LAT_SEQ)
    def _():
        scan(HEADS, s0b_ref, qb_ref, m_col, range(per_seq - 1, -1, -1))


def _ret_prefix(lg, rk, rv, s0f, s0b):
    n_tiles, per_seq = TS // TQ, LAT_LEN // TQ
    seq_of = lambda s: jnp.where(s < N_LAT_SEQ, s, 2 * N_LAT_SEQ - 1 - s)
    st = pl.BlockSpec((None, HEADS // 2, LANES, LANES), lambda s, lg: (seq_of(s), 0, 0, 0))
    kv = pl.BlockSpec((LAT_LEN, HEADS * RDK), lambda s, lg: (TP // LAT_LEN + seq_of(s), 0))
    pf_blk = pl.BlockSpec((per_seq, HEADS // 2, LANES, LANES), lambda s, lg: (jnp.minimum(s, N_LAT_SEQ - 1), 0, 0, 0))
    qb_blk = pl.BlockSpec((per_seq, HEADS // 2, LANES, LANES),
                          lambda s, lg: (jnp.minimum(2 * N_LAT_SEQ - 1 - s, N_LAT_SEQ - 1), 0, 0, 0))
    shape = jax.ShapeDtypeStruct((n_tiles, HEADS // 2, LANES, LANES), BF16)
    return pl.pallas_call(
        _ret_prefix_kernel,
        grid_spec=pltpu.PrefetchScalarGridSpec(
            num_scalar_prefetch=1, grid=(2 * N_LAT_SEQ,),
            in_specs=[kv, kv, st, st], out_specs=[pf_blk, qb_blk]),
        out_shape=[shape, shape],
        compiler_params=_cparams(("arbitrary",)),
        name="ret_prefix",
    )(lg, rk, rv, s0f, s0b)


def _retention_kernel(latent, lg_ref, rq_ref, rk_ref, rv_ref, rg_ref, *refs):
    seq_len = rq_ref.shape[0]
    if latent:
        s0f_ref, s0b_ref, o_ref = refs
    else:
        unperm_ref, o_ref, sf_ref, sb_ref = refs
    tq, tk = rq_ref.shape[0], rk_ref.shape[0]
    nt = (((1,), (1,)), ((), ()))
    n_idx = lax.broadcasted_iota(jnp.int32, (tq, tk), 0).astype(F32)
    m_idx = lax.broadcasted_iota(jnp.int32, (tq, tk), 1).astype(F32)
    dist = n_idx - m_idx
    adist = jnp.abs(dist)
    fwd = dist > 0.0
    diag = jnp.where(dist == 0.0, 1.0, 0.0)
    lane = lax.broadcasted_iota(jnp.int32, (1, LANES), 1)
    lo = lane < RDK
    n_col = lax.broadcasted_iota(jnp.int32, (tq, 1), 0).astype(F32)
    m_col = lax.broadcasted_iota(jnp.int32, (tk, 1), 0).astype(F32)
    for pair in range(HEADS // 2):
        sl = slice(pair * LANES, (pair + 1) * LANES)
        qb, kb, vb = rq_ref[:, sl], rk_ref[:, sl], rv_ref[:, sl]
        acc = jnp.zeros((tq, LANES), F32)
        for sub in range(2):
            h = 2 * pair + sub
            lgf, lgb = lg_ref[h], lg_ref[HEADS + h]
            half = lo if sub == 0 else jnp.logical_not(lo)
            qm = jnp.where(half, qb, jnp.zeros_like(qb))
            vm = jnp.where(half, vb, jnp.zeros_like(vb))
            s = lax.dot_general(qm, kb, nt, preferred_element_type=F32)
            w = jnp.exp2(adist * jnp.where(fwd, lgf, lgb)) + diag
            acc = acc + _bdot((s * w).astype(BF16), vm)
        lgf_l = jnp.where(lo, lg_ref[2 * pair], lg_ref[2 * pair + 1])
        lgb_l = jnp.where(lo, lg_ref[HEADS + 2 * pair], lg_ref[HEADS + 2 * pair + 1])
        if latent:
            acc = acc + _bdot(qb, s0f_ref[pair]) * jnp.exp2((n_col + 1.0) * lgf_l)
            acc = acc + _bdot(qb, s0b_ref[pair]) * jnp.exp2((seq_len - n_col) * lgb_l)
        else:
            v_swapped = pltpu.roll(vb.astype(F32), RDK, axis=1).astype(BF16)
            for st_ref, dec in ((sf_ref, jnp.exp2((seq_len - 1.0 - m_col) * lgf_l)),
                                (sb_ref, jnp.exp2(m_col * lgb_l))):
                kt = (kb.astype(F32) * dec).T.astype(BF16)
                kt = _bdot(unperm_ref[...], kt).astype(BF16)
                st_ref[2 * pair] = _bdot(kt, vb)[0:RDK, 0:RDK]
                st_ref[2 * pair + 1] = _bdot(kt, v_swapped)[RDK:, 0:RDK]
        inv = 1.0 / RDK
        mu = jnp.where(lo, jnp.sum(jnp.where(lo, acc, 0.0), axis=-1, keepdims=True),
                       jnp.sum(jnp.where(lo, 0.0, acc), axis=-1, keepdims=True)) * inv
        dlt = acc - mu
        d2 = dlt * dlt
        var = jnp.where(lo, jnp.sum(jnp.where(lo, d2, 0.0), axis=-1, keepdims=True),
                        jnp.sum(jnp.where(lo, 0.0, d2), axis=-1, keepdims=True)) * inv
        g = rg_ref[:, sl]
        o_ref[:, sl] = (dlt * lax.rsqrt(var + 1e-5) * (g * _sigmoid(g))).astype(BF16)


def _retention_tiles_kernel(n_prev, lg_ref, rq_ref, rk_ref, rv_ref, rg_ref, pf_ref, qb_ref, unperm_ref, *refs):
    prev_refs, (o_ref, sf_ref, sb_ref) = refs[:2 * n_prev], refs[2 * n_prev:]
    is_latent = pl.program_id(0) >= TP // TQR
    views = lambda s: [r.at[pl.ds(s * TQ, TQ)] for r in (rq_ref, rk_ref, rv_ref, rg_ref, o_ref)]

    @pl.when(jnp.logical_not(is_latent))
    def _():
        for k in range(n_prev):
            sf_ref[:, k] = prev_refs[2 * k][...]
            sb_ref[:, k] = prev_refs[2 * k + 1][...]
        for s in range(TQR // TQ):
            rq, rk, rv, rg, o = views(s)
            _retention_kernel(False, lg_ref, rq, rk, rv, rg, unperm_ref, o, sf_ref.at[s, n_prev], sb_ref.at[s, n_prev])

    @pl.when(is_latent)
    def _():
        for s in range(TQR // TQ):
            rq, rk, rv, rg, o = views(s)
            _retention_kernel(True, lg_ref, rq, rk, rv, rg, pf_ref.at[s], qb_ref.at[s], o)


def _retention(lg, rq, rk, rv, rg, s0f, s0b, unperm, prev_states):
    w = HEADS * RDK
    n_prev = 0 if prev_states is None else prev_states[0].shape[1]
    pf, qb = _ret_prefix(lg, rk, rv, s0f, s0b)
    assert TQ == PROMPT_LEN
    per_tile = TQR // TQ
    tile = lambda i, lg: (i, 0)
    s0_blk = pl.BlockSpec((per_tile, HEADS // 2, LANES, LANES),
                          lambda i, lg: (jnp.maximum(i - TP // TQR, 0), 0, 0, 0))
    seq = lambda i, lg: (jnp.minimum(i, TP // TQR - 1), 0, 0, 0, 0)
    st_blk = pl.BlockSpec((per_tile, n_prev + 1, HEADS, RDK, RDK), seq)
    st_shape = jax.ShapeDtypeStruct((N_PROMPT_SEQ, n_prev + 1, HEADS, RDK, RDK), F32)
    prev = () if prev_states is None else tuple(prev_states)
    prev_specs = [pl.BlockSpec((per_tile, None, HEADS, RDK, RDK),
                               functools.partial(lambda k, i, lg: (jnp.minimum(i, TP // TQR - 1), k, 0, 0, 0), k))
                  for k in range(n_prev) for _ in range(2)]
    prev_args = [p for k in range(n_prev) for p in prev]
    return pl.pallas_call(
        functools.partial(_retention_tiles_kernel, n_prev),
        grid_spec=pltpu.PrefetchScalarGridSpec(
            num_scalar_prefetch=1, grid=(T // TQR,),
            in_specs=[pl.BlockSpec((TQR, w), tile)] * 4 + [s0_blk, s0_blk,
                      pl.BlockSpec((LANES, LANES), lambda i, lg: (0, 0))] + prev_specs,
            out_specs=[pl.BlockSpec((TQR, w), tile), st_blk, st_blk]),
        out_shape=[jax.ShapeDtypeStruct((T, w), BF16), st_shape, st_shape],
        compiler_params=_cparams(("arbitrary",)),
        name="retention",
    )(lg, rq, rk, rv, rg, pf, qb, unperm, *prev_args)


def _mix_ffn_kernel(n_x, *refs):
    x_refs, refs = refs[:n_x], refs[n_x:]
    a_ref, r_ref, mod_ref, wo_ref, wg_ref, wu_ref, wd_ref, ln_ref, o_ref = refs
    half = HEADS * VDIM
    y = _bdot(a_ref[...], wo_ref[0:half, :]) + _bdot(r_ref[...], wo_ref[half:, :])
    x1 = _layer_norm(ALPHA * _tile_rows(x_refs, TM_FF) + mod_ref[2:3, :] * y, ln_ref[0:1, :], ln_ref[1:2, :])
    h = (x1 * (1.0 + mod_ref[4:5, :]) + mod_ref[3:4, :]).astype(BF16)
    acc = None
    for f in range(D_FF // TF):
        cols = slice(f * TF, (f + 1) * TF)
        g = _bdot(h, wg_ref[:, cols])
        u = _bdot(h, wu_ref[:, cols])
        part = _bdot((g * _sigmoid(g) * u).astype(BF16), wd_ref[cols, :])
        acc = part if acc is None else acc + part
    o_ref[...] = _layer_norm(ALPHA * x1 + mod_ref[5:6, :] * acc, ln_ref[2:3, :], ln_ref[3:4, :])


def _mix_ffn(x, attn, ret, mod_l, w_out, wg, wu, wd, ln, j):
    x_args, x_specs = _tile_specs(x, TM_FF)
    tok = lambda w: pl.BlockSpec((TM_FF, w), lambda i: (i, 0))
    resident = lambda a: pl.BlockSpec((None,) + a.shape[1:], lambda i: (j, 0, 0), pipeline_mode=pl.Buffered(1))
    return pl.pallas_call(
        functools.partial(_mix_ffn_kernel, len(x_args)),
        grid=(T // TM_FF,),
        in_specs=x_specs + [tok(HEADS * VDIM), tok(HEADS * RDK),
                  pl.BlockSpec((None, 8, D), lambda i: (_group_of_tile(i, TM_FF), 0, 0)),
                  resident(w_out), resident(wg), resident(wu), resident(wd), pl.BlockSpec((8, D), lambda i: (0, 0))],
        out_specs=tok(D),
        out_shape=jax.ShapeDtypeStruct((T, D), F32),
        compiler_params=_cparams(("arbitrary",)),
        name="mix_ffn",
    )(*x_args, attn, ret, mod_l, w_out, wg, wu, wd, ln)


def _conv_in_kernel(x_ref, mod_ref, w_ref, b_ref, z_ref):
    h = (x_ref[...] * (1.0 + mod_ref[1:2, :]) + mod_ref[0:1, :]).astype(BF16)
    p = _bdot(h, w_ref[...])
    b_ref[...] = p[:, 0:D]
    z_ref[...] = p[:, D:2 * D] * p[:, 2 * D:3 * D]


def _conv_in(x, mod_l, w_in, j):
    tm = 2 * TM_FF
    tok = pl.BlockSpec((tm, D), lambda i: (i, 0))
    return pl.pallas_call(
        _conv_in_kernel,
        grid=(T // tm,),
        in_specs=[tok, pl.BlockSpec((None, 8, D), lambda i: (_group_of_tile(i, tm), 0, 0)),
                  pl.BlockSpec((None,) + w_in.shape[1:], lambda i: (j, 0, 0), pipeline_mode=pl.Buffered(1))],
        out_specs=[tok, tok],
        out_shape=[jax.ShapeDtypeStruct((T, D), F32)] * 2,
        compiler_params=_cparams(("parallel",)),
        name="conv_in",
    )(x, mod_l, w_in)


def _conv_out_kernel(x_ref, b_ref, z_ref, zp_ref, zn_ref, mod_ref, cw_ref, w_ref, ln_ref, rw_ref, rb_ref,
                     o_ref, h_ref, route_ref):
    i = pl.program_id(0)
    z = z_ref[...]
    tm = z.shape[0]
    row = lax.broadcasted_iota(jnp.int32, (tm, 1), 0)
    seq_len = jnp.where(i < TP // tm, PROMPT_LEN, LAT_LEN)
    pos = (i * tm + row) & (seq_len - 1)
    prev = jnp.where(row == 0, zp_ref[7:8, :], pltpu.roll(z, 1, axis=0))
    prev = jnp.where(pos == 0, 0.0, prev)
    nxt = jnp.where(row == tm - 1, zn_ref[0:1, :], pltpu.roll(z, tm - 1, axis=0))
    nxt = jnp.where(pos == seq_len - 1, 0.0, nxt)
    y = prev * cw_ref[0:1, :] + z * cw_ref[1:2, :] + nxt * cw_ref[2:3, :]
    t = _bdot((b_ref[...] * y).astype(BF16), w_ref[...])
    x1 = _layer_norm(ALPHA * x_ref[...] + mod_ref[2:3, :] * t, ln_ref[0:1, :], ln_ref[1:2, :])
    o_ref[...] = x1
    h = x1 * (1.0 + mod_ref[4:5, :]) + mod_ref[3:4, :]
    h_hi = h.astype(BF16)
    h_ref[...] = h_hi
    h_lo = (h - h_hi.astype(F32)).astype(BF16)
    both = _bdot(h_hi, rw_ref[...])
    logits = both[:, :LANES] + both[:, LANES:] + _bdot(h_lo, rw_ref[:, :LANES]) + rb_ref[...]
    lane = lax.broadcasted_iota(jnp.int32, logits.shape, 1).astype(F32)
    t1 = jnp.max(logits, axis=-1, keepdims=True)
    i1 = jnp.min(jnp.where(logits == t1, lane, float(LANES)), axis=-1, keepdims=True)
    rest = jnp.where(lane == i1, -jnp.inf, logits)
    t2 = jnp.max(rest, axis=-1, keepdims=True)
    i2 = jnp.min(jnp.where(rest == t2, lane, float(LANES)), axis=-1, keepdims=True)
    e = jnp.exp(t2 - t1)
    den = 1.0 + e
    route_ref[...] = jnp.where(lane == 0.0, i1, jnp.where(lane == 1.0, i2,
                               jnp.where(lane == 2.0, 1.0 / den, jnp.where(lane == 3.0, e / den, 0.0))))


def _conv_out(x, b, z, mod_l, cw, w_out, ln, rw, rb, j):
    tok = pl.BlockSpec((TM_FF, D), lambda i: (i, 0))
    sub = TM_FF // 8
    return pl.pallas_call(
        _conv_out_kernel,
        grid=(T // TM_FF,),
        in_specs=[tok, tok, tok,
                  pl.BlockSpec((8, D), lambda i: (jnp.maximum(i * sub - 1, 0), 0)),
                  pl.BlockSpec((8, D), lambda i: (jnp.minimum((i + 1) * sub, T // 8 - 1), 0)),
                  pl.BlockSpec((None, 8, D), lambda i: (_group_of_tile(i, TM_FF), 0, 0)),
                  pl.BlockSpec((8, D), lambda i: (0, 0)), pl.BlockSpec((None, D, D), lambda i: (j, 0, 0)),
                  pl.BlockSpec((8, D), lambda i: (0, 0)),
                  pl.BlockSpec((D, 2 * LANES), lambda i: (0, 0)), pl.BlockSpec((1, LANES), lambda i: (0, 0))],
        out_specs=[tok, tok, pl.BlockSpec((TM_FF, LANES), lambda i: (i, 0))],
        out_shape=[jax.ShapeDtypeStruct((T, D), F32), jax.ShapeDtypeStruct((T, D), BF16),
                   jax.ShapeDtypeStruct((T, LANES), F32)],
        compiler_params=_cparams(("parallel",)),
        name="conv_out",
    )(x, b, z, z, z, mod_l, cw, w_out, ln, rw, rb)


def _moe_up_kernel(te_ref, nv_ref, x_ref, wg_ref, wu_ref, a_ref):
    @pl.when(pl.program_id(1) < nv_ref[0])
    def _():
        h = x_ref[...]
        g = _wdot(h, wg_ref[...])
        u = _wdot(h, wu_ref[...])
        a_ref[...] = (g * _sigmoid(g) * u).astype(BF16)

    @pl.when(pl.program_id(1) >= nv_ref[0])
    def _():
        a_ref[...] = jnp.zeros_like(a_ref)


def _moe_down_kernel(te_ref, nv_ref, a_ref, wd_ref, o_ref):
    @pl.when(pl.program_id(0) < nv_ref[0])
    def _():
        o_ref[...] = _wdot(a_ref[...], wd_ref[...])

    @pl.when(pl.program_id(0) >= nv_ref[0])
    def _():
        o_ref[...] = jnp.zeros_like(o_ref)


def _moe(tile_expert, n_valid, xs, wg, wu, wd, j):
    n_tiles = NP_ROWS // TM_FF
    act = pl.pallas_call(
        _moe_up_kernel,
        grid_spec=pltpu.PrefetchScalarGridSpec(
            num_scalar_prefetch=2, grid=(D_FF // TF, n_tiles),
            in_specs=[pl.BlockSpec((TM_FF, D), lambda f, i, te, nv: (i, 0)),
                      pl.BlockSpec((None, None, D, TF), lambda f, i, te, nv: (j, te[i], 0, f)),
                      pl.BlockSpec((None, None, D, TF), lambda f, i, te, nv: (j, te[i], 0, f))],
            out_specs=pl.BlockSpec((TM_FF, TF), lambda f, i, te, nv: (i, f))),
        out_shape=jax.ShapeDtypeStruct((NP_ROWS, D_FF), BF16),
        compiler_params=_cparams(("arbitrary", "arbitrary")),
        name="moe_up",
    )(tile_expert, n_valid, xs, wg, wu)
    return pl.pallas_call(
        _moe_down_kernel,
        grid_spec=pltpu.PrefetchScalarGridSpec(
            num_scalar_prefetch=2, grid=(n_tiles,),
            in_specs=[pl.BlockSpec((TM_FF, D_FF), lambda i, te, nv: (i, 0)),
                      pl.BlockSpec((None, None, D_FF, D), lambda i, te, nv: (j, te[i], 0, 0))],
            out_specs=pl.BlockSpec((TM_FF, D), lambda i, te, nv: (i, 0))),
        out_shape=jax.ShapeDtypeStruct((NP_ROWS, D), F32),
        compiler_params=_cparams(("arbitrary",)),
        name="moe_down",
    )(tile_expert, n_valid, act, wd)


def _combine_kernel(split, x_ref, o0_ref, o1_ref, route_ref, mod_ref, ln_ref, *o_refs):
    y = route_ref[:, 2:3] * o0_ref[...] + route_ref[:, 3:4] * o1_ref[...]
    out = _layer_norm(ALPHA * x_ref[...] + mod_ref[5:6, :] * y, ln_ref[2:3, :], ln_ref[3:4, :])
    if not split:
        o_refs[0][...] = out
        return
    is_latent = pl.program_id(0) >= TP // TM_FF

    @pl.when(jnp.logical_not(is_latent))
    def _():
        o_refs[0][...] = out

    @pl.when(is_latent)
    def _():
        o_refs[1][...] = out


def _combine(x, o0, o1, route, mod_l, ln, split):
    tok = pl.BlockSpec((TM_FF, D), lambda i: (i, 0))
    if split:
        out_specs = [pl.BlockSpec((TM_FF, D), lambda i: (jnp.minimum(i, TP // TM_FF - 1), 0)),
                     pl.BlockSpec((TM_FF, D), lambda i: (jnp.maximum(i - TP // TM_FF, 0), 0))]
        out_shape = [jax.ShapeDtypeStruct((TP, D), F32), jax.ShapeDtypeStruct((TS, D), F32)]
    else:
        out_specs, out_shape = tok, jax.ShapeDtypeStruct((T, D), F32)
    return pl.pallas_call(
        functools.partial(_combine_kernel, split),
        grid=(T // TM_FF,),
        in_specs=[tok, tok, tok, pl.BlockSpec((TM_FF, LANES), lambda i: (i, 0)),
                  pl.BlockSpec((None, 8, D), lambda i: (_group_of_tile(i, TM_FF), 0, 0)),
                  pl.BlockSpec((8, D), lambda i: (0, 0))],
        out_specs=out_specs,
        out_shape=out_shape,
        compiler_params=_cparams(("arbitrary",)),
        name="moe_combine",
    )(x, o0, o1, route, mod_l, ln)


def _routing_plan(route):
    e = jnp.concatenate([route[:, 0], route[:, 1]]).astype(jnp.int32)
    onehot = (e[:, None] == jnp.arange(N_EXP, dtype=jnp.int32)[None, :]).astype(jnp.int32)
    csum = jnp.cumsum(onehot, axis=0)
    counts = csum[-1]
    rank = jnp.sum((csum - onehot) * onehot, axis=1)
    padded = (counts + TM_FF - 1) // TM_FF * TM_FF
    pend = jnp.cumsum(padded)
    dest = jnp.sum(onehot * (pend - padded)[None, :], axis=1) + rank
    order = jnp.argsort(e, stable=True).astype(jnp.int32)
    rows = jnp.arange(NP_ROWS, dtype=jnp.int32)
    before = (rows[:, None] >= pend[None, :]).astype(jnp.int32)
    row_e = jnp.minimum(jnp.sum(before, axis=1), N_EXP - 1)
    row_cnt = jnp.sum((row_e[:, None] == jnp.arange(N_EXP, dtype=jnp.int32)[None, :]) * counts[None, :], axis=1)
    q = jnp.clip(rows - jnp.sum(before * padded[None, :], axis=1), 0, jnp.maximum(row_cnt - 1, 0))
    src = jnp.minimum(jnp.sum(before * counts[None, :], axis=1) + q, 2 * T - 1)
    row_token = order[src] % T
    n_valid = (pend[-1] // TM_FF).astype(jnp.int32)
    tile_start = jnp.minimum(jnp.arange(NP_ROWS // TM_FF, dtype=jnp.int32), n_valid - 1) * TM_FF
    tile_expert = jnp.minimum(jnp.sum((tile_start[:, None] >= pend[None, :]).astype(jnp.int32), axis=1), N_EXP - 1)
    return dest[:T], dest[T:], row_token, tile_expert.astype(jnp.int32), n_valid.reshape(1)


_INV_PERM_RDK = np.argsort(np.concatenate([np.arange(0, RDK, 2), np.arange(1, RDK, 2)]))


def _pad_cols(a, width):
    return jnp.pad(a, ((0, 0), (0, width - a.shape[1])))


def _deinterleave(a):
    n = a.shape[-1]
    return jnp.swapaxes(a.reshape(a.shape[:-1] + (n // 2, 2)), -1, -2).reshape(a.shape)


def _prep_even_weights(w_in, w_q_b, w_kv_b):
    o_kpe = Q_LORA + KV_LORA
    o_rq, o_rk, o_rv = o_kpe + ROPE, o_kpe + ROPE + RET_W, o_kpe + ROPE + 2 * RET_W
    heads = lambda a: _deinterleave(a.reshape(D, HEADS, RDK)).reshape(D, HEADS * RDK)
    kpe = w_in[:, o_kpe:o_kpe + ROPE]
    w_in_p = jnp.concatenate([
        w_in[:, :o_kpe], heads(w_in[:, o_rq:o_rk]), heads(w_in[:, o_rk:o_rv]) * (RDK ** -0.5), w_in[:, o_rv:],
        _pad_cols(kpe, LANES), _pad_cols(_deinterleave(kpe), LANES),
        _pad_cols(jnp.concatenate([-kpe[:, 1::2], kpe[:, 0::2]], axis=1), LANES)], axis=1).astype(BF16)
    assert w_in_p.shape == (D, IN_COLS)
    wq = w_q_b.reshape(Q_LORA, HEADS, NOPE + ROPE)
    wq = jnp.concatenate([wq[:, :, :NOPE], _deinterleave(wq[:, :, NOPE:]),
                          jnp.zeros((Q_LORA, HEADS, LANES - NOPE - ROPE), F32)], axis=2)
    wq = wq.reshape(Q_LORA, HEADS * LANES).astype(BF16)
    wkv = w_kv_b.reshape(KV_LORA, HEADS, NOPE + VDIM)
    zero = jnp.zeros((KV_LORA, HEADS, LANES - NOPE), F32)
    wk = jnp.concatenate([wkv[:, :, :NOPE], zero], axis=2).reshape(KV_LORA, HEADS * LANES)
    wv = wkv[:, :, NOPE:].reshape(KV_LORA, HEADS // 2, 2, VDIM)
    zv = jnp.zeros((KV_LORA, HEADS // 2, VDIM), F32)
    wv = jnp.stack([jnp.concatenate([wv[:, :, 0], zv], axis=2), jnp.concatenate([zv, wv[:, :, 1]], axis=2)], axis=2)
    wkv_p = jnp.concatenate([wk, wv.reshape(KV_LORA, HEADS * LANES)], axis=1).astype(BF16)
    return w_in_p, wq, wkv_p


def _placement():
    ek = np.zeros((LANES, HEADS * LANES), np.float32)
    for h in range(HEADS):
        ek[np.arange(ROPE), h * LANES + NOPE + np.arange(ROPE)] = 1.0
    return jnp.asarray(ek, BF16)


def _rotary_tables():
    rows = LAT_LEN // GRID_W
    r, col = jnp.meshgrid(jnp.arange(rows, dtype=F32), jnp.arange(GRID_W, dtype=F32), indexing='ij')
    n_freq = ROPE // 4
    freqs = 1.0 / (10000.0 ** (jnp.arange(n_freq, dtype=F32) / n_freq))
    ang = jnp.concatenate([r.reshape(-1)[:, None] * freqs, col.reshape(-1)[:, None] * freqs], axis=-1)
    cos, sin = jnp.cos(ang), jnp.sin(ang)
    theta = 1.0 / (10000.0 ** jnp.linspace(0.0, 1.0, RDK // 2, dtype=F32))
    rang = jnp.arange(LAT_LEN, dtype=F32)[:, None] * theta
    rcos, rsin = jnp.cos(rang), jnp.sin(rang)
    one = lambda w: jnp.ones((LAT_LEN, w), F32)
    zero = lambda w: jnp.zeros((LAT_LEN, w), F32)
    lat = [jnp.concatenate([one(NOPE), cos, cos, one(LANES - NOPE - ROPE)], axis=1),
           jnp.concatenate([zero(NOPE), sin, sin, zero(LANES - NOPE - ROPE)], axis=1),
           jnp.concatenate([cos, cos, zero(LANES - ROPE)], axis=1),
           jnp.concatenate([sin, sin, zero(LANES - ROPE)], axis=1),
           jnp.concatenate([rcos] * 4, axis=1), jnp.concatenate([rsin] * 4, axis=1)]
    ident = [np.ones((TM, LANES), np.float32), np.zeros((TM, LANES), np.float32)]
    ident_k = np.concatenate([np.ones((TM, ROPE), np.float32), np.zeros((TM, LANES - ROPE), np.float32)], axis=1)
    ident = [ident[0], ident[1], ident_k, ident[1], ident[0], ident[1]]
    return [jnp.concatenate([l, jnp.asarray(c)], axis=0) for l, c in zip(lat, ident)]


def _block_diag_states(s0):
    s = jnp.swapaxes(_deinterleave(jnp.swapaxes(s0, -1, -2)), -1, -2)
    s = s.reshape(s0.shape[0], HEADS // 2, 2, RDK, RDK)
    z = jnp.zeros_like(s[:, :, 0])
    top = jnp.concatenate([s[:, :, 0], z], axis=-1)
    bot = jnp.concatenate([z, s[:, :, 1]], axis=-1)
    return jnp.concatenate([top, bot], axis=-2)


def _unpermute_matrix():
    m = np.zeros((LANES, LANES), np.float32)
    for blk in range(LANES // RDK):
        m[blk * RDK + np.arange(RDK), blk * RDK + _INV_PERM_RDK] = 1.0
    return jnp.asarray(m, BF16)


def kernel(x_prompt, x_sample, c, cache_ckv, cache_kpe, state_ret_fwd, state_ret_bwd, c_ctx, w_mod, b_mod, ln_g, ln_b, w_in_mix, q_a_gain, kv_a_gain, w_q_b, w_kv_b, ret_decay_fwd, ret_decay_bwd, w_out_mix, w_in_conv, conv_w, w_out_conv, ffn_gate, ffn_up, ffn_down, router_w, router_b, exp_gate, exp_up, exp_down):
    x = (x_prompt.reshape(TP, D), x_sample.reshape(TS, D))
    cond8 = jnp.concatenate([c_ctx[None], c, jnp.zeros((8 - 1 - N_LAT_SEQ, D), F32)], axis=0)
    mods = _modulation(cond8, w_mod, b_mod)
    mods = jnp.pad(mods.reshape(DEPTH, N_GROUPS, 6, D), ((0, 0), (0, 0), (0, 2), (0, 0)))
    ln = jnp.pad(jnp.concatenate([ln_g, ln_b], axis=1)[:, jnp.array([0, 2, 1, 3])], ((0, 0), (0, 4), (0, 0)))
    tabs = _rotary_tables()
    ek = _placement()
    unperm = _unpermute_matrix()
    bf = lambda a: a.astype(BF16)
    w_out_mix_b, w_in_conv_b, w_out_conv_b = bf(w_out_mix), bf(w_in_conv), bf(w_out_conv)
    ffn_b = (bf(ffn_gate), bf(ffn_up), bf(ffn_down))
    exp_b = (exp_gate, exp_up, exp_down)
    cache, states = None, None
    for layer in range(DEPTH):
        j = layer // 2
        mod_l, ln_l = mods[layer], ln[layer]
        if layer % 2 == 0:
            w_in_p, wq, wkv = _prep_even_weights(w_in_mix[j], w_q_b[j], w_kv_b[j])
            q, k, v, cache, rq, rk, rv, rg = _even_in(
                x, mod_l, w_in_p, q_a_gain[j][None], kv_a_gain[j][None], wq, wkv, ek, tabs, cache)
            kpe_c = _pad_cols(_deinterleave(cache_kpe[:, j]).reshape(N_LAT_SEQ * PAST, ROPE), LANES)
            kc, vc = _ctx_kv(cache_ckv[:, j].reshape(N_LAT_SEQ * PAST, KV_LORA), kpe_c, wkv, ek)
            attn = _attention(q, k, v, kc, vc)
            lg = LOG2E * jnp.concatenate([jax.nn.log_sigmoid(ret_decay_fwd[j].astype(F32)),
                                          jax.nn.log_sigmoid(ret_decay_bwd[j].astype(F32))])
            ret, sf, sb = _retention(lg, rq, rk, rv, rg, _block_diag_states(state_ret_fwd[:, j]),
                                     _block_diag_states(state_ret_bwd[:, j]), unperm, states)
            states = (sf, sb)
            x = _mix_ffn(x, attn, ret, mod_l, w_out_mix_b, *ffn_b, ln_l, j)
        else:
            b, z = _conv_in(x, mod_l, w_in_conv_b, j)
            cw = jnp.pad(conv_w[j], ((0, 5), (0, 0)))
            rw = _pad_cols(router_w[j], LANES)
            rw_hi = rw.astype(BF16)
            rw = jnp.concatenate([rw_hi, (rw - rw_hi.astype(F32)).astype(BF16)], axis=1)
            rb = jnp.concatenate([router_b[j].astype(F32), jnp.full((LANES - N_EXP,), -1e30, F32)])[None]
            x, h, route = _conv_out(x, b, z, mod_l, cw, w_out_conv_b, ln_l, rw, rb, j)
            dest0, dest1, row_token, tile_expert, n_valid = _routing_plan(route)
            out_sorted = _moe(tile_expert, n_valid, h[row_token], *exp_b, j)
            x = _combine(x, out_sorted[dest0], out_sorted[dest1], route, mod_l, ln_l, split=layer == DEPTH - 1)
    y_prompt = x[0].reshape(N_PROMPT_SEQ, PROMPT_LEN, D)
    y_sample = x[1].reshape(N_LAT_SEQ, LAT_LEN, D)
    return (y_prompt, y_sample, cache[0], cache[1], states[0], states[1])
```

```python
import functools

import numpy as np
import jax
import jax.numpy as jnp
from jax import lax
from jax.experimental import pallas as pl
from jax.experimental.pallas import tpu as pltpu

F32 = jnp.float32
BF16 = jnp.bfloat16

D = 1024
DEPTH = 4
N_PROMPT_SEQ, PROMPT_LEN = 32, 256
N_LAT_SEQ, LAT_LEN = 2, 2048
PAST = 512
GRID_W = 64
TP = N_PROMPT_SEQ * PROMPT_LEN
TS = N_LAT_SEQ * LAT_LEN
T = TP + TS
HEADS = 8
NOPE, ROPE, VDIM = 64, 32, 64
Q_LORA, KV_LORA = 384, 256
RDK = 64
D_FF = 2816
N_EXP = 8
ALPHA = (2.0 * DEPTH) ** 0.25
LOG2E = float(np.log2(np.e))
Q_SCALE = float((NOPE + ROPE) ** -0.5) * LOG2E
LANES = 128
N_GROUPS = 8

TM = 512
TM_FF = 512
TF = D_FF // 2
TQ = 256
TQA = 512
KEY_BLOCK = 512
TQR = 1024
NP_ROWS = 2 * T + N_EXP * TM_FF
VMEM_LIMIT = 56 * 1024 * 1024

RET_W = HEADS * RDK
COL_KV = Q_LORA
COL_RQ = COL_KV + KV_LORA
COL_RK, COL_RV, COL_RG = COL_RQ + RET_W, COL_RQ + 2 * RET_W, COL_RQ + 3 * RET_W
COL_KPE = COL_RQ + 4 * RET_W
IN_COLS = COL_KPE + 3 * LANES


def _cparams(sem):
    return pltpu.CompilerParams(dimension_semantics=sem, vmem_limit_bytes=VMEM_LIMIT)


def _group_of_tile(i, tm):
    per_seq = LAT_LEN // tm
    return jnp.maximum(i - TP // tm + per_seq, 0) // per_seq


def _bdot(a, b):
    return jnp.dot(a, b, preferred_element_type=F32)


def _wdot(a, w):
    return lax.dot_general(a, w, (((1,), (0,)), ((), ())), preferred_element_type=F32)


def _sigmoid(v):
    return 1.0 / (1.0 + jnp.exp(-v))


def _layer_norm(v, g, b):
    mu = jnp.mean(v, axis=-1, keepdims=True)
    d = v - mu
    var = jnp.mean(d * d, axis=-1, keepdims=True)
    return d * lax.rsqrt(var + 1e-5) * g + b


def _rms(v, g):
    return v * lax.rsqrt(jnp.mean(v * v, axis=-1, keepdims=True) + 1e-6) * g


def _mod_kernel(c_ref, w_ref, b_ref, o_ref):
    c = c_ref[...]
    s = (c * _sigmoid(c)).astype(BF16)
    o_ref[...] = _bdot(s, w_ref[...].astype(BF16)) + b_ref[...]


def _modulation(cond8, w_mod, b_mod):
    tn = 1536
    return pl.pallas_call(
        _mod_kernel,
        grid=(DEPTH, 6 * D // tn),
        in_specs=[pl.BlockSpec((8, D), lambda l, n: (0, 0)),
                  pl.BlockSpec((None, D, tn), lambda l, n: (l, 0, n)),
                  pl.BlockSpec((None, 1, tn), lambda l, n: (l, 0, n))],
        out_specs=pl.BlockSpec((None, 8, tn), lambda l, n: (l, 0, n)),
        out_shape=jax.ShapeDtypeStruct((DEPTH, 8, 6 * D), F32),
        compiler_params=_cparams(("arbitrary", "arbitrary")),
        name="modulation",
    )(cond8, w_mod, b_mod.reshape(DEPTH, 1, 6 * D))


def _swap_halves(a, half):
    n = a.shape[-1]
    lane = lax.broadcasted_iota(jnp.int32, a.shape, 1)
    first = (lane & (2 * half - 1)) < half
    return jnp.where(first, -pltpu.roll(a, n - half, axis=1), pltpu.roll(a, half, axis=1))


def _tile_rows(x_refs, tm):
    if len(x_refs) == 1:
        return x_refs[0][...]
    return jnp.where(pl.program_id(0) >= TP // tm, x_refs[1][...], x_refs[0][...])


def _tile_specs(x, tm):
    if not isinstance(x, tuple):
        return (x,), [pl.BlockSpec((tm, D), lambda i, *_: (i, 0))]
    return x, [pl.BlockSpec((tm, D), lambda i, *_: (jnp.minimum(i, TP // tm - 1), 0)),
               pl.BlockSpec((tm, D), lambda i, *_: (jnp.maximum(i - TP // tm, 0), 0))]


def _even_in_kernel(n_x, n_prev, *refs):
    x_refs, refs = refs[:n_x], refs[n_x:]
    (mod_ref, w_in_ref, qg_ref, kvg_ref, wq_ref, wkv_ref, ek_ref,
     cq_ref, sq_ref, ck_ref, sk_ref, cr_ref, sr_ref), refs = refs[:13], refs[13:]
    prev_refs, refs = refs[:2 * n_prev], refs[2 * n_prev:]
    q_ref, k_ref, v_ref, ckv_ref, kpe_ref, rq_ref, rk_ref, rv_ref, rg_ref = refs
    x = _tile_rows(x_refs, TM)
    h = (x * (1.0 + mod_ref[1:2, :]) + mod_ref[0:1, :]).astype(BF16)
    p = _bdot(h, w_in_ref[...])
    qn = _rms(p[:, 0:Q_LORA], qg_ref[...]).astype(BF16)
    qa = _bdot(qn, wq_ref[...])
    ckv = _rms(p[:, COL_KV:COL_RQ], kvg_ref[...])
    kv = _bdot(ckv.astype(BF16), wkv_ref[...])
    v_ref[...] = kv[:, HEADS * LANES:].astype(BF16)
    base = COL_KPE
    ka = p[:, base + LANES:base + 2 * LANES]
    kb = p[:, base + 2 * LANES:base + 3 * LANES]
    rq = p[:, COL_RQ:COL_RK]
    rk = p[:, COL_RK:COL_RV]
    rv_ref[...] = p[:, COL_RV:COL_RG].astype(BF16)
    rg_ref[...] = p[:, COL_RG:COL_KPE]
    lane = lax.broadcasted_iota(jnp.int32, qa.shape, 1) & (LANES - 1)
    qb = jnp.where(lane < NOPE + ROPE // 2,
                   -pltpu.roll(qa, qa.shape[1] - ROPE // 2, axis=1),
                   pltpu.roll(qa, ROPE // 2, axis=1))
    cq = jnp.concatenate([cq_ref[...]] * HEADS, axis=1)
    sq = jnp.concatenate([sq_ref[...]] * HEADS, axis=1)
    q_ref[...] = ((qa * cq + qb * sq) * Q_SCALE).astype(BF16)
    kpe_rot = ka * ck_ref[...] + kb * sk_ref[...]
    k_ref[...] = (kv[:, :HEADS * LANES] + _bdot(kpe_rot.astype(BF16), ek_ref[...])).astype(BF16)
    cr = jnp.concatenate([cr_ref[...]] * (RET_W // LANES), axis=1)
    sr = jnp.concatenate([sr_ref[...]] * (RET_W // LANES), axis=1)
    rq_ref[...] = (rq * cr + _swap_halves(rq, RDK // 2) * sr).astype(BF16)
    rk_ref[...] = (rk * cr + _swap_halves(rk, RDK // 2) * sr).astype(BF16)

    @pl.when(pl.program_id(0) < TP // TM)
    def _():
        for k in range(n_prev):
            ckv_ref[:, k] = prev_refs[2 * k][...]
            kpe_ref[:, k] = prev_refs[2 * k + 1][...]
        for s in range(TM // PROMPT_LEN):
            rows = slice(s * PROMPT_LEN, (s + 1) * PROMPT_LEN)
            ckv_ref[s, n_prev] = ckv[rows]
            kpe_ref[s, n_prev] = p[rows, base:base + ROPE]


def _even_in(x, mod_l, w_in, qg, kvg, wq, wkv, ek, tabs, prev_cache):
    x_args, x_specs = _tile_specs(x, TM)
    n_prev = 0 if prev_cache is None else prev_cache[0].shape[1]
    spt = TM // PROMPT_LEN
    tok = lambda w: pl.BlockSpec((TM, w), lambda i: (i, 0))
    full = lambda a: pl.BlockSpec(a.shape, lambda i: (0,) * a.ndim, pipeline_mode=pl.Buffered(1))
    lat_tiles = LAT_LEN // TM
    tab = pl.BlockSpec((TM, LANES), lambda i: (
        jnp.where(i < TP // TM, lat_tiles, jnp.maximum(i - TP // TM, 0) % lat_tiles), 0))
    seq = lambda i: (jnp.minimum(i, TP // TM - 1), 0, 0, 0)
    seq_k = lambda k, i: (jnp.minimum(i, TP // TM - 1), k, 0, 0)
    prev_args, prev_specs = [], []
    for k in range(n_prev):
        for a, w in zip(prev_cache, (KV_LORA, ROPE)):
            prev_args.append(a)
            prev_specs.append(pl.BlockSpec((spt, None, PROMPT_LEN, w), functools.partial(seq_k, k)))
    tok_outs = lambda dims: ([tok(w) for w, _ in dims], [jax.ShapeDtypeStruct((T, w), dt) for w, dt in dims])
    qkv_specs, qkv_shapes = tok_outs([(HEADS * LANES, BF16)] * 3)
    ret_specs, ret_shapes = tok_outs([(RET_W, BF16), (RET_W, BF16), (RET_W, BF16), (RET_W, F32)])
    cache_specs = [pl.BlockSpec((spt, n_prev + 1, PROMPT_LEN, w), seq) for w in (KV_LORA, ROPE)]
    cache_shapes = [jax.ShapeDtypeStruct((N_PROMPT_SEQ, n_prev + 1, PROMPT_LEN, w), F32) for w in (KV_LORA, ROPE)]
    q, k, v, ckv, kpe, rq, rk, rv, rg = pl.pallas_call(
        functools.partial(_even_in_kernel, len(x_args), n_prev),
        grid=(T // TM,),
        in_specs=x_specs + [pl.BlockSpec((None, 8, D), lambda i: (_group_of_tile(i, TM), 0, 0)),
                            full(w_in), full(qg), full(kvg), full(wq), full(wkv), full(ek)] + [tab] * 6 + prev_specs,
        out_specs=qkv_specs + cache_specs + ret_specs,
        out_shape=qkv_shapes + cache_shapes + ret_shapes,
        compiler_params=_cparams(("arbitrary",)),
        name="even_in",
    )(*x_args, mod_l, w_in, qg, kvg, wq, wkv, ek, *tabs, *prev_args)
    return q, k, v, (ckv, kpe), rq, rk, rv, rg


def _ctx_kv_kernel(ckv_ref, kpe_ref, wkv_ref, ek_ref, k_ref, v_ref):
    kv = _bdot(ckv_ref[...].astype(BF16), wkv_ref[...])
    k_ref[...] = (kv[:, :HEADS * LANES] + _bdot(kpe_ref[...].astype(BF16), ek_ref[...])).astype(BF16)
    v_ref[...] = kv[:, HEADS * LANES:].astype(BF16)


def _ctx_kv(ckv_c, kpe_c, wkv, ek):
    n = ckv_c.shape[0]
    full = lambda a: pl.BlockSpec(a.shape, lambda i: (0,) * a.ndim)
    return pl.pallas_call(
        _ctx_kv_kernel,
        grid=(n // PAST,),
        in_specs=[pl.BlockSpec((PAST, KV_LORA), lambda i: (i, 0)), pl.BlockSpec((PAST, LANES), lambda i: (i, 0)),
                  full(wkv), full(ek)],
        out_specs=[pl.BlockSpec((PAST, HEADS * LANES), lambda i: (i, 0))] * 2,
        out_shape=[jax.ShapeDtypeStruct((n, HEADS * LANES), BF16)] * 2,
        compiler_params=_cparams(("parallel",)),
        name="ctx_kv",
    )(ckv_c, kpe_c, wkv, ek)


def _attn_kernel(n_kv, q_ref, *refs):
    k_refs = refs[0:2 * n_kv:2]
    v_refs = refs[1:2 * n_kv:2]
    o_ref = refs[2 * n_kv]
    nt = (((1,), (1,)), ((), ()))
    blocks = [(k, v, slice(r0, min(r0 + KEY_BLOCK, k.shape[0])))
              for k, v in zip(k_refs, v_refs) for r0 in range(0, k.shape[0], KEY_BLOCK)]
    for pair in range(HEADS // 2):
        acc = None
        for sub in range(2):
            sl = slice((2 * pair + sub) * LANES, (2 * pair + sub + 1) * LANES)
            qh = q_ref[:, sl]
            m = den = o = None
            for k, v, rows in blocks:
                s = lax.dot_general(qh, k[rows, sl], nt, preferred_element_type=F32)
                m_blk = jnp.max(s, axis=-1, keepdims=True)
                m_new = m_blk if m is None else jnp.maximum(m, m_blk)
                e = jnp.exp2(s - m_new)
                part = _bdot(e.astype(BF16), v[rows, sl])
                if m is None:
                    den, o = jnp.sum(e, axis=-1, keepdims=True), part
                else:
                    rescale = jnp.exp2(m - m_new)
                    den = rescale * den + jnp.sum(e, axis=-1, keepdims=True)
                    o = rescale * o + part
                m = m_new
            o = o / den
            acc = o if acc is None else acc + o
        o_ref[:, pair * LANES:(pair + 1) * LANES] = acc.astype(BF16)


def _latent_seq(i):
    return jnp.maximum(i - TP // TQA, 0) // (LAT_LEN // TQA)


def _attn_tiles_kernel(q_ref, kp_ref, vp_ref, kc_ref, vc_ref, kl_ref, vl_ref, o_ref):
    is_latent = pl.program_id(0) >= TP // TQA

    @pl.when(jnp.logical_not(is_latent))
    def _():
        for s in range(TQA // PROMPT_LEN):
            rows = pl.ds(s * PROMPT_LEN, PROMPT_LEN)
            _attn_kernel(1, q_ref.at[rows], kp_ref.at[rows], vp_ref.at[rows], o_ref.at[rows])

    @pl.when(is_latent)
    def _():
        _attn_kernel(2, q_ref, kc_ref, vc_ref, kl_ref, vl_ref, o_ref)


def _attention(q, k, v, kc, vc):
    w = HEADS * LANES
    tile = lambda i: (i, 0)
    ctx_own = lambda i: (jnp.minimum(i, TP // TQA - 1), 0)
    cache = lambda i: (_latent_seq(i), 0)
    lat_own = lambda i: (TP // LAT_LEN + _latent_seq(i), 0)
    once = lambda rows, index: pl.BlockSpec((rows, w), index, pipeline_mode=pl.Buffered(1))
    return pl.pallas_call(
        _attn_tiles_kernel,
        grid=(T // TQA,),
        in_specs=[pl.BlockSpec((TQA, w), tile),
                  pl.BlockSpec((TQA, w), ctx_own), pl.BlockSpec((TQA, w), ctx_own),
                  once(PAST, cache), once(PAST, cache), once(LAT_LEN, lat_own), once(LAT_LEN, lat_own)],
        out_specs=pl.BlockSpec((TQA, HEADS * VDIM), tile),
        out_shape=jax.ShapeDtypeStruct((T, HEADS * VDIM), BF16),
        compiler_params=_cparams(("arbitrary",)),
        name="attention",
    )(q, k, v, kc, vc, k, v)


def _ret_prefix_kernel(lg_ref, rk_ref, rv_ref, s0f_ref, s0b_ref, pf_ref, qb_ref):
    s = pl.program_id(0)
    per_seq = LAT_LEN // TQ
    row = lax.broadcasted_iota(jnp.int32, (LANES, 1), 0)
    lane = lax.broadcasted_iota(jnp.int32, (1, LANES), 1)
    top, lo = row < RDK, lane < RDK
    same_head = top == lo
    m_col = lax.broadcasted_iota(jnp.int32, (TQ, 1), 0).astype(F32)

    def scan(lg_off, s0_ref, out_ref, pos, tiles):
        for pair in range(HEADS // 2):
            sl = slice(pair * LANES, (pair + 1) * LANES)
            lg_even, lg_odd = lg_ref[lg_off + 2 * pair], lg_ref[lg_off + 2 * pair + 1]
            dec = jnp.exp2(pos * jnp.where(lo, lg_even, lg_odd))
            tile_decay = jnp.exp2(float(TQ) * jnp.where(top, lg_even, lg_odd))
            state = s0_ref[pair]
            for c in tiles:
                rows = slice(c * TQ, (c + 1) * TQ)
                out_ref[c, pair] = state.astype(BF16)
                local = _bdot((rk_ref[rows, sl].astype(F32) * dec).T.astype(BF16), rv_ref[rows, sl])
                state = state * tile_decay + jnp.where(same_head, local, 0.0)

    @pl.when(s < N_LAT_SEQ)
    def _():
        scan(0, s0f_ref, pf_ref, TQ - 1.0 - m_col, range(per_seq))

    @pl.when(s >= N_LAT_SEQ)
    def _():
        scan(HEADS, s0b_ref, qb_ref, m_col, range(per_seq - 1, -1, -1))


def _ret_prefix(lg, rk, rv, s0f, s0b):
    n_tiles, per_seq = TS // TQ, LAT_LEN // TQ
    seq_of = lambda s: jnp.where(s < N_LAT_SEQ, s, 2 * N_LAT_SEQ - 1 - s)
    st = pl.BlockSpec((None, HEADS // 2, LANES, LANES), lambda s, lg: (seq_of(s), 0, 0, 0))
    kv = pl.BlockSpec((LAT_LEN, HEADS * RDK), lambda s, lg: (TP // LAT_LEN + seq_of(s), 0))
    pf_blk = pl.BlockSpec((per_seq, HEADS // 2, LANES, LANES), lambda s, lg: (jnp.minimum(s, N_LAT_SEQ - 1), 0, 0, 0))
    qb_blk = pl.BlockSpec((per_seq, HEADS // 2, LANES, LANES),
                          lambda s, lg: (jnp.minimum(2 * N_LAT_SEQ - 1 - s, N_LAT_SEQ - 1), 0, 0, 0))
    shape = jax.ShapeDtypeStruct((n_tiles, HEADS // 2, LANES, LANES), BF16)
    return pl.pallas_call(
        _ret_prefix_kernel,
        grid_spec=pltpu.PrefetchScalarGridSpec(
            num_scalar_prefetch=1, grid=(2 * N_LAT_SEQ,),
            in_specs=[kv, kv, st, st], out_specs=[pf_blk, qb_blk]),
        out_shape=[shape, shape],
        compiler_params=_cparams(("arbitrary",)),
        name="ret_prefix",
    )(lg, rk, rv, s0f, s0b)


def _retention_kernel(latent, lg_ref, rq_ref, rk_ref, rv_ref, rg_ref, *refs):
    seq_len = rq_ref.shape[0]
    if latent:
        s0f_ref, s0b_ref, o_ref = refs
    else:
        unperm_ref, o_ref, sf_ref, sb_ref = refs
    tq, tk = rq_ref.shape[0], rk_ref.shape[0]
    nt = (((1,), (1,)), ((), ()))
    n_idx = lax.broadcasted_iota(jnp.int32, (tq, tk), 0).astype(F32)
    m_idx = lax.broadcasted_iota(jnp.int32, (tq, tk), 1).astype(F32)
    dist = n_idx - m_idx
    adist = jnp.abs(dist)
    fwd = dist > 0.0
    diag = jnp.where(dist == 0.0, 1.0, 0.0)
    lane = lax.broadcasted_iota(jnp.int32, (1, LANES), 1)
    lo = lane < RDK
    n_col = lax.broadcasted_iota(jnp.int32, (tq, 1), 0).astype(F32)
    m_col = lax.broadcasted_iota(jnp.int32, (tk, 1), 0).astype(F32)
    for pair in range(HEADS // 2):
        sl = slice(pair * LANES, (pair + 1) * LANES)
        qb, kb, vb = rq_ref[:, sl], rk_ref[:, sl], rv_ref[:, sl]
        acc = jnp.zeros((tq, LANES), F32)
        for sub in range(2):
            h = 2 * pair + sub
            lgf, lgb = lg_ref[h], lg_ref[HEADS + h]
            half = lo if sub == 0 else jnp.logical_not(lo)
            qm = jnp.where(half, qb, jnp.zeros_like(qb))
            vm = jnp.where(half, vb, jnp.zeros_like(vb))
            s = lax.dot_general(qm, kb, nt, preferred_element_type=F32)
            w = jnp.exp2(adist * jnp.where(fwd, lgf, lgb)) + diag
            acc = acc + _bdot((s * w).astype(BF16), vm)
        lgf_l = jnp.where(lo, lg_ref[2 * pair], lg_ref[2 * pair + 1])
        lgb_l = jnp.where(lo, lg_ref[HEADS + 2 * pair], lg_ref[HEADS + 2 * pair + 1])
        if latent:
            acc = acc + _bdot(qb, s0f_ref[pair]) * jnp.exp2((n_col + 1.0) * lgf_l)
            acc = acc + _bdot(qb, s0b_ref[pair]) * jnp.exp2((seq_len - n_col) * lgb_l)
        else:
            v_swapped = pltpu.roll(vb.astype(F32), RDK, axis=1).astype(BF16)
            for st_ref, dec in ((sf_ref, jnp.exp2((seq_len - 1.0 - m_col) * lgf_l)),
                                (sb_ref, jnp.exp2(m_col * lgb_l))):
                kt = (kb.astype(F32) * dec).T.astype(BF16)
                kt = _bdot(unperm_ref[...], kt).astype(BF16)
                st_ref[2 * pair] = _bdot(kt, vb)[0:RDK, 0:RDK]
                st_ref[2 * pair + 1] = _bdot(kt, v_swapped)[RDK:, 0:RDK]
        inv = 1.0 / RDK
        mu = jnp.where(lo, jnp.sum(jnp.where(lo, acc, 0.0), axis=-1, keepdims=True),
                       jnp.sum(jnp.where(lo, 0.0, acc), axis=-1, keepdims=True)) * inv
        dlt = acc - mu
        d2 = dlt * dlt
        var = jnp.where(lo, jnp.sum(jnp.where(lo, d2, 0.0), axis=-1, keepdims=True),
                        jnp.sum(jnp.where(lo, 0.0, d2), axis=-1, keepdims=True)) * inv
        g = rg_ref[:, sl]
        o_ref[:, sl] = (dlt * lax.rsqrt(var + 1e-5) * (g * _sigmoid(g))).astype(BF16)


def _retention_tiles_kernel(n_prev, lg_ref, rq_ref, rk_ref, rv_ref, rg_ref, pf_ref, qb_ref, unperm_ref, *refs):
    prev_refs, (o_ref, sf_ref, sb_ref) = refs[:2 * n_prev], refs[2 * n_prev:]
    is_latent = pl.program_id(0) >= TP // TQR
    views = lambda s: [r.at[pl.ds(s * TQ, TQ)] for r in (rq_ref, rk_ref, rv_ref, rg_ref, o_ref)]

    @pl.when(jnp.logical_not(is_latent))
    def _():
        for k in range(n_prev):
            sf_ref[:, k] = prev_refs[2 * k][...]
            sb_ref[:, k] = prev_refs[2 * k + 1][...]
        for s in range(TQR // TQ):
            rq, rk, rv, rg, o = views(s)
            _retention_kernel(False, lg_ref, rq, rk, rv, rg, unperm_ref, o, sf_ref.at[s, n_prev], sb_ref.at[s, n_prev])

    @pl.when(is_latent)
    def _():
        for s in range(TQR // TQ):
            rq, rk, rv, rg, o = views(s)
            _retention_kernel(True, lg_ref, rq, rk, rv, rg, pf_ref.at[s], qb_ref.at[s], o)


def _retention(lg, rq, rk, rv, rg, s0f, s0b, unperm, prev_states):
    w = HEADS * RDK
    n_prev = 0 if prev_states is None else prev_states[0].shape[1]
    pf, qb = _ret_prefix(lg, rk, rv, s0f, s0b)
    assert TQ == PROMPT_LEN
    per_tile = TQR // TQ
    tile = lambda i, lg: (i, 0)
    s0_blk = pl.BlockSpec((per_tile, HEADS // 2, LANES, LANES),
                          lambda i, lg: (jnp.maximum(i - TP // TQR, 0), 0, 0, 0))
    seq = lambda i, lg: (jnp.minimum(i, TP // TQR - 1), 0, 0, 0, 0)
    st_blk = pl.BlockSpec((per_tile, n_prev + 1, HEADS, RDK, RDK), seq)
    st_shape = jax.ShapeDtypeStruct((N_PROMPT_SEQ, n_prev + 1, HEADS, RDK, RDK), F32)
    prev = () if prev_states is None else tuple(prev_states)
    prev_specs = [pl.BlockSpec((per_tile, None, HEADS, RDK, RDK),
                               functools.partial(lambda k, i, lg: (jnp.minimum(i, TP // TQR - 1), k, 0, 0, 0), k))
                  for k in range(n_prev) for _ in range(2)]
    prev_args = [p for k in range(n_prev) for p in prev]
    return pl.pallas_call(
        functools.partial(_retention_tiles_kernel, n_prev),
        grid_spec=pltpu.PrefetchScalarGridSpec(
            num_scalar_prefetch=1, grid=(T // TQR,),
            in_specs=[pl.BlockSpec((TQR, w), tile)] * 4 + [s0_blk, s0_blk,
                      pl.BlockSpec((LANES, LANES), lambda i, lg: (0, 0))] + prev_specs,
            out_specs=[pl.BlockSpec((TQR, w), tile), st_blk, st_blk]),
        out_shape=[jax.ShapeDtypeStruct((T, w), BF16), st_shape, st_shape],
        compiler_params=_cparams(("arbitrary",)),
        name="retention",
    )(lg, rq, rk, rv, rg, pf, qb, unperm, *prev_args)


def _mix_ffn_kernel(n_x, *refs):
    x_refs, refs = refs[:n_x], refs[n_x:]
    a_ref, r_ref, mod_ref, wo_ref, wg_ref, wu_ref, wd_ref, ln_ref, o_ref = refs
    half = HEADS * VDIM
    y = _bdot(a_ref[...], wo_ref[0:half, :]) + _bdot(r_ref[...], wo_ref[half:, :])
    x1 = _layer_norm(ALPHA * _tile_rows(x_refs, TM_FF) + mod_ref[2:3, :] * y, ln_ref[0:1, :], ln_ref[1:2, :])
    h = (x1 * (1.0 + mod_ref[4:5, :]) + mod_ref[3:4, :]).astype(BF16)
    acc = None
    for f in range(D_FF // TF):
        cols = slice(f * TF, (f + 1) * TF)
        g = _bdot(h, wg_ref[:, cols])
        u = _bdot(h, wu_ref[:, cols])
        part = _bdot((g * _sigmoid(g) * u).astype(BF16), wd_ref[cols, :])
        acc = part if acc is None else acc + part
    o_ref[...] = _layer_norm(ALPHA * x1 + mod_ref[5:6, :] * acc, ln_ref[2:3, :], ln_ref[3:4, :])


def _mix_ffn(x, attn, ret, mod_l, w_out, wg, wu, wd, ln, j):
    x_args, x_specs = _tile_specs(x, TM_FF)
    tok = lambda w: pl.BlockSpec((TM_FF, w), lambda i: (i, 0))
    resident = lambda a: pl.BlockSpec((None,) + a.shape[1:], lambda i: (j, 0, 0), pipeline_mode=pl.Buffered(1))
    return pl.pallas_call(
        functools.partial(_mix_ffn_kernel, len(x_args)),
        grid=(T // TM_FF,),
        in_specs=x_specs + [tok(HEADS * VDIM), tok(HEADS * RDK),
                  pl.BlockSpec((None, 8, D), lambda i: (_group_of_tile(i, TM_FF), 0, 0)),
                  resident(w_out), resident(wg), resident(wu), resident(wd), pl.BlockSpec((8, D), lambda i: (0, 0))],
        out_specs=tok(D),
        out_shape=jax.ShapeDtypeStruct((T, D), F32),
        compiler_params=_cparams(("arbitrary",)),
        name="mix_ffn",
    )(*x_args, attn, ret, mod_l, w_out, wg, wu, wd, ln)


def _conv_in_kernel(x_ref, mod_ref, w_ref, b_ref, z_ref):
    h = (x_ref[...] * (1.0 + mod_ref[1:2, :]) + mod_ref[0:1, :]).astype(BF16)
    p = _bdot(h, w_ref[...])
    b_ref[...] = p[:, 0:D]
    z_ref[...] = p[:, D:2 * D] * p[:, 2 * D:3 * D]


def _conv_in(x, mod_l, w_in, j):
    tm = 2 * TM_FF
    tok = pl.BlockSpec((tm, D), lambda i: (i, 0))
    return pl.pallas_call(
        _conv_in_kernel,
        grid=(T // tm,),
        in_specs=[tok, pl.BlockSpec((None, 8, D), lambda i: (_group_of_tile(i, tm), 0, 0)),
                  pl.BlockSpec((None,) + w_in.shape[1:], lambda i: (j, 0, 0), pipeline_mode=pl.Buffered(1))],
        out_specs=[tok, tok],
        out_shape=[jax.ShapeDtypeStruct((T, D), F32)] * 2,
        compiler_params=_cparams(("parallel",)),
        name="conv_in",
    )(x, mod_l, w_in)


def _conv_out_kernel(x_ref, b_ref, z_ref, zp_ref, zn_ref, mod_ref, cw_ref, w_ref, ln_ref, rw_ref, rb_ref,
                     o_ref, h_ref, route_ref):
    i = pl.program_id(0)
    z = z_ref[...]
    tm = z.shape[0]
    row = lax.broadcasted_iota(jnp.int32, (tm, 1), 0)
    seq_len = jnp.where(i < TP // tm, PROMPT_LEN, LAT_LEN)
    pos = (i * tm + row) & (seq_len - 1)
    prev = jnp.where(row == 0, zp_ref[7:8, :], pltpu.roll(z, 1, axis=0))
    prev = jnp.where(pos == 0, 0.0, prev)
    nxt = jnp.where(row == tm - 1, zn_ref[0:1, :], pltpu.roll(z, tm - 1, axis=0))
    nxt = jnp.where(pos == seq_len - 1, 0.0, nxt)
    y = prev * cw_ref[0:1, :] + z * cw_ref[1:2, :] + nxt * cw_ref[2:3, :]
    t = _bdot((b_ref[...] * y).astype(BF16), w_ref[...])
    x1 = _layer_norm(ALPHA * x_ref[...] + mod_ref[2:3, :] * t, ln_ref[0:1, :], ln_ref[1:2, :])
    o_ref[...] = x1
    h = x1 * (1.0 + mod_ref[4:5, :]) + mod_ref[3:4, :]
    h_hi = h.astype(BF16)
    h_ref[...] = h_hi
    h_lo = (h - h_hi.astype(F32)).astype(BF16)
    both = _bdot(h_hi, rw_ref[...])
    logits = both[:, :LANES] + both[:, LANES:] + _bdot(h_lo, rw_ref[:, :LANES]) + rb_ref[...]
    lane = lax.broadcasted_iota(jnp.int32, logits.shape, 1).astype(F32)
    t1 = jnp.max(logits, axis=-1, keepdims=True)
    i1 = jnp.min(jnp.where(logits == t1, lane, float(LANES)), axis=-1, keepdims=True)
    rest = jnp.where(lane == i1, -jnp.inf, logits)
    t2 = jnp.max(rest, axis=-1, keepdims=True)
    i2 = jnp.min(jnp.where(rest == t2, lane, float(LANES)), axis=-1, keepdims=True)
    e = jnp.exp(t2 - t1)
    den = 1.0 + e
    route_ref[...] = jnp.where(lane == 0.0, i1, jnp.where(lane == 1.0, i2,
                               jnp.where(lane == 2.0, 1.0 / den, jnp.where(lane == 3.0, e / den, 0.0))))


def _conv_out(x, b, z, mod_l, cw, w_out, ln, rw, rb, j):
    tok = pl.BlockSpec((TM_FF, D), lambda i: (i, 0))
    sub = TM_FF // 8
    return pl.pallas_call(
        _conv_out_kernel,
        grid=(T // TM_FF,),
        in_specs=[tok, tok, tok,
                  pl.BlockSpec((8, D), lambda i: (jnp.maximum(i * sub - 1, 0), 0)),
                  pl.BlockSpec((8, D), lambda i: (jnp.minimum((i + 1) * sub, T // 8 - 1), 0)),
                  pl.BlockSpec((None, 8, D), lambda i: (_group_of_tile(i, TM_FF), 0, 0)),
                  pl.BlockSpec((8, D), lambda i: (0, 0)), pl.BlockSpec((None, D, D), lambda i: (j, 0, 0)),
                  pl.BlockSpec((8, D), lambda i: (0, 0)),
                  pl.BlockSpec((D, 2 * LANES), lambda i: (0, 0)), pl.BlockSpec((1, LANES), lambda i: (0, 0))],
        out_specs=[tok, tok, pl.BlockSpec((TM_FF, LANES), lambda i: (i, 0))],
        out_shape=[jax.ShapeDtypeStruct((T, D), F32), jax.ShapeDtypeStruct((T, D), BF16),
                   jax.ShapeDtypeStruct((T, LANES), F32)],
        compiler_params=_cparams(("parallel",)),
        name="conv_out",
    )(x, b, z, z, z, mod_l, cw, w_out, ln, rw, rb)


def _moe_up_kernel(te_ref, nv_ref, x_ref, wg_ref, wu_ref, a_ref):
    @pl.when(pl.program_id(1) < nv_ref[0])
    def _():
        h = x_ref[...]
        g = _wdot(h, wg_ref[...])
        u = _wdot(h, wu_ref[...])
        a_ref[...] = (g * _sigmoid(g) * u).astype(BF16)

    @pl.when(pl.program_id(1) >= nv_ref[0])
    def _():
        a_ref[...] = jnp.zeros_like(a_ref)


def _moe_down_kernel(te_ref, nv_ref, a_ref, wd_ref, o_ref):
    @pl.when(pl.program_id(0) < nv_ref[0])
    def _():
        o_ref[...] = _wdot(a_ref[...], wd_ref[...])

    @pl.when(pl.program_id(0) >= nv_ref[0])
    def _():
        o_ref[...] = jnp.zeros_like(o_ref)


def _moe(tile_expert, n_valid, xs, wg, wu, wd, j):
    n_tiles = NP_ROWS // TM_FF
    act = pl.pallas_call(
        _moe_up_kernel,
        grid_spec=pltpu.PrefetchScalarGridSpec(
            num_scalar_prefetch=2, grid=(D_FF // TF, n_tiles),
            in_specs=[pl.BlockSpec((TM_FF, D), lambda f, i, te, nv: (i, 0)),
                      pl.BlockSpec((None, None, D, TF), lambda f, i, te, nv: (j, te[i], 0, f)),
                      pl.BlockSpec((None, None, D, TF), lambda f, i, te, nv: (j, te[i], 0, f))],
            out_specs=pl.BlockSpec((TM_FF, TF), lambda f, i, te, nv: (i, f))),
        out_shape=jax.ShapeDtypeStruct((NP_ROWS, D_FF), BF16),
        compiler_params=_cparams(("arbitrary", "arbitrary")),
        name="moe_up",
    )(tile_expert, n_valid, xs, wg, wu)
    return pl.pallas_call(
        _moe_down_kernel,
        grid_spec=pltpu.PrefetchScalarGridSpec(
            num_scalar_prefetch=2, grid=(n_tiles,),
            in_specs=[pl.BlockSpec((TM_FF, D_FF), lambda i, te, nv: (i, 0)),
                      pl.BlockSpec((None, None, D_FF, D), lambda i, te, nv: (j, te[i], 0, 0))],
            out_specs=pl.BlockSpec((TM_FF, D), lambda i, te, nv: (i, 0))),
        out_shape=jax.ShapeDtypeStruct((NP_ROWS, D), F32),
        compiler_params=_cparams(("arbitrary",)),
        name="moe_down",
    )(tile_expert, n_valid, act, wd)


def _combine_kernel(split, x_ref, o0_ref, o1_ref, route_ref, mod_ref, ln_ref, *o_refs):
    y = route_ref[:, 2:3] * o0_ref[...] + route_ref[:, 3:4] * o1_ref[...]
    out = _layer_norm(ALPHA * x_ref[...] + mod_ref[5:6, :] * y, ln_ref[2:3, :], ln_ref[3:4, :])
    if not split:
        o_refs[0][...] = out
        return
    is_latent = pl.program_id(0) >= TP // TM_FF

    @pl.when(jnp.logical_not(is_latent))
    def _():
        o_refs[0][...] = out

    @pl.when(is_latent)
    def _():
        o_refs[1][...] = out


def _combine(x, o0, o1, route, mod_l, ln, split):
    tok = pl.BlockSpec((TM_FF, D), lambda i: (i, 0))
    if split:
        out_specs = [pl.BlockSpec((TM_FF, D), lambda i: (jnp.minimum(i, TP // TM_FF - 1), 0)),
                     pl.BlockSpec((TM_FF, D), lambda i: (jnp.maximum(i - TP // TM_FF, 0), 0))]
        out_shape = [jax.ShapeDtypeStruct((TP, D), F32), jax.ShapeDtypeStruct((TS, D), F32)]
    else:
        out_specs, out_shape = tok, jax.ShapeDtypeStruct((T, D), F32)
    return pl.pallas_call(
        functools.partial(_combine_kernel, split),
        grid=(T // TM_FF,),
        in_specs=[tok, tok, tok, pl.BlockSpec((TM_FF, LANES), lambda i: (i, 0)),
                  pl.BlockSpec((None, 8, D), lambda i: (_group_of_tile(i, TM_FF), 0, 0)),
                  pl.BlockSpec((8, D), lambda i: (0, 0))],
        out_specs=out_specs,
        out_shape=out_shape,
        compiler_params=_cparams(("arbitrary",)),
        name="moe_combine",
    )(x, o0, o1, route, mod_l, ln)


def _routing_plan(route):
    e = jnp.concatenate([route[:, 0], route[:, 1]]).astype(jnp.int32)
    onehot = (e[:, None] == jnp.arange(N_EXP, dtype=jnp.int32)[None, :]).astype(jnp.int32)
    csum = jnp.cumsum(onehot, axis=0)
    counts = csum[-1]
    rank = jnp.sum((csum - onehot) * onehot, axis=1)
    padded = (counts + TM_FF - 1) // TM_FF * TM_FF
    pend = jnp.cumsum(padded)
    dest = jnp.sum(onehot * (pend - padded)[None, :], axis=1) + rank
    order = jnp.argsort(e, stable=True).astype(jnp.int32)
    rows = jnp.arange(NP_ROWS, dtype=jnp.int32)
    before = (rows[:, None] >= pend[None, :]).astype(jnp.int32)
    row_e = jnp.minimum(jnp.sum(before, axis=1), N_EXP - 1)
    row_cnt = jnp.sum((row_e[:, None] == jnp.arange(N_EXP, dtype=jnp.int32)[None, :]) * counts[None, :], axis=1)
    q = jnp.clip(rows - jnp.sum(before * padded[None, :], axis=1), 0, jnp.maximum(row_cnt - 1, 0))
    src = jnp.minimum(jnp.sum(before * counts[None, :], axis=1) + q, 2 * T - 1)
    row_token = order[src] % T
    n_valid = (pend[-1] // TM_FF).astype(jnp.int32)
    tile_start = jnp.minimum(jnp.arange(NP_ROWS // TM_FF, dtype=jnp.int32), n_valid - 1) * TM_FF
    tile_expert = jnp.minimum(jnp.sum((tile_start[:, None] >= pend[None, :]).astype(jnp.int32), axis=1), N_EXP - 1)
    return dest[:T], dest[T:], row_token, tile_expert.astype(jnp.int32), n_valid.reshape(1)


_INV_PERM_RDK = np.argsort(np.concatenate([np.arange(0, RDK, 2), np.arange(1, RDK, 2)]))


def _pad_cols(a, width):
    return jnp.pad(a, ((0, 0), (0, width - a.shape[1])))


def _deinterleave(a):
    n = a.shape[-1]
    return jnp.swapaxes(a.reshape(a.shape[:-1] + (n // 2, 2)), -1, -2).reshape(a.shape)


def _prep_even_weights(w_in, w_q_b, w_kv_b):
    o_kpe = Q_LORA + KV_LORA
    o_rq, o_rk, o_rv = o_kpe + ROPE, o_kpe + ROPE + RET_W, o_kpe + ROPE + 2 * RET_W
    heads = lambda a: _deinterleave(a.reshape(D, HEADS, RDK)).reshape(D, HEADS * RDK)
    kpe = w_in[:, o_kpe:o_kpe + ROPE]
    w_in_p = jnp.concatenate([
        w_in[:, :o_kpe], heads(w_in[:, o_rq:o_rk]), heads(w_in[:, o_rk:o_rv]) * (RDK ** -0.5), w_in[:, o_rv:],
        _pad_cols(kpe, LANES), _pad_cols(_deinterleave(kpe), LANES),
        _pad_cols(jnp.concatenate([-kpe[:, 1::2], kpe[:, 0::2]], axis=1), LANES)], axis=1).astype(BF16)
    assert w_in_p.shape == (D, IN_COLS)
    wq = w_q_b.reshape(Q_LORA, HEADS, NOPE + ROPE)
    wq = jnp.concatenate([wq[:, :, :NOPE], _deinterleave(wq[:, :, NOPE:]),
                          jnp.zeros((Q_LORA, HEADS, LANES - NOPE - ROPE), F32)], axis=2)
    wq = wq.reshape(Q_LORA, HEADS * LANES).astype(BF16)
    wkv = w_kv_b.reshape(KV_LORA, HEADS, NOPE + VDIM)
    zero = jnp.zeros((KV_LORA, HEADS, LANES - NOPE), F32)
    wk = jnp.concatenate([wkv[:, :, :NOPE], zero], axis=2).reshape(KV_LORA, HEADS * LANES)
    wv = wkv[:, :, NOPE:].reshape(KV_LORA, HEADS // 2, 2, VDIM)
    zv = jnp.zeros((KV_LORA, HEADS // 2, VDIM), F32)
    wv = jnp.stack([jnp.concatenate([wv[:, :, 0], zv], axis=2), jnp.concatenate([zv, wv[:, :, 1]], axis=2)], axis=2)
    wkv_p = jnp.concatenate([wk, wv.reshape(KV_LORA, HEADS * LANES)], axis=1).astype(BF16)
    return w_in_p, wq, wkv_p


def _placement():
    ek = np.zeros((LANES, HEADS * LANES), np.float32)
    for h in range(HEADS):
        ek[np.arange(ROPE), h * LANES + NOPE + np.arange(ROPE)] = 1.0
    return jnp.asarray(ek, BF16)


def _rotary_tables():
    rows = LAT_LEN // GRID_W
    r, col = jnp.meshgrid(jnp.arange(rows, dtype=F32), jnp.arange(GRID_W, dtype=F32), indexing='ij')
    n_freq = ROPE // 4
    freqs = 1.0 / (10000.0 ** (jnp.arange(n_freq, dtype=F32) / n_freq))
    ang = jnp.concatenate([r.reshape(-1)[:, None] * freqs, col.reshape(-1)[:, None] * freqs], axis=-1)
    cos, sin = jnp.cos(ang), jnp.sin(ang)
    theta = 1.0 / (10000.0 ** jnp.linspace(0.0, 1.0, RDK // 2, dtype=F32))
    rang = jnp.arange(LAT_LEN, dtype=F32)[:, None] * theta
    rcos, rsin = jnp.cos(rang), jnp.sin(rang)
    one = lambda w: jnp.ones((LAT_LEN, w), F32)
    zero = lambda w: jnp.zeros((LAT_LEN, w), F32)
    lat = [jnp.concatenate([one(NOPE), cos, cos, one(LANES - NOPE - ROPE)], axis=1),
           jnp.concatenate([zero(NOPE), sin, sin, zero(LANES - NOPE - ROPE)], axis=1),
           jnp.concatenate([cos, cos, zero(LANES - ROPE)], axis=1),
           jnp.concatenate([sin, sin, zero(LANES - ROPE)], axis=1),
           jnp.concatenate([rcos] * 4, axis=1), jnp.concatenate([rsin] * 4, axis=1)]
    ident = [np.ones((TM, LANES), np.float32), np.zeros((TM, LANES), np.float32)]
    ident_k = np.concatenate([np.ones((TM, ROPE), np.float32), np.zeros((TM, LANES - ROPE), np.float32)], axis=1)
    ident = [ident[0], ident[1], ident_k, ident[1], ident[0], ident[1]]
    return [jnp.concatenate([l, jnp.asarray(c)], axis=0) for l, c in zip(lat, ident)]


def _block_diag_states(s0):
    s = jnp.swapaxes(_deinterleave(jnp.swapaxes(s0, -1, -2)), -1, -2)
    s = s.reshape(s0.shape[0], HEADS // 2, 2, RDK, RDK)
    z = jnp.zeros_like(s[:, :, 0])
    top = jnp.concatenate([s[:, :, 0], z], axis=-1)
    bot = jnp.concatenate([z, s[:, :, 1]], axis=-1)
    return jnp.concatenate([top, bot], axis=-2)


def _unpermute_matrix():
    m = np.zeros((LANES, LANES), np.float32)
    for blk in range(LANES // RDK):
        m[blk * RDK + np.arange(RDK), blk * RDK + _INV_PERM_RDK] = 1.0
    return jnp.asarray(m, BF16)


def kernel(x_prompt, x_sample, c, cache_ckv, cache_kpe, state_ret_fwd, state_ret_bwd, c_ctx, w_mod, b_mod, ln_g, ln_b, w_in_mix, q_a_gain, kv_a_gain, w_q_b, w_kv_b, ret_decay_fwd, ret_decay_bwd, w_out_mix, w_in_conv, conv_w, w_out_conv, ffn_gate, ffn_up, ffn_down, router_w, router_b, exp_gate, exp_up, exp_down):
    x = (x_prompt.reshape(TP, D), x_sample.reshape(TS, D))
    cond8 = jnp.concatenate([c_ctx[None], c, jnp.zeros((8 - 1 - N_LAT_SEQ, D), F32)], axis=0)
    mods = _modulation(cond8, w_mod, b_mod)
    mods = jnp.pad(mods.reshape(DEPTH, N_GROUPS, 6, D), ((0, 0), (0, 0), (0, 2), (0, 0)))
    ln = jnp.pad(jnp.concatenate([ln_g, ln_b], axis=1)[:, jnp.array([0, 2, 1, 3])], ((0, 0), (0, 4), (0, 0)))
    tabs = _rotary_tables()
    ek = _placement()
    unperm = _unpermute_matrix()
    bf = lambda a: a.astype(BF16)
    w_out_mix_b, w_in_conv_b, w_out_conv_b = bf(w_out_mix), bf(w_in_conv), bf(w_out_conv)
    ffn_b = (bf(ffn_gate), bf(ffn_up), bf(ffn_down))
    exp_b = (exp_gate, exp_up, exp_down)
    cache, states = None, None
    for layer in range(DEPTH):
        j = layer // 2
        mod_l, ln_l = mods[layer], ln[layer]
        if layer % 2 == 0:
            w_in_p, wq, wkv = _prep_even_weights(w_in_mix[j], w_q_b[j], w_kv_b[j])
            q, k, v, cache, rq, rk, rv, rg = _even_in(
                x, mod_l, w_in_p, q_a_gain[j][None], kv_a_gain[j][None], wq, wkv, ek, tabs, cache)
            kpe_c = _pad_cols(_deinterleave(cache_kpe[:, j]).reshape(N_LAT_SEQ * PAST, ROPE), LANES)
            kc, vc = _ctx_kv(cache_ckv[:, j].reshape(N_LAT_SEQ * PAST, KV_LORA), kpe_c, wkv, ek)
            attn = _attention(q, k, v, kc, vc)
            lg = LOG2E * jnp.concatenate([jax.nn.log_sigmoid(ret_decay_fwd[j].astype(F32)),
                                          jax.nn.log_sigmoid(ret_decay_bwd[j].astype(F32))])
            ret, sf, sb = _retention(lg, rq, rk, rv, rg, _block_diag_states(state_ret_fwd[:, j]),
                                     _block_diag_states(state_ret_bwd[:, j]), unperm, states)
            states = (sf, sb)
            x = _mix_ffn(x, attn, ret, mod_l, w_out_mix_b, *ffn_b, ln_l, j)
        else:
            b, z = _conv_in(x, mod_l, w_in_conv_b, j)
            cw = jnp.pad(conv_w[j], ((0, 5), (0, 0)))
            rw = _pad_cols(router_w[j], LANES)
            rw_hi = rw.astype(BF16)
            rw = jnp.concatenate([rw_hi, (rw - rw_hi.astype(F32)).astype(BF16)], axis=1)
            rb = jnp.concatenate([router_b[j].astype(F32), jnp.full((LANES - N_EXP,), -1e30, F32)])[None]
            x, h, route = _conv_out(x, b, z, mod_l, cw, w_out_conv_b, ln_l, rw, rb, j)
            dest0, dest1, row_token, tile_expert, n_valid = _routing_plan(route)
            out_sorted = _moe(tile_expert, n_valid, h[row_token], *exp_b, j)
            x = _combine(x, out_sorted[dest0], out_sorted[dest1], route, mod_l, ln_l, split=layer == DEPTH - 1)
    y_prompt = x[0].reshape(N_PROMPT_SEQ, PROMPT_LEN, D)
    y_sample = x[1].reshape(N_LAT_SEQ, LAT_LEN, D)
    return (y_prompt, y_sample, cache[0], cache[1], states[0], states[1])
```
